```python
import math
import jax, jax.numpy as jnp
from jax import lax
import numpy as np

D_MODEL = 1024
BATCH = 8
SEQ = 4096
DEPTH = 1

CTX_LEN = 256
GRID_W = 64
D_MIX = D_MODEL
D_S5 = D_MIX // 2
D_HY = D_MIX - D_S5
S5_GROUP = 16
S5_GROUPS = D_S5 // S5_GROUP
S5_STATE = 64
S5_DT_MIN = 1e-3
S5_DT_MAX = 1e-1
HY_ORDER = 2
HY_BANDS = 16
HY_EMB = 1 + 2 * HY_BANDS
HY_FILTER_HIDDEN = 64
HY_DECAY_TARGET = 1e-2
HY_FAST_DECAY = 0.3
HY_SLOW_DECAY = 1.5
SHORT_CONV = 3
D_FF = 4 * D_MODEL
N_MOD = 6
POS_BASE = 10000.0
EPS = 1e-6

kernel_name = "hymba_s5_hyena_prefix_dit_block"


def rmsnorm(x, g):
    xf = x.astype(jnp.float32)
    y = xf * lax.rsqrt(jnp.mean(xf * xf, axis=-1, keepdims=True) + EPS)
    return (y * g.astype(jnp.float32)).astype(x.dtype)


def pos_embed_2d(n, d):
    rows = n // GRID_W
    row = jnp.repeat(jnp.arange(rows, dtype=jnp.float32), GRID_W)
    col = jnp.tile(jnp.arange(GRID_W, dtype=jnp.float32), rows)
    quarter = d // 4
    omega = 1.0 / (POS_BASE ** (jnp.arange(quarter, dtype=jnp.float32) / quarter))

    def enc(p):
        ang = p[:, None] * omega[None, :]
        return jnp.concatenate([jnp.sin(ang), jnp.cos(ang)], axis=-1)

    return jnp.concatenate([enc(row), enc(col)], axis=-1)


def _linear_recurrence(left, right):
    a1, b1 = left
    a2, b2 = right
    return a1 * a2, a2 * b1 + b2


def s5_discretize(a_re, a_im, log_step, b_re, b_im):
    lam = lax.complex(a_re.astype(jnp.float32), a_im.astype(jnp.float32))
    step = jnp.exp(log_step.astype(jnp.float32))[:, None]
    lam_bar = jnp.exp(lam * step)
    bmat = lax.complex(b_re.astype(jnp.float32), b_im.astype(jnp.float32))
    b_bar = ((lam_bar - 1.0) / lam)[..., None] * bmat
    return lam_bar, b_bar


def s5_scan(u, lam_bar, b_bar, s0, reverse):
    bu = jnp.einsum('blgc,gpc->lbgp', u.astype(jnp.complex64), b_bar)
    if s0 is not None:
        edge = bu.shape[0] - 1 if reverse else 0
        bu = bu.at[edge].add(lam_bar[None] * s0)
    a = jnp.broadcast_to(lam_bar[None, None], (bu.shape[0], 1) + lam_bar.shape)
    _, states = lax.associative_scan(_linear_recurrence, (a, bu), reverse=reverse)
    return states


def s5_mixer(u, u_ctx, a_re, a_im, log_step, b_re, b_im, c_re, c_im, d_skip, glu_w, glu_b):
    bsz, n, _ = u.shape
    uf = u.astype(jnp.float32).reshape(bsz, n, S5_GROUPS, S5_GROUP)
    uc = u_ctx.astype(jnp.float32).reshape(bsz, u_ctx.shape[1], S5_GROUPS, S5_GROUP)
    outs = []
    for direction, reverse in enumerate((False, True)):
        lam_bar, b_bar = s5_discretize(a_re[direction], a_im[direction], log_step[direction],
                                       b_re[direction], b_im[direction])
        ctx_states = s5_scan(uc, lam_bar, b_bar, None, reverse)
        s0 = ctx_states[0] if reverse else ctx_states[-1]
        states = s5_scan(uf, lam_bar, b_bar, s0, reverse)
        c_mat = lax.complex(c_re[direction].astype(jnp.float32),
                            c_im[direction].astype(jnp.float32))
        outs.append(jnp.real(jnp.einsum('lbgp,gcp->blgc', states, c_mat)))
    y = outs[0] + outs[1] + uf * d_skip.astype(jnp.float32).reshape(S5_GROUPS, S5_GROUP)
    y = jax.nn.gelu(y.reshape(bsz, n, D_S5))
    ab = jnp.einsum('bld,de->ble', y, glu_w.astype(jnp.float32)) + glu_b.astype(jnp.float32)
    val, gate = jnp.split(ab, 2, axis=-1)
    return (val * jax.nn.sigmoid(gate)).astype(u.dtype)


def short_conv(x, w, b):
    xp = jnp.pad(x, ((0, 0), (1, 1), (0, 0)))
    return xp[:, :-2] * w[0] + xp[:, 1:-1] * w[1] + xp[:, 2:] * w[2] + b


def hyena_filters(n, w1, b1, w2, b2, freq, w3, decay):
    f32 = jnp.float32
    t = jnp.linspace(0.0, 1.0, n, dtype=f32)[:, None]
    w = 2.0 * math.pi * jnp.arange(n, dtype=f32) / n
    bands = jnp.linspace(1e-4, HY_BANDS - 1, HY_BANDS, dtype=f32)
    ang = w[:, None] * bands[None, :]
    emb = jnp.concatenate([t, jnp.cos(ang), -jnp.sin(ang)], axis=-1)
    fr = freq.astype(f32)
    h = jnp.sin(fr * (emb @ w1.astype(f32) + b1.astype(f32)))
    h = jnp.sin(fr * (h @ w2.astype(f32) + b2.astype(f32)))
    h = (h @ w3.astype(f32)).reshape(n, HY_ORDER, 2, D_HY)
    window = jnp.exp(-t[:, :, None] * jnp.abs(decay.astype(f32))[None])
    return h * window[:, :, None, :]


def fft_conv_bidir(u, h_fwd, h_bwd, bias):
    n = u.shape[1]
    filt = jnp.concatenate([h_fwd, jnp.zeros((1, h_fwd.shape[1]), h_fwd.dtype), h_bwd[:0:-1]], axis=0)
    u_f = jnp.fft.rfft(u, n=2 * n, axis=1)
    k_f = jnp.fft.rfft(filt, n=2 * n, axis=0)
    y = jnp.fft.irfft(u_f * k_f[None], n=2 * n, axis=1)[:, :n]
    return y + u * bias


def hyena_mixer(z_in, conv_w, conv_b, f_w1, f_b1, f_w2, f_b2, f_freq, f_w3, decay, bias):
    f32 = jnp.float32
    n = z_in.shape[1]
    zc = short_conv(z_in.astype(f32), conv_w.astype(f32), conv_b.astype(f32))
    v, x1, x2 = jnp.split(zc, 3, axis=-1)
    filt = hyena_filters(n, f_w1, f_b1, f_w2, f_b2, f_freq, f_w3, decay)
    z = v
    for order, gate in enumerate((x1, x2)):
        z = gate * fft_conv_bidir(z, filt[:, order, 0], filt[:, order, 1], bias[order].astype(f32))
    return z.astype(z_in.dtype)


def setup_inputs(seed: int = 0) -> dict:
    key = jax.random.key(seed)
    ks = iter(jax.random.split(key, 48))
    f32 = jnp.float32

    def nrm(shape, scale):
        return jax.random.normal(next(ks), shape, f32) * scale

    G, P, C, H = S5_GROUPS, S5_STATE, S5_GROUP, HY_FILTER_HIDDEN
    min_decay = math.log(HY_DECAY_TARGET) / HY_SLOW_DECAY
    max_decay = math.log(HY_DECAY_TARGET) / HY_FAST_DECAY
    decay_lin = jnp.abs(jnp.linspace(min_decay, max_decay, D_HY, dtype=f32))
    return {
        "x": nrm((BATCH, SEQ, D_MODEL), 1.0),
        "c": nrm((BATCH, D_MODEL), 1.0),
        "ctx": nrm((BATCH, CTX_LEN, D_MODEL), 1.0),
        "c_ctx": nrm((D_MODEL,), 1.0),
        "ada_w": nrm((DEPTH, D_MODEL, N_MOD * D_MODEL), 0.5 * D_MODEL ** -0.5),
        "ada_b": nrm((DEPTH, N_MOD * D_MODEL), 0.02),
        "norm1_g": 1.0 + nrm((DEPTH, D_MODEL), 0.02),
        "w_in": nrm((DEPTH, D_MODEL, D_S5 + 3 * D_HY), D_MODEL ** -0.5),
        "s5_a_re": -0.5 + nrm((DEPTH, 2, G, P), 0.01),
        "s5_a_im": math.pi * jnp.arange(P, dtype=f32) + nrm((DEPTH, 2, G, P), 0.01),
        "s5_log_step": jax.random.uniform(next(ks), (DEPTH, 2, G), f32,
                                          math.log(S5_DT_MIN), math.log(S5_DT_MAX)),
        "s5_b_re": nrm((DEPTH, 2, G, P, C), (2 * C) ** -0.5),
        "s5_b_im": nrm((DEPTH, 2, G, P, C), (2 * C) ** -0.5),
        "s5_c_re": nrm((DEPTH, 2, G, C, P), (2 * P) ** -0.5),
        "s5_c_im": nrm((DEPTH, 2, G, C, P), (2 * P) ** -0.5),
        "s5_d": nrm((DEPTH, D_S5), 1.0),
        "s5_glu_w": nrm((DEPTH, D_S5, 2 * D_S5), D_S5 ** -0.5),
        "s5_glu_b": nrm((DEPTH, 2 * D_S5), 0.02),
        "hy_conv_w": nrm((DEPTH, SHORT_CONV, 3 * D_HY), SHORT_CONV ** -0.5),
        "hy_conv_b": nrm((DEPTH, 3 * D_HY), 0.02),
        "hy_f_w1": nrm((DEPTH, HY_EMB, H), HY_EMB ** -0.5),
        "hy_f_b1": nrm((DEPTH, H), 0.02),
        "hy_f_w2": nrm((DEPTH, H, H), H ** -0.5),
        "hy_f_b2": nrm((DEPTH, H), 0.02),
        "hy_f_freq": 1.0 + nrm((DEPTH, H), 0.02),
        "hy_f_w3": nrm((DEPTH, H, HY_ORDER * 2 * D_HY), H ** -0.5),
        "hy_decay": decay_lin * (1.0 + nrm((DEPTH, HY_ORDER, D_HY), 0.02)),
        "hy_bias": nrm((DEPTH, HY_ORDER, D_HY), 1.0),
        "mix_g_s5": 1.0 + nrm((DEPTH, D_S5), 0.02),
        "mix_g_hy": 1.0 + nrm((DEPTH, D_HY), 0.02),
        "w_out": nrm((DEPTH, D_MIX, D_MODEL), D_MIX ** -0.5),
        "norm2_g": 1.0 + nrm((DEPTH, D_MODEL), 0.02),
        "mlp_w1": nrm((DEPTH, D_MODEL, D_FF), D_MODEL ** -0.5),
        "mlp_w2": nrm((DEPTH, D_FF, D_MODEL), D_FF ** -0.5),
        "final_g": 1.0 + nrm((D_MODEL,), 0.02),
    }


def reference(x, c, ctx, c_ctx, ada_w, ada_b, norm1_g, w_in,
              s5_a_re, s5_a_im, s5_log_step, s5_b_re, s5_b_im, s5_c_re, s5_c_im,
              s5_d, s5_glu_w, s5_glu_b,
              hy_conv_w, hy_conv_b, hy_f_w1, hy_f_b1, hy_f_w2, hy_f_b2, hy_f_freq,
              hy_f_w3, hy_decay, hy_bias,
              mix_g_s5, mix_g_hy, w_out, norm2_g, mlp_w1, mlp_w2, final_g):
    n = x.shape[1]
    h = x + pos_embed_2d(n, D_MODEL).astype(x.dtype)[None]
    c_act = jax.nn.silu(c)
    c_ctx_act = jax.nn.silu(c_ctx)
    for i in range(DEPTH):
        mod = c_act @ ada_w[i] + ada_b[i]
        shift1, scale1, gate1, shift2, scale2, gate2 = jnp.split(mod[:, None, :], N_MOD, axis=-1)
        mod_ctx = c_ctx_act @ ada_w[i] + ada_b[i]
        shift1_c, scale1_c = mod_ctx[:D_MODEL], mod_ctx[D_MODEL:2 * D_MODEL]

        hn = rmsnorm(h, norm1_g[i]) * (1.0 + scale1) + shift1
        proj = hn @ w_in[i]
        u_s5 = proj[..., :D_S5]
        z_hy = proj[..., D_S5:]
        cn = rmsnorm(ctx, norm1_g[i]) * (1.0 + scale1_c) + shift1_c
        u_ctx = cn @ w_in[i][:, :D_S5]

        y_s5 = s5_mixer(u_s5, u_ctx, s5_a_re[i], s5_a_im[i], s5_log_step[i],
                        s5_b_re[i], s5_b_im[i], s5_c_re[i], s5_c_im[i],
                        s5_d[i], s5_glu_w[i], s5_glu_b[i])
        y_hy = hyena_mixer(z_hy, hy_conv_w[i], hy_conv_b[i], hy_f_w1[i], hy_f_b1[i],
                           hy_f_w2[i], hy_f_b2[i], hy_f_freq[i], hy_f_w3[i],
                           hy_decay[i], hy_bias[i])
        mix = jnp.concatenate([rmsnorm(y_s5, mix_g_s5[i]), rmsnorm(y_hy, mix_g_hy[i])], axis=-1)
        h = h + gate1 * (mix @ w_out[i])

        hn2 = rmsnorm(h, norm2_g[i]) * (1.0 + scale2) + shift2
        hid = jnp.square(jax.nn.relu(hn2 @ mlp_w1[i]))
        h = h + gate2 * (hid @ mlp_w2[i])
    return rmsnorm(h, final_g)
```

```python
import functools
import math

import numpy as np
import jax
import jax.numpy as jnp
from jax import lax
from jax.experimental import pallas as pl
from jax.experimental.pallas import tpu as pltpu

F32 = jnp.float32
BF16 = jnp.bfloat16
HI = lax.Precision.HIGHEST

EPS = 1e-6
GRID_W = 64
POS_BASE = 10000.0
N_MOD = 6
S5_C = 16
S5_P = 64
S5_T = 16
HY_BANDS = 16
HY_ORDER = 2

FFT_N = 8192
FFT_NA = 128
FFT_NB = 64

LANES = 128
MXU_N = 256
VMEM_LIMIT = 56 * 1024 * 1024


def _cparams(*sem):
    return pltpu.CompilerParams(dimension_semantics=sem, vmem_limit_bytes=VMEM_LIMIT)


def _rms(x, g):
    return x * lax.rsqrt(jnp.mean(x * x, axis=-1, keepdims=True) + EPS) * g


def _mod_kernel(c_ref, w_ref, b_ref, o_ref):
    c = c_ref[...]
    a = c * jax.nn.sigmoid(c)
    o_ref[...] = jnp.dot(a.astype(BF16), w_ref[...].astype(BF16),
                         preferred_element_type=F32) + b_ref[...]


def _modulation(c_rows, ada_w, ada_b):
    rows, d = c_rows.shape
    n = ada_w.shape[1]
    bn = 1024
    return pl.pallas_call(
        _mod_kernel,
        grid=(n // bn,),
        in_specs=[pl.BlockSpec((rows, d), lambda j: (0, 0)),
                  pl.BlockSpec((d, bn), lambda j: (0, j)),
                  pl.BlockSpec((1, bn), lambda j: (0, j))],
        out_specs=pl.BlockSpec((rows, bn), lambda j: (0, j)),
        out_shape=jax.ShapeDtypeStruct((rows, n), F32),
        compiler_params=_cparams("arbitrary"),
        name="ada_mod",
    )(c_rows, ada_w, ada_b.reshape(1, n))


def _proj_kernel(*refs, d, use_pos, n_split):
    if use_pos:
        x_ref, pos_ref, mod_ref, g_ref, w_ref = refs[:5]
        outs = refs[5:]
        h = x_ref[0] + pos_ref[...]
    else:
        x_ref, mod_ref, g_ref, w_ref = refs[:4]
        outs = refs[4:]
        h = x_ref[0]
    m = mod_ref[0]
    hn = _rms(h, g_ref[...]) * (1.0 + m[:, d:2 * d]) + m[:, 0:d]
    proj = jnp.dot(hn.astype(BF16), w_ref[...], preferred_element_type=F32)
    off = 0
    for o_ref, width in zip(outs, n_split):
        o_ref[0] = proj[:, off:off + width].astype(o_ref.dtype)
        off += width


def _project(x, pos, mod3, mod_row_of_batch, g, w_bf16, n_split, tm):
    bsz, n, d = x.shape
    width = sum(n_split)
    use_pos = pos is not None
    in_specs = [pl.BlockSpec((1, tm, d), lambda i, b: (b, i, 0))]
    args = [x]
    if use_pos:
        in_specs.append(pl.BlockSpec((tm, d), lambda i, b: (i, 0)))
        args.append(pos)
    in_specs += [pl.BlockSpec((1, 1, mod3.shape[-1]), lambda i, b: (mod_row_of_batch(b), 0, 0)),
                 pl.BlockSpec((1, d), lambda i, b: (0, 0)),
                 pl.BlockSpec((d, width), lambda i, b: (0, 0))]
    args += [mod3, g.reshape(1, d), w_bf16]
    return pl.pallas_call(
        functools.partial(_proj_kernel, d=d, use_pos=use_pos, n_split=n_split),
        grid=(n // tm, bsz),
        in_specs=in_specs,
        out_specs=[pl.BlockSpec((1, tm, wd), lambda i, b: (b, i, 0)) for wd in n_split],
        out_shape=[jax.ShapeDtypeStruct((bsz, n, wd), BF16) for wd in n_split],
        compiler_params=_cparams("arbitrary", "arbitrary"),
        name="norm_proj",
    )(*args)


def _s5_tables(a_re, a_im, log_step, b_re, b_im, c_re, c_im, d_skip):
    T, C, P = S5_T, S5_C, S5_P
    G = a_re.shape[1]
    step = jnp.exp(log_step.astype(F32))[..., None]
    ar = a_re.astype(F32) * step
    ai = a_im.astype(F32) * step
    tau = jnp.arange(T + 1, dtype=F32)[None, None, :, None]
    mag = jnp.exp(ar[:, :, None, :] * tau)
    pr = mag * jnp.cos(ai[:, :, None, :] * tau)
    pi = mag * jnp.sin(ai[:, :, None, :] * tau)
    lr, li = a_re.astype(F32), a_im.astype(F32)
    nr, ni = pr[:, :, 1, :] - 1.0, pi[:, :, 1, :]
    den = lr * lr + li * li
    qr = (nr * lr + ni * li) / den
    qi = (ni * lr - nr * li) / den
    bbr = qr[..., None] * b_re - qi[..., None] * b_im
    bbi = qr[..., None] * b_im + qi[..., None] * b_re
    cr, ci = c_re.astype(F32), c_im.astype(F32)

    wr = cr[:, :, None, :, :] * pr[:, :, :T, None, :] - ci[:, :, None, :, :] * pi[:, :, :T, None, :]
    wi = cr[:, :, None, :, :] * pi[:, :, :T, None, :] + ci[:, :, None, :, :] * pr[:, :, :T, None, :]
    kern = (jnp.einsum('dgtcp,dgpe->dgtce', wr, bbr, precision=HI)
            - jnp.einsum('dgtcp,dgpe->dgtce', wi, bbi, precision=HI))

    tt = np.arange(T)
    diff = tt[None, :] - tt[:, None]
    fsel = np.clip(diff, 0, T - 1)
    bsel = np.clip(-diff, 0, T - 1)
    kf = jnp.where((diff >= 0)[None, :, :, None, None], kern[0][:, fsel], 0.0)
    kb = jnp.where((diff <= 0)[None, :, :, None, None], kern[1][:, bsel], 0.0)
    dsk = d_skip.astype(F32).reshape(G, C)
    eye_t = jnp.asarray(np.eye(T, dtype=np.float32))
    eye_c = jnp.asarray(np.eye(C, dtype=np.float32))
    skip = eye_t[None, :, :, None, None] * (dsk[:, None, None, :, None] * eye_c[None, None, None])
    m1 = (kf + kb + skip).transpose(0, 1, 4, 2, 3).reshape(G, T * C, T * C)

    pf_r, pf_i = pr[0][:, T - 1 - tt], pi[0][:, T - 1 - tt]
    pb_r, pb_i = pr[1][:, tt], pi[1][:, tt]

    def in_mat(p_r, p_i, br, bi):
        er = p_r[:, :, :, None] * br[:, None] - p_i[:, :, :, None] * bi[:, None]
        ei = p_r[:, :, :, None] * bi[:, None] + p_i[:, :, :, None] * br[:, None]
        er = er.transpose(0, 1, 3, 2).reshape(G, T * C, P)
        ei = ei.transpose(0, 1, 3, 2).reshape(G, T * C, P)
        return er, ei

    efr, efi = in_mat(pf_r, pf_i, bbr[0], bbi[0])
    ebr, ebi = in_mat(pb_r, pb_i, bbr[1], bbi[1])
    m2 = jnp.concatenate([efr, ebr, efi, ebi], axis=-1)

    def out_mat(p_r, p_i, c_r, c_i):
        vr = c_r[:, None] * p_r[:, :, None, :] - c_i[:, None] * p_i[:, :, None, :]
        vi = c_r[:, None] * p_i[:, :, None, :] + c_i[:, None] * p_r[:, :, None, :]
        vr = vr.transpose(0, 3, 1, 2).reshape(G, P, T * C)
        vi = vi.transpose(0, 3, 1, 2).reshape(G, P, T * C)
        return vr, -vi

    vfr, vfi = out_mat(pr[0][:, tt + 1], pi[0][:, tt + 1], cr[0], ci[0])
    vbr, vbi = out_mat(pr[1][:, T - tt], pi[1][:, T - tt], cr[1], ci[1])
    m3 = jnp.concatenate([vfr, vbr, vfi, vbi], axis=1)

    perm = np.zeros((G, T * C), dtype=np.int32)
    for g in range(G):
        for l in range(T * C):
            h, slot, cc = l // 128, (l % 128) // 16, l % 16
            t = 8 * h + (slot - g % 8) % 8
            perm[g, l] = t * C + cc
    perm = jnp.asarray(perm)
    m1 = jnp.take_along_axis(m1, perm[:, :, None], axis=1)
    m1 = jnp.take_along_axis(m1, perm[:, None, :], axis=2)
    m2 = jnp.take_along_axis(m2, perm[:, :, None], axis=1)
    m3 = jnp.take_along_axis(m3, perm[:, None, :], axis=2)

    lam = jnp.stack([jnp.concatenate([pr[0][:, T], pr[1][:, T]], axis=-1),
                     jnp.concatenate([pi[0][:, T], pi[1][:, T]], axis=-1)], axis=1)
    return m1.astype(BF16), m2.astype(BF16), m3.astype(BF16), lam


def _gelu_tanh(x):
    return 0.5 * x * (1.0 + jnp.tanh(math.sqrt(2.0 / math.pi) * (x + 0.044715 * (x * x * x))))


def _s5_kernel(u_ref, uc_ref, m1_ref, m2_ref, m3_ref, lam_ref, o_ref,
               z_ref, e_ref, s1_ref, s2_ref, *, nbatch, nchunk, nctx, pitch):
    groups = LANES // S5_C
    rows = nbatch * nchunk
    lane = lax.broadcasted_iota(jnp.int32, (1, LANES), 1)
    slot_bits = [(((lane // S5_C) >> j) & 1) == 1 for j in range(3)]

    def skew(xs):
        cur = [xs[(-k) % groups] for k in range(groups)]
        for j, bit in enumerate(slot_bits):
            cur = [jnp.where(bit, cur[(i - (1 << j)) % groups], cur[i]) for i in range(groups)]
        return cur

    def to_chunk_rows(read_t, nrows, row0):
        for h in range(2):
            rolled = []
            for t8 in range(groups):
                v = read_t(8 * h + t8).astype(F32)
                rolled.append(pltpu.roll(v, t8 * S5_C, axis=1) if t8 else v)
            for g, zg in enumerate(skew(rolled)):
                z_ref[g, row0:row0 + nrows, h * LANES:(h + 1) * LANES] = zg.astype(BF16)

    for b in range(nbatch):
        to_chunk_rows(lambda t, b=b: u_ref[t, b], nchunk, b * nchunk)
    to_chunk_rows(lambda t: uc_ref[t].reshape(nbatch * nctx, LANES), nbatch * nctx, rows)

    half = lane < (LANES // 2)

    for g in range(groups):
        zg = z_ref[g]
        e = jnp.dot(zg, m2_ref[g], preferred_element_type=F32)
        for b in range(nbatch):
            for c in range(2):
                e_ref[c, b * pitch:b * pitch + nchunk, :] = \
                    e[b * nchunk:(b + 1) * nchunk, c * LANES:(c + 1) * LANES]
                e_ref[c, b * pitch + nchunk:b * pitch + nchunk + nctx, :] = \
                    e[rows + b * nctx:rows + (b + 1) * nctx, c * LANES:(c + 1) * LANES]
        lam_r = lam_ref[g, 0:1, :]
        lam_i = lam_ref[g, 1:2, :]

        def gather(c, rf, rb):
            ef = e_ref[c, pl.ds(rf, nbatch, stride=pitch), :]
            eb = e_ref[c, pl.ds(rb, nbatch, stride=pitch), :]
            return jnp.where(half, ef, eb)

        def advance(sr, si, rf, rb):
            er, ei = gather(0, rf, rb), gather(1, rf, rb)
            return lam_r * sr - lam_i * si + er, lam_r * si + lam_i * sr + ei

        sr = jnp.zeros((nbatch, LANES), F32)
        si = jnp.zeros((nbatch, LANES), F32)
        for i in range(nctx):
            sr, si = advance(sr, si, nchunk + i, nchunk + nctx - 1 - i)

        def body(i, carry):
            sr, si = carry
            kf, kb = i, nchunk - 1 - i
            s1_ref[0, pl.ds(kf, nbatch, stride=pitch), :] = sr
            s1_ref[1, pl.ds(kf, nbatch, stride=pitch), :] = si
            s2_ref[0, pl.ds(kb, nbatch, stride=pitch), :] = sr
            s2_ref[1, pl.ds(kb, nbatch, stride=pitch), :] = si
            return advance(sr, si, kf, kb)

        lax.fori_loop(0, nchunk, body, (sr, si))

        parts = []
        for b in range(nbatch):
            lo = b * pitch
            parts.append(jnp.concatenate(
                [jnp.where(half, s1_ref[c, lo:lo + nchunk, :], s2_ref[c, lo:lo + nchunk, :])
                 for c in range(2)], axis=-1))
        s_in = jnp.concatenate(parts, axis=0).astype(BF16)
        y = (jnp.dot(zg[:rows], m1_ref[g], preferred_element_type=F32)
             + jnp.dot(s_in, m3_ref[g], preferred_element_type=F32))
        z_ref[g, 0:rows, :] = _gelu_tanh(y).astype(BF16)

    for b in range(nbatch):
        r0 = b * nchunk
        for h in range(2):
            ys = [z_ref[g, r0:r0 + nchunk, h * LANES:(h + 1) * LANES].astype(F32)
                  for g in range(groups)]
            for t8, acc in enumerate(skew(ys)):
                if t8:
                    acc = pltpu.roll(acc, LANES - t8 * S5_C, axis=1)
                o_ref[8 * h + t8, b] = acc.astype(BF16)


def _s5_scan(u_t, uc_t, m1, m2, m3, lam):
    T, bsz, nchunk, width = u_t.shape
    nctx = uc_t.shape[2]
    gpb = LANES // S5_C
    pitch = 8 * (((nchunk + nctx) // 8) | 1)
    if pitch < nchunk + nctx:
        pitch += 16
    kern = functools.partial(_s5_kernel, nbatch=bsz, nchunk=nchunk, nctx=nctx, pitch=pitch)
    one = pl.Buffered(1)
    return pl.pallas_call(
        kern,
        grid=(width // LANES,),
        in_specs=[pl.BlockSpec((T, bsz, nchunk, LANES), lambda j: (0, 0, 0, j), pipeline_mode=one),
                  pl.BlockSpec((T, bsz, nctx, LANES), lambda j: (0, 0, 0, j)),
                  pl.BlockSpec((gpb, 256, 256), lambda j: (j, 0, 0)),
                  pl.BlockSpec((gpb, 256, 256), lambda j: (j, 0, 0)),
                  pl.BlockSpec((gpb, 256, 256), lambda j: (j, 0, 0)),
                  pl.BlockSpec((gpb, 2, LANES), lambda j: (j, 0, 0))],
        out_specs=pl.BlockSpec((T, bsz, nchunk, LANES), lambda j: (0, 0, 0, j), pipeline_mode=one),
        out_shape=jax.ShapeDtypeStruct(u_t.shape, BF16),
        scratch_shapes=[pltpu.VMEM((gpb, bsz * (nchunk + nctx), 256), BF16),
                        pltpu.VMEM((2, bsz * pitch, LANES), F32),
                        pltpu.VMEM((2, bsz * pitch, LANES), F32),
                        pltpu.VMEM((2, bsz * pitch, LANES), F32)],
        compiler_params=_cparams("arbitrary"),
        name="s5_scan",
    )(u_t, uc_t, m1, m2, m3, lam)


def _dft_tables(n_seq):
    na_sig = n_seq // FFT_NB
    ka = np.arange(FFT_NA)[:, None]
    nb = np.arange(FFT_NB)[:, None, None]

    def stage1(n_in):
        na = np.arange(n_in)[None, :]
        ang = -2.0 * np.pi * (na * ka / FFT_NA)[None] - 2.0 * np.pi * (nb * ka[None]) / FFT_N
        return np.cos(ang), np.sin(ang)

    c, s = stage1(na_sig)
    f1 = np.concatenate([np.concatenate([c, -s], axis=2), np.concatenate([s, c], axis=2)], axis=1)
    c, s = stage1(FFT_NA)
    f1_real = np.concatenate([c, s], axis=1)
    kb = np.arange(FFT_NB)[:, None]
    nbv = np.arange(FFT_NB)[None, :]
    ang = -2.0 * np.pi * kb * nbv / FFT_NB
    c, s = np.cos(ang), np.sin(ang)
    f2 = np.block([[c, -s], [s, c]])
    f2_inv = np.block([[c, s], [-s, c]]) / FFT_NB
    nap = np.arange(na_sig)[:, None]
    kav = np.arange(FFT_NA)[None, :]
    ang = 2.0 * np.pi * (nap * kav / FFT_NA)[None] + 2.0 * np.pi * (nb * kav[None]) / FFT_N
    c, s = np.cos(ang) / FFT_NA, np.sin(ang) / FFT_NA
    f3 = np.concatenate([np.concatenate([c, -s], axis=2), np.concatenate([s, c], axis=2)], axis=1)
    as_bf16 = lambda a: jnp.asarray(a.astype(np.float32)).astype(BF16)
    return as_bf16(f1), as_bf16(f1_real), as_bf16(f2), as_bf16(f2_inv), as_bf16(f3)


def _conv3_rows(z_ref, plane, nb, w, b, nbk):
    def rows(i):
        return z_ref[plane, 0, i].astype(F32)

    cur = rows(nb)
    na = cur.shape[0]
    ridx = lax.broadcasted_iota(jnp.int32, cur.shape, 0)
    if isinstance(nb, int) and nb == 0:
        prev = jnp.where(ridx == 0, 0.0, pltpu.roll(rows(nbk - 1), 1, axis=0))
    else:
        prev = rows(nb - 1)
    if isinstance(nb, int) and nb == nbk - 1:
        nxt = jnp.where(ridx == na - 1, 0.0, pltpu.roll(rows(0), na - 1, axis=0))
    else:
        nxt = rows(nb + 1)
    return prev * w[0:1] + cur * w[1:2] + nxt * w[2:3] + b


def _for_row_blocks(nbk, fn):
    fn(0)
    lax.fori_loop(1, nbk - 1, lambda i, c: (fn(i), c)[1], 0)
    fn(nbk - 1)


def _hy_stage1_kernel(*refs, conv, nbk):
    if conv:
        z_ref, w_ref, b_ref, f_ref, a_ref = refs
        w, bias = w_ref[...], b_ref[...]
    else:
        z_ref, f_ref, a_ref = refs

    def step(nb):
        if conv:
            parts = [_conv3_rows(z_ref, pln, nb, w, bias, nbk) for pln in range(2)]
        else:
            parts = [z_ref[pln, 0, nb].astype(F32) for pln in range(2)]
        rhs = jnp.concatenate(parts, axis=0).astype(BF16)
        a_ref[0, nb] = jnp.dot(f_ref[nb], rhs, preferred_element_type=F32).astype(BF16)

    _for_row_blocks(nbk, step)


def _hy_stage1(z5, col0, f1, conv_w=None, conv_b=None, cb=MXU_N):
    _, npair, nbk, na, _ = z5.shape
    c = 2 * cb
    conv = conv_w is not None
    cblk0 = col0 // cb
    in_specs = [pl.BlockSpec((2, 1, nbk, na, cb), lambda p, j: (0, p, 0, 0, cblk0 + j))]
    args = [z5]
    if conv:
        in_specs += [pl.BlockSpec((3, cb), lambda p, j: (0, cblk0 + j)),
                     pl.BlockSpec((1, cb), lambda p, j: (0, cblk0 + j))]
        args += [conv_w, conv_b]
    in_specs.append(pl.BlockSpec(f1.shape, lambda p, j: (0, 0, 0)))
    args.append(f1)
    return pl.pallas_call(
        functools.partial(_hy_stage1_kernel, conv=conv, nbk=nbk),
        grid=(npair, c // cb),
        in_specs=in_specs,
        out_specs=pl.BlockSpec((1, nbk, 2 * FFT_NA, cb), lambda p, j: (p, 0, 0, j)),
        out_shape=jax.ShapeDtypeStruct((npair, nbk, 2 * FFT_NA, c), BF16),
        compiler_params=_cparams("arbitrary", "arbitrary"),
        name="hy_dft_stage1",
    )(*args)


def _filt_stage1_kernel(h_ref, f_ref, a_ref, *, nbk):
    def step(nb, carry):
        a_ref[0, nb] = jnp.dot(f_ref[nb], h_ref[0, nb].astype(BF16),
                               preferred_element_type=F32).astype(BF16)
        return carry
    lax.fori_loop(0, nbk, step, 0)


def _filt_stage1(filt, f1_real, cb=MXU_N):
    no, nbk, na, c = filt.shape
    return pl.pallas_call(
        functools.partial(_filt_stage1_kernel, nbk=nbk),
        grid=(no, c // cb),
        in_specs=[pl.BlockSpec((1, nbk, na, cb), lambda o, j: (o, 0, 0, j)),
                  pl.BlockSpec(f1_real.shape, lambda o, j: (0, 0, 0))],
        out_specs=pl.BlockSpec((1, nbk, 2 * FFT_NA, cb), lambda o, j: (o, 0, 0, j)),
        out_shape=jax.ShapeDtypeStruct((no, nbk, 2 * FFT_NA, c), BF16),
        compiler_params=_cparams("arbitrary", "arbitrary"),
        name="hy_filter_stage1",
    )(filt, f1_real)


def _filt_stage2_kernel(a_ref, f2_ref, k_ref, *, kag):
    for i in range(kag):
        k_ref[0, i] = jnp.dot(f2_ref[...], a_ref[0, i], preferred_element_type=F32)


def _filt_stage2(a_t, f2, kag=16):
    no, nka, r, c = a_t.shape
    return pl.pallas_call(
        functools.partial(_filt_stage2_kernel, kag=kag),
        grid=(no, nka // kag),
        in_specs=[pl.BlockSpec((1, kag, r, c), lambda o, i: (o, i, 0, 0)),
                  pl.BlockSpec(f2.shape, lambda o, i: (0, 0))],
        out_specs=pl.BlockSpec((1, kag, r, c), lambda o, i: (o, i, 0, 0)),
        out_shape=jax.ShapeDtypeStruct(a_t.shape, F32),
        compiler_params=_cparams("arbitrary", "arbitrary"),
        name="hy_filter_stage2",
    )(a_t, f2)


def _hy_stage2_kernel(a_ref, k_ref, f2_ref, f2i_ref, o_ref, *, kag):
    hb = FFT_NB
    for i in range(kag):
        s = jnp.dot(f2_ref[...], a_ref[0, i], preferred_element_type=F32)
        kf = k_ref[0, i]
        sr, si = s[:hb], s[hb:]
        kr, ki = kf[:hb], kf[hb:]
        prod = jnp.concatenate([sr * kr - si * ki, sr * ki + si * kr], axis=0).astype(BF16)
        o_ref[0, i] = jnp.dot(f2i_ref[...], prod, preferred_element_type=F32).astype(BF16)


def _hy_stage2(a_t, kf_o, f2, f2i, kag=16):
    npair, nka, r, c = a_t.shape
    return pl.pallas_call(
        functools.partial(_hy_stage2_kernel, kag=kag),
        grid=(nka // kag, npair),
        in_specs=[pl.BlockSpec((1, kag, r, c), lambda i, p: (p, i, 0, 0)),
                  pl.BlockSpec((1, kag, r, c), lambda i, p: (0, i, 0, 0)),
                  pl.BlockSpec(f2.shape, lambda i, p: (0, 0)),
                  pl.BlockSpec(f2i.shape, lambda i, p: (0, 0))],
        out_specs=pl.BlockSpec((1, kag, r, c), lambda i, p: (p, i, 0, 0)),
        out_shape=jax.ShapeDtypeStruct(a_t.shape, BF16),
        compiler_params=_cparams("arbitrary", "arbitrary"),
        name="hy_dft_stage2",
    )(a_t, kf_o, f2, f2i)


def _hy_stage3_kernel(*refs, conv_v, nbk, na):
    if conv_v:
        a_ref, zg_ref, zv_ref, wg_ref, bg_ref, wv_ref, bv_ref, bias_ref, f_ref, o_ref = refs
        wv, bv = wv_ref[...], bv_ref[...]
    else:
        a_ref, zg_ref, zv_ref, wg_ref, bg_ref, bias_ref, f_ref, o_ref = refs
    wg, bg = wg_ref[...], bg_ref[...]
    bias = bias_ref[...]

    def step(nb):
        y = jnp.dot(f_ref[nb], a_ref[0, nb], preferred_element_type=F32)
        for pln in range(2):
            gate = _conv3_rows(zg_ref, pln, nb, wg, bg, nbk)
            if conv_v:
                v = _conv3_rows(zv_ref, pln, nb, wv, bv, nbk)
            else:
                v = zv_ref[pln, 0, nb].astype(F32)
            o_ref[pln, 0, nb] = (gate * (y[pln * na:(pln + 1) * na] + bias * v)).astype(o_ref.dtype)

    _for_row_blocks(nbk, step)


def _hy_stage3(a, z5, gate_col0, v_src, v_col0, conv_w, conv_b, bias, f3, conv_v, cb=MXU_N):
    npair, nbk, r, c = a.shape
    na = z5.shape[3]
    gblk, vblk = gate_col0 // cb, v_col0 // cb
    zspec = lambda blk0: pl.BlockSpec((2, 1, nbk, na, cb), lambda p, j: (0, p, 0, 0, blk0 + j))
    wspec = lambda blk0: pl.BlockSpec((3, cb), lambda p, j: (0, blk0 + j))
    bspec = lambda blk0: pl.BlockSpec((1, cb), lambda p, j: (0, blk0 + j))
    in_specs = [pl.BlockSpec((1, nbk, r, cb), lambda p, j: (p, 0, 0, j)), zspec(gblk), zspec(vblk),
                wspec(gblk), bspec(gblk)]
    args = [a, z5, v_src, conv_w, conv_b]
    if conv_v:
        in_specs += [wspec(vblk), bspec(vblk)]
        args += [conv_w, conv_b]
    in_specs += [bspec(0), pl.BlockSpec(f3.shape, lambda p, j: (0, 0, 0))]
    args += [bias, f3]
    return pl.pallas_call(
        functools.partial(_hy_stage3_kernel, conv_v=conv_v, nbk=nbk, na=na),
        grid=(npair, c // cb),
        in_specs=in_specs,
        out_specs=pl.BlockSpec((2, 1, nbk, na, cb), lambda p, j: (0, p, 0, 0, j)),
        out_shape=jax.ShapeDtypeStruct((2, npair, nbk, na, c), BF16),
        compiler_params=_cparams("arbitrary", "arbitrary"),
        name="hy_dft_stage3",
    )(*args)


def _filter_mlp_kernel(emb_ref, w1_ref, b1_ref, w2_ref, b2_ref, fr_ref, w3_ref, dec_ref, o_ref):
    emb = emb_ref[...]
    fr = fr_ref[...]
    h = jnp.sin(fr * (jnp.dot(emb, w1_ref[...], precision=HI, preferred_element_type=F32) + b1_ref[...]))
    h = jnp.sin(fr * (jnp.dot(h, w2_ref[...], precision=HI, preferred_element_type=F32) + b2_ref[...]))
    h = jnp.dot(h, w3_ref[...], precision=HI, preferred_element_type=F32)
    o_ref[...] = h * jnp.exp(-emb[:, 0:1] * dec_ref[...])


def _hyena_filters(n, w1, b1, w2, b2, freq, w3, decay, tm=512):
    hid = w2.shape[0]
    c = decay.shape[-1]
    t = np.linspace(0.0, 1.0, n, dtype=np.float32)[:, None]
    w = (2.0 * np.pi * np.arange(n, dtype=np.float32) / n).astype(np.float32)
    bands = np.linspace(1e-4, HY_BANDS - 1, HY_BANDS, dtype=np.float32)
    ang = w[:, None] * bands[None, :]
    emb = np.concatenate([t, np.cos(ang), -np.sin(ang)], axis=-1).astype(np.float32)
    kpad = 64
    emb = jnp.asarray(np.pad(emb, ((0, 0), (0, kpad - emb.shape[1]))))
    w1p = jnp.pad(w1.astype(F32), ((0, kpad - w1.shape[0]), (0, 0)))
    dec = jnp.broadcast_to(jnp.abs(decay.astype(F32))[:, None, :], (HY_ORDER, 2, c)).reshape(1, -1)
    nout = w3.shape[1]
    full = lambda shape: pl.BlockSpec(shape, lambda i: tuple(0 for _ in shape))
    return pl.pallas_call(
        _filter_mlp_kernel,
        grid=(n // tm,),
        in_specs=[pl.BlockSpec((tm, kpad), lambda i: (i, 0)), full((kpad, hid)), full((1, hid)),
                  full((hid, hid)), full((1, hid)), full((1, hid)), full((hid, nout)), full((1, nout))],
        out_specs=pl.BlockSpec((tm, nout), lambda i: (i, 0)),
        out_shape=jax.ShapeDtypeStruct((n, nout), F32),
        compiler_params=_cparams("arbitrary"),
        name="hy_filter_mlp",
    )(emb, w1p, b1.reshape(1, hid).astype(F32), w2.astype(F32), b2.reshape(1, hid).astype(F32),
      freq.reshape(1, hid).astype(F32), w3.astype(F32), dec)


def _swap_stage_axes(a, inner):
    p, x, _, c = a.shape
    return a.reshape(p, x, 2, inner, c).transpose(0, 3, 2, 1, 4).reshape(p, inner, 2 * x, c)


def _hyena(z5, conv_w, conv_b, filt_h, bias, tables):
    f1, f1_real, f2, f2i, f3 = tables
    c = bias.shape[-1]
    n = filt_h.shape[0]
    h = filt_h.reshape(n, HY_ORDER, 2, c)
    zero = jnp.zeros((1, HY_ORDER, c), F32)
    filt = jnp.concatenate([h[:, :, 0], zero, h[:0:-1, :, 1]], axis=0)
    filt = filt.reshape(FFT_NA, FFT_NB, HY_ORDER, c).transpose(2, 1, 0, 3)
    kf = _filt_stage2(_swap_stage_axes(_filt_stage1(filt, f1_real), FFT_NA), f2)

    cw = conv_w.astype(F32)
    cbias = conv_b.astype(F32).reshape(1, -1)
    bias = bias.astype(F32)

    def long_conv(src, col0, conv, order):
        a = _hy_stage1(src, col0, f1, cw if conv else None, cbias if conv else None)
        a = _hy_stage2(_swap_stage_axes(a, FFT_NA), kf[order:order + 1], f2, f2i)
        return _swap_stage_axes(a, FFT_NB)

    y = long_conv(z5, 0, True, 0)
    z1 = _hy_stage3(y, z5, c, z5, 0, cw, cbias, bias[0:1], f3, conv_v=True)
    y = long_conv(z1, 0, False, 1)
    return _hy_stage3(y, z5, 2 * c, z1, 0, cw, cbias, bias[1:2], f3, conv_v=False)


def _out_kernel(x_ref, pos_ref, mod_ref, ys_ref, yh_ref, gw_ref, gb_ref, g5_ref, gh_ref, wo_ref,
                g2_ref, w1_ref, w2_ref, gf_ref, o_ref, *, d):
    m = mod_ref[0]
    gate1, shift2 = m[:, 2 * d:3 * d], m[:, 3 * d:4 * d]
    scale2, gate2 = m[:, 4 * d:5 * d], m[:, 5 * d:6 * d]
    h = x_ref[0] + pos_ref[...]
    ab = jnp.dot(ys_ref[0], gw_ref[...], preferred_element_type=F32) + gb_ref[...]
    half = ab.shape[-1] // 2
    y5 = ab[:, :half] * jax.nn.sigmoid(ab[:, half:])
    mix = jnp.concatenate([_rms(y5, g5_ref[...]), _rms(yh_ref[0].astype(F32), gh_ref[...])], axis=-1)
    h = h + gate1 * jnp.dot(mix.astype(BF16), wo_ref[...], preferred_element_type=F32)
    hn = _rms(h, g2_ref[...]) * (1.0 + scale2) + shift2
    hid = jnp.dot(hn.astype(BF16), w1_ref[...], preferred_element_type=F32)
    hid = jnp.square(jnp.maximum(hid, 0.0))
    h = h + gate2 * jnp.dot(hid.astype(BF16), w2_ref[...], preferred_element_type=F32)
    o_ref[0] = _rms(h, gf_ref[...])


def _output_stage(x, pos, mod3, ys, yh, glu_w, glu_b, g5, gh, w_out, g2, w1, w2, gf, tm=512):
    bsz, n, d = x.shape
    dh = ys.shape[-1]
    one = pl.Buffered(1)
    const = lambda a: pl.BlockSpec(a.shape, lambda i, b: tuple(0 for _ in a.shape), pipeline_mode=one)
    row = lambda a: a.reshape(1, -1).astype(F32)
    glu_b, g5, gh, g2, gf = row(glu_b), row(g5), row(gh), row(g2), row(gf)
    return pl.pallas_call(
        functools.partial(_out_kernel, d=d),
        grid=(n // tm, bsz),
        in_specs=[pl.BlockSpec((1, tm, d), lambda i, b: (b, i, 0)),
                  pl.BlockSpec((tm, d), lambda i, b: (i, 0)),
                  pl.BlockSpec((1, 1, mod3.shape[-1]), lambda i, b: (b, 0, 0)),
                  pl.BlockSpec((1, tm, dh), lambda i, b: (b, i, 0)),
                  pl.BlockSpec((1, tm, dh), lambda i, b: (b, i, 0)),
                  const(glu_w), const(glu_b), const(g5), const(gh), const(w_out), const(g2),
                  const(w1), const(w2), const(gf)],
        out_specs=pl.BlockSpec((1, tm, d), lambda i, b: (b, i, 0)),
        out_shape=jax.ShapeDtypeStruct((bsz, n, d), F32),
        compiler_params=_cparams("arbitrary", "arbitrary"),
        name="mix_mlp_out",
    )(x, pos, mod3, ys, yh, glu_w, glu_b, g5, gh, w_out, g2, w1, w2, gf)


def _pos_table(n, d):
    rows = n // GRID_W
    row = np.repeat(np.arange(rows, dtype=np.float32), GRID_W)
    col = np.tile(np.arange(GRID_W, dtype=np.float32), rows)
    quarter = d // 4
    omega = (1.0 / (POS_BASE ** (np.arange(quarter, dtype=np.float32) / quarter))).astype(np.float32)

    def enc(p):
        ang = p[:, None] * omega[None, :]
        return np.concatenate([np.sin(ang), np.cos(ang)], axis=-1)

    return jnp.asarray(np.concatenate([enc(row), enc(col)], axis=-1).astype(np.float32))


def kernel(x, c, ctx, c_ctx, ada_w, ada_b, norm1_g, w_in, s5_a_re, s5_a_im, s5_log_step, s5_b_re,
           s5_b_im, s5_c_re, s5_c_im, s5_d, s5_glu_w, s5_glu_b, hy_conv_w, hy_conv_b, hy_f_w1,
           hy_f_b1, hy_f_w2, hy_f_b2, hy_f_freq, hy_f_w3, hy_decay, hy_bias, mix_g_s5, mix_g_hy,
           w_out, norm2_g, mlp_w1, mlp_w2, final_g):
    bsz, n, d = x.shape
    depth = ada_w.shape[0]
    d_s5 = s5_d.shape[-1]
    d_hy = hy_bias.shape[-1]
    nctx = ctx.shape[1]
    npair = bsz // 2
    nbk, na = FFT_NB, n // FFT_NB
    pos = _pos_table(n, d)
    tables = _dft_tables(n)

    assert depth == 1 and bsz % 2 == 0 and n % (FFT_NB * 8) == 0 and 2 * n == FFT_N
    mod_rows = 16
    c_rows = jnp.concatenate([c, c_ctx[None], jnp.zeros((mod_rows - bsz - 1, d), c.dtype)], axis=0)
    mod3 = _modulation(c_rows, ada_w[0], ada_b[0]).reshape(mod_rows, 1, N_MOD * d)

    w_in_b = w_in[0].astype(BF16)
    u, z = _project(x, pos, mod3, lambda b: b, norm1_g[0], w_in_b, (d_s5, 3 * d_hy), tm=512)
    (uc,) = _project(ctx, None, mod3, lambda b: bsz, norm1_g[0], w_in_b[:, :d_s5], (d_s5,), tm=nctx)

    m1, m2, m3, lam = _s5_tables(s5_a_re[0], s5_a_im[0], s5_log_step[0], s5_b_re[0], s5_b_im[0],
                                 s5_c_re[0], s5_c_im[0], s5_d[0])
    u_t = u.reshape(bsz, n // S5_T, S5_T, d_s5).transpose(2, 0, 1, 3)
    uc_t = uc.reshape(bsz, nctx // S5_T, S5_T, d_s5).transpose(2, 0, 1, 3)
    ys = _s5_scan(u_t, uc_t, m1, m2, m3, lam).transpose(1, 2, 0, 3).reshape(bsz, n, d_s5)

    filt_h = _hyena_filters(n, hy_f_w1[0], hy_f_b1[0], hy_f_w2[0], hy_f_b2[0], hy_f_freq[0],
                            hy_f_w3[0], hy_decay[0])
    z5 = z.reshape(bsz, na, nbk, 3 * d_hy).transpose(0, 2, 1, 3).reshape(2, npair, nbk, na, 3 * d_hy)
    yh = _hyena(z5, hy_conv_w[0], hy_conv_b[0], filt_h, hy_bias[0], tables)
    yh = yh.reshape(bsz, nbk, na, d_hy).transpose(0, 2, 1, 3).reshape(bsz, n, d_hy)

    return _output_stage(x, pos, mod3, ys, yh, s5_glu_w[0].astype(BF16), s5_glu_b[0], mix_g_s5[0],
                         mix_g_hy[0], w_out[0].astype(BF16), norm2_g[0], mlp_w1[0].astype(BF16),
                         mlp_w2[0].astype(BF16), final_g)
```

```python
import functools
import math

import numpy as np
import jax
import jax.numpy as jnp
from jax import lax
from jax.experimental import pallas as pl
from jax.experimental.pallas import tpu as pltpu

F32 = jnp.float32
BF16 = jnp.bfloat16
HI = lax.Precision.HIGHEST

EPS = 1e-6
GRID_W = 64
POS_BASE = 10000.0
N_MOD = 6
S5_C = 16
S5_P = 64
S5_T = 16
HY_BANDS = 16
HY_ORDER = 2

FFT_N = 8192
FFT_NA = 128
FFT_NB = 64

LANES = 128
MXU_N = 256
VMEM_LIMIT = 56 * 1024 * 1024


def _cparams(*sem):
    return pltpu.CompilerParams(dimension_semantics=sem, vmem_limit_bytes=VMEM_LIMIT)


def _rms(x, g):
    return x * lax.rsqrt(jnp.mean(x * x, axis=-1, keepdims=True) + EPS) * g


def _mod_kernel(c_ref, w_ref, b_ref, o_ref):
    c = c_ref[...]
    a = c * jax.nn.sigmoid(c)
    o_ref[...] = jnp.dot(a.astype(BF16), w_ref[...].astype(BF16),
                         preferred_element_type=F32) + b_ref[...]


def _modulation(c_rows, ada_w, ada_b):
    rows, d = c_rows.shape
    n = ada_w.shape[1]
    bn = 1024
    return pl.pallas_call(
        _mod_kernel,
        grid=(n // bn,),
        in_specs=[pl.BlockSpec((rows, d), lambda j: (0, 0)),
                  pl.BlockSpec((d, bn), lambda j: (0, j)),
                  pl.BlockSpec((1, bn), lambda j: (0, j))],
        out_specs=pl.BlockSpec((rows, bn), lambda j: (0, j)),
        out_shape=jax.ShapeDtypeStruct((rows, n), F32),
        compiler_params=_cparams("arbitrary"),
        name="ada_mod",
    )(c_rows, ada_w, ada_b.reshape(1, n))


def _proj_kernel(*refs, d, use_pos, n_split):
    if use_pos:
        x_ref, pos_ref, mod_ref, g_ref, w_ref = refs[:5]
        outs = refs[5:]
        h = x_ref[0] + pos_ref[...]
    else:
        x_ref, mod_ref, g_ref, w_ref = refs[:4]
        outs = refs[4:]
        h = x_ref[0]
    m = mod_ref[0]
    hn = _rms(h, g_ref[...]) * (1.0 + m[:, d:2 * d]) + m[:, 0:d]
    proj = jnp.dot(hn.astype(BF16), w_ref[...], preferred_element_type=F32)
    off = 0
    for o_ref, width in zip(outs, n_split):
        o_ref[0] = proj[:, off:off + width].astype(o_ref.dtype)
        off += width


def _project(x, pos, mod3, mod_row_of_batch, g, w_bf16, n_split, tm):
    bsz, n, d = x.shape
    width = sum(n_split)
    use_pos = pos is not None
    in_specs = [pl.BlockSpec((1, tm, d), lambda i, b: (b, i, 0))]
    args = [x]
    if use_pos:
        in_specs.append(pl.BlockSpec((tm, d), lambda i, b: (i, 0)))
        args.append(pos)
    in_specs += [pl.BlockSpec((1, 1, mod3.shape[-1]), lambda i, b: (mod_row_of_batch(b), 0, 0)),
                 pl.BlockSpec((1, d), lambda i, b: (0, 0)),
                 pl.BlockSpec((d, width), lambda i, b: (0, 0))]
    args += [mod3, g.reshape(1, d), w_bf16]
    return pl.pallas_call(
        functools.partial(_proj_kernel, d=d, use_pos=use_pos, n_split=n_split),
        grid=(n // tm, bsz),
        in_specs=in_specs,
        out_specs=[pl.BlockSpec((1, tm, wd), lambda i, b: (b, i, 0)) for wd in n_split],
        out_shape=[jax.ShapeDtypeStruct((bsz, n, wd), BF16) for wd in n_split],
        compiler_params=_cparams("arbitrary", "arbitrary"),
        name="norm_proj",
    )(*args)


def _s5_tables(a_re, a_im, log_step, b_re, b_im, c_re, c_im, d_skip):
    T, C, P = S5_T, S5_C, S5_P
    G = a_re.shape[1]
    gpb = LANES // C
    slots = np.arange(T)
    tok = np.stack([8 * (slots // 8) + (slots % 8 - o) % 8 for o in range(gpb)])
    tok_g = tok[np.arange(G) % gpb]

    step = jnp.exp(log_step.astype(F32))[..., None]
    ar = a_re.astype(F32) * step
    ai = a_im.astype(F32) * step

    def powers(d, expo):
        e = jnp.asarray(np.broadcast_to(expo, (G, expo.shape[-1])).astype(np.float32))[:, :, None]
        mag = jnp.exp(ar[d][:, None, :] * e)
        return mag * jnp.cos(ai[d][:, None, :] * e), mag * jnp.sin(ai[d][:, None, :] * e)

    lr, li = a_re.astype(F32), a_im.astype(F32)
    first = [powers(d, np.ones(1)) for d in range(2)]
    nr = jnp.stack([first[d][0][:, 0] for d in range(2)]) - 1.0
    ni = jnp.stack([first[d][1][:, 0] for d in range(2)])
    den = lr * lr + li * li
    qr = (nr * lr + ni * li) / den
    qi = (ni * lr - nr * li) / den
    bbr = qr[..., None] * b_re - qi[..., None] * b_im
    bbi = qr[..., None] * b_im + qi[..., None] * b_re
    cr, ci = c_re.astype(F32), c_im.astype(F32)

    kerns = []
    for d in range(2):
        pr, pi = powers(d, np.arange(T))
        wr = cr[d][:, None] * pr[:, :, None, :] - ci[d][:, None] * pi[:, :, None, :]
        wi = cr[d][:, None] * pi[:, :, None, :] + ci[d][:, None] * pr[:, :, None, :]
        kerns.append(jnp.einsum('gtcp,gpe->gtce', wr, bbr[d], precision=HI)
                     - jnp.einsum('gtcp,gpe->gtce', wi, bbi[d], precision=HI))
    kcat = jnp.concatenate(kerns, axis=1).reshape(G // gpb, gpb, 2 * T, C, C)

    lag = tok[:, None, :] - tok[:, :, None]
    lags = np.arange(T)[None, :, None, None]
    sel = np.concatenate([lag[:, None] == lags, -lag[:, None] == lags], axis=1).astype(np.float32)
    m1 = jnp.einsum('oxab,joxce->joaebc', jnp.asarray(sel), kcat, precision=HI)
    eye = np.eye(T, dtype=np.float32)
    skip = (jnp.asarray(eye[None, :, None, :, None] * eye[None, None, :, None, :])
            * d_skip.astype(F32).reshape(G, 1, 1, 1, C))
    m1 = (m1.reshape(G, T, C, T, C) + skip).reshape(G, T * C, T * C)

    def in_mat(d, expo):
        p_r, p_i = powers(d, expo)
        br, bi = bbr[d].transpose(0, 2, 1)[:, None], bbi[d].transpose(0, 2, 1)[:, None]
        er = p_r[:, :, None, :] * br - p_i[:, :, None, :] * bi
        ei = p_r[:, :, None, :] * bi + p_i[:, :, None, :] * br
        return er.reshape(G, T * C, P), ei.reshape(G, T * C, P)

    efr, efi = in_mat(0, T - 1 - tok_g)
    ebr, ebi = in_mat(1, tok_g)
    m2 = jnp.concatenate([efr, ebr, efi, ebi], axis=-1)

    def out_mat(d, expo):
        p_r, p_i = powers(d, expo)
        crt = cr[d].transpose(0, 2, 1)[:, :, None, :]
        cit = ci[d].transpose(0, 2, 1)[:, :, None, :]
        prt = p_r.transpose(0, 2, 1)[:, :, :, None]
        pit = p_i.transpose(0, 2, 1)[:, :, :, None]
        vr = crt * prt - cit * pit
        vi = crt * pit + cit * prt
        return vr.reshape(G, P, T * C), -vi.reshape(G, P, T * C)

    vfr, vfi = out_mat(0, tok_g + 1)
    vbr, vbi = out_mat(1, T - tok_g)
    m3 = jnp.concatenate([vfr, vbr, vfi, vbi], axis=1)

    last = [powers(d, np.full(1, T)) for d in range(2)]
    lam = jnp.stack([jnp.concatenate([last[0][c][:, 0], last[1][c][:, 0]], axis=-1)
                     for c in range(2)], axis=1)
    return m1.astype(BF16), m2.astype(BF16), m3.astype(BF16), lam


def _gelu_tanh(x):
    return 0.5 * x * (1.0 + jnp.tanh(math.sqrt(2.0 / math.pi) * (x + 0.044715 * (x * x * x))))


def _s5_kernel(u_ref, uc_ref, m1_ref, m2_ref, m3_ref, lam_ref, o_ref,
               z_ref, e_ref, s1_ref, s2_ref, *, nbatch, nchunk, nctx, pitch):
    groups = LANES // S5_C
    rows = nbatch * nchunk
    lane = lax.broadcasted_iota(jnp.int32, (1, LANES), 1)
    slot_bits = [(((lane // S5_C) >> j) & 1) == 1 for j in range(3)]

    def skew(xs):
        cur = [xs[(-k) % groups] for k in range(groups)]
        for j, bit in enumerate(slot_bits):
            cur = [jnp.where(bit, cur[(i - (1 << j)) % groups], cur[i]) for i in range(groups)]
        return cur

    def to_chunk_rows(read_t, nrows, row0):
        for h in range(2):
            rolled = []
            for t8 in range(groups):
                v = read_t(8 * h + t8).astype(F32)
                rolled.append(pltpu.roll(v, t8 * S5_C, axis=1) if t8 else v)
            for g, zg in enumerate(skew(rolled)):
                z_ref[g, row0:row0 + nrows, h * LANES:(h + 1) * LANES] = zg.astype(BF16)

    for b in range(nbatch):
        to_chunk_rows(lambda t, b=b: u_ref[t, b], nchunk, b * nchunk)
    to_chunk_rows(lambda t: uc_ref[t].reshape(nbatch * nctx, LANES), nbatch * nctx, rows)

    half = lane < (LANES // 2)

    for g in range(groups):
        zg = z_ref[g]
        e = jnp.dot(zg, m2_ref[g], preferred_element_type=F32)
        for b in range(nbatch):
            for c in range(2):
                e_ref[c, b * pitch:b * pitch + nchunk, :] = \
                    e[b * nchunk:(b + 1) * nchunk, c * LANES:(c + 1) * LANES]
                e_ref[c, b * pitch + nchunk:b * pitch + nchunk + nctx, :] = \
                    e[rows + b * nctx:rows + (b + 1) * nctx, c * LANES:(c + 1) * LANES]
        lam_r = lam_ref[g, 0:1, :]
        lam_i = lam_ref[g, 1:2, :]

        def gather(c, rf, rb):
            ef = e_ref[c, pl.ds(rf, nbatch, stride=pitch), :]
            eb = e_ref[c, pl.ds(rb, nbatch, stride=pitch), :]
            return jnp.where(half, ef, eb)

        def advance(sr, si, rf, rb):
            er, ei = gather(0, rf, rb), gather(1, rf, rb)
            return lam_r * sr - lam_i * si + er, lam_r * si + lam_i * sr + ei

        sr = jnp.zeros((nbatch, LANES), F32)
        si = jnp.zeros((nbatch, LANES), F32)
        for i in range(nctx):
            sr, si = advance(sr, si, nchunk + i, nchunk + nctx - 1 - i)

        def body(i, carry):
            sr, si = carry
            kf, kb = i, nchunk - 1 - i
            s1_ref[0, pl.ds(kf, nbatch, stride=pitch), :] = sr
            s1_ref[1, pl.ds(kf, nbatch, stride=pitch), :] = si
            s2_ref[0, pl.ds(kb, nbatch, stride=pitch), :] = sr
            s2_ref[1, pl.ds(kb, nbatch, stride=pitch), :] = si
            return advance(sr, si, kf, kb)

        lax.fori_loop(0, nchunk, body, (sr, si))

        parts = []
        for b in range(nbatch):
            lo = b * pitch
            parts.append(jnp.concatenate(
                [jnp.where(half, s1_ref[c, lo:lo + nchunk, :], s2_ref[c, lo:lo + nchunk, :])
                 for c in range(2)], axis=-1))
        s_in = jnp.concatenate(parts, axis=0).astype(BF16)
        y = (jnp.dot(zg[:rows], m1_ref[g], preferred_element_type=F32)
             + jnp.dot(s_in, m3_ref[g], preferred_element_type=F32))
        z_ref[g, 0:rows, :] = _gelu_tanh(y).astype(BF16)

    for b in range(nbatch):
        r0 = b * nchunk
        for h in range(2):
            ys = [z_ref[g, r0:r0 + nchunk, h * LANES:(h + 1) * LANES].astype(F32)
                  for g in range(groups)]
            for t8, acc in enumerate(skew(ys)):
                if t8:
                    acc = pltpu.roll(acc, LANES - t8 * S5_C, axis=1)
                o_ref[8 * h + t8, b] = acc.astype(BF16)


def _s5_scan(u_t, uc_t, m1, m2, m3, lam):
    T, bsz, nchunk, width = u_t.shape
    nctx = uc_t.shape[2]
    gpb = LANES // S5_C
    pitch = 8 * (((nchunk + nctx) // 8) | 1)
    if pitch < nchunk + nctx:
        pitch += 16
    kern = functools.partial(_s5_kernel, nbatch=bsz, nchunk=nchunk, nctx=nctx, pitch=pitch)
    one = pl.Buffered(1)
    return pl.pallas_call(
        kern,
        grid=(width // LANES,),
        in_specs=[pl.BlockSpec((T, bsz, nchunk, LANES), lambda j: (0, 0, 0, j), pipeline_mode=one),
                  pl.BlockSpec((T, bsz, nctx, LANES), lambda j: (0, 0, 0, j)),
                  pl.BlockSpec((gpb, 256, 256), lambda j: (j, 0, 0)),
                  pl.BlockSpec((gpb, 256, 256), lambda j: (j, 0, 0)),
                  pl.BlockSpec((gpb, 256, 256), lambda j: (j, 0, 0)),
                  pl.BlockSpec((gpb, 2, LANES), lambda j: (j, 0, 0))],
        out_specs=pl.BlockSpec((T, bsz, nchunk, LANES), lambda j: (0, 0, 0, j), pipeline_mode=one),
        out_shape=jax.ShapeDtypeStruct(u_t.shape, BF16),
        scratch_shapes=[pltpu.VMEM((gpb, bsz * (nchunk + nctx), 256), BF16),
                        pltpu.VMEM((2, bsz * pitch, LANES), F32),
                        pltpu.VMEM((2, bsz * pitch, LANES), F32),
                        pltpu.VMEM((2, bsz * pitch, LANES), F32)],
        compiler_params=_cparams("arbitrary"),
        name="s5_scan",
    )(u_t, uc_t, m1, m2, m3, lam)


def _dft_tables(n_seq):
    na_sig = n_seq // FFT_NB
    ka = np.arange(FFT_NA)[:, None]
    nb = np.arange(FFT_NB)[:, None, None]

    def stage1(n_in):
        na = np.arange(n_in)[None, :]
        ang = -2.0 * np.pi * (na * ka / FFT_NA)[None] - 2.0 * np.pi * (nb * ka[None]) / FFT_N
        return np.cos(ang), np.sin(ang)

    c, s = stage1(na_sig)
    f1 = np.concatenate([np.concatenate([c, -s], axis=2), np.concatenate([s, c], axis=2)], axis=1)
    c, s = stage1(FFT_NA)
    f1_real = np.concatenate([c, s], axis=1)
    kb = np.arange(FFT_NB)[:, None]
    nbv = np.arange(FFT_NB)[None, :]
    ang = -2.0 * np.pi * kb * nbv / FFT_NB
    c, s = np.cos(ang), np.sin(ang)
    f2 = np.block([[c, -s], [s, c]])
    f2_inv = np.block([[c, s], [-s, c]]) / FFT_NB
    nap = np.arange(na_sig)[:, None]
    kav = np.arange(FFT_NA)[None, :]
    ang = 2.0 * np.pi * (nap * kav / FFT_NA)[None] + 2.0 * np.pi * (nb * kav[None]) / FFT_N
    c, s = np.cos(ang) / FFT_NA, np.sin(ang) / FFT_NA
    f3 = np.concatenate([np.concatenate([c, -s], axis=2), np.concatenate([s, c], axis=2)], axis=1)
    as_bf16 = lambda a: jnp.asarray(a.astype(np.float32)).astype(BF16)
    return as_bf16(f1), as_bf16(f1_real), as_bf16(f2), as_bf16(f2_inv), as_bf16(f3)


def _conv3_rows(z_ref, plane, nb, w, b, nbk):
    def rows(i):
        return z_ref[plane, 0, i].astype(F32)

    cur = rows(nb)
    na = cur.shape[0]
    ridx = lax.broadcasted_iota(jnp.int32, cur.shape, 0)
    before = rows((nb + nbk - 1) % nbk)
    after = rows((nb + 1) % nbk)
    prev = jnp.where(nb == 0, jnp.where(ridx == 0, 0.0, pltpu.roll(before, 1, axis=0)), before)
    nxt = jnp.where(nb == nbk - 1, jnp.where(ridx == na - 1, 0.0, pltpu.roll(after, na - 1, axis=0)), after)
    return prev * w[0:1] + cur * w[1:2] + nxt * w[2:3] + b


def _regroup(x):
    return jnp.swapaxes(x, 0, 1)


def _hyena_kernel(zv_ref, zg1_ref, zg2_ref, wv_ref, bv_ref, wg1_ref, bg1_ref, wg2_ref, bg2_ref,
                  bias_ref, kf_ref, f1_ref, f2_ref, f2i_ref, f3_ref, o_ref, s_ref, z1_ref,
                  *, nbk, na, ngrp, kag):
    step = pl.program_id(2)
    hb = FFT_NB

    def stage1(read_rows):
        def body(nb, carry):
            rhs = jnp.concatenate([read_rows(0, nb), read_rows(1, nb)], axis=0).astype(BF16)
            s_ref[nb] = jnp.dot(f1_ref[nb], rhs, preferred_element_type=F32).astype(BF16)
            return carry
        lax.fori_loop(0, nbk, body, 0, unroll=2)

    def stage3(gate_ref, wg, bg, read_v, bias, dst_ref):
        def body(nb, carry):
            y = jnp.dot(f3_ref[nb], s_ref[nb], preferred_element_type=F32)
            for pln in range(2):
                gate = _conv3_rows(gate_ref, pln, nb, wg, bg, nbk)
                out = gate * (y[pln * na:(pln + 1) * na] + bias * read_v(pln, nb))
                dst_ref[pln, 0, nb] = out.astype(dst_ref.dtype)
            return carry
        lax.fori_loop(0, nbk, body, 0, unroll=2)

    conv_v = lambda pln, nb: _conv3_rows(zv_ref, pln, nb, wv_ref[...], bv_ref[...], nbk)
    z1_rows = lambda pln, nb: z1_ref[pln, 0, nb].astype(F32)

    @pl.when(step == 0)
    def _():
        stage1(conv_v)

    @pl.when(step == ngrp)
    def _():
        stage1(z1_rows)

    r0 = pl.multiple_of((step % ngrp) * kag, kag)
    re_t = _regroup(s_ref[:, pl.ds(r0, kag), :].astype(F32))
    im_t = _regroup(s_ref[:, pl.ds(FFT_NA + r0, kag), :].astype(F32))
    back_re, back_im = [], []
    for k in range(kag):
        rhs = jnp.concatenate([re_t[k], im_t[k]], axis=0).astype(BF16)
        spec = jnp.dot(f2_ref[...], rhs, preferred_element_type=F32)
        kf = kf_ref[0, k]
        sr, si = spec[:hb], spec[hb:]
        kr, ki = kf[:hb], kf[hb:]
        prod = jnp.concatenate([sr * kr - si * ki, sr * ki + si * kr], axis=0).astype(BF16)
        back = jnp.dot(f2i_ref[...], prod, preferred_element_type=F32)
        back_re.append(back[:hb])
        back_im.append(back[hb:])
    s_ref[:, pl.ds(r0, kag), :] = _regroup(jnp.stack(back_re, axis=0)).astype(BF16)
    s_ref[:, pl.ds(FFT_NA + r0, kag), :] = _regroup(jnp.stack(back_im, axis=0)).astype(BF16)

    @pl.when(step == ngrp - 1)
    def _():
        stage3(zg1_ref, wg1_ref[...], bg1_ref[...], conv_v, bias_ref[0:1, :], z1_ref)

    @pl.when(step == 2 * ngrp - 1)
    def _():
        stage3(zg2_ref, wg2_ref[...], bg2_ref[...], z1_rows, bias_ref[1:2, :], o_ref)


def _hyena(z5, conv_w, conv_b, bias, kf, tables, cb=MXU_N, kag=16):
    f1, _, f2, f2i, f3 = tables
    _, npair, nbk, na, _ = z5.shape
    c = bias.shape[-1]
    ncb = c // cb
    ngrp = FFT_NA // kag
    one = pl.Buffered(1)
    zspec = lambda grp: pl.BlockSpec((2, 1, nbk, na, cb), lambda p, j, s: (0, p, 0, 0, grp * ncb + j),
                                     pipeline_mode=one)
    wspec = lambda grp: pl.BlockSpec((3, cb), lambda p, j, s: (0, grp * ncb + j))
    bspec = lambda grp: pl.BlockSpec((1, cb), lambda p, j, s: (0, grp * ncb + j))
    const = lambda a: pl.BlockSpec(a.shape, lambda p, j, s: tuple(0 for _ in a.shape), pipeline_mode=one)
    cw = conv_w.astype(F32)
    cbias = conv_b.astype(F32).reshape(1, -1)
    return pl.pallas_call(
        functools.partial(_hyena_kernel, nbk=nbk, na=na, ngrp=ngrp, kag=kag),
        grid=(npair, ncb, 2 * ngrp),
        in_specs=[zspec(0), zspec(1), zspec(2), wspec(0), bspec(0), wspec(1), bspec(1), wspec(2), bspec(2),
                  pl.BlockSpec((HY_ORDER, cb), lambda p, j, s: (0, j)),
                  pl.BlockSpec((1, kag, 2 * FFT_NB, cb), lambda p, j, s: (s // ngrp, s % ngrp, 0, j)),
                  const(f1), const(f2), const(f2i), const(f3)],
        out_specs=pl.BlockSpec((2, 1, nbk, na, cb), lambda p, j, s: (0, p, 0, 0, j)),
        out_shape=jax.ShapeDtypeStruct((2, npair, nbk, na, c), BF16),
        scratch_shapes=[pltpu.VMEM((nbk, 2 * FFT_NA, cb), BF16),
                        pltpu.VMEM((2, 1, nbk, na, cb), BF16)],
        compiler_params=_cparams("arbitrary", "arbitrary", "arbitrary"),
        name="hyena_conv",
    )(z5, z5, z5, cw, cbias, cw, cbias, cw, cbias, bias.astype(F32), kf, f1, f2, f2i, f3)


def _filter_kernel(embt_ref, tv_ref, w1t_ref, b1_ref, w2t_ref, b2_ref, fr_ref, w3_ref, dec_ref, f1_ref,
                   f2_ref, k_ref, s_ref, h_ref, *, nbk, kag):
    first = (pl.program_id(0) == 0) & (pl.program_id(1) == 0)
    step = pl.program_id(2)
    half = FFT_NA // 2
    lanes_per_pass = 8 * FFT_NA

    @pl.when(first & (step == 0))
    def _():
        fr = fr_ref[...]
        for i in range(embt_ref.shape[1] // lanes_per_pass):
            cols = slice(i * lanes_per_pass, (i + 1) * lanes_per_pass)
            h = jnp.sin(fr * (jnp.dot(w1t_ref[...], embt_ref[:, cols], precision=HI,
                                      preferred_element_type=F32) + b1_ref[...]))
            h_ref[:, cols] = jnp.sin(fr * (jnp.dot(w2t_ref[...], h, precision=HI,
                                                   preferred_element_type=F32) + b2_ref[...]))

    @pl.when(step == 0)
    def _():
        dec = dec_ref[0]

        def body(nb, carry):
            h = h_ref[:, pl.ds(pl.multiple_of(nb * FFT_NA, FFT_NA), FFT_NA)].T
            fwd = jnp.dot(h[:half], w3_ref[0, 0], precision=HI, preferred_element_type=F32)
            bwd = jnp.dot(h[half:], w3_ref[0, 1], precision=HI, preferred_element_type=F32)
            tv = tv_ref[nb]
            window = jnp.exp(-tv[:, 0:1] * dec) * tv[:, 1:2]
            filt = jnp.concatenate([fwd, bwd], axis=0) * window
            s_ref[nb] = jnp.dot(f1_ref[nb], filt.astype(BF16), preferred_element_type=F32).astype(BF16)
            return carry
        lax.fori_loop(0, nbk, body, 0)

    r0 = pl.multiple_of(step * kag, kag)
    re_t = _regroup(s_ref[:, pl.ds(r0, kag), :].astype(F32))
    im_t = _regroup(s_ref[:, pl.ds(FFT_NA + r0, kag), :].astype(F32))
    for k in range(kag):
        rhs = jnp.concatenate([re_t[k], im_t[k]], axis=0).astype(BF16)
        k_ref[0, k] = jnp.dot(f2_ref[...], rhs, preferred_element_type=F32)


def _filter_spectra(n, w1, b1, w2, b2, freq, w3, decay, tables, cb=MXU_N, kag=16):
    _, f1_real, f2, _, _ = tables
    hid = w2.shape[0]
    c = decay.shape[-1]
    t = np.linspace(0.0, 1.0, n, dtype=np.float32)[:, None]
    w = (2.0 * np.pi * np.arange(n, dtype=np.float32) / n).astype(np.float32)
    bands = np.linspace(1e-4, HY_BANDS - 1, HY_BANDS, dtype=np.float32)
    ang = w[:, None] * bands[None, :]
    emb = np.concatenate([t, np.cos(ang), -np.sin(ang)], axis=-1).astype(np.float32)
    kpad = 64
    idx = FFT_NB * np.arange(FFT_NA)[None, :] + np.arange(FFT_NB)[:, None]
    lagi = np.minimum(np.where(idx < n, idx, FFT_N - idx), n - 1)
    embt = np.zeros((kpad, FFT_N), np.float32)
    embt[:emb.shape[1]] = emb[lagi.reshape(-1)].T
    tv = np.stack([t[lagi, 0], (idx != n).astype(np.float32)], axis=-1)
    col = lambda a: a.reshape(hid, 1).astype(F32)
    w1t = jnp.pad(w1.astype(F32), ((0, kpad - w1.shape[0]), (0, 0))).T
    w3r = w3.astype(F32).reshape(hid, HY_ORDER, 2, c).transpose(1, 2, 0, 3)
    dec = jnp.abs(decay.astype(F32)).reshape(HY_ORDER, 1, c)
    one = pl.Buffered(1)
    const = lambda shape: pl.BlockSpec(shape, lambda o, j, s: tuple(0 for _ in shape), pipeline_mode=one)
    return pl.pallas_call(
        functools.partial(_filter_kernel, nbk=FFT_NB, kag=kag),
        grid=(HY_ORDER, c // cb, FFT_NA // kag),
        in_specs=[const(embt.shape), const(tv.shape), const((hid, kpad)), const((hid, 1)),
                  const((hid, hid)), const((hid, 1)), const((hid, 1)),
                  pl.BlockSpec((1, 2, hid, cb), lambda o, j, s: (o, 0, 0, j)),
                  pl.BlockSpec((1, 1, cb), lambda o, j, s: (o, 0, j)),
                  const(f1_real.shape), const(f2.shape)],
        out_specs=pl.BlockSpec((1, kag, 2 * FFT_NB, cb), lambda o, j, s: (o, s, 0, j)),
        out_shape=jax.ShapeDtypeStruct((HY_ORDER, FFT_NA, 2 * FFT_NB, c), F32),
        scratch_shapes=[pltpu.VMEM((FFT_NB, 2 * FFT_NA, cb), BF16),
                        pltpu.VMEM((hid, FFT_N), F32)],
        compiler_params=_cparams("arbitrary", "arbitrary", "arbitrary"),
        name="hyena_filter",
    )(jnp.asarray(embt), jnp.asarray(tv), w1t, col(b1), w2.astype(F32).T, col(b2), col(freq), w3r, dec,
      f1_real, f2)


def _out_kernel(x_ref, pos_ref, mod_ref, ys_ref, yh_ref, gw_ref, gb_ref, g5_ref, gh_ref, wo_ref,
                g2_ref, w1_ref, w2_ref, gf_ref, o_ref, *, d):
    m = mod_ref[0]
    gate1, shift2 = m[:, 2 * d:3 * d], m[:, 3 * d:4 * d]
    scale2, gate2 = m[:, 4 * d:5 * d], m[:, 5 * d:6 * d]
    h = x_ref[0] + pos_ref[...]
    ab = jnp.dot(ys_ref[0], gw_ref[...], preferred_element_type=F32) + gb_ref[...]
    half = ab.shape[-1] // 2
    y5 = ab[:, :half] * jax.nn.sigmoid(ab[:, half:])
    mix = jnp.concatenate([_rms(y5, g5_ref[...]), _rms(yh_ref[0].astype(F32), gh_ref[...])], axis=-1)
    h = h + gate1 * jnp.dot(mix.astype(BF16), wo_ref[...], preferred_element_type=F32)
    hn = _rms(h, g2_ref[...]) * (1.0 + scale2) + shift2
    hid = jnp.dot(hn.astype(BF16), w1_ref[...], preferred_element_type=F32)
    hid = jnp.square(jnp.maximum(hid, 0.0))
    h = h + gate2 * jnp.dot(hid.astype(BF16), w2_ref[...], preferred_element_type=F32)
    o_ref[0] = _rms(h, gf_ref[...])


def _output_stage(x, pos, mod3, ys, yh, glu_w, glu_b, g5, gh, w_out, g2, w1, w2, gf, tm=512):
    bsz, n, d = x.shape
    dh = ys.shape[-1]
    one = pl.Buffered(1)
    const = lambda a: pl.BlockSpec(a.shape, lambda i, b: tuple(0 for _ in a.shape), pipeline_mode=one)
    row = lambda a: a.reshape(1, -1).astype(F32)
    glu_b, g5, gh, g2, gf = row(glu_b), row(g5), row(gh), row(g2), row(gf)
    return pl.pallas_call(
        functools.partial(_out_kernel, d=d),
        grid=(n // tm, bsz),
        in_specs=[pl.BlockSpec((1, tm, d), lambda i, b: (b, i, 0)),
                  pl.BlockSpec((tm, d), lambda i, b: (i, 0)),
                  pl.BlockSpec((1, 1, mod3.shape[-1]), lambda i, b: (b, 0, 0)),
                  pl.BlockSpec((1, tm, dh), lambda i, b: (b, i, 0)),
                  pl.BlockSpec((1, tm, dh), lambda i, b: (b, i, 0)),
                  const(glu_w), const(glu_b), const(g5), const(gh), const(w_out), const(g2),
                  const(w1), const(w2), const(gf)],
        out_specs=pl.BlockSpec((1, tm, d), lambda i, b: (b, i, 0)),
        out_shape=jax.ShapeDtypeStruct((bsz, n, d), F32),
        compiler_params=_cparams("arbitrary", "arbitrary"),
        name="mix_mlp_out",
    )(x, pos, mod3, ys, yh, glu_w, glu_b, g5, gh, w_out, g2, w1, w2, gf)


def _pos_table(n, d):
    rows = n // GRID_W
    row = np.repeat(np.arange(rows, dtype=np.float32), GRID_W)
    col = np.tile(np.arange(GRID_W, dtype=np.float32), rows)
    quarter = d // 4
    omega = (1.0 / (POS_BASE ** (np.arange(quarter, dtype=np.float32) / quarter))).astype(np.float32)

    def enc(p):
        ang = p[:, None] * omega[None, :]
        return np.concatenate([np.sin(ang), np.cos(ang)], axis=-1)

    return jnp.asarray(np.concatenate([enc(row), enc(col)], axis=-1).astype(np.float32))


def kernel(x, c, ctx, c_ctx, ada_w, ada_b, norm1_g, w_in, s5_a_re, s5_a_im, s5_log_step, s5_b_re,
           s5_b_im, s5_c_re, s5_c_im, s5_d, s5_glu_w, s5_glu_b, hy_conv_w, hy_conv_b, hy_f_w1,
           hy_f_b1, hy_f_w2, hy_f_b2, hy_f_freq, hy_f_w3, hy_decay, hy_bias, mix_g_s5, mix_g_hy,
           w_out, norm2_g, mlp_w1, mlp_w2, final_g):
    bsz, n, d = x.shape
    depth = ada_w.shape[0]
    d_s5 = s5_d.shape[-1]
    d_hy = hy_bias.shape[-1]
    nctx = ctx.shape[1]
    npair = bsz // 2
    nbk, na = FFT_NB, n // FFT_NB
    pos = _pos_table(n, d)
    tables = _dft_tables(n)

    assert depth == 1 and bsz % 2 == 0 and n % (FFT_NB * 8) == 0 and 2 * n == FFT_N
    mod_rows = 16
    c_rows = jnp.concatenate([c, c_ctx[None], jnp.zeros((mod_rows - bsz - 1, d), c.dtype)], axis=0)
    mod3 = _modulation(c_rows, ada_w[0], ada_b[0]).reshape(mod_rows, 1, N_MOD * d)

    w_in_b = w_in[0].astype(BF16)
    u, z = _project(x, pos, mod3, lambda b: b, norm1_g[0], w_in_b, (d_s5, 3 * d_hy), tm=512)
    (uc,) = _project(ctx, None, mod3, lambda b: bsz, norm1_g[0], w_in_b, (d_s5,), tm=nctx)

    m1, m2, m3, lam = _s5_tables(s5_a_re[0], s5_a_im[0], s5_log_step[0], s5_b_re[0], s5_b_im[0],
                                 s5_c_re[0], s5_c_im[0], s5_d[0])
    u_t = u.reshape(bsz, n // S5_T, S5_T, d_s5).transpose(2, 0, 1, 3)
    uc_t = uc.reshape(bsz, nctx // S5_T, S5_T, d_s5).transpose(2, 0, 1, 3)
    ys = _s5_scan(u_t, uc_t, m1, m2, m3, lam).transpose(1, 2, 0, 3).reshape(bsz, n, d_s5)

    kf = _filter_spectra(n, hy_f_w1[0], hy_f_b1[0], hy_f_w2[0], hy_f_b2[0], hy_f_freq[0],
                         hy_f_w3[0], hy_decay[0], tables)
    z5 = z.reshape(bsz, na, nbk, 3 * d_hy).transpose(0, 2, 1, 3).reshape(2, npair, nbk, na, 3 * d_hy)
    yh = _hyena(z5, hy_conv_w[0], hy_conv_b[0], hy_bias[0], kf, tables)
    yh = yh.reshape(bsz, nbk, na, d_hy).transpose(0, 2, 1, 3).reshape(bsz, n, d_hy)

    return _output_stage(x, pos, mod3, ys, yh, s5_glu_w[0].astype(BF16), s5_glu_b[0], mix_g_s5[0],
                         mix_g_hy[0], w_out[0].astype(BF16), norm2_g[0], mlp_w1[0].astype(BF16),
                         mlp_w2[0].astype(BF16), final_g)
```

```python
import functools
import math

import numpy as np
import jax
import jax.numpy as jnp
from jax import lax
from jax.experimental import pallas as pl
from jax.experimental.pallas import tpu as pltpu

F32 = jnp.float32
BF16 = jnp.bfloat16
HI = lax.Precision.HIGHEST

EPS = 1e-6
GRID_W = 64
POS_BASE = 10000.0
N_MOD = 6
S5_C = 16
S5_P = 64
S5_T = 16
HY_BANDS = 16
HY_ORDER = 2

FFT_N = 8192
FFT_NA = 128
FFT_NB = 64

LANES = 128
MXU_N = 256
SLAB = 16
VMEM_LIMIT = 56 * 1024 * 1024


def _cparams(*sem):
    return pltpu.CompilerParams(dimension_semantics=sem, vmem_limit_bytes=VMEM_LIMIT)


def _rms(x, g):
    return x * lax.rsqrt(jnp.mean(x * x, axis=-1, keepdims=True) + EPS) * g


def _mod_kernel(c_ref, w_ref, b_ref, o_ref):
    c = c_ref[...]
    a = c * jax.nn.sigmoid(c)
    o_ref[...] = jnp.dot(a.astype(BF16), w_ref[...].astype(BF16),
                         preferred_element_type=F32) + b_ref[...]


def _modulation(c_rows, ada_w, ada_b):
    rows, d = c_rows.shape
    n = ada_w.shape[1]
    bn = 1024
    return pl.pallas_call(
        _mod_kernel,
        grid=(n // bn,),
        in_specs=[pl.BlockSpec((rows, d), lambda j: (0, 0)),
                  pl.BlockSpec((d, bn), lambda j: (0, j)),
                  pl.BlockSpec((1, bn), lambda j: (0, j))],
        out_specs=pl.BlockSpec((rows, bn), lambda j: (0, j)),
        out_shape=jax.ShapeDtypeStruct((rows, n), F32),
        compiler_params=_cparams("arbitrary"),
        name="ada_mod",
    )(c_rows, ada_w, ada_b.reshape(1, n))


def _proj_kernel(*refs, d, use_pos, n_split):
    if use_pos:
        x_ref, pos_ref, mod_ref, g_ref, w_ref = refs[:5]
        outs = refs[5:]
        h = x_ref[0] + pos_ref[...]
    else:
        x_ref, mod_ref, g_ref, w_ref = refs[:4]
        outs = refs[4:]
        h = x_ref[0]
    m = mod_ref[0]
    hn = _rms(h, g_ref[...]) * (1.0 + m[:, d:2 * d]) + m[:, 0:d]
    proj = jnp.dot(hn.astype(BF16), w_ref[...], preferred_element_type=F32)
    off = 0
    for o_ref, width in zip(outs, n_split):
        o_ref[0] = proj[:, off:off + width].astype(o_ref.dtype)
        off += width


def _project(x, pos, mod3, mod_row_of_batch, g, w_bf16, n_split, tm):
    bsz, n, d = x.shape
    width = sum(n_split)
    use_pos = pos is not None
    in_specs = [pl.BlockSpec((1, tm, d), lambda i, b: (b, i, 0))]
    args = [x]
    if use_pos:
        in_specs.append(pl.BlockSpec((tm, d), lambda i, b: (i, 0)))
        args.append(pos)
    in_specs += [pl.BlockSpec((1, 1, mod3.shape[-1]), lambda i, b: (mod_row_of_batch(b), 0, 0)),
                 pl.BlockSpec((1, d), lambda i, b: (0, 0)),
                 pl.BlockSpec((d, width), lambda i, b: (0, 0))]
    args += [mod3, g.reshape(1, d), w_bf16]
    return pl.pallas_call(
        functools.partial(_proj_kernel, d=d, use_pos=use_pos, n_split=n_split),
        grid=(n // tm, bsz),
        in_specs=in_specs,
        out_specs=[pl.BlockSpec((1, tm, wd), lambda i, b: (b, i, 0)) for wd in n_split],
        out_shape=[jax.ShapeDtypeStruct((bsz, n, wd), BF16) for wd in n_split],
        compiler_params=_cparams("arbitrary", "arbitrary"),
        name="norm_proj",
    )(*args)


def _s5_tables(a_re, a_im, log_step, b_re, b_im, c_re, c_im, d_skip):
    T, C, P = S5_T, S5_C, S5_P
    G = a_re.shape[1]
    gpb = LANES // C
    slots = np.arange(T)
    tok = np.stack([8 * (slots // 8) + (slots % 8 - o) % 8 for o in range(gpb)])
    tok_g = tok[np.arange(G) % gpb]

    step = jnp.exp(log_step.astype(F32))[..., None]
    ar = a_re.astype(F32) * step
    ai = a_im.astype(F32) * step

    def powers(d, expo):
        e = jnp.asarray(np.broadcast_to(expo, (G, expo.shape[-1])).astype(np.float32))[:, :, None]
        mag = jnp.exp(ar[d][:, None, :] * e)
        return mag * jnp.cos(ai[d][:, None, :] * e), mag * jnp.sin(ai[d][:, None, :] * e)

    lr, li = a_re.astype(F32), a_im.astype(F32)
    first = [powers(d, np.ones(1)) for d in range(2)]
    nr = jnp.stack([first[d][0][:, 0] for d in range(2)]) - 1.0
    ni = jnp.stack([first[d][1][:, 0] for d in range(2)])
    den = lr * lr + li * li
    qr = (nr * lr + ni * li) / den
    qi = (ni * lr - nr * li) / den
    bbr = qr[..., None] * b_re - qi[..., None] * b_im
    bbi = qr[..., None] * b_im + qi[..., None] * b_re
    cr, ci = c_re.astype(F32), c_im.astype(F32)

    kerns = []
    for d in range(2):
        pr, pi = powers(d, np.arange(T))
        wr = cr[d][:, None] * pr[:, :, None, :] - ci[d][:, None] * pi[:, :, None, :]
        wi = cr[d][:, None] * pi[:, :, None, :] + ci[d][:, None] * pr[:, :, None, :]
        kerns.append(jnp.einsum('gtcp,gpe->gtce', wr, bbr[d], precision=HI)
                     - jnp.einsum('gtcp,gpe->gtce', wi, bbi[d], precision=HI))
    kcat = jnp.concatenate(kerns, axis=1).reshape(G // gpb, gpb, 2 * T, C, C)

    lag = tok[:, None, :] - tok[:, :, None]
    lags = np.arange(T)[None, :, None, None]
    sel = np.concatenate([lag[:, None] == lags, -lag[:, None] == lags], axis=1).astype(np.float32)
    m1 = jnp.einsum('oxab,joxce->joaebc', jnp.asarray(sel), kcat, precision=HI)
    eye = np.eye(T, dtype=np.float32)
    skip = (jnp.asarray(eye[None, :, None, :, None] * eye[None, None, :, None, :])
            * d_skip.astype(F32).reshape(G, 1, 1, 1, C))
    m1 = (m1.reshape(G, T, C, T, C) + skip).reshape(G, T * C, T * C)

    def in_mat(d, expo):
        p_r, p_i = powers(d, expo)
        br, bi = bbr[d].transpose(0, 2, 1)[:, None], bbi[d].transpose(0, 2, 1)[:, None]
        er = p_r[:, :, None, :] * br - p_i[:, :, None, :] * bi
        ei = p_r[:, :, None, :] * bi + p_i[:, :, None, :] * br
        return er.reshape(G, T * C, P), ei.reshape(G, T * C, P)

    efr, efi = in_mat(0, T - 1 - tok_g)
    ebr, ebi = in_mat(1, tok_g)
    m2 = jnp.concatenate([efr, ebr, efi, ebi], axis=-1)

    def out_mat(d, expo):
        p_r, p_i = powers(d, expo)
        crt = cr[d].transpose(0, 2, 1)[:, :, None, :]
        cit = ci[d].transpose(0, 2, 1)[:, :, None, :]
        prt = p_r.transpose(0, 2, 1)[:, :, :, None]
        pit = p_i.transpose(0, 2, 1)[:, :, :, None]
        vr = crt * prt - cit * pit
        vi = crt * pit + cit * prt
        return vr.reshape(G, P, T * C), -vi.reshape(G, P, T * C)

    vfr, vfi = out_mat(0, tok_g + 1)
    vbr, vbi = out_mat(1, T - tok_g)
    m3 = jnp.concatenate([vfr, vbr, vfi, vbi], axis=1)

    last = [powers(d, np.full(1, T)) for d in range(2)]
    lam = jnp.stack([jnp.concatenate([last[0][c][:, 0], last[1][c][:, 0]], axis=-1)
                     for c in range(2)], axis=1)
    return m1.astype(BF16), m2.astype(BF16), m3.astype(BF16), lam


def _gelu_tanh(x):
    return 0.5 * x * (1.0 + jnp.tanh(math.sqrt(2.0 / math.pi) * (x + 0.044715 * (x * x * x))))


def _s5_kernel(u_ref, uc_ref, m1_ref, m2_ref, m3_ref, lam_ref, o_ref,
               z_ref, e_ref, s1_ref, s2_ref, *, nbatch, nchunk, nctx, pitch):
    groups = LANES // S5_C
    rows = nbatch * nchunk
    lane = lax.broadcasted_iota(jnp.int32, (1, LANES), 1)
    slot_bits = [(((lane // S5_C) >> j) & 1) == 1 for j in range(3)]

    def skew(xs):
        cur = [xs[(-k) % groups] for k in range(groups)]
        for j, bit in enumerate(slot_bits):
            cur = [jnp.where(bit, cur[(i - (1 << j)) % groups], cur[i]) for i in range(groups)]
        return cur

    def to_chunk_rows(read_t, nrows, row0):
        for h in range(2):
            rolled = []
            for t8 in range(groups):
                v = read_t(8 * h + t8).astype(F32)
                rolled.append(pltpu.roll(v, t8 * S5_C, axis=1) if t8 else v)
            for g, zg in enumerate(skew(rolled)):
                z_ref[g, row0:row0 + nrows, h * LANES:(h + 1) * LANES] = zg.astype(BF16)

    for b in range(nbatch):
        to_chunk_rows(lambda t, b=b: u_ref[t, b], nchunk, b * nchunk)
    to_chunk_rows(lambda t: uc_ref[t].reshape(nbatch * nctx, LANES), nbatch * nctx, rows)

    half = lane < (LANES // 2)

    for g in range(groups):
        zg = z_ref[g]
        e = jnp.dot(zg, m2_ref[g], preferred_element_type=F32)
        for b in range(nbatch):
            for c in range(2):
                e_ref[c, b * pitch:b * pitch + nchunk, :] = \
                    e[b * nchunk:(b + 1) * nchunk, c * LANES:(c + 1) * LANES]
                e_ref[c, b * pitch + nchunk:b * pitch + nchunk + nctx, :] = \
                    e[rows + b * nctx:rows + (b + 1) * nctx, c * LANES:(c + 1) * LANES]
        lam_r = lam_ref[g, 0:1, :]
        lam_i = lam_ref[g, 1:2, :]

        def gather(c, rf, rb):
            ef = e_ref[c, pl.ds(rf, nbatch, stride=pitch), :]
            eb = e_ref[c, pl.ds(rb, nbatch, stride=pitch), :]
            return jnp.where(half, ef, eb)

        def advance(sr, si, rf, rb):
            er, ei = gather(0, rf, rb), gather(1, rf, rb)
            return lam_r * sr - lam_i * si + er, lam_r * si + lam_i * sr + ei

        sr = jnp.zeros((nbatch, LANES), F32)
        si = jnp.zeros((nbatch, LANES), F32)
        for i in range(nctx):
            sr, si = advance(sr, si, nchunk + i, nchunk + nctx - 1 - i)

        def body(i, carry):
            sr, si = carry
            kf, kb = i, nchunk - 1 - i
            s1_ref[0, pl.ds(kf, nbatch, stride=pitch), :] = sr
            s1_ref[1, pl.ds(kf, nbatch, stride=pitch), :] = si
            s2_ref[0, pl.ds(kb, nbatch, stride=pitch), :] = sr
            s2_ref[1, pl.ds(kb, nbatch, stride=pitch), :] = si
            return advance(sr, si, kf, kb)

        lax.fori_loop(0, nchunk, body, (sr, si))

        parts = []
        for b in range(nbatch):
            lo = b * pitch
            parts.append(jnp.concatenate(
                [jnp.where(half, s1_ref[c, lo:lo + nchunk, :], s2_ref[c, lo:lo + nchunk, :])
                 for c in range(2)], axis=-1))
        s_in = jnp.concatenate(parts, axis=0).astype(BF16)
        y = (jnp.dot(zg[:rows], m1_ref[g], preferred_element_type=F32)
             + jnp.dot(s_in, m3_ref[g], preferred_element_type=F32))
        z_ref[g, 0:rows, :] = _gelu_tanh(y).astype(BF16)

    for b in range(nbatch):
        r0 = b * nchunk
        for h in range(2):
            ys = [z_ref[g, r0:r0 + nchunk, h * LANES:(h + 1) * LANES].astype(F32)
                  for g in range(groups)]
            for t8, acc in enumerate(skew(ys)):
                if t8:
                    acc = pltpu.roll(acc, LANES - t8 * S5_C, axis=1)
                o_ref[8 * h + t8, b] = acc.astype(BF16)


def _s5_scan(u_t, uc_t, m1, m2, m3, lam):
    T, bsz, nchunk, width = u_t.shape
    nctx = uc_t.shape[2]
    gpb = LANES // S5_C
    pitch = 8 * (((nchunk + nctx) // 8) | 1)
    if pitch < nchunk + nctx:
        pitch += 16
    kern = functools.partial(_s5_kernel, nbatch=bsz, nchunk=nchunk, nctx=nctx, pitch=pitch)
    one = pl.Buffered(1)
    return pl.pallas_call(
        kern,
        grid=(width // LANES,),
        in_specs=[pl.BlockSpec((T, bsz, nchunk, LANES), lambda j: (0, 0, 0, j), pipeline_mode=one),
                  pl.BlockSpec((T, bsz, nctx, LANES), lambda j: (0, 0, 0, j)),
                  pl.BlockSpec((gpb, 256, 256), lambda j: (j, 0, 0)),
                  pl.BlockSpec((gpb, 256, 256), lambda j: (j, 0, 0)),
                  pl.BlockSpec((gpb, 256, 256), lambda j: (j, 0, 0)),
                  pl.BlockSpec((gpb, 2, LANES), lambda j: (j, 0, 0))],
        out_specs=pl.BlockSpec((T, bsz, nchunk, LANES), lambda j: (0, 0, 0, j), pipeline_mode=one),
        out_shape=jax.ShapeDtypeStruct(u_t.shape, BF16),
        scratch_shapes=[pltpu.VMEM((gpb, bsz * (nchunk + nctx), 256), BF16),
                        pltpu.VMEM((2, bsz * pitch, LANES), F32),
                        pltpu.VMEM((2, bsz * pitch, LANES), F32),
                        pltpu.VMEM((2, bsz * pitch, LANES), F32)],
        compiler_params=_cparams("arbitrary"),
        name="s5_scan",
    )(u_t, uc_t, m1, m2, m3, lam)


def _dft_tables(n_seq):
    na_sig = n_seq // FFT_NB
    ka = np.arange(FFT_NA)[:, None]
    nb = np.arange(FFT_NB)[:, None, None]

    def stage1(n_in):
        na = np.arange(n_in)[None, :]
        ang = -2.0 * np.pi * (na * ka / FFT_NA)[None] - 2.0 * np.pi * (nb * ka[None]) / FFT_N
        return np.cos(ang), np.sin(ang)

    c, s = stage1(na_sig)
    f1 = np.concatenate([np.concatenate([c, -s], axis=2), np.concatenate([s, c], axis=2)], axis=1)
    c, s = stage1(FFT_NA)
    f1_real = np.concatenate([c, s], axis=1)
    kb = np.arange(FFT_NB)[:, None]
    nbv = np.arange(FFT_NB)[None, :]
    ang = -2.0 * np.pi * kb * nbv / FFT_NB
    c, s = np.cos(ang), np.sin(ang)
    f2 = np.block([[c, -s], [s, c]])
    f2_inv = np.block([[c, s], [-s, c]]) / FFT_NB
    nap = np.arange(na_sig)[:, None]
    kav = np.arange(FFT_NA)[None, :]
    ang = 2.0 * np.pi * (nap * kav / FFT_NA)[None] + 2.0 * np.pi * (nb * kav[None]) / FFT_N
    c, s = np.cos(ang) / FFT_NA, np.sin(ang) / FFT_NA
    f3 = np.concatenate([np.concatenate([c, -s], axis=2), np.concatenate([s, c], axis=2)], axis=1)
    as_bf16 = lambda a: jnp.asarray(a.astype(np.float32)).astype(BF16)
    return as_bf16(f1), as_bf16(f1_real), as_bf16(f2), as_bf16(f2_inv), as_bf16(f3)


def _conv3_rows(z_ref, plane, nb, w, b, nbk):
    def rows(i):
        return z_ref[plane, 0, i].astype(F32)

    cur = rows(nb)
    na = cur.shape[0]
    ridx = lax.broadcasted_iota(jnp.int32, cur.shape, 0)
    if isinstance(nb, int) and nb == 0:
        prev = jnp.where(ridx == 0, 0.0, pltpu.roll(rows(nbk - 1), 1, axis=0))
    else:
        prev = rows(nb - 1)
    if isinstance(nb, int) and nb == nbk - 1:
        nxt = jnp.where(ridx == na - 1, 0.0, pltpu.roll(rows(0), na - 1, axis=0))
    else:
        nxt = rows(nb + 1)
    return prev * w[0:1] + cur * w[1:2] + nxt * w[2:3] + b


def _for_row_blocks(nbk, fn, edges_differ):
    peel = 2 if edges_differ else 0
    for i in range(peel):
        fn(i)
    lax.fori_loop(peel, nbk - peel, lambda i, c: (fn(i), c)[1], 0, unroll=4)
    for i in range(nbk - peel, nbk):
        fn(i)


def _regroup(x):
    return jnp.swapaxes(x, 0, 1)


def _hyena_kernel(zv_ref, zg1_ref, zg2_ref, wv_ref, bv_ref, wg1_ref, bg1_ref, wg2_ref, bg2_ref,
                  bias_ref, kf_ref, f1_ref, f2_ref, f2i_ref, f3_ref, o_ref, s_ref, v_ref, z1_ref,
                  *, nbk, na, ngrp, kag):
    step = pl.program_id(2)
    hb = FFT_NB

    def stage1(src_ref, conv):
        def body(nb):
            parts = []
            for pln in range(2):
                if conv:
                    v = _conv3_rows(src_ref, pln, nb, wv_ref[...], bv_ref[...], nbk).astype(BF16)
                    v_ref[pln, 0, nb] = v
                else:
                    v = src_ref[pln, 0, nb]
                parts.append(v)
            rhs = jnp.concatenate(parts, axis=0)
            s_ref[nb] = jnp.dot(f1_ref[nb], rhs, preferred_element_type=F32).astype(BF16)
        _for_row_blocks(nbk, body, conv)

    def stage3(gate_ref, wg, bg, vsrc_ref, bias, dst_ref):
        def body(nb):
            y = jnp.dot(f3_ref[nb], s_ref[nb], preferred_element_type=F32)
            for pln in range(2):
                gate = _conv3_rows(gate_ref, pln, nb, wg, bg, nbk)
                out = gate * (y[pln * na:(pln + 1) * na] + bias * vsrc_ref[pln, 0, nb].astype(F32))
                dst_ref[pln, 0, nb] = out.astype(dst_ref.dtype)
        _for_row_blocks(nbk, body, True)

    @pl.when(step == 0)
    def _():
        stage1(zv_ref, True)

    @pl.when(step == ngrp)
    def _():
        stage1(z1_ref, False)

    starts = [pl.multiple_of((step % ngrp) * kag + slab * SLAB, SLAB) for slab in range(kag // SLAB)]
    loaded = [(_regroup(s_ref[:, pl.ds(r0, SLAB), :]),
               _regroup(s_ref[:, pl.ds(FFT_NA + r0, SLAB), :])) for r0 in starts]
    results = []
    for slab, (re_t, im_t) in enumerate(loaded):
        back_re, back_im = [], []
        for k in range(SLAB):
            rhs = jnp.concatenate([re_t[k], im_t[k]], axis=0)
            spec = jnp.dot(f2_ref[...], rhs, preferred_element_type=F32).astype(BF16)
            kf = kf_ref[0, slab * SLAB + k]
            sr, si = spec[:hb], spec[hb:]
            kr, ki = kf[:hb], kf[hb:]
            prod = jnp.concatenate([sr * kr - si * ki, sr * ki + si * kr], axis=0)
            back = jnp.dot(f2i_ref[...], prod, preferred_element_type=F32).astype(BF16)
            back_re.append(back[:hb])
            back_im.append(back[hb:])
        results.append((_regroup(jnp.stack(back_re, axis=0)), _regroup(jnp.stack(back_im, axis=0))))
    for r0, (out_re, out_im) in zip(starts, results):
        s_ref[:, pl.ds(r0, SLAB), :] = out_re
        s_ref[:, pl.ds(FFT_NA + r0, SLAB), :] = out_im

    @pl.when(step == ngrp - 1)
    def _():
        stage3(zg1_ref, wg1_ref[...], bg1_ref[...], v_ref, bias_ref[0:1, :], z1_ref)

    @pl.when(step == 2 * ngrp - 1)
    def _():
        stage3(zg2_ref, wg2_ref[...], bg2_ref[...], z1_ref, bias_ref[1:2, :], o_ref)


def _hyena(z5, conv_w, conv_b, bias, kf, tables, cb=MXU_N, kag=2 * SLAB):
    f1, _, f2, f2i, f3 = tables
    _, npair, nbk, na, _ = z5.shape
    c = bias.shape[-1]
    ncb = c // cb
    ngrp = FFT_NA // kag
    one = pl.Buffered(1)
    zspec = lambda grp: pl.BlockSpec((2, 1, nbk, na, cb), lambda p, j, s: (0, p, 0, 0, grp * ncb + j),
                                     pipeline_mode=one)
    wspec = lambda grp: pl.BlockSpec((3, cb), lambda p, j, s: (0, grp * ncb + j))
    bspec = lambda grp: pl.BlockSpec((1, cb), lambda p, j, s: (0, grp * ncb + j))
    const = lambda a: pl.BlockSpec(a.shape, lambda p, j, s: tuple(0 for _ in a.shape), pipeline_mode=one)
    cw = conv_w.astype(F32)
    cbias = conv_b.astype(F32).reshape(1, -1)
    return pl.pallas_call(
        functools.partial(_hyena_kernel, nbk=nbk, na=na, ngrp=ngrp, kag=kag),
        grid=(npair, ncb, 2 * ngrp),
        in_specs=[zspec(0), zspec(1), zspec(2), wspec(0), bspec(0), wspec(1), bspec(1), wspec(2), bspec(2),
                  pl.BlockSpec((HY_ORDER, cb), lambda p, j, s: (0, j)),
                  pl.BlockSpec((1, kag, 2 * FFT_NB, cb), lambda p, j, s: (s // ngrp, s % ngrp, 0, j)),
                  const(f1), const(f2), const(f2i), const(f3)],
        out_specs=pl.BlockSpec((2, 1, nbk, na, cb), lambda p, j, s: (0, p, 0, 0, j), pipeline_mode=one),
        out_shape=jax.ShapeDtypeStruct((2, npair, nbk, na, c), BF16),
        scratch_shapes=[pltpu.VMEM((nbk, 2 * FFT_NA, cb), BF16),
                        pltpu.VMEM((2, 1, nbk, na, cb), BF16),
                        pltpu.VMEM((2, 1, nbk, na, cb), BF16)],
        compiler_params=_cparams("arbitrary", "arbitrary", "arbitrary"),
        name="hyena_conv",
    )(z5, z5, z5, cw, cbias, cw, cbias, cw, cbias, bias.astype(F32), kf, f1, f2, f2i, f3)


def _filter_kernel(embt_ref, tv_ref, w1t_ref, b1_ref, w2t_ref, b2_ref, fr_ref, w3_ref, dec_ref, f1_ref,
                   f2_ref, k_ref, s_ref, h_ref, *, nbk, kag):
    first = (pl.program_id(0) == 0) & (pl.program_id(1) == 0)
    step = pl.program_id(2)
    half = FFT_NA // 2
    lanes_per_pass = 8 * FFT_NA

    @pl.when(first & (step == 0))
    def _():
        fr = fr_ref[...]
        for i in range(embt_ref.shape[1] // lanes_per_pass):
            cols = slice(i * lanes_per_pass, (i + 1) * lanes_per_pass)
            h = jnp.sin(fr * (jnp.dot(w1t_ref[...], embt_ref[:, cols], precision=HI,
                                      preferred_element_type=F32) + b1_ref[...]))
            h_ref[:, cols] = jnp.sin(fr * (jnp.dot(w2t_ref[...], h, precision=HI,
                                                   preferred_element_type=F32) + b2_ref[...]))

    @pl.when(step == 0)
    def _():
        dec = dec_ref[0]

        def body(nb, carry):
            h = h_ref[:, pl.ds(pl.multiple_of(nb * FFT_NA, FFT_NA), FFT_NA)].T.astype(BF16)
            fwd = jnp.dot(h[:half], w3_ref[0, 0], preferred_element_type=F32)
            bwd = jnp.dot(h[half:], w3_ref[0, 1], preferred_element_type=F32)
            tv = tv_ref[nb]
            window = jnp.exp(-tv[:, 0:1] * dec) * tv[:, 1:2]
            filt = jnp.concatenate([fwd, bwd], axis=0) * window
            s_ref[nb] = jnp.dot(f1_ref[nb], filt.astype(BF16), preferred_element_type=F32).astype(BF16)
            return carry
        lax.fori_loop(0, nbk, body, 0, unroll=2)

    r0 = pl.multiple_of(step * kag, kag)
    re_t = _regroup(s_ref[:, pl.ds(r0, kag), :])
    im_t = _regroup(s_ref[:, pl.ds(FFT_NA + r0, kag), :])
    for k in range(kag):
        rhs = jnp.concatenate([re_t[k], im_t[k]], axis=0)
        k_ref[0, k] = jnp.dot(f2_ref[...], rhs, preferred_element_type=F32).astype(BF16)


def _filter_spectra(n, w1, b1, w2, b2, freq, w3, decay, tables, cb=MXU_N, kag=16):
    _, f1_real, f2, _, _ = tables
    hid = w2.shape[0]
    c = decay.shape[-1]
    t = np.linspace(0.0, 1.0, n, dtype=np.float32)[:, None]
    w = (2.0 * np.pi * np.arange(n, dtype=np.float32) / n).astype(np.float32)
    bands = np.linspace(1e-4, HY_BANDS - 1, HY_BANDS, dtype=np.float32)
    ang = w[:, None] * bands[None, :]
    emb = np.concatenate([t, np.cos(ang), -np.sin(ang)], axis=-1).astype(np.float32)
    kpad = 64
    idx = FFT_NB * np.arange(FFT_NA)[None, :] + np.arange(FFT_NB)[:, None]
    lagi = np.minimum(np.where(idx < n, idx, FFT_N - idx), n - 1)
    embt = np.zeros((kpad, FFT_N), np.float32)
    embt[:emb.shape[1]] = emb[lagi.reshape(-1)].T
    tv = np.stack([t[lagi, 0], (idx != n).astype(np.float32)], axis=-1)
    col = lambda a: a.reshape(hid, 1).astype(F32)
    w1t = jnp.pad(w1.astype(F32), ((0, kpad - w1.shape[0]), (0, 0))).T
    w3r = w3.astype(BF16).reshape(hid, HY_ORDER, 2, c).transpose(1, 2, 0, 3)
    dec = jnp.abs(decay.astype(F32)).reshape(HY_ORDER, 1, c)
    one = pl.Buffered(1)
    const = lambda shape: pl.BlockSpec(shape, lambda o, j, s: tuple(0 for _ in shape), pipeline_mode=one)
    return pl.pallas_call(
        functools.partial(_filter_kernel, nbk=FFT_NB, kag=kag),
        grid=(HY_ORDER, c // cb, FFT_NA // kag),
        in_specs=[const(embt.shape), const(tv.shape), const((hid, kpad)), const((hid, 1)),
                  const((hid, hid)), const((hid, 1)), const((hid, 1)),
                  pl.BlockSpec((1, 2, hid, cb), lambda o, j, s: (o, 0, 0, j)),
                  pl.BlockSpec((1, 1, cb), lambda o, j, s: (o, 0, j)),
                  const(f1_real.shape), const(f2.shape)],
        out_specs=pl.BlockSpec((1, kag, 2 * FFT_NB, cb), lambda o, j, s: (o, s, 0, j)),
        out_shape=jax.ShapeDtypeStruct((HY_ORDER, FFT_NA, 2 * FFT_NB, c), BF16),
        scratch_shapes=[pltpu.VMEM((FFT_NB, 2 * FFT_NA, cb), BF16),
                        pltpu.VMEM((hid, FFT_N), F32)],
        compiler_params=_cparams("arbitrary", "arbitrary", "arbitrary"),
        name="hyena_filter",
    )(jnp.asarray(embt), jnp.asarray(tv), w1t, col(b1), w2.astype(F32).T, col(b2), col(freq), w3r, dec,
      f1_real, f2)


def _out_kernel(x_ref, pos_ref, mod_ref, ys_ref, yh_ref, gw_ref, gb_ref, g5_ref, gh_ref, wo_ref,
                g2_ref, w1_ref, w2_ref, gf_ref, o_ref, *, d):
    m = mod_ref[0]
    gate1, shift2 = m[:, 2 * d:3 * d], m[:, 3 * d:4 * d]
    scale2, gate2 = m[:, 4 * d:5 * d], m[:, 5 * d:6 * d]
    h = x_ref[0] + pos_ref[...]
    ab = jnp.dot(ys_ref[0], gw_ref[...], preferred_element_type=F32) + gb_ref[...]
    half = ab.shape[-1] // 2
    y5 = ab[:, :half] * jax.nn.sigmoid(ab[:, half:])
    mix = jnp.concatenate([_rms(y5, g5_ref[...]), _rms(yh_ref[0].astype(F32), gh_ref[...])], axis=-1)
    h = h + gate1 * jnp.dot(mix.astype(BF16), wo_ref[...], preferred_element_type=F32)
    hn = _rms(h, g2_ref[...]) * (1.0 + scale2) + shift2
    hid = jnp.dot(hn.astype(BF16), w1_ref[...], preferred_element_type=F32)
    hid = jnp.square(jnp.maximum(hid, 0.0))
    h = h + gate2 * jnp.dot(hid.astype(BF16), w2_ref[...], preferred_element_type=F32)
    o_ref[0] = _rms(h, gf_ref[...])


def _output_stage(x, pos, mod3, ys, yh, glu_w, glu_b, g5, gh, w_out, g2, w1, w2, gf, tm=512):
    bsz, n, d = x.shape
    dh = ys.shape[-1]
    one = pl.Buffered(1)
    const = lambda a: pl.BlockSpec(a.shape, lambda i, b: tuple(0 for _ in a.shape), pipeline_mode=one)
    row = lambda a: a.reshape(1, -1).astype(F32)
    glu_b, g5, gh, g2, gf = row(glu_b), row(g5), row(gh), row(g2), row(gf)
    return pl.pallas_call(
        functools.partial(_out_kernel, d=d),
        grid=(n // tm, bsz),
        in_specs=[pl.BlockSpec((1, tm, d), lambda i, b: (b, i, 0)),
                  pl.BlockSpec((tm, d), lambda i, b: (i, 0)),
                  pl.BlockSpec((1, 1, mod3.shape[-1]), lambda i, b: (b, 0, 0)),
                  pl.BlockSpec((1, tm, dh), lambda i, b: (b, i, 0)),
                  pl.BlockSpec((1, tm, dh), lambda i, b: (b, i, 0)),
                  const(glu_w), const(glu_b), const(g5), const(gh), const(w_out), const(g2),
                  const(w1), const(w2), const(gf)],
        out_specs=pl.BlockSpec((1, tm, d), lambda i, b: (b, i, 0)),
        out_shape=jax.ShapeDtypeStruct((bsz, n, d), F32),
        compiler_params=_cparams("arbitrary", "arbitrary"),
        name="mix_mlp_out",
    )(x, pos, mod3, ys, yh, glu_w, glu_b, g5, gh, w_out, g2, w1, w2, gf)


def _pos_table(n, d):
    rows = n // GRID_W
    row = np.repeat(np.arange(rows, dtype=np.float32), GRID_W)
    col = np.tile(np.arange(GRID_W, dtype=np.float32), rows)
    quarter = d // 4
    omega = (1.0 / (POS_BASE ** (np.arange(quarter, dtype=np.float32) / quarter))).astype(np.float32)

    def enc(p):
        ang = p[:, None] * omega[None, :]
        return np.concatenate([np.sin(ang), np.cos(ang)], axis=-1)

    return jnp.asarray(np.concatenate([enc(row), enc(col)], axis=-1).astype(np.float32))


def kernel(x, c, ctx, c_ctx, ada_w, ada_b, norm1_g, w_in, s5_a_re, s5_a_im, s5_log_step, s5_b_re,
           s5_b_im, s5_c_re, s5_c_im, s5_d, s5_glu_w, s5_glu_b, hy_conv_w, hy_conv_b, hy_f_w1,
           hy_f_b1, hy_f_w2, hy_f_b2, hy_f_freq, hy_f_w3, hy_decay, hy_bias, mix_g_s5, mix_g_hy,
           w_out, norm2_g, mlp_w1, mlp_w2, final_g):
    bsz, n, d = x.shape
    depth = ada_w.shape[0]
    d_s5 = s5_d.shape[-1]
    d_hy = hy_bias.shape[-1]
    nctx = ctx.shape[1]
    npair = bsz // 2
    nbk, na = FFT_NB, n // FFT_NB
    pos = _pos_table(n, d)
    tables = _dft_tables(n)

    assert depth == 1 and bsz % 2 == 0 and n % (FFT_NB * 8) == 0 and 2 * n == FFT_N
    mod_rows = 16
    c_rows = jnp.concatenate([c, c_ctx[None], jnp.zeros((mod_rows - bsz - 1, d), c.dtype)], axis=0)
    mod3 = _modulation(c_rows, ada_w[0], ada_b[0]).reshape(mod_rows, 1, N_MOD * d)

    w_in_b = w_in[0].astype(BF16)
    u, z = _project(x, pos, mod3, lambda b: b, norm1_g[0], w_in_b, (d_s5, 3 * d_hy), tm=512)
    (uc,) = _project(ctx, None, mod3, lambda b: bsz, norm1_g[0], w_in_b, (d_s5,), tm=nctx)

    m1, m2, m3, lam = _s5_tables(s5_a_re[0], s5_a_im[0], s5_log_step[0], s5_b_re[0], s5_b_im[0],
                                 s5_c_re[0], s5_c_im[0], s5_d[0])
    u_t = u.reshape(bsz, n // S5_T, S5_T, d_s5).transpose(2, 0, 1, 3)
    uc_t = uc.reshape(bsz, nctx // S5_T, S5_T, d_s5).transpose(2, 0, 1, 3)
    ys = _s5_scan(u_t, uc_t, m1, m2, m3, lam).transpose(1, 2, 0, 3).reshape(bsz, n, d_s5)

    kf = _filter_spectra(n, hy_f_w1[0], hy_f_b1[0], hy_f_w2[0], hy_f_b2[0], hy_f_freq[0],
                         hy_f_w3[0], hy_decay[0], tables)
    z5 = z.reshape(bsz, na, nbk, 3 * d_hy).transpose(0, 2, 1, 3).reshape(2, npair, nbk, na, 3 * d_hy)
    yh = _hyena(z5, hy_conv_w[0], hy_conv_b[0], hy_bias[0], kf, tables)
    yh = yh.reshape(bsz, nbk, na, d_hy).transpose(0, 2, 1, 3).reshape(bsz, n, d_hy)

    return _output_stage(x, pos, mod3, ys, yh, s5_glu_w[0].astype(BF16), s5_glu_b[0], mix_g_s5[0],
                         mix_g_hy[0], w_out[0].astype(BF16), norm2_g[0], mlp_w1[0].astype(BF16),
                         mlp_w2[0].astype(BF16), final_g)
```

```python
import functools
import math

import numpy as np
import jax
import jax.numpy as jnp
from jax import lax
from jax.experimental import pallas as pl
from jax.experimental.pallas import tpu as pltpu

F32 = jnp.float32
BF16 = jnp.bfloat16
HI = lax.Precision.HIGHEST

EPS = 1e-6
GRID_W = 64
POS_BASE = 10000.0
N_MOD = 6
S5_C = 16
S5_P = 64
S5_T = 16
HY_BANDS = 16
HY_ORDER = 2

FFT_N = 8192
FFT_NA = 128
FFT_NB = 64

LANES = 128
MXU_N = 256
SLAB = 16
RELAYOUT_ROWS = 256
VMEM_LIMIT = 56 * 1024 * 1024


def _cparams(*sem):
    return pltpu.CompilerParams(dimension_semantics=sem, vmem_limit_bytes=VMEM_LIMIT)


def _rms(x, g):
    return x * lax.rsqrt(jnp.mean(x * x, axis=-1, keepdims=True) + EPS) * g


def _mod_kernel(c_ref, w_ref, b_ref, o_ref):
    c = c_ref[...]
    a = c * jax.nn.sigmoid(c)
    o_ref[...] = jnp.dot(a.astype(BF16), w_ref[...].astype(BF16),
                         preferred_element_type=F32) + b_ref[...]


def _modulation(c_rows, ada_w, ada_b):
    rows, d = c_rows.shape
    n = ada_w.shape[1]
    bn = 1024
    return pl.pallas_call(
        _mod_kernel,
        grid=(n // bn,),
        in_specs=[pl.BlockSpec((rows, d), lambda j: (0, 0)),
                  pl.BlockSpec((d, bn), lambda j: (0, j)),
                  pl.BlockSpec((1, bn), lambda j: (0, j))],
        out_specs=pl.BlockSpec((rows, bn), lambda j: (0, j)),
        out_shape=jax.ShapeDtypeStruct((rows, n), F32),
        compiler_params=_cparams("arbitrary"),
        name="ada_mod",
    )(c_rows, ada_w, ada_b.reshape(1, n))


def _proj_kernel(*refs, d, use_pos, n_split):
    if use_pos:
        x_ref, pos_ref, mod_ref, g_ref, w_ref = refs[:5]
        outs = refs[5:]
        h = x_ref[0] + pos_ref[...]
    else:
        x_ref, mod_ref, g_ref, w_ref = refs[:4]
        outs = refs[4:]
        h = x_ref[0]
    m = mod_ref[0]
    hn = _rms(h, g_ref[...]) * (1.0 + m[:, d:2 * d]) + m[:, 0:d]
    proj = jnp.dot(hn.astype(BF16), w_ref[...], preferred_element_type=F32)
    off = 0
    for o_ref, width in zip(outs, n_split):
        o_ref[0] = proj[:, off:off + width].astype(o_ref.dtype)
        off += width


def _project(x, pos, mod3, mod_row_of_batch, g, w_bf16, n_split, tm):
    bsz, n, d = x.shape
    width = sum(n_split)
    use_pos = pos is not None
    in_specs = [pl.BlockSpec((1, tm, d), lambda i, b: (b, i, 0))]
    args = [x]
    if use_pos:
        in_specs.append(pl.BlockSpec((tm, d), lambda i, b: (i, 0)))
        args.append(pos)
    in_specs += [pl.BlockSpec((1, 1, mod3.shape[-1]), lambda i, b: (mod_row_of_batch(b), 0, 0)),
                 pl.BlockSpec((1, d), lambda i, b: (0, 0)),
                 pl.BlockSpec((d, width), lambda i, b: (0, 0))]
    args += [mod3, g.reshape(1, d), w_bf16]
    return pl.pallas_call(
        functools.partial(_proj_kernel, d=d, use_pos=use_pos, n_split=n_split),
        grid=(n // tm, bsz),
        in_specs=in_specs,
        out_specs=[pl.BlockSpec((1, tm, wd), lambda i, b: (b, i, 0)) for wd in n_split],
        out_shape=[jax.ShapeDtypeStruct((bsz, n, wd), BF16) for wd in n_split],
        compiler_params=_cparams("arbitrary", "arbitrary"),
        name="norm_proj",
    )(*args)


def _s5_tables(a_re, a_im, log_step, b_re, b_im, c_re, c_im, d_skip):
    T, C, P = S5_T, S5_C, S5_P
    G = a_re.shape[1]
    gpb = LANES // C
    slots = np.arange(T)
    tok = np.stack([8 * (slots // 8) + (slots % 8 - o) % 8 for o in range(gpb)])
    tok_g = tok[np.arange(G) % gpb]

    step = jnp.exp(log_step.astype(F32))[..., None]
    ar = a_re.astype(F32) * step
    ai = a_im.astype(F32) * step

    def powers(d, expo):
        e = jnp.asarray(np.broadcast_to(expo, (G, expo.shape[-1])).astype(np.float32))[:, :, None]
        mag = jnp.exp(ar[d][:, None, :] * e)
        return mag * jnp.cos(ai[d][:, None, :] * e), mag * jnp.sin(ai[d][:, None, :] * e)

    lr, li = a_re.astype(F32), a_im.astype(F32)
    first = [powers(d, np.ones(1)) for d in range(2)]
    nr = jnp.stack([first[d][0][:, 0] for d in range(2)]) - 1.0
    ni = jnp.stack([first[d][1][:, 0] for d in range(2)])
    den = lr * lr + li * li
    qr = (nr * lr + ni * li) / den
    qi = (ni * lr - nr * li) / den
    bbr = qr[..., None] * b_re - qi[..., None] * b_im
    bbi = qr[..., None] * b_im + qi[..., None] * b_re
    cr, ci = c_re.astype(F32), c_im.astype(F32)

    kerns = []
    for d in range(2):
        pr, pi = powers(d, np.arange(T))
        wr = cr[d][:, None] * pr[:, :, None, :] - ci[d][:, None] * pi[:, :, None, :]
        wi = cr[d][:, None] * pi[:, :, None, :] + ci[d][:, None] * pr[:, :, None, :]
        kerns.append(jnp.einsum('gtcp,gpe->gtce', wr, bbr[d], precision=HI)
                     - jnp.einsum('gtcp,gpe->gtce', wi, bbi[d], precision=HI))
    kcat = jnp.concatenate(kerns, axis=1)
    kt = jnp.tile(kcat.transpose(0, 1, 3, 2).reshape(G, 2 * T * C, C), (1, 1, T))
    lag = np.repeat(tok[:, None, :] - tok[:, :, None], C, axis=2).astype(np.int32)
    d_lanes = jnp.tile(d_skip.astype(F32).reshape(G, 1, C), (1, 1, T))
    m1 = pl.pallas_call(
        _s5_intra_kernel,
        grid=(G,),
        in_specs=[pl.BlockSpec((1, 2 * T * C, T * C), lambda g: (g, 0, 0)),
                  pl.BlockSpec((1, T, T * C), lambda g: (g % gpb, 0, 0)),
                  pl.BlockSpec((1, 1, T * C), lambda g: (g, 0, 0))],
        out_specs=pl.BlockSpec((1, T * C, T * C), lambda g: (g, 0, 0)),
        out_shape=jax.ShapeDtypeStruct((G, T * C, T * C), BF16),
        compiler_params=_cparams("arbitrary"),
        name="s5_intra_table",
    )(kt, jnp.asarray(lag), d_lanes)

    def in_mat(d, expo):
        p_r, p_i = powers(d, expo)
        br, bi = bbr[d].transpose(0, 2, 1)[:, None], bbi[d].transpose(0, 2, 1)[:, None]
        er = p_r[:, :, None, :] * br - p_i[:, :, None, :] * bi
        ei = p_r[:, :, None, :] * bi + p_i[:, :, None, :] * br
        return er.reshape(G, T * C, P), ei.reshape(G, T * C, P)

    efr, efi = in_mat(0, T - 1 - tok_g)
    ebr, ebi = in_mat(1, tok_g)
    m2 = jnp.concatenate([efr, ebr, efi, ebi], axis=-1)

    def out_mat(d, expo):
        p_r, p_i = powers(d, expo)
        crt = cr[d].transpose(0, 2, 1)[:, :, None, :]
        cit = ci[d].transpose(0, 2, 1)[:, :, None, :]
        prt = p_r.transpose(0, 2, 1)[:, :, :, None]
        pit = p_i.transpose(0, 2, 1)[:, :, :, None]
        vr = crt * prt - cit * pit
        vi = crt * pit + cit * prt
        return vr.reshape(G, P, T * C), -vi.reshape(G, P, T * C)

    vfr, vfi = out_mat(0, tok_g + 1)
    vbr, vbi = out_mat(1, T - tok_g)
    m3 = jnp.concatenate([vfr, vbr, vfi, vbi], axis=1)

    last = [powers(d, np.full(1, T)) for d in range(2)]
    lam = jnp.stack([jnp.concatenate([last[0][c][:, 0], last[1][c][:, 0]], axis=-1)
                     for c in range(2)], axis=1)
    is_fwd = jnp.asarray((np.arange(4 * P) // P) % 2 == 0)[None, :, None]
    m3f = jnp.where(is_fwd, m3, 0.0).astype(BF16)
    m3b = jnp.where(is_fwd, 0.0, m3).astype(BF16)
    return m1, m2.astype(BF16), m3f, m3b, lam


def _s5_intra_kernel(kt_ref, lag_ref, d_ref, o_ref):
    T, C = S5_T, S5_C
    block = lambda x: kt_ref[0, x * C:(x + 1) * C, :]
    row = lax.broadcasted_iota(jnp.int32, (C, T * C), 0)
    lane_c = lax.broadcasted_iota(jnp.int32, (C, T * C), 1) % C
    centre = block(0) + block(T) + jnp.where(row == lane_c, d_ref[0], 0.0)
    for lp in range(T):
        lag = lag_ref[0, lp:lp + 1, :]
        acc = centre
        for x in range(1, T):
            acc = jnp.where(lag == x, block(x), acc)
            acc = jnp.where(lag == -x, block(T + x), acc)
        o_ref[0, lp * C:(lp + 1) * C, :] = acc.astype(BF16)


def _gelu_tanh(x):
    return 0.5 * x * (1.0 + jnp.tanh(math.sqrt(2.0 / math.pi) * (x + 0.044715 * (x * x * x))))


def _s5_kernel(u_ref, uc_ref, m1_ref, m2_ref, m3f_ref, m3b_ref, lam_ref, o_ref,
               z_ref, e_ref, sf_ref, sb_ref, *, nbatch, nchunk, nctx):
    groups = LANES // S5_C
    rows = nbatch * nchunk
    lane = lax.broadcasted_iota(jnp.int32, (1, LANES), 1)
    slot_bits = [(((lane // S5_C) >> j) & 1) == 1 for j in range(3)]

    def skew(xs):
        cur = [xs[(-k) % groups] for k in range(groups)]
        for j, bit in enumerate(slot_bits):
            cur = [jnp.where(bit, cur[(i - (1 << j)) % groups], cur[i]) for i in range(groups)]
        return cur

    def to_chunk_rows(read_t, nrows, row0):
        for h in range(2):
            rolled = []
            for t8 in range(groups):
                v = read_t(8 * h + t8).astype(F32)
                rolled.append(pltpu.roll(v, t8 * S5_C, axis=1) if t8 else v)
            for g, zg in enumerate(skew(rolled)):
                z_ref[g, row0:row0 + nrows, h * LANES:(h + 1) * LANES] = zg.astype(BF16)

    for r0 in range(0, rows, RELAYOUT_ROWS):
        to_chunk_rows(lambda t, r0=r0: u_ref[t, r0:r0 + RELAYOUT_ROWS], RELAYOUT_ROWS, r0)
    to_chunk_rows(lambda t: uc_ref[t], nbatch * nctx, rows)

    wide = 2 * LANES
    fwd_lane = (lax.broadcasted_iota(jnp.int32, (1, wide), 1) % LANES) < (LANES // 2)
    tile = lambda k: pl.ds(pl.multiple_of(k * nbatch, nbatch), nbatch)

    for g in range(groups):
        zg = z_ref[g]
        e_ref[...] = jnp.dot(zg, m2_ref[g], preferred_element_type=F32)
        lam_r = lam_ref[g, 0:1, :]
        lam_i = lam_ref[g, 1:2, :]

        def advance(sr, si, kf, kb):
            e2 = jnp.where(fwd_lane, e_ref[tile(kf), :], e_ref[tile(kb), :])
            return (lam_r * sr - lam_i * si + e2[:, :LANES], lam_r * si + lam_i * sr + e2[:, LANES:])

        sr = jnp.zeros((nbatch, LANES), F32)
        si = jnp.zeros((nbatch, LANES), F32)
        for i in range(nctx):
            sr, si = advance(sr, si, nchunk + i, nchunk + nctx - 1 - i)

        def body(i, carry):
            sr, si = carry
            kf, kb = i, nchunk - 1 - i
            s = jnp.concatenate([sr, si], axis=-1)
            sf_ref[tile(kf), :] = s
            sb_ref[tile(kb), :] = s
            return advance(sr, si, kf, kb)

        lax.fori_loop(0, nchunk, body, (sr, si), unroll=2)

        y = (jnp.dot(zg[:rows], m1_ref[g], preferred_element_type=F32)
             + jnp.dot(sf_ref[...].astype(BF16), m3f_ref[g], preferred_element_type=F32)
             + jnp.dot(sb_ref[...].astype(BF16), m3b_ref[g], preferred_element_type=F32))
        z_ref[g, 0:rows, :] = _gelu_tanh(y).astype(BF16)

    for r0 in range(0, rows, RELAYOUT_ROWS):
        for h in range(2):
            ys = [z_ref[g, r0:r0 + RELAYOUT_ROWS, h * LANES:(h + 1) * LANES].astype(F32)
                  for g in range(groups)]
            for t8, acc in enumerate(skew(ys)):
                if t8:
                    acc = pltpu.roll(acc, LANES - t8 * S5_C, axis=1)
                o_ref[8 * h + t8, r0:r0 + RELAYOUT_ROWS] = acc.astype(BF16)


def _s5_scan(u_t, uc_t, bsz, m1, m2, m3f, m3b, lam):
    T, rows, width = u_t.shape
    ctx_rows = uc_t.shape[1]
    gpb = LANES // S5_C
    kern = functools.partial(_s5_kernel, nbatch=bsz, nchunk=rows // bsz, nctx=ctx_rows // bsz)
    one = pl.Buffered(1)
    mat = pl.BlockSpec((gpb, 2 * LANES, 2 * LANES), lambda j: (j, 0, 0))
    return pl.pallas_call(
        kern,
        grid=(width // LANES,),
        in_specs=[pl.BlockSpec((T, rows, LANES), lambda j: (0, 0, j), pipeline_mode=one),
                  pl.BlockSpec((T, ctx_rows, LANES), lambda j: (0, 0, j)),
                  mat, mat, mat, mat,
                  pl.BlockSpec((gpb, 2, LANES), lambda j: (j, 0, 0))],
        out_specs=pl.BlockSpec((T, rows, LANES), lambda j: (0, 0, j), pipeline_mode=one),
        out_shape=jax.ShapeDtypeStruct(u_t.shape, BF16),
        scratch_shapes=[pltpu.VMEM((gpb, rows + ctx_rows, 2 * LANES), BF16),
                        pltpu.VMEM((rows + ctx_rows, 2 * LANES), F32),
                        pltpu.VMEM((rows, 2 * LANES), F32),
                        pltpu.VMEM((rows, 2 * LANES), F32)],
        compiler_params=_cparams("arbitrary"),
        name="s5_scan",
    )(u_t, uc_t, m1, m2, m3f, m3b, lam)


def _dft_tables(n_seq):
    na_sig = n_seq // FFT_NB
    ka = np.arange(FFT_NA)[:, None]
    nb = np.arange(FFT_NB)[:, None, None]

    def stage1(n_in):
        na = np.arange(n_in)[None, :]
        ang = -2.0 * np.pi * (na * ka / FFT_NA)[None] - 2.0 * np.pi * (nb * ka[None]) / FFT_N
        return np.cos(ang), np.sin(ang)

    c, s = stage1(na_sig)
    f1 = np.concatenate([np.concatenate([c, -s], axis=2), np.concatenate([s, c], axis=2)], axis=1)
    c, s = stage1(FFT_NA)
    f1_real = np.concatenate([c, s], axis=1)
    kb = np.arange(FFT_NB)[:, None]
    nbv = np.arange(FFT_NB)[None, :]
    ang = -2.0 * np.pi * kb * nbv / FFT_NB
    c, s = np.cos(ang), np.sin(ang)
    f2 = np.block([[c, -s], [s, c]])
    f2_inv = np.block([[c, s], [-s, c]]) / FFT_NB
    nap = np.arange(na_sig)[:, None]
    kav = np.arange(FFT_NA)[None, :]
    ang = 2.0 * np.pi * (nap * kav / FFT_NA)[None] + 2.0 * np.pi * (nb * kav[None]) / FFT_N
    c, s = np.cos(ang) / FFT_NA, np.sin(ang) / FFT_NA
    f3 = np.concatenate([np.concatenate([c, -s], axis=2), np.concatenate([s, c], axis=2)], axis=1)
    as_bf16 = lambda a: jnp.asarray(a.astype(np.float32)).astype(BF16)
    return as_bf16(f1), as_bf16(f1_real), as_bf16(f2), as_bf16(f2_inv), as_bf16(f3)


def _conv3_rows(z_ref, plane, nb, w, b, nbk):
    def rows(i):
        return z_ref[plane, 0, i].astype(F32)

    cur = rows(nb)
    na = cur.shape[0]
    ridx = lax.broadcasted_iota(jnp.int32, cur.shape, 0)
    if isinstance(nb, int) and nb == 0:
        prev = jnp.where(ridx == 0, 0.0, pltpu.roll(rows(nbk - 1), 1, axis=0))
    else:
        prev = rows(nb - 1)
    if isinstance(nb, int) and nb == nbk - 1:
        nxt = jnp.where(ridx == na - 1, 0.0, pltpu.roll(rows(0), na - 1, axis=0))
    else:
        nxt = rows(nb + 1)
    return prev * w[0:1] + cur * w[1:2] + nxt * w[2:3] + b


def _for_row_blocks(nbk, fn, edges_differ):
    peel = 2 if edges_differ else 0
    for i in range(peel):
        fn(i)
    lax.fori_loop(peel, nbk - peel, lambda i, c: (fn(i), c)[1], 0, unroll=4)
    for i in range(nbk - peel, nbk):
        fn(i)


def _regroup(x):
    return jnp.swapaxes(x, 0, 1)


def _hyena_kernel(zv_ref, zg1_ref, zg2_ref, wv_ref, bv_ref, wg1_ref, bg1_ref, wg2_ref, bg2_ref,
                  bias_ref, kf_ref, f1_ref, f2_ref, f2i_ref, f3_ref, o_ref, s_ref, v_ref, z1_ref,
                  *, nbk, na, ngrp, kag):
    step = pl.program_id(2)
    hb = FFT_NB

    def stage1(src_ref, conv):
        def body(nb):
            parts = []
            for pln in range(2):
                if conv:
                    v = _conv3_rows(src_ref, pln, nb, wv_ref[...], bv_ref[...], nbk).astype(BF16)
                    v_ref[pln, 0, nb] = v
                else:
                    v = src_ref[pln, 0, nb]
                parts.append(v)
            rhs = jnp.concatenate(parts, axis=0)
            s_ref[nb] = jnp.dot(f1_ref[nb], rhs, preferred_element_type=F32).astype(BF16)
        _for_row_blocks(nbk, body, conv)

    def stage3(gate_ref, wg, bg, vsrc_ref, bias, dst_ref):
        def body(nb):
            y = jnp.dot(f3_ref[nb], s_ref[nb], preferred_element_type=F32)
            for pln in range(2):
                gate = _conv3_rows(gate_ref, pln, nb, wg, bg, nbk)
                out = gate * (y[pln * na:(pln + 1) * na] + bias * vsrc_ref[pln, 0, nb].astype(F32))
                dst_ref[pln, 0, nb] = out.astype(dst_ref.dtype)
        _for_row_blocks(nbk, body, True)

    @pl.when(step == 0)
    def _():
        stage1(zv_ref, True)

    @pl.when(step == ngrp)
    def _():
        stage1(z1_ref, False)

    starts = [pl.multiple_of((step % ngrp) * kag + slab * SLAB, SLAB) for slab in range(kag // SLAB)]
    loaded = [(_regroup(s_ref[:, pl.ds(r0, SLAB), :]),
               _regroup(s_ref[:, pl.ds(FFT_NA + r0, SLAB), :])) for r0 in starts]
    results = []
    for slab, (re_t, im_t) in enumerate(loaded):
        back_re, back_im = [], []
        for k in range(SLAB):
            rhs = jnp.concatenate([re_t[k], im_t[k]], axis=0)
            spec = jnp.dot(f2_ref[...], rhs, preferred_element_type=F32).astype(BF16)
            kf = kf_ref[0, slab * SLAB + k]
            sr, si = spec[:hb], spec[hb:]
            kr, ki = kf[:hb], kf[hb:]
            prod = jnp.concatenate([sr * kr - si * ki, sr * ki + si * kr], axis=0)
            back = jnp.dot(f2i_ref[...], prod, preferred_element_type=F32).astype(BF16)
            back_re.append(back[:hb])
            back_im.append(back[hb:])
        results.append((_regroup(jnp.stack(back_re, axis=0)), _regroup(jnp.stack(back_im, axis=0))))
    for r0, (out_re, out_im) in zip(starts, results):
        s_ref[:, pl.ds(r0, SLAB), :] = out_re
        s_ref[:, pl.ds(FFT_NA + r0, SLAB), :] = out_im

    @pl.when(step == ngrp - 1)
    def _():
        stage3(zg1_ref, wg1_ref[...], bg1_ref[...], v_ref, bias_ref[0:1, :], z1_ref)

    @pl.when(step == 2 * ngrp - 1)
    def _():
        stage3(zg2_ref, wg2_ref[...], bg2_ref[...], z1_ref, bias_ref[1:2, :], o_ref)


def _hyena(z5, conv_w, conv_b, bias, kf, tables, cb=MXU_N, kag=2 * SLAB):
    f1, _, f2, f2i, f3 = tables
    _, npair, nbk, na, _ = z5.shape
    c = bias.shape[-1]
    ncb = c // cb
    ngrp = FFT_NA // kag
    one = pl.Buffered(1)
    zspec = lambda grp: pl.BlockSpec((2, 1, nbk, na, cb), lambda p, j, s: (0, p, 0, 0, grp * ncb + j),
                                     pipeline_mode=one)
    wspec = lambda grp: pl.BlockSpec((3, cb), lambda p, j, s: (0, grp * ncb + j))
    bspec = lambda grp: pl.BlockSpec((1, cb), lambda p, j, s: (0, grp * ncb + j))
    const = lambda a: pl.BlockSpec(a.shape, lambda p, j, s: tuple(0 for _ in a.shape), pipeline_mode=one)
    cw = conv_w.astype(F32)
    cbias = conv_b.astype(F32).reshape(1, -1)
    return pl.pallas_call(
        functools.partial(_hyena_kernel, nbk=nbk, na=na, ngrp=ngrp, kag=kag),
        grid=(npair, ncb, 2 * ngrp),
        in_specs=[zspec(0), zspec(1), zspec(2), wspec(0), bspec(0), wspec(1), bspec(1), wspec(2), bspec(2),
                  pl.BlockSpec((HY_ORDER, cb), lambda p, j, s: (0, j)),
                  pl.BlockSpec((1, kag, 2 * FFT_NB, cb), lambda p, j, s: (s // ngrp, s % ngrp, 0, j)),
                  const(f1), const(f2), const(f2i), const(f3)],
        out_specs=pl.BlockSpec((2, 1, nbk, na, cb), lambda p, j, s: (0, p, 0, 0, j), pipeline_mode=one),
        out_shape=jax.ShapeDtypeStruct((2, npair, nbk, na, c), BF16),
        scratch_shapes=[pltpu.VMEM((nbk, 2 * FFT_NA, cb), BF16),
                        pltpu.VMEM((2, 1, nbk, na, cb), BF16),
                        pltpu.VMEM((2, 1, nbk, na, cb), BF16)],
        compiler_params=_cparams("arbitrary", "arbitrary", "arbitrary"),
        name="hyena_conv",
    )(z5, z5, z5, cw, cbias, cw, cbias, cw, cbias, bias.astype(F32), kf, f1, f2, f2i, f3)


def _filter_kernel(embt_ref, tv_ref, w1t_ref, b1_ref, w2t_ref, b2_ref, fr_ref, w3_ref, dec_ref, f1_ref,
                   f2_ref, k_ref, s_ref, h_ref, *, nbk, kag):
    first = (pl.program_id(0) == 0) & (pl.program_id(1) == 0)
    step = pl.program_id(2)
    half = FFT_NA // 2
    lanes_per_pass = 8 * FFT_NA

    @pl.when(first & (step == 0))
    def _():
        fr = fr_ref[...]
        for i in range(embt_ref.shape[1] // lanes_per_pass):
            cols = slice(i * lanes_per_pass, (i + 1) * lanes_per_pass)
            h = jnp.sin(fr * (jnp.dot(w1t_ref[...], embt_ref[:, cols], precision=HI,
                                      preferred_element_type=F32) + b1_ref[...]))
            h_ref[:, cols] = jnp.sin(fr * (jnp.dot(w2t_ref[...], h, precision=HI,
                                                   preferred_element_type=F32) + b2_ref[...]))

    @pl.when(step == 0)
    def _():
        dec = dec_ref[0]

        def body(nb, carry):
            h = h_ref[:, pl.ds(pl.multiple_of(nb * FFT_NA, FFT_NA), FFT_NA)].T.astype(BF16)
            fwd = jnp.dot(h[:half], w3_ref[0, 0], preferred_element_type=F32)
            bwd = jnp.dot(h[half:], w3_ref[0, 1], preferred_element_type=F32)
            tv = tv_ref[nb]
            window = jnp.exp(-tv[:, 0:1] * dec) * tv[:, 1:2]
            filt = jnp.concatenate([fwd, bwd], axis=0) * window
            s_ref[nb] = jnp.dot(f1_ref[nb], filt.astype(BF16), preferred_element_type=F32).astype(BF16)
            return carry
        lax.fori_loop(0, nbk, body, 0, unroll=2)

    r0 = pl.multiple_of(step * kag, kag)
    re_t = _regroup(s_ref[:, pl.ds(r0, kag), :])
    im_t = _regroup(s_ref[:, pl.ds(FFT_NA + r0, kag), :])
    for k in range(kag):
        rhs = jnp.concatenate([re_t[k], im_t[k]], axis=0)
        k_ref[0, k] = jnp.dot(f2_ref[...], rhs, preferred_element_type=F32).astype(BF16)


def _filter_spectra(n, w1, b1, w2, b2, freq, w3, decay, tables, cb=MXU_N, kag=16):
    _, f1_real, f2, _, _ = tables
    hid = w2.shape[0]
    c = decay.shape[-1]
    t = np.linspace(0.0, 1.0, n, dtype=np.float32)[:, None]
    w = (2.0 * np.pi * np.arange(n, dtype=np.float32) / n).astype(np.float32)
    bands = np.linspace(1e-4, HY_BANDS - 1, HY_BANDS, dtype=np.float32)
    ang = w[:, None] * bands[None, :]
    emb = np.concatenate([t, np.cos(ang), -np.sin(ang)], axis=-1).astype(np.float32)
    kpad = 64
    idx = FFT_NB * np.arange(FFT_NA)[None, :] + np.arange(FFT_NB)[:, None]
    lagi = np.minimum(np.where(idx < n, idx, FFT_N - idx), n - 1)
    embt = np.zeros((kpad, FFT_N), np.float32)
    embt[:emb.shape[1]] = emb[lagi.reshape(-1)].T
    tv = np.stack([t[lagi, 0], (idx != n).astype(np.float32)], axis=-1)
    col = lambda a: a.reshape(hid, 1).astype(F32)
    w1t = jnp.pad(w1.astype(F32), ((0, kpad - w1.shape[0]), (0, 0))).T
    w3r = w3.astype(BF16).reshape(hid, HY_ORDER, 2, c).transpose(1, 2, 0, 3)
    dec = jnp.abs(decay.astype(F32)).reshape(HY_ORDER, 1, c)
    one = pl.Buffered(1)
    const = lambda shape: pl.BlockSpec(shape, lambda o, j, s: tuple(0 for _ in shape), pipeline_mode=one)
    return pl.pallas_call(
        functools.partial(_filter_kernel, nbk=FFT_NB, kag=kag),
        grid=(HY_ORDER, c // cb, FFT_NA // kag),
        in_specs=[const(embt.shape), const(tv.shape), const((hid, kpad)), const((hid, 1)),
                  const((hid, hid)), const((hid, 1)), const((hid, 1)),
                  pl.BlockSpec((1, 2, hid, cb), lambda o, j, s: (o, 0, 0, j)),
                  pl.BlockSpec((1, 1, cb), lambda o, j, s: (o, 0, j)),
                  const(f1_real.shape), const(f2.shape)],
        out_specs=pl.BlockSpec((1, kag, 2 * FFT_NB, cb), lambda o, j, s: (o, s, 0, j)),
        out_shape=jax.ShapeDtypeStruct((HY_ORDER, FFT_NA, 2 * FFT_NB, c), BF16),
        scratch_shapes=[pltpu.VMEM((FFT_NB, 2 * FFT_NA, cb), BF16),
                        pltpu.VMEM((hid, FFT_N), F32)],
        compiler_params=_cparams("arbitrary", "arbitrary", "arbitrary"),
        name="hyena_filter",
    )(jnp.asarray(embt), jnp.asarray(tv), w1t, col(b1), w2.astype(F32).T, col(b2), col(freq), w3r, dec,
      f1_real, f2)


def _out_kernel(x_ref, pos_ref, mod_ref, ys_ref, yh_ref, gw_ref, gb_ref, g5_ref, gh_ref, wo_ref,
                g2_ref, w1_ref, w2_ref, gf_ref, o_ref, *, d):
    m = mod_ref[0]
    gate1, shift2 = m[:, 2 * d:3 * d], m[:, 3 * d:4 * d]
    scale2, gate2 = m[:, 4 * d:5 * d], m[:, 5 * d:6 * d]
    h = x_ref[0] + pos_ref[...]
    ab = jnp.dot(ys_ref[0], gw_ref[...], preferred_element_type=F32) + gb_ref[...]
    half = ab.shape[-1] // 2
    y5 = ab[:, :half] * jax.nn.sigmoid(ab[:, half:])
    mix = jnp.concatenate([_rms(y5, g5_ref[...]), _rms(yh_ref[0].astype(F32), gh_ref[...])], axis=-1)
    h = h + gate1 * jnp.dot(mix.astype(BF16), wo_ref[...], preferred_element_type=F32)
    hn = _rms(h, g2_ref[...]) * (1.0 + scale2) + shift2
    hid = jnp.dot(hn.astype(BF16), w1_ref[...], preferred_element_type=F32)
    hid = jnp.square(jnp.maximum(hid, 0.0))
    h = h + gate2 * jnp.dot(hid.astype(BF16), w2_ref[...], preferred_element_type=F32)
    o_ref[0] = _rms(h, gf_ref[...])


def _output_stage(x, pos, mod3, ys, yh, glu_w, glu_b, g5, gh, w_out, g2, w1, w2, gf, tm=512):
    bsz, n, d = x.shape
    dh = ys.shape[-1]
    one = pl.Buffered(1)
    const = lambda a: pl.BlockSpec(a.shape, lambda i, b: tuple(0 for _ in a.shape), pipeline_mode=one)
    row = lambda a: a.reshape(1, -1).astype(F32)
    glu_b, g5, gh, g2, gf = row(glu_b), row(g5), row(gh), row(g2), row(gf)
    return pl.pallas_call(
        functools.partial(_out_kernel, d=d),
        grid=(n // tm, bsz),
        in_specs=[pl.BlockSpec((1, tm, d), lambda i, b: (b, i, 0)),
                  pl.BlockSpec((tm, d), lambda i, b: (i, 0)),
                  pl.BlockSpec((1, 1, mod3.shape[-1]), lambda i, b: (b, 0, 0)),
                  pl.BlockSpec((1, tm, dh), lambda i, b: (b, i, 0)),
                  pl.BlockSpec((1, tm, dh), lambda i, b: (b, i, 0)),
                  const(glu_w), const(glu_b), const(g5), const(gh), const(w_out), const(g2),
                  const(w1), const(w2), const(gf)],
        out_specs=pl.BlockSpec((1, tm, d), lambda i, b: (b, i, 0)),
        out_shape=jax.ShapeDtypeStruct((bsz, n, d), F32),
        compiler_params=_cparams("arbitrary", "arbitrary"),
        name="mix_mlp_out",
    )(x, pos, mod3, ys, yh, glu_w, glu_b, g5, gh, w_out, g2, w1, w2, gf)


def _pos_table(n, d):
    rows = n // GRID_W
    row = np.repeat(np.arange(rows, dtype=np.float32), GRID_W)
    col = np.tile(np.arange(GRID_W, dtype=np.float32), rows)
    quarter = d // 4
    omega = (1.0 / (POS_BASE ** (np.arange(quarter, dtype=np.float32) / quarter))).astype(np.float32)

    def enc(p):
        ang = p[:, None] * omega[None, :]
        return np.concatenate([np.sin(ang), np.cos(ang)], axis=-1)

    return jnp.asarray(np.concatenate([enc(row), enc(col)], axis=-1).astype(np.float32))


def kernel(x, c, ctx, c_ctx, ada_w, ada_b, norm1_g, w_in, s5_a_re, s5_a_im, s5_log_step, s5_b_re,
           s5_b_im, s5_c_re, s5_c_im, s5_d, s5_glu_w, s5_glu_b, hy_conv_w, hy_conv_b, hy_f_w1,
           hy_f_b1, hy_f_w2, hy_f_b2, hy_f_freq, hy_f_w3, hy_decay, hy_bias, mix_g_s5, mix_g_hy,
           w_out, norm2_g, mlp_w1, mlp_w2, final_g):
    bsz, n, d = x.shape
    depth = ada_w.shape[0]
    d_s5 = s5_d.shape[-1]
    d_hy = hy_bias.shape[-1]
    nctx = ctx.shape[1]
    npair = bsz // 2
    nbk, na = FFT_NB, n // FFT_NB
    pos = _pos_table(n, d)
    tables = _dft_tables(n)

    assert depth == 1 and bsz % 2 == 0 and n % (FFT_NB * 8) == 0 and 2 * n == FFT_N
    mod_rows = 16
    c_rows = jnp.concatenate([c, c_ctx[None], jnp.zeros((mod_rows - bsz - 1, d), c.dtype)], axis=0)
    mod3 = _modulation(c_rows, ada_w[0], ada_b[0]).reshape(mod_rows, 1, N_MOD * d)

    w_in_b = w_in[0].astype(BF16)
    u, z = _project(x, pos, mod3, lambda b: b, norm1_g[0], w_in_b, (d_s5, 3 * d_hy), tm=512)
    (uc,) = _project(ctx, None, mod3, lambda b: bsz, norm1_g[0], w_in_b, (d_s5,), tm=nctx)

    s5_mats = _s5_tables(s5_a_re[0], s5_a_im[0], s5_log_step[0], s5_b_re[0], s5_b_im[0],
                         s5_c_re[0], s5_c_im[0], s5_d[0])
    u_t = u.reshape(bsz, n // S5_T, S5_T, d_s5).transpose(2, 1, 0, 3).reshape(S5_T, -1, d_s5)
    uc_t = uc.reshape(bsz, nctx // S5_T, S5_T, d_s5).transpose(2, 1, 0, 3).reshape(S5_T, -1, d_s5)
    ys = _s5_scan(u_t, uc_t, bsz, *s5_mats)
    ys = ys.reshape(S5_T, n // S5_T, bsz, d_s5).transpose(2, 1, 0, 3).reshape(bsz, n, d_s5)

    kf = _filter_spectra(n, hy_f_w1[0], hy_f_b1[0], hy_f_w2[0], hy_f_b2[0], hy_f_freq[0],
                         hy_f_w3[0], hy_decay[0], tables)
    z5 = z.reshape(bsz, na, nbk, 3 * d_hy).transpose(0, 2, 1, 3).reshape(2, npair, nbk, na, 3 * d_hy)
    yh = _hyena(z5, hy_conv_w[0], hy_conv_b[0], hy_bias[0], kf, tables)
    yh = yh.reshape(bsz, nbk, na, d_hy).transpose(0, 2, 1, 3).reshape(bsz, n, d_hy)

    return _output_stage(x, pos, mod3, ys, yh, s5_glu_w[0].astype(BF16), s5_glu_b[0], mix_g_s5[0],
                         mix_g_hy[0], w_out[0].astype(BF16), norm2_g[0], mlp_w1[0].astype(BF16),
                         mlp_w2[0].astype(BF16), final_g)
```

```python
import functools
import math

import numpy as np
import jax
import jax.numpy as jnp
from jax import lax
from jax.experimental import pallas as pl
from jax.experimental.pallas import tpu as pltpu

F32 = jnp.float32
BF16 = jnp.bfloat16
HI = lax.Precision.HIGHEST

EPS = 1e-6
GRID_W = 64
POS_BASE = 10000.0
N_MOD = 6
S5_C = 16
S5_P = 64
S5_T = 16
HY_BANDS = 16
HY_ORDER = 2

FFT_N = 8192
FFT_NA = 128
FFT_NB = 64

LANES = 128
MXU_N = 256
SLAB = 16
RELAYOUT_ROWS = 256
VMEM_LIMIT = 56 * 1024 * 1024


def _cparams(*sem):
    return pltpu.CompilerParams(dimension_semantics=sem, vmem_limit_bytes=VMEM_LIMIT)


def _rms(x, g):
    return x * lax.rsqrt(jnp.mean(x * x, axis=-1, keepdims=True) + EPS) * g


def _mod_kernel(c_ref, w_ref, b_ref, o_ref):
    c = c_ref[...]
    a = c * jax.nn.sigmoid(c)
    o_ref[...] = jnp.dot(a.astype(BF16), w_ref[...].astype(BF16),
                         preferred_element_type=F32) + b_ref[...]


def _modulation(c_rows, ada_w, ada_b):
    rows, d = c_rows.shape
    n = ada_w.shape[1]
    bn = 1024
    return pl.pallas_call(
        _mod_kernel,
        grid=(n // bn,),
        in_specs=[pl.BlockSpec((rows, d), lambda j: (0, 0)),
                  pl.BlockSpec((d, bn), lambda j: (0, j)),
                  pl.BlockSpec((1, bn), lambda j: (0, j))],
        out_specs=pl.BlockSpec((rows, bn), lambda j: (0, j)),
        out_shape=jax.ShapeDtypeStruct((rows, n), F32),
        compiler_params=_cparams("arbitrary"),
        name="ada_mod",
    )(c_rows, ada_w, ada_b.reshape(1, n))


def _proj_kernel(*refs, d, use_pos, n_split):
    if use_pos:
        x_ref, pos_ref, mod_ref, g_ref, w_ref = refs[:5]
        outs = refs[5:]
        h = x_ref[0] + pos_ref[...]
    else:
        x_ref, mod_ref, g_ref, w_ref = refs[:4]
        outs = refs[4:]
        h = x_ref[0]
    m = mod_ref[0]
    hn = _rms(h, g_ref[...]) * (1.0 + m[:, d:2 * d]) + m[:, 0:d]
    proj = jnp.dot(hn.astype(BF16), w_ref[...], preferred_element_type=F32)
    off = 0
    for o_ref, width in zip(outs, n_split):
        o_ref[0] = proj[:, off:off + width].astype(o_ref.dtype)
        off += width


def _project(x, pos, mod3, mod_row_of_batch, g, w_bf16, n_split, tm):
    bsz, n, d = x.shape
    width = sum(n_split)
    use_pos = pos is not None
    in_specs = [pl.BlockSpec((1, tm, d), lambda i, b: (b, i, 0))]
    args = [x]
    if use_pos:
        in_specs.append(pl.BlockSpec((tm, d), lambda i, b: (i, 0)))
        args.append(pos)
    in_specs += [pl.BlockSpec((1, 1, mod3.shape[-1]), lambda i, b: (mod_row_of_batch(b), 0, 0)),
                 pl.BlockSpec((1, d), lambda i, b: (0, 0)),
                 pl.BlockSpec((d, width), lambda i, b: (0, 0))]
    args += [mod3, g.reshape(1, d), w_bf16]
    return pl.pallas_call(
        functools.partial(_proj_kernel, d=d, use_pos=use_pos, n_split=n_split),
        grid=(n // tm, bsz),
        in_specs=in_specs,
        out_specs=[pl.BlockSpec((1, tm, wd), lambda i, b: (b, i, 0)) for wd in n_split],
        out_shape=[jax.ShapeDtypeStruct((bsz, n, wd), BF16) for wd in n_split],
        compiler_params=_cparams("arbitrary", "arbitrary"),
        name="norm_proj",
    )(*args)


def _s5_tables(a_re, a_im, log_step, b_re, b_im, c_re, c_im, d_skip):
    T, C, P = S5_T, S5_C, S5_P
    G = a_re.shape[1]
    gpb = LANES // C
    slots = np.arange(T)
    tok = np.stack([8 * (slots // 8) + (slots % 8 - o) % 8 for o in range(gpb)])
    tok_g = tok[np.arange(G) % gpb]

    step = jnp.exp(log_step.astype(F32))[..., None]
    ar = a_re.astype(F32) * step
    ai = a_im.astype(F32) * step

    def powers(d, expo):
        e = jnp.asarray(np.broadcast_to(expo, (G, expo.shape[-1])).astype(np.float32))[:, :, None]
        mag = jnp.exp(ar[d][:, None, :] * e)
        return mag * jnp.cos(ai[d][:, None, :] * e), mag * jnp.sin(ai[d][:, None, :] * e)

    lr, li = a_re.astype(F32), a_im.astype(F32)
    first = [powers(d, np.ones(1)) for d in range(2)]
    nr = jnp.stack([first[d][0][:, 0] for d in range(2)]) - 1.0
    ni = jnp.stack([first[d][1][:, 0] for d in range(2)])
    den = lr * lr + li * li
    qr = (nr * lr + ni * li) / den
    qi = (ni * lr - nr * li) / den
    bbr = qr[..., None] * b_re - qi[..., None] * b_im
    bbi = qr[..., None] * b_im + qi[..., None] * b_re
    cr, ci = c_re.astype(F32), c_im.astype(F32)

    kerns = []
    for d in range(2):
        pr, pi = powers(d, np.arange(T))
        wr = cr[d][:, None] * pr[:, :, None, :] - ci[d][:, None] * pi[:, :, None, :]
        wi = cr[d][:, None] * pi[:, :, None, :] + ci[d][:, None] * pr[:, :, None, :]
        kerns.append(jnp.einsum('gtcp,gpe->gtce', wr, bbr[d], precision=HI)
                     - jnp.einsum('gtcp,gpe->gtce', wi, bbi[d], precision=HI))
    kcat = jnp.concatenate(kerns, axis=1)
    kpad = jnp.pad(kcat.reshape(G, 2 * T * C, C), ((0, 0), (0, 0), (0, LANES - C)))
    lag = np.repeat(tok[:, :, None] - tok[:, None, :], C, axis=2).astype(np.int32)
    d_lanes = jnp.tile(d_skip.astype(F32).reshape(G, 1, C), (1, 1, T))
    m1 = pl.pallas_call(
        _s5_intra_kernel,
        grid=(G,),
        in_specs=[pl.BlockSpec((1, 2 * T * C, LANES), lambda g: (g, 0, 0)),
                  pl.BlockSpec((1, T, T * C), lambda g: (g % gpb, 0, 0)),
                  pl.BlockSpec((1, 1, T * C), lambda g: (g, 0, 0))],
        out_specs=pl.BlockSpec((1, T * C, T * C), lambda g: (g, 0, 0)),
        out_shape=jax.ShapeDtypeStruct((G, T * C, T * C), BF16),
        compiler_params=_cparams("arbitrary"),
        name="s5_intra_table",
    )(kpad, jnp.asarray(lag), d_lanes)

    def in_mat(d, expo):
        p_r, p_i = powers(d, expo)
        br, bi = bbr[d].transpose(0, 2, 1)[:, None], bbi[d].transpose(0, 2, 1)[:, None]
        er = p_r[:, :, None, :] * br - p_i[:, :, None, :] * bi
        ei = p_r[:, :, None, :] * bi + p_i[:, :, None, :] * br
        return er.reshape(G, T * C, P), ei.reshape(G, T * C, P)

    efr, efi = in_mat(0, T - 1 - tok_g)
    ebr, ebi = in_mat(1, tok_g)
    m2 = jnp.concatenate([efr, ebr, efi, ebi], axis=-1)

    def out_mat(d, expo):
        p_r, p_i = powers(d, expo)
        crt = cr[d].transpose(0, 2, 1)[:, :, None, :]
        cit = ci[d].transpose(0, 2, 1)[:, :, None, :]
        prt = p_r.transpose(0, 2, 1)[:, :, :, None]
        pit = p_i.transpose(0, 2, 1)[:, :, :, None]
        vr = crt * prt - cit * pit
        vi = crt * pit + cit * prt
        return vr.reshape(G, P, T * C), -vi.reshape(G, P, T * C)

    vfr, vfi = out_mat(0, tok_g + 1)
    vbr, vbi = out_mat(1, T - tok_g)
    m3 = jnp.concatenate([vfr, vbr, vfi, vbi], axis=1)

    last = [powers(d, np.full(1, T)) for d in range(2)]
    lam = jnp.stack([jnp.concatenate([last[0][c][:, 0], last[1][c][:, 0]], axis=-1)
                     for c in range(2)], axis=1)
    is_fwd = jnp.asarray((np.arange(4 * P) // P) % 2 == 0)[None, :, None]
    m3f = jnp.where(is_fwd, m3, 0.0).astype(BF16)
    m3b = jnp.where(is_fwd, 0.0, m3).astype(BF16)
    return m1, m2.astype(BF16), m3f, m3b, lam


def _s5_intra_kernel(k_ref, lag_ref, d_ref, o_ref):
    T, C = S5_T, S5_C
    k = k_ref[0]
    shift = C
    while shift < LANES:
        k = k + pltpu.roll(k, shift, axis=1)
        shift *= 2
    k = jnp.concatenate([k, k], axis=1)
    block = lambda x: k[x * C:(x + 1) * C]
    row = lax.broadcasted_iota(jnp.int32, (C, T * C), 0)
    lane_c = lax.broadcasted_iota(jnp.int32, (C, T * C), 1) % C
    centre = block(0) + block(T) + jnp.where(row == lane_c, d_ref[0], 0.0)
    block_rows = []
    for l in range(T):
        lag = lag_ref[0, l:l + 1, :]
        acc = centre
        for x in range(1, T):
            acc = jnp.where(lag == x, block(x), acc)
            acc = jnp.where(lag == -x, block(T + x), acc)
        block_rows.append(acc)
    o_ref[0] = jnp.concatenate(block_rows, axis=0).T.astype(BF16)


def _gelu_tanh(x):
    return 0.5 * x * (1.0 + jnp.tanh(math.sqrt(2.0 / math.pi) * (x + 0.044715 * (x * x * x))))


def _s5_kernel(u_ref, uc_ref, m1_ref, m2_ref, m3f_ref, m3b_ref, lam_ref, o_ref,
               z_ref, e_ref, sf_ref, sb_ref, *, nbatch, nchunk, nctx):
    groups = LANES // S5_C
    rows = nbatch * nchunk
    lane = lax.broadcasted_iota(jnp.int32, (1, LANES), 1)
    slot_bits = [(((lane // S5_C) >> j) & 1) == 1 for j in range(3)]

    def skew(xs):
        cur = [xs[(-k) % groups] for k in range(groups)]
        for j, bit in enumerate(slot_bits):
            cur = [jnp.where(bit, cur[(i - (1 << j)) % groups], cur[i]) for i in range(groups)]
        return cur

    def to_chunk_rows(read_t, nrows, row0):
        for h in range(2):
            rolled = []
            for t8 in range(groups):
                v = read_t(8 * h + t8).astype(F32)
                rolled.append(pltpu.roll(v, t8 * S5_C, axis=1) if t8 else v)
            for g, zg in enumerate(skew(rolled)):
                z_ref[g, row0:row0 + nrows, h * LANES:(h + 1) * LANES] = zg.astype(BF16)

    for r0 in range(0, rows, RELAYOUT_ROWS):
        to_chunk_rows(lambda t, r0=r0: u_ref[t, r0:r0 + RELAYOUT_ROWS], RELAYOUT_ROWS, r0)
    to_chunk_rows(lambda t: uc_ref[t], nbatch * nctx, rows)

    wide = 2 * LANES
    fwd_lane = (lax.broadcasted_iota(jnp.int32, (1, wide), 1) % LANES) < (LANES // 2)
    tile = lambda k: pl.ds(pl.multiple_of(k * nbatch, nbatch), nbatch)

    for g in range(groups):
        zg = z_ref[g]
        e_ref[...] = jnp.dot(zg, m2_ref[g], preferred_element_type=F32)
        lam_r = lam_ref[g, 0:1, :]
        lam_i = lam_ref[g, 1:2, :]

        def advance(sr, si, kf, kb):
            e2 = jnp.where(fwd_lane, e_ref[tile(kf), :], e_ref[tile(kb), :])
            return (lam_r * sr - lam_i * si + e2[:, :LANES], lam_r * si + lam_i * sr + e2[:, LANES:])

        sr = jnp.zeros((nbatch, LANES), F32)
        si = jnp.zeros((nbatch, LANES), F32)
        for i in range(nctx):
            sr, si = advance(sr, si, nchunk + i, nchunk + nctx - 1 - i)

        def body(i, carry):
            sr, si = carry
            kf, kb = i, nchunk - 1 - i
            s = jnp.concatenate([sr, si], axis=-1)
            sf_ref[tile(kf), :] = s
            sb_ref[tile(kb), :] = s
            return advance(sr, si, kf, kb)

        lax.fori_loop(0, nchunk, body, (sr, si), unroll=2)

        y = (jnp.dot(zg[:rows], m1_ref[g], preferred_element_type=F32)
             + jnp.dot(sf_ref[...].astype(BF16), m3f_ref[g], preferred_element_type=F32)
             + jnp.dot(sb_ref[...].astype(BF16), m3b_ref[g], preferred_element_type=F32))
        z_ref[g, 0:rows, :] = _gelu_tanh(y).astype(BF16)

    for r0 in range(0, rows, RELAYOUT_ROWS):
        for h in range(2):
            ys = [z_ref[g, r0:r0 + RELAYOUT_ROWS, h * LANES:(h + 1) * LANES].astype(F32)
                  for g in range(groups)]
            for t8, acc in enumerate(skew(ys)):
                if t8:
                    acc = pltpu.roll(acc, LANES - t8 * S5_C, axis=1)
                o_ref[8 * h + t8, r0:r0 + RELAYOUT_ROWS] = acc.astype(BF16)


def _s5_scan(u_t, uc_t, bsz, m1, m2, m3f, m3b, lam):
    T, rows, width = u_t.shape
    ctx_rows = uc_t.shape[1]
    gpb = LANES // S5_C
    kern = functools.partial(_s5_kernel, nbatch=bsz, nchunk=rows // bsz, nctx=ctx_rows // bsz)
    one = pl.Buffered(1)
    mat = pl.BlockSpec((gpb, 2 * LANES, 2 * LANES), lambda j: (j, 0, 0))
    return pl.pallas_call(
        kern,
        grid=(width // LANES,),
        in_specs=[pl.BlockSpec((T, rows, LANES), lambda j: (0, 0, j), pipeline_mode=one),
                  pl.BlockSpec((T, ctx_rows, LANES), lambda j: (0, 0, j)),
                  mat, mat, mat, mat,
                  pl.BlockSpec((gpb, 2, LANES), lambda j: (j, 0, 0))],
        out_specs=pl.BlockSpec((T, rows, LANES), lambda j: (0, 0, j), pipeline_mode=one),
        out_shape=jax.ShapeDtypeStruct(u_t.shape, BF16),
        scratch_shapes=[pltpu.VMEM((gpb, rows + ctx_rows, 2 * LANES), BF16),
                        pltpu.VMEM((rows + ctx_rows, 2 * LANES), F32),
                        pltpu.VMEM((rows, 2 * LANES), F32),
                        pltpu.VMEM((rows, 2 * LANES), F32)],
        compiler_params=_cparams("arbitrary"),
        name="s5_scan",
    )(u_t, uc_t, m1, m2, m3f, m3b, lam)


def _dft_tables(n_seq):
    na_sig = n_seq // FFT_NB
    ka = np.arange(FFT_NA)[:, None]
    nb = np.arange(FFT_NB)[:, None, None]

    def stage1(n_in):
        na = np.arange(n_in)[None, :]
        ang = -2.0 * np.pi * (na * ka / FFT_NA)[None] - 2.0 * np.pi * (nb * ka[None]) / FFT_N
        return np.cos(ang), np.sin(ang)

    c, s = stage1(na_sig)
    f1 = np.concatenate([np.concatenate([c, -s], axis=2), np.concatenate([s, c], axis=2)], axis=1)
    c, s = stage1(FFT_NA)
    f1_real = np.concatenate([c, s], axis=1)
    kb = np.arange(FFT_NB)[:, None]
    nbv = np.arange(FFT_NB)[None, :]
    ang = -2.0 * np.pi * kb * nbv / FFT_NB
    c, s = np.cos(ang), np.sin(ang)
    f2 = np.block([[c, -s], [s, c]])
    f2_inv = np.block([[c, s], [-s, c]]) / FFT_NB
    nap = np.arange(na_sig)[:, None]
    kav = np.arange(FFT_NA)[None, :]
    ang = 2.0 * np.pi * (nap * kav / FFT_NA)[None] + 2.0 * np.pi * (nb * kav[None]) / FFT_N
    c, s = np.cos(ang) / FFT_NA, np.sin(ang) / FFT_NA
    f3 = np.concatenate([np.concatenate([c, -s], axis=2), np.concatenate([s, c], axis=2)], axis=1)
    as_bf16 = lambda a: jnp.asarray(a.astype(np.float32)).astype(BF16)
    return as_bf16(f1), as_bf16(f1_real), as_bf16(f2), as_bf16(f2_inv), as_bf16(f3)


def _conv3_rows(z_ref, plane, nb, w, b, nbk):
    def rows(i):
        return z_ref[plane, 0, i].astype(F32)

    cur = rows(nb)
    na = cur.shape[0]
    ridx = lax.broadcasted_iota(jnp.int32, cur.shape, 0)
    if isinstance(nb, int) and nb == 0:
        prev = jnp.where(ridx == 0, 0.0, pltpu.roll(rows(nbk - 1), 1, axis=0))
    else:
        prev = rows(nb - 1)
    if isinstance(nb, int) and nb == nbk - 1:
        nxt = jnp.where(ridx == na - 1, 0.0, pltpu.roll(rows(0), na - 1, axis=0))
    else:
        nxt = rows(nb + 1)
    return prev * w[0:1] + cur * w[1:2] + nxt * w[2:3] + b


def _for_row_blocks(nbk, fn, edges_differ):
    peel = 2 if edges_differ else 0
    for i in range(peel):
        fn(i)
    lax.fori_loop(peel, nbk - peel, lambda i, c: (fn(i), c)[1], 0, unroll=4)
    for i in range(nbk - peel, nbk):
        fn(i)


def _regroup(x):
    return jnp.swapaxes(x, 0, 1)


def _hyena_kernel(zv_ref, zg1_ref, zg2_ref, wv_ref, bv_ref, wg1_ref, bg1_ref, wg2_ref, bg2_ref,
                  bias_ref, kf_ref, f1_ref, f2_ref, f2i_ref, f3_ref, o_ref, s_ref, v_ref, z1_ref,
                  *, nbk, na, ngrp, kag):
    step = pl.program_id(2)
    hb = FFT_NB

    def stage1(src_ref, conv):
        def body(nb):
            parts = []
            for pln in range(2):
                if conv:
                    v = _conv3_rows(src_ref, pln, nb, wv_ref[...], bv_ref[...], nbk).astype(BF16)
                    v_ref[pln, 0, nb] = v
                else:
                    v = src_ref[pln, 0, nb]
                parts.append(v)
            rhs = jnp.concatenate(parts, axis=0)
            s_ref[nb] = jnp.dot(f1_ref[nb], rhs, preferred_element_type=F32).astype(BF16)
        _for_row_blocks(nbk, body, conv)

    def stage3(gate_ref, wg, bg, vsrc_ref, bias, dst_ref):
        def body(nb):
            y = jnp.dot(f3_ref[nb], s_ref[nb], preferred_element_type=F32)
            for pln in range(2):
                gate = _conv3_rows(gate_ref, pln, nb, wg, bg, nbk)
                out = gate * (y[pln * na:(pln + 1) * na] + bias * vsrc_ref[pln, 0, nb].astype(F32))
                dst_ref[pln, 0, nb] = out.astype(dst_ref.dtype)
        _for_row_blocks(nbk, body, True)

    @pl.when(step == 0)
    def _():
        stage1(zv_ref, True)

    @pl.when(step == ngrp)
    def _():
        stage1(z1_ref, False)

    starts = [pl.multiple_of((step % ngrp) * kag + slab * SLAB, SLAB) for slab in range(kag // SLAB)]
    loaded = [(_regroup(s_ref[:, pl.ds(r0, SLAB), :]),
               _regroup(s_ref[:, pl.ds(FFT_NA + r0, SLAB), :])) for r0 in starts]
    results = []
    for slab, (re_t, im_t) in enumerate(loaded):
        back_re, back_im = [], []
        for k in range(SLAB):
            rhs = jnp.concatenate([re_t[k], im_t[k]], axis=0)
            spec = jnp.dot(f2_ref[...], rhs, preferred_element_type=F32).astype(BF16)
            kf = kf_ref[0, slab * SLAB + k]
            sr, si = spec[:hb], spec[hb:]
            kr, ki = kf[:hb], kf[hb:]
            prod = jnp.concatenate([sr * kr - si * ki, sr * ki + si * kr], axis=0)
            back = jnp.dot(f2i_ref[...], prod, preferred_element_type=F32).astype(BF16)
            back_re.append(back[:hb])
            back_im.append(back[hb:])
        results.append((_regroup(jnp.stack(back_re, axis=0)), _regroup(jnp.stack(back_im, axis=0))))
    for r0, (out_re, out_im) in zip(starts, results):
        s_ref[:, pl.ds(r0, SLAB), :] = out_re
        s_ref[:, pl.ds(FFT_NA + r0, SLAB), :] = out_im

    @pl.when(step == ngrp - 1)
    def _():
        stage3(zg1_ref, wg1_ref[...], bg1_ref[...], v_ref, bias_ref[0:1, :], z1_ref)

    @pl.when(step == 2 * ngrp - 1)
    def _():
        stage3(zg2_ref, wg2_ref[...], bg2_ref[...], z1_ref, bias_ref[1:2, :], o_ref)


def _hyena(z5, conv_w, conv_b, bias, kf, tables, cb=MXU_N, kag=2 * SLAB):
    f1, _, f2, f2i, f3 = tables
    _, npair, nbk, na, _ = z5.shape
    c = bias.shape[-1]
    ncb = c // cb
    ngrp = FFT_NA // kag
    one = pl.Buffered(1)
    zspec = lambda grp: pl.BlockSpec((2, 1, nbk, na, cb), lambda p, j, s: (0, p, 0, 0, grp * ncb + j),
                                     pipeline_mode=one)
    wspec = lambda grp: pl.BlockSpec((3, cb), lambda p, j, s: (0, grp * ncb + j))
    bspec = lambda grp: pl.BlockSpec((1, cb), lambda p, j, s: (0, grp * ncb + j))
    const = lambda a: pl.BlockSpec(a.shape, lambda p, j, s: tuple(0 for _ in a.shape), pipeline_mode=one)
    cw = conv_w.astype(F32)
    cbias = conv_b.astype(F32).reshape(1, -1)
    return pl.pallas_call(
        functools.partial(_hyena_kernel, nbk=nbk, na=na, ngrp=ngrp, kag=kag),
        grid=(npair, ncb, 2 * ngrp),
        in_specs=[zspec(0), zspec(1), zspec(2), wspec(0), bspec(0), wspec(1), bspec(1), wspec(2), bspec(2),
                  pl.BlockSpec((HY_ORDER, cb), lambda p, j, s: (0, j)),
                  pl.BlockSpec((1, kag, 2 * FFT_NB, cb), lambda p, j, s: (s // ngrp, s % ngrp, 0, j)),
                  const(f1), const(f2), const(f2i), const(f3)],
        out_specs=pl.BlockSpec((2, 1, nbk, na, cb), lambda p, j, s: (0, p, 0, 0, j), pipeline_mode=one),
        out_shape=jax.ShapeDtypeStruct((2, npair, nbk, na, c), BF16),
        scratch_shapes=[pltpu.VMEM((nbk, 2 * FFT_NA, cb), BF16),
                        pltpu.VMEM((2, 1, nbk, na, cb), BF16),
                        pltpu.VMEM((2, 1, nbk, na, cb), BF16)],
        compiler_params=_cparams("arbitrary", "arbitrary", "arbitrary"),
        name="hyena_conv",
    )(z5, z5, z5, cw, cbias, cw, cbias, cw, cbias, bias.astype(F32), kf, f1, f2, f2i, f3)


def _filter_kernel(embt_ref, tv_ref, w1t_ref, b1_ref, w2t_ref, b2_ref, fr_ref, w3_ref, dec_ref, f1_ref,
                   f2_ref, k_ref, s_ref, h_ref, *, nbk, kag):
    first = (pl.program_id(0) == 0) & (pl.program_id(1) == 0)
    step = pl.program_id(2)
    half = FFT_NA // 2
    lanes_per_pass = 8 * FFT_NA

    @pl.when(first & (step == 0))
    def _():
        fr = fr_ref[...]
        for i in range(embt_ref.shape[1] // lanes_per_pass):
            cols = slice(i * lanes_per_pass, (i + 1) * lanes_per_pass)
            h = jnp.sin(fr * (jnp.dot(w1t_ref[...], embt_ref[:, cols], precision=HI,
                                      preferred_element_type=F32) + b1_ref[...]))
            h_ref[:, cols] = jnp.sin(fr * (jnp.dot(w2t_ref[...], h, precision=HI,
                                                   preferred_element_type=F32) + b2_ref[...]))

    @pl.when(step == 0)
    def _():
        dec = dec_ref[0]

        def body(nb, carry):
            h = h_ref[:, pl.ds(pl.multiple_of(nb * FFT_NA, FFT_NA), FFT_NA)].T.astype(BF16)
            fwd = jnp.dot(h[:half], w3_ref[0, 0], preferred_element_type=F32)
            bwd = jnp.dot(h[half:], w3_ref[0, 1], preferred_element_type=F32)
            tv = tv_ref[nb]
            window = jnp.exp(-tv[:, 0:1] * dec) * tv[:, 1:2]
            filt = jnp.concatenate([fwd, bwd], axis=0) * window
            s_ref[nb] = jnp.dot(f1_ref[nb], filt.astype(BF16), preferred_element_type=F32).astype(BF16)
            return carry
        lax.fori_loop(0, nbk, body, 0, unroll=2)

    r0 = pl.multiple_of(step * kag, kag)
    re_t = _regroup(s_ref[:, pl.ds(r0, kag), :])
    im_t = _regroup(s_ref[:, pl.ds(FFT_NA + r0, kag), :])
    for k in range(kag):
        rhs = jnp.concatenate([re_t[k], im_t[k]], axis=0)
        k_ref[0, k] = jnp.dot(f2_ref[...], rhs, preferred_element_type=F32).astype(BF16)


def _filter_spectra(n, w1, b1, w2, b2, freq, w3, decay, tables, cb=MXU_N, kag=16):
    _, f1_real, f2, _, _ = tables
    hid = w2.shape[0]
    c = decay.shape[-1]
    t = np.linspace(0.0, 1.0, n, dtype=np.float32)[:, None]
    w = (2.0 * np.pi * np.arange(n, dtype=np.float32) / n).astype(np.float32)
    bands = np.linspace(1e-4, HY_BANDS - 1, HY_BANDS, dtype=np.float32)
    ang = w[:, None] * bands[None, :]
    emb = np.concatenate([t, np.cos(ang), -np.sin(ang)], axis=-1).astype(np.float32)
    kpad = 64
    idx = FFT_NB * np.arange(FFT_NA)[None, :] + np.arange(FFT_NB)[:, None]
    lagi = np.minimum(np.where(idx < n, idx, FFT_N - idx), n - 1)
    embt = np.zeros((kpad, FFT_N), np.float32)
    embt[:emb.shape[1]] = emb[lagi.reshape(-1)].T
    tv = np.stack([t[lagi, 0], (idx != n).astype(np.float32)], axis=-1)
    col = lambda a: a.reshape(hid, 1).astype(F32)
    w1t = jnp.pad(w1.astype(F32), ((0, kpad - w1.shape[0]), (0, 0))).T
    w3r = w3.astype(BF16).reshape(hid, HY_ORDER, 2, c).transpose(1, 2, 0, 3)
    dec = jnp.abs(decay.astype(F32)).reshape(HY_ORDER, 1, c)
    one = pl.Buffered(1)
    const = lambda shape: pl.BlockSpec(shape, lambda o, j, s: tuple(0 for _ in shape), pipeline_mode=one)
    return pl.pallas_call(
        functools.partial(_filter_kernel, nbk=FFT_NB, kag=kag),
        grid=(HY_ORDER, c // cb, FFT_NA // kag),
        in_specs=[const(embt.shape), const(tv.shape), const((hid, kpad)), const((hid, 1)),
                  const((hid, hid)), const((hid, 1)), const((hid, 1)),
                  pl.BlockSpec((1, 2, hid, cb), lambda o, j, s: (o, 0, 0, j)),
                  pl.BlockSpec((1, 1, cb), lambda o, j, s: (o, 0, j)),
                  const(f1_real.shape), const(f2.shape)],
        out_specs=pl.BlockSpec((1, kag, 2 * FFT_NB, cb), lambda o, j, s: (o, s, 0, j)),
        out_shape=jax.ShapeDtypeStruct((HY_ORDER, FFT_NA, 2 * FFT_NB, c), BF16),
        scratch_shapes=[pltpu.VMEM((FFT_NB, 2 * FFT_NA, cb), BF16),
                        pltpu.VMEM((hid, FFT_N), F32)],
        compiler_params=_cparams("arbitrary", "arbitrary", "arbitrary"),
        name="hyena_filter",
    )(jnp.asarray(embt), jnp.asarray(tv), w1t, col(b1), w2.astype(F32).T, col(b2), col(freq), w3r, dec,
      f1_real, f2)


def _out_kernel(x_ref, pos_ref, mod_ref, ys_ref, yh_ref, gw_ref, gb_ref, g5_ref, gh_ref, wo_ref,
                g2_ref, w1_ref, w2_ref, gf_ref, o_ref, *, d):
    m = mod_ref[0]
    gate1, shift2 = m[:, 2 * d:3 * d], m[:, 3 * d:4 * d]
    scale2, gate2 = m[:, 4 * d:5 * d], m[:, 5 * d:6 * d]
    h = x_ref[0] + pos_ref[...]
    ab = jnp.dot(ys_ref[0], gw_ref[...], preferred_element_type=F32) + gb_ref[...]
    half = ab.shape[-1] // 2
    y5 = ab[:, :half] * jax.nn.sigmoid(ab[:, half:])
    mix = jnp.concatenate([_rms(y5, g5_ref[...]), _rms(yh_ref[0].astype(F32), gh_ref[...])], axis=-1)
    h = h + gate1 * jnp.dot(mix.astype(BF16), wo_ref[...], preferred_element_type=F32)
    hn = _rms(h, g2_ref[...]) * (1.0 + scale2) + shift2
    hid = jnp.dot(hn.astype(BF16), w1_ref[...], preferred_element_type=F32)
    hid = jnp.square(jnp.maximum(hid, 0.0))
    h = h + gate2 * jnp.dot(hid.astype(BF16), w2_ref[...], preferred_element_type=F32)
    o_ref[0] = _rms(h, gf_ref[...])


def _output_stage(x, pos, mod3, ys, yh, glu_w, glu_b, g5, gh, w_out, g2, w1, w2, gf, tm=512):
    bsz, n, d = x.shape
    dh = ys.shape[-1]
    one = pl.Buffered(1)
    const = lambda a: pl.BlockSpec(a.shape, lambda i, b: tuple(0 for _ in a.shape), pipeline_mode=one)
    row = lambda a: a.reshape(1, -1).astype(F32)
    glu_b, g5, gh, g2, gf = row(glu_b), row(g5), row(gh), row(g2), row(gf)
    return pl.pallas_call(
        functools.partial(_out_kernel, d=d),
        grid=(n // tm, bsz),
        in_specs=[pl.BlockSpec((1, tm, d), lambda i, b: (b, i, 0)),
                  pl.BlockSpec((tm, d), lambda i, b: (i, 0)),
                  pl.BlockSpec((1, 1, mod3.shape[-1]), lambda i, b: (b, 0, 0)),
                  pl.BlockSpec((1, tm, dh), lambda i, b: (b, i, 0)),
                  pl.BlockSpec((1, tm, dh), lambda i, b: (b, i, 0)),
                  const(glu_w), const(glu_b), const(g5), const(gh), const(w_out), const(g2),
                  const(w1), const(w2), const(gf)],
        out_specs=pl.BlockSpec((1, tm, d), lambda i, b: (b, i, 0)),
        out_shape=jax.ShapeDtypeStruct((bsz, n, d), F32),
        compiler_params=_cparams("arbitrary", "arbitrary"),
        name="mix_mlp_out",
    )(x, pos, mod3, ys, yh, glu_w, glu_b, g5, gh, w_out, g2, w1, w2, gf)


def _pos_table(n, d):
    rows = n // GRID_W
    row = np.repeat(np.arange(rows, dtype=np.float32), GRID_W)
    col = np.tile(np.arange(GRID_W, dtype=np.float32), rows)
    quarter = d // 4
    omega = (1.0 / (POS_BASE ** (np.arange(quarter, dtype=np.float32) / quarter))).astype(np.float32)

    def enc(p):
        ang = p[:, None] * omega[None, :]
        return np.concatenate([np.sin(ang), np.cos(ang)], axis=-1)

    return jnp.asarray(np.concatenate([enc(row), enc(col)], axis=-1).astype(np.float32))


def kernel(x, c, ctx, c_ctx, ada_w, ada_b, norm1_g, w_in, s5_a_re, s5_a_im, s5_log_step, s5_b_re,
           s5_b_im, s5_c_re, s5_c_im, s5_d, s5_glu_w, s5_glu_b, hy_conv_w, hy_conv_b, hy_f_w1,
           hy_f_b1, hy_f_w2, hy_f_b2, hy_f_freq, hy_f_w3, hy_decay, hy_bias, mix_g_s5, mix_g_hy,
           w_out, norm2_g, mlp_w1, mlp_w2, final_g):
    bsz, n, d = x.shape
    depth = ada_w.shape[0]
    d_s5 = s5_d.shape[-1]
    d_hy = hy_bias.shape[-1]
    nctx = ctx.shape[1]
    npair = bsz // 2
    nbk, na = FFT_NB, n // FFT_NB
    pos = _pos_table(n, d)
    tables = _dft_tables(n)

    assert depth == 1 and bsz % 2 == 0 and n % (FFT_NB * 8) == 0 and 2 * n == FFT_N
    mod_rows = 16
    c_rows = jnp.concatenate([c, c_ctx[None], jnp.zeros((mod_rows - bsz - 1, d), c.dtype)], axis=0)
    mod3 = _modulation(c_rows, ada_w[0], ada_b[0]).reshape(mod_rows, 1, N_MOD * d)

    w_in_b = w_in[0].astype(BF16)
    u, z = _project(x, pos, mod3, lambda b: b, norm1_g[0], w_in_b, (d_s5, 3 * d_hy), tm=1024)
    (uc,) = _project(ctx, None, mod3, lambda b: bsz, norm1_g[0], w_in_b, (d_s5,), tm=nctx)

    s5_mats = _s5_tables(s5_a_re[0], s5_a_im[0], s5_log_step[0], s5_b_re[0], s5_b_im[0],
                         s5_c_re[0], s5_c_im[0], s5_d[0])
    u_t = u.reshape(bsz, n // S5_T, S5_T, d_s5).transpose(2, 1, 0, 3).reshape(S5_T, -1, d_s5)
    uc_t = uc.reshape(bsz, nctx // S5_T, S5_T, d_s5).transpose(2, 1, 0, 3).reshape(S5_T, -1, d_s5)
    ys = _s5_scan(u_t, uc_t, bsz, *s5_mats)
    ys = ys.reshape(S5_T, n // S5_T, bsz, d_s5).transpose(2, 1, 0, 3).reshape(bsz, n, d_s5)

    kf = _filter_spectra(n, hy_f_w1[0], hy_f_b1[0], hy_f_w2[0], hy_f_b2[0], hy_f_freq[0],
                         hy_f_w3[0], hy_decay[0], tables)
    z5 = z.reshape(bsz, na, nbk, 3 * d_hy).transpose(0, 2, 1, 3).reshape(2, npair, nbk, na, 3 * d_hy)
    yh = _hyena(z5, hy_conv_w[0], hy_conv_b[0], hy_bias[0], kf, tables)
    yh = yh.reshape(bsz, nbk, na, d_hy).transpose(0, 2, 1, 3).reshape(bsz, n, d_hy)

    return _output_stage(x, pos, mod3, ys, yh, s5_glu_w[0].astype(BF16), s5_glu_b[0], mix_g_s5[0],
                         mix_g_hy[0], w_out[0].astype(BF16), norm2_g[0], mlp_w1[0].astype(BF16),
                         mlp_w2[0].astype(BF16), final_g)
```

```python
import functools
import math

import numpy as np
import jax
import jax.numpy as jnp
from jax import lax
from jax.experimental import pallas as pl
from jax.experimental.pallas import tpu as pltpu

F32 = jnp.float32
BF16 = jnp.bfloat16
HI = lax.Precision.HIGHEST

EPS = 1e-6
GRID_W = 64
POS_BASE = 10000.0
N_MOD = 6
S5_C = 16
S5_P = 64
S5_T = 16
HY_BANDS = 16
HY_ORDER = 2

FFT_N = 8192
FFT_NA = 128
FFT_NB = 64

LANES = 128
MXU_N = 256
SLAB = 16
RELAYOUT_ROWS = 256
VMEM_LIMIT = 56 * 1024 * 1024


def _cparams(*sem):
    return pltpu.CompilerParams(dimension_semantics=sem, vmem_limit_bytes=VMEM_LIMIT)


def _rms(x, g):
    return x * lax.rsqrt(jnp.mean(x * x, axis=-1, keepdims=True) + EPS) * g


def _mod_kernel(c_ref, w_ref, b_ref, o_ref):
    c = c_ref[...]
    a = c * jax.nn.sigmoid(c)
    o_ref[...] = jnp.dot(a.astype(BF16), w_ref[...].astype(BF16),
                         preferred_element_type=F32) + b_ref[...]


def _modulation(c_rows, ada_w, ada_b):
    rows, d = c_rows.shape
    n = ada_w.shape[1]
    bn = 1024
    return pl.pallas_call(
        _mod_kernel,
        grid=(n // bn,),
        in_specs=[pl.BlockSpec((rows, d), lambda j: (0, 0)),
                  pl.BlockSpec((d, bn), lambda j: (0, j)),
                  pl.BlockSpec((1, bn), lambda j: (0, j))],
        out_specs=pl.BlockSpec((rows, bn), lambda j: (0, j)),
        out_shape=jax.ShapeDtypeStruct((rows, n), F32),
        compiler_params=_cparams("arbitrary"),
        name="ada_mod",
    )(c_rows, ada_w, ada_b.reshape(1, n))


def _proj_kernel(*refs, d, use_pos, n_split):
    if use_pos:
        x_ref, pos_ref, mod_ref, g_ref, w_ref = refs[:5]
        outs = refs[5:]
        h = x_ref[0] + pos_ref[...]
    else:
        x_ref, mod_ref, g_ref, w_ref = refs[:4]
        outs = refs[4:]
        h = x_ref[0]
    m = mod_ref[0]
    hn = _rms(h, g_ref[...]) * (1.0 + m[:, d:2 * d]) + m[:, 0:d]
    proj = jnp.dot(hn.astype(BF16), w_ref[...], preferred_element_type=F32)
    off = 0
    for o_ref, width in zip(outs, n_split):
        o_ref[0] = proj[:, off:off + width].astype(o_ref.dtype)
        off += width


def _project(x, pos, mod3, mod_row_of_batch, g, w_bf16, n_split, tm):
    bsz, n, d = x.shape
    width = sum(n_split)
    use_pos = pos is not None
    in_specs = [pl.BlockSpec((1, tm, d), lambda i, b: (b, i, 0))]
    args = [x]
    if use_pos:
        in_specs.append(pl.BlockSpec((tm, d), lambda i, b: (i, 0)))
        args.append(pos)
    in_specs += [pl.BlockSpec((1, 1, mod3.shape[-1]), lambda i, b: (mod_row_of_batch(b), 0, 0)),
                 pl.BlockSpec((1, d), lambda i, b: (0, 0)),
                 pl.BlockSpec((d, width), lambda i, b: (0, 0))]
    args += [mod3, g.reshape(1, d), w_bf16]
    return pl.pallas_call(
        functools.partial(_proj_kernel, d=d, use_pos=use_pos, n_split=n_split),
        grid=(n // tm, bsz),
        in_specs=in_specs,
        out_specs=[pl.BlockSpec((1, tm, wd), lambda i, b: (b, i, 0)) for wd in n_split],
        out_shape=[jax.ShapeDtypeStruct((bsz, n, wd), BF16) for wd in n_split],
        compiler_params=_cparams("arbitrary", "arbitrary"),
        name="norm_proj",
    )(*args)


def _s5_tables(a_re, a_im, log_step, b_re, b_im, c_re, c_im, d_skip):
    T, C, P = S5_T, S5_C, S5_P
    G = a_re.shape[1]
    gpb = LANES // C
    slots = np.arange(T)
    tok = np.stack([8 * (slots // 8) + (slots % 8 - o) % 8 for o in range(gpb)])
    tok_g = tok[np.arange(G) % gpb]

    step = jnp.exp(log_step.astype(F32))[..., None]
    ar = a_re.astype(F32) * step
    ai = a_im.astype(F32) * step

    def powers(d, expo):
        e = jnp.asarray(np.broadcast_to(expo, (G, expo.shape[-1])).astype(np.float32))[:, :, None]
        mag = jnp.exp(ar[d][:, None, :] * e)
        return mag * jnp.cos(ai[d][:, None, :] * e), mag * jnp.sin(ai[d][:, None, :] * e)

    lr, li = a_re.astype(F32), a_im.astype(F32)
    first = [powers(d, np.ones(1)) for d in range(2)]
    nr = jnp.stack([first[d][0][:, 0] for d in range(2)]) - 1.0
    ni = jnp.stack([first[d][1][:, 0] for d in range(2)])
    den = lr * lr + li * li
    qr = (nr * lr + ni * li) / den
    qi = (ni * lr - nr * li) / den
    bbr = qr[..., None] * b_re - qi[..., None] * b_im
    bbi = qr[..., None] * b_im + qi[..., None] * b_re
    cr, ci = c_re.astype(F32), c_im.astype(F32)

    kerns = []
    for d in range(2):
        pr, pi = powers(d, np.arange(T))
        wr = cr[d][:, None] * pr[:, :, None, :] - ci[d][:, None] * pi[:, :, None, :]
        wi = cr[d][:, None] * pi[:, :, None, :] + ci[d][:, None] * pr[:, :, None, :]
        kerns.append(jnp.einsum('gtcp,gpe->gtce', wr, bbr[d], precision=HI)
                     - jnp.einsum('gtcp,gpe->gtce', wi, bbi[d], precision=HI))
    kcat = jnp.concatenate(kerns, axis=1)
    kpad = jnp.pad(kcat.reshape(G, 2 * T * C, C), ((0, 0), (0, 0), (0, LANES - C)))
    lag = np.repeat(tok[:, :, None] - tok[:, None, :], C, axis=2).astype(np.int32)
    d_lanes = jnp.tile(d_skip.astype(F32).reshape(G, 1, C), (1, 1, T))
    m1 = pl.pallas_call(
        _s5_intra_kernel,
        grid=(G,),
        in_specs=[pl.BlockSpec((1, 2 * T * C, LANES), lambda g: (g, 0, 0)),
                  pl.BlockSpec((1, T, T * C), lambda g: (g % gpb, 0, 0)),
                  pl.BlockSpec((1, 1, T * C), lambda g: (g, 0, 0))],
        out_specs=pl.BlockSpec((1, T * C, T * C), lambda g: (g, 0, 0)),
        out_shape=jax.ShapeDtypeStruct((G, T * C, T * C), BF16),
        compiler_params=_cparams("arbitrary"),
        name="s5_intra_table",
    )(kpad, jnp.asarray(lag), d_lanes)

    def in_mat(d, expo):
        p_r, p_i = powers(d, expo)
        br, bi = bbr[d].transpose(0, 2, 1)[:, None], bbi[d].transpose(0, 2, 1)[:, None]
        er = p_r[:, :, None, :] * br - p_i[:, :, None, :] * bi
        ei = p_r[:, :, None, :] * bi + p_i[:, :, None, :] * br
        return er.reshape(G, T * C, P), ei.reshape(G, T * C, P)

    efr, efi = in_mat(0, T - 1 - tok_g)
    ebr, ebi = in_mat(1, tok_g)
    m2 = jnp.concatenate([efr, ebr, efi, ebi], axis=-1)

    def out_mat(d, expo):
        p_r, p_i = powers(d, expo)
        crt = cr[d].transpose(0, 2, 1)[:, :, None, :]
        cit = ci[d].transpose(0, 2, 1)[:, :, None, :]
        prt = p_r.transpose(0, 2, 1)[:, :, :, None]
        pit = p_i.transpose(0, 2, 1)[:, :, :, None]
        vr = crt * prt - cit * pit
        vi = crt * pit + cit * prt
        return vr.reshape(G, P, T * C), -vi.reshape(G, P, T * C)

    vfr, vfi = out_mat(0, tok_g + 1)
    vbr, vbi = out_mat(1, T - tok_g)
    m3 = jnp.concatenate([vfr, vbr, vfi, vbi], axis=1)

    last = [powers(d, np.full(1, T)) for d in range(2)]
    lam = jnp.stack([jnp.concatenate([last[0][c][:, 0], last[1][c][:, 0]], axis=-1)
                     for c in range(2)], axis=1)
    is_fwd = jnp.asarray((np.arange(4 * P) // P) % 2 == 0)[None, :, None]
    m3f = jnp.where(is_fwd, m3, 0.0).astype(BF16)
    m3b = jnp.where(is_fwd, 0.0, m3).astype(BF16)
    return m1, m2.astype(BF16), m3f, m3b, lam


def _s5_intra_kernel(k_ref, lag_ref, d_ref, o_ref):
    T, C = S5_T, S5_C
    k = k_ref[0]
    shift = C
    while shift < LANES:
        k = k + pltpu.roll(k, shift, axis=1)
        shift *= 2
    k = jnp.concatenate([k, k], axis=1)
    block = lambda x: k[x * C:(x + 1) * C]
    row = lax.broadcasted_iota(jnp.int32, (C, T * C), 0)
    lane_c = lax.broadcasted_iota(jnp.int32, (C, T * C), 1) % C
    centre = block(0) + block(T) + jnp.where(row == lane_c, d_ref[0], 0.0)
    block_rows = []
    for l in range(T):
        lag = lag_ref[0, l:l + 1, :]
        acc = centre
        for x in range(1, T):
            acc = jnp.where(lag == x, block(x), acc)
            acc = jnp.where(lag == -x, block(T + x), acc)
        block_rows.append(acc)
    o_ref[0] = jnp.concatenate(block_rows, axis=0).T.astype(BF16)


def _gelu_tanh(x):
    return 0.5 * x * (1.0 + jnp.tanh(math.sqrt(2.0 / math.pi) * (x + 0.044715 * (x * x * x))))


def _s5_kernel(u_ref, uc_ref, m1_ref, m2_ref, m3f_ref, m3b_ref, lam_ref, o_ref,
               z_ref, e_ref, sf_ref, sb_ref, *, nbatch, nchunk, nctx):
    groups = LANES // S5_C
    rows = nbatch * nchunk
    lane = lax.broadcasted_iota(jnp.int32, (1, LANES), 1)
    slot_bits = [(((lane // S5_C) >> j) & 1) == 1 for j in range(3)]

    def skew(xs):
        cur = [xs[(-k) % groups] for k in range(groups)]
        for j, bit in enumerate(slot_bits):
            cur = [jnp.where(bit, cur[(i - (1 << j)) % groups], cur[i]) for i in range(groups)]
        return cur

    def to_chunk_rows(read_t, nrows, row0):
        for h in range(2):
            rolled = []
            for t8 in range(groups):
                v = read_t(8 * h + t8).astype(F32)
                rolled.append(pltpu.roll(v, t8 * S5_C, axis=1) if t8 else v)
            for g, zg in enumerate(skew(rolled)):
                z_ref[g, row0:row0 + nrows, h * LANES:(h + 1) * LANES] = zg.astype(BF16)

    for r0 in range(0, rows, RELAYOUT_ROWS):
        to_chunk_rows(lambda t, r0=r0: u_ref[t, r0:r0 + RELAYOUT_ROWS], RELAYOUT_ROWS, r0)
    to_chunk_rows(lambda t: uc_ref[t], nbatch * nctx, rows)

    wide = 2 * LANES
    fwd_lane = (lax.broadcasted_iota(jnp.int32, (1, wide), 1) % LANES) < (LANES // 2)
    tile = lambda k: pl.ds(pl.multiple_of(k * nbatch, nbatch), nbatch)

    for g in range(groups):
        zg = z_ref[g]
        e_ref[...] = jnp.dot(zg, m2_ref[g], preferred_element_type=F32)
        lam_r = lam_ref[g, 0:1, :]
        lam_i = lam_ref[g, 1:2, :]

        def advance(sr, si, kf, kb):
            e2 = jnp.where(fwd_lane, e_ref[tile(kf), :], e_ref[tile(kb), :])
            return (lam_r * sr - lam_i * si + e2[:, :LANES], lam_r * si + lam_i * sr + e2[:, LANES:])

        sr = jnp.zeros((nbatch, LANES), F32)
        si = jnp.zeros((nbatch, LANES), F32)
        for i in range(nctx):
            sr, si = advance(sr, si, nchunk + i, nchunk + nctx - 1 - i)

        def body(i, carry):
            sr, si = carry
            kf, kb = i, nchunk - 1 - i
            s = jnp.concatenate([sr, si], axis=-1)
            sf_ref[tile(kf), :] = s
            sb_ref[tile(kb), :] = s
            return advance(sr, si, kf, kb)

        lax.fori_loop(0, nchunk, body, (sr, si), unroll=2)

        y = (jnp.dot(zg[:rows], m1_ref[g], preferred_element_type=F32)
             + jnp.dot(sf_ref[...].astype(BF16), m3f_ref[g], preferred_element_type=F32)
             + jnp.dot(sb_ref[...].astype(BF16), m3b_ref[g], preferred_element_type=F32))
        z_ref[g, 0:rows, :] = _gelu_tanh(y).astype(BF16)

    for r0 in range(0, rows, RELAYOUT_ROWS):
        for h in range(2):
            ys = [z_ref[g, r0:r0 + RELAYOUT_ROWS, h * LANES:(h + 1) * LANES].astype(F32)
                  for g in range(groups)]
            for t8, acc in enumerate(skew(ys)):
                if t8:
                    acc = pltpu.roll(acc, LANES - t8 * S5_C, axis=1)
                o_ref[8 * h + t8, r0:r0 + RELAYOUT_ROWS] = acc.astype(BF16)


def _s5_scan(u_t, uc_t, bsz, m1, m2, m3f, m3b, lam):
    T, rows, width = u_t.shape
    ctx_rows = uc_t.shape[1]
    gpb = LANES // S5_C
    kern = functools.partial(_s5_kernel, nbatch=bsz, nchunk=rows // bsz, nctx=ctx_rows // bsz)
    one = pl.Buffered(1)
    mat = pl.BlockSpec((gpb, 2 * LANES, 2 * LANES), lambda j: (j, 0, 0))
    return pl.pallas_call(
        kern,
        grid=(width // LANES,),
        in_specs=[pl.BlockSpec((T, rows, LANES), lambda j: (0, 0, j), pipeline_mode=one),
                  pl.BlockSpec((T, ctx_rows, LANES), lambda j: (0, 0, j)),
                  mat, mat, mat, mat,
                  pl.BlockSpec((gpb, 2, LANES), lambda j: (j, 0, 0))],
        out_specs=pl.BlockSpec((T, rows, LANES), lambda j: (0, 0, j), pipeline_mode=one),
        out_shape=jax.ShapeDtypeStruct(u_t.shape, BF16),
        scratch_shapes=[pltpu.VMEM((gpb, rows + ctx_rows, 2 * LANES), BF16),
                        pltpu.VMEM((rows + ctx_rows, 2 * LANES), F32),
                        pltpu.VMEM((rows, 2 * LANES), F32),
                        pltpu.VMEM((rows, 2 * LANES), F32)],
        compiler_params=_cparams("arbitrary"),
        name="s5_scan",
    )(u_t, uc_t, m1, m2, m3f, m3b, lam)


def _dft_tables(n_seq):
    na_sig = n_seq // FFT_NB
    ka = np.arange(FFT_NA)[:, None]
    nb = np.arange(FFT_NB)[:, None, None]

    def stage1(n_in):
        na = np.arange(n_in)[None, :]
        ang = -2.0 * np.pi * (na * ka / FFT_NA)[None] - 2.0 * np.pi * (nb * ka[None]) / FFT_N
        return np.cos(ang), np.sin(ang)

    c, s = stage1(na_sig)
    f1 = np.concatenate([np.concatenate([c, -s], axis=2), np.concatenate([s, c], axis=2)], axis=1)
    c, s = stage1(FFT_NA)
    f1_real = np.concatenate([c, s], axis=1)
    kb = np.arange(FFT_NB)[:, None]
    nbv = np.arange(FFT_NB)[None, :]
    ang = -2.0 * np.pi * kb * nbv / FFT_NB
    c, s = np.cos(ang), np.sin(ang)
    f2 = np.block([[c, -s], [s, c]])
    f2_inv = np.block([[c, s], [-s, c]]) / FFT_NB
    nap = np.arange(na_sig)[:, None]
    kav = np.arange(FFT_NA)[None, :]
    ang = 2.0 * np.pi * (nap * kav / FFT_NA)[None] + 2.0 * np.pi * (nb * kav[None]) / FFT_N
    c, s = np.cos(ang) / FFT_NA, np.sin(ang) / FFT_NA
    f3 = np.concatenate([np.concatenate([c, -s], axis=2), np.concatenate([s, c], axis=2)], axis=1)
    as_bf16 = lambda a: jnp.asarray(a.astype(np.float32)).astype(BF16)
    return as_bf16(f1), as_bf16(f1_real), as_bf16(f2), as_bf16(f2_inv), as_bf16(f3)


def _conv3_slab(z_ref, plane, nb0, w, b, nbk):
    def rows(i):
        return z_ref[plane, 0, i].astype(F32)

    mid = [rows(nb0 + i) for i in range(SLAB)]
    na = mid[0].shape[0]
    ridx = lax.broadcasted_iota(jnp.int32, mid[0].shape, 0)
    wrapped = rows((nb0 + nbk - 1) % nbk)
    before = jnp.where(nb0 == 0, jnp.where(ridx == 0, 0.0, pltpu.roll(wrapped, 1, axis=0)), wrapped)
    wrapped = rows((nb0 + SLAB) % nbk)
    after = jnp.where(nb0 + SLAB == nbk, jnp.where(ridx == na - 1, 0.0, pltpu.roll(wrapped, na - 1, axis=0)),
                      wrapped)
    ext = [before] + mid + [after]
    return [ext[i] * w[0:1] + ext[i + 1] * w[1:2] + ext[i + 2] * w[2:3] + b for i in range(SLAB)]


def _for_row_slabs(nbk, fn):
    lax.fori_loop(0, nbk // SLAB, lambda j, c: (fn(pl.multiple_of(j * SLAB, SLAB)), c)[1], 0)


def _regroup(x):
    return jnp.swapaxes(x, 0, 1)


def _hyena_kernel(zv_ref, zg1_ref, zg2_ref, wv_ref, bv_ref, wg1_ref, bg1_ref, wg2_ref, bg2_ref,
                  bias_ref, kf_ref, f1_ref, f2_ref, f2i_ref, f3_ref, o_ref, s_ref, v_ref, z1_ref,
                  *, nbk, na, ngrp, kag):
    step = pl.program_id(2)
    hb = FFT_NB

    def stage1(src_ref, conv):
        def slab(nb0):
            if conv:
                planes = [[v.astype(BF16) for v in _conv3_slab(src_ref, pln, nb0, wv_ref[...], bv_ref[...], nbk)]
                          for pln in range(2)]
            tiles = []
            for i in range(SLAB):
                nb = nb0 + i
                if conv:
                    parts = [planes[0][i], planes[1][i]]
                    v_ref[0, 0, nb] = parts[0]
                    v_ref[1, 0, nb] = parts[1]
                else:
                    parts = [src_ref[0, 0, nb], src_ref[1, 0, nb]]
                rhs = jnp.concatenate(parts, axis=0)
                tiles.append(jnp.dot(f1_ref[nb], rhs, preferred_element_type=F32).astype(BF16))
            s_ref[:, pl.ds(nb0, SLAB), :] = _regroup(jnp.stack(tiles, axis=0))
        _for_row_slabs(nbk, slab)

    def stage3(gate_ref, wg, bg, vsrc_ref, bias, dst_ref):
        def slab(nb0):
            rows = _regroup(s_ref[:, pl.ds(nb0, SLAB), :])
            gates = [_conv3_slab(gate_ref, pln, nb0, wg, bg, nbk) for pln in range(2)]
            for i in range(SLAB):
                nb = nb0 + i
                y = jnp.dot(f3_ref[nb], rows[i], preferred_element_type=F32)
                for pln in range(2):
                    out = gates[pln][i] * (y[pln * na:(pln + 1) * na] + bias * vsrc_ref[pln, 0, nb].astype(F32))
                    dst_ref[pln, 0, nb] = out.astype(dst_ref.dtype)
        _for_row_slabs(nbk, slab)

    @pl.when(step == 0)
    def _():
        stage1(zv_ref, True)

    @pl.when(step == ngrp)
    def _():
        stage1(z1_ref, False)

    base = (step % ngrp) * kag
    loaded = [jnp.concatenate([s_ref[base + k], s_ref[FFT_NA + base + k]], axis=0) for k in range(kag)]
    results = []
    for k, rhs in enumerate(loaded):
        spec = jnp.dot(f2_ref[...], rhs, preferred_element_type=F32).astype(BF16)
        kf = kf_ref[0, k]
        sr, si = spec[:hb], spec[hb:]
        kr, ki = kf[:hb], kf[hb:]
        prod = jnp.concatenate([sr * kr - si * ki, sr * ki + si * kr], axis=0)
        results.append(jnp.dot(f2i_ref[...], prod, preferred_element_type=F32).astype(BF16))
    for k, back in enumerate(results):
        s_ref[base + k] = back[:hb]
        s_ref[FFT_NA + base + k] = back[hb:]

    @pl.when(step == ngrp - 1)
    def _():
        stage3(zg1_ref, wg1_ref[...], bg1_ref[...], v_ref, bias_ref[0:1, :], z1_ref)

    @pl.when(step == 2 * ngrp - 1)
    def _():
        stage3(zg2_ref, wg2_ref[...], bg2_ref[...], z1_ref, bias_ref[1:2, :], o_ref)


def _hyena(z5, conv_w, conv_b, bias, kf, tables, cb=MXU_N, kag=2 * SLAB):
    f1, _, f2, f2i, f3 = tables
    _, npair, nbk, na, _ = z5.shape
    c = bias.shape[-1]
    ncb = c // cb
    ngrp = FFT_NA // kag
    one = pl.Buffered(1)
    zspec = lambda grp: pl.BlockSpec((2, 1, nbk, na, cb), lambda p, j, s: (0, p, 0, 0, grp * ncb + j),
                                     pipeline_mode=one)
    wspec = lambda grp: pl.BlockSpec((3, cb), lambda p, j, s: (0, grp * ncb + j))
    bspec = lambda grp: pl.BlockSpec((1, cb), lambda p, j, s: (0, grp * ncb + j))
    const = lambda a: pl.BlockSpec(a.shape, lambda p, j, s: tuple(0 for _ in a.shape), pipeline_mode=one)
    cw = conv_w.astype(F32)
    cbias = conv_b.astype(F32).reshape(1, -1)
    return pl.pallas_call(
        functools.partial(_hyena_kernel, nbk=nbk, na=na, ngrp=ngrp, kag=kag),
        grid=(npair, ncb, 2 * ngrp),
        in_specs=[zspec(0), zspec(1), zspec(2), wspec(0), bspec(0), wspec(1), bspec(1), wspec(2), bspec(2),
                  pl.BlockSpec((HY_ORDER, cb), lambda p, j, s: (0, j)),
                  pl.BlockSpec((1, kag, 2 * FFT_NB, cb), lambda p, j, s: (s // ngrp, s % ngrp, 0, j)),
                  const(f1), const(f2), const(f2i), const(f3)],
        out_specs=pl.BlockSpec((2, 1, nbk, na, cb), lambda p, j, s: (0, p, 0, 0, j), pipeline_mode=one),
        out_shape=jax.ShapeDtypeStruct((2, npair, nbk, na, c), BF16),
        scratch_shapes=[pltpu.VMEM((2 * FFT_NA, nbk, cb), BF16),
                        pltpu.VMEM((2, 1, nbk, na, cb), BF16),
                        pltpu.VMEM((2, 1, nbk, na, cb), BF16)],
        compiler_params=_cparams("arbitrary", "arbitrary", "arbitrary"),
        name="hyena_conv",
    )(z5, z5, z5, cw, cbias, cw, cbias, cw, cbias, bias.astype(F32), kf, f1, f2, f2i, f3)


def _filter_kernel(embt_ref, tv_ref, w1t_ref, b1_ref, w2t_ref, b2_ref, fr_ref, w3_ref, dec_ref, f1_ref,
                   f2_ref, k_ref, s_ref, h_ref, *, nbk, kag):
    first = (pl.program_id(0) == 0) & (pl.program_id(1) == 0)
    step = pl.program_id(2)
    half = FFT_NA // 2
    lanes_per_pass = 8 * FFT_NA

    @pl.when(first & (step == 0))
    def _():
        fr = fr_ref[...]
        for i in range(embt_ref.shape[1] // lanes_per_pass):
            cols = slice(i * lanes_per_pass, (i + 1) * lanes_per_pass)
            h = jnp.sin(fr * (jnp.dot(w1t_ref[...], embt_ref[:, cols], precision=HI,
                                      preferred_element_type=F32) + b1_ref[...]))
            h_ref[:, cols] = jnp.sin(fr * (jnp.dot(w2t_ref[...], h, precision=HI,
                                                   preferred_element_type=F32) + b2_ref[...]))

    @pl.when(step == 0)
    def _():
        dec = dec_ref[0]

        def body(nb, carry):
            h = h_ref[:, pl.ds(pl.multiple_of(nb * FFT_NA, FFT_NA), FFT_NA)].T.astype(BF16)
            fwd = jnp.dot(h[:half], w3_ref[0, 0], preferred_element_type=F32)
            bwd = jnp.dot(h[half:], w3_ref[0, 1], preferred_element_type=F32)
            tv = tv_ref[nb]
            window = jnp.exp(-tv[:, 0:1] * dec) * tv[:, 1:2]
            filt = jnp.concatenate([fwd, bwd], axis=0) * window
            s_ref[nb] = jnp.dot(f1_ref[nb], filt.astype(BF16), preferred_element_type=F32).astype(BF16)
            return carry
        lax.fori_loop(0, nbk, body, 0, unroll=2)

    r0 = pl.multiple_of(step * kag, kag)
    re_t = _regroup(s_ref[:, pl.ds(r0, kag), :])
    im_t = _regroup(s_ref[:, pl.ds(FFT_NA + r0, kag), :])
    for k in range(kag):
        rhs = jnp.concatenate([re_t[k], im_t[k]], axis=0)
        k_ref[0, k] = jnp.dot(f2_ref[...], rhs, preferred_element_type=F32).astype(BF16)


def _filter_spectra(n, w1, b1, w2, b2, freq, w3, decay, tables, cb=MXU_N, kag=16):
    _, f1_real, f2, _, _ = tables
    hid = w2.shape[0]
    c = decay.shape[-1]
    t = np.linspace(0.0, 1.0, n, dtype=np.float32)[:, None]
    w = (2.0 * np.pi * np.arange(n, dtype=np.float32) / n).astype(np.float32)
    bands = np.linspace(1e-4, HY_BANDS - 1, HY_BANDS, dtype=np.float32)
    ang = w[:, None] * bands[None, :]
    emb = np.concatenate([t, np.cos(ang), -np.sin(ang)], axis=-1).astype(np.float32)
    kpad = 64
    idx = FFT_NB * np.arange(FFT_NA)[None, :] + np.arange(FFT_NB)[:, None]
    lagi = np.minimum(np.where(idx < n, idx, FFT_N - idx), n - 1)
    embt = np.zeros((kpad, FFT_N), np.float32)
    embt[:emb.shape[1]] = emb[lagi.reshape(-1)].T
    tv = np.stack([t[lagi, 0], (idx != n).astype(np.float32)], axis=-1)
    col = lambda a: a.reshape(hid, 1).astype(F32)
    w1t = jnp.pad(w1.astype(F32), ((0, kpad - w1.shape[0]), (0, 0))).T
    w3r = w3.astype(BF16).reshape(hid, HY_ORDER, 2, c).transpose(1, 2, 0, 3)
    dec = jnp.abs(decay.astype(F32)).reshape(HY_ORDER, 1, c)
    one = pl.Buffered(1)
    const = lambda shape: pl.BlockSpec(shape, lambda o, j, s: tuple(0 for _ in shape), pipeline_mode=one)
    return pl.pallas_call(
        functools.partial(_filter_kernel, nbk=FFT_NB, kag=kag),
        grid=(HY_ORDER, c // cb, FFT_NA // kag),
        in_specs=[const(embt.shape), const(tv.shape), const((hid, kpad)), const((hid, 1)),
                  const((hid, hid)), const((hid, 1)), const((hid, 1)),
                  pl.BlockSpec((1, 2, hid, cb), lambda o, j, s: (o, 0, 0, j)),
                  pl.BlockSpec((1, 1, cb), lambda o, j, s: (o, 0, j)),
                  const(f1_real.shape), const(f2.shape)],
        out_specs=pl.BlockSpec((1, kag, 2 * FFT_NB, cb), lambda o, j, s: (o, s, 0, j)),
        out_shape=jax.ShapeDtypeStruct((HY_ORDER, FFT_NA, 2 * FFT_NB, c), BF16),
        scratch_shapes=[pltpu.VMEM((FFT_NB, 2 * FFT_NA, cb), BF16),
                        pltpu.VMEM((hid, FFT_N), F32)],
        compiler_params=_cparams("arbitrary", "arbitrary", "arbitrary"),
        name="hyena_filter",
    )(jnp.asarray(embt), jnp.asarray(tv), w1t, col(b1), w2.astype(F32).T, col(b2), col(freq), w3r, dec,
      f1_real, f2)


def _out_kernel(x_ref, pos_ref, mod_ref, ys_ref, yh_ref, gw_ref, gb_ref, g5_ref, gh_ref, wo_ref,
                g2_ref, w1_ref, w2_ref, gf_ref, o_ref, *, d):
    m = mod_ref[0]
    gate1, shift2 = m[:, 2 * d:3 * d], m[:, 3 * d:4 * d]
    scale2, gate2 = m[:, 4 * d:5 * d], m[:, 5 * d:6 * d]
    h = x_ref[0] + pos_ref[...]
    ab = jnp.dot(ys_ref[0], gw_ref[...], preferred_element_type=F32) + gb_ref[...]
    half = ab.shape[-1] // 2
    y5 = ab[:, :half] * jax.nn.sigmoid(ab[:, half:])
    mix = jnp.concatenate([_rms(y5, g5_ref[...]), _rms(yh_ref[0].astype(F32), gh_ref[...])], axis=-1)
    h = h + gate1 * jnp.dot(mix.astype(BF16), wo_ref[...], preferred_element_type=F32)
    hn = _rms(h, g2_ref[...]) * (1.0 + scale2) + shift2
    hid = jnp.dot(hn.astype(BF16), w1_ref[...], preferred_element_type=F32)
    hid = jnp.square(jnp.maximum(hid, 0.0))
    h = h + gate2 * jnp.dot(hid.astype(BF16), w2_ref[...], preferred_element_type=F32)
    o_ref[0] = _rms(h, gf_ref[...])


def _output_stage(x, pos, mod3, ys, yh, glu_w, glu_b, g5, gh, w_out, g2, w1, w2, gf, tm=512):
    bsz, n, d = x.shape
    dh = ys.shape[-1]
    one = pl.Buffered(1)
    const = lambda a: pl.BlockSpec(a.shape, lambda i, b: tuple(0 for _ in a.shape), pipeline_mode=one)
    row = lambda a: a.reshape(1, -1).astype(F32)
    glu_b, g5, gh, g2, gf = row(glu_b), row(g5), row(gh), row(g2), row(gf)
    return pl.pallas_call(
        functools.partial(_out_kernel, d=d),
        grid=(n // tm, bsz),
        in_specs=[pl.BlockSpec((1, tm, d), lambda i, b: (b, i, 0)),
                  pl.BlockSpec((tm, d), lambda i, b: (i, 0)),
                  pl.BlockSpec((1, 1, mod3.shape[-1]), lambda i, b: (b, 0, 0)),
                  pl.BlockSpec((1, tm, dh), lambda i, b: (b, i, 0)),
                  pl.BlockSpec((1, tm, dh), lambda i, b: (b, i, 0)),
                  const(glu_w), const(glu_b), const(g5), const(gh), const(w_out), const(g2),
                  const(w1), const(w2), const(gf)],
        out_specs=pl.BlockSpec((1, tm, d), lambda i, b: (b, i, 0)),
        out_shape=jax.ShapeDtypeStruct((bsz, n, d), F32),
        compiler_params=_cparams("arbitrary", "arbitrary"),
        name="mix_mlp_out",
    )(x, pos, mod3, ys, yh, glu_w, glu_b, g5, gh, w_out, g2, w1, w2, gf)


def _pos_table(n, d):
    rows = n // GRID_W
    row = np.repeat(np.arange(rows, dtype=np.float32), GRID_W)
    col = np.tile(np.arange(GRID_W, dtype=np.float32), rows)
    quarter = d // 4
    omega = (1.0 / (POS_BASE ** (np.arange(quarter, dtype=np.float32) / quarter))).astype(np.float32)

    def enc(p):
        ang = p[:, None] * omega[None, :]
        return np.concatenate([np.sin(ang), np.cos(ang)], axis=-1)

    return jnp.asarray(np.concatenate([enc(row), enc(col)], axis=-1).astype(np.float32))


def kernel(x, c, ctx, c_ctx, ada_w, ada_b, norm1_g, w_in, s5_a_re, s5_a_im, s5_log_step, s5_b_re,
           s5_b_im, s5_c_re, s5_c_im, s5_d, s5_glu_w, s5_glu_b, hy_conv_w, hy_conv_b, hy_f_w1,
           hy_f_b1, hy_f_w2, hy_f_b2, hy_f_freq, hy_f_w3, hy_decay, hy_bias, mix_g_s5, mix_g_hy,
           w_out, norm2_g, mlp_w1, mlp_w2, final_g):
    bsz, n, d = x.shape
    depth = ada_w.shape[0]
    d_s5 = s5_d.shape[-1]
    d_hy = hy_bias.shape[-1]
    nctx = ctx.shape[1]
    npair = bsz // 2
    nbk, na = FFT_NB, n // FFT_NB
    pos = _pos_table(n, d)
    tables = _dft_tables(n)

    assert depth == 1 and bsz % 2 == 0 and n % (FFT_NB * 8) == 0 and 2 * n == FFT_N
    mod_rows = 16
    c_rows = jnp.concatenate([c, c_ctx[None], jnp.zeros((mod_rows - bsz - 1, d), c.dtype)], axis=0)
    mod3 = _modulation(c_rows, ada_w[0], ada_b[0]).reshape(mod_rows, 1, N_MOD * d)

    w_in_b = w_in[0].astype(BF16)
    u, z = _project(x, pos, mod3, lambda b: b, norm1_g[0], w_in_b, (d_s5, 3 * d_hy), tm=1024)
    (uc,) = _project(ctx, None, mod3, lambda b: bsz, norm1_g[0], w_in_b, (d_s5,), tm=nctx)

    s5_mats = _s5_tables(s5_a_re[0], s5_a_im[0], s5_log_step[0], s5_b_re[0], s5_b_im[0],
                         s5_c_re[0], s5_c_im[0], s5_d[0])
    u_t = u.reshape(bsz, n // S5_T, S5_T, d_s5).transpose(2, 1, 0, 3).reshape(S5_T, -1, d_s5)
    uc_t = uc.reshape(bsz, nctx // S5_T, S5_T, d_s5).transpose(2, 1, 0, 3).reshape(S5_T, -1, d_s5)
    ys = _s5_scan(u_t, uc_t, bsz, *s5_mats)
    ys = ys.reshape(S5_T, n // S5_T, bsz, d_s5).transpose(2, 1, 0, 3).reshape(bsz, n, d_s5)

    kf = _filter_spectra(n, hy_f_w1[0], hy_f_b1[0], hy_f_w2[0], hy_f_b2[0], hy_f_freq[0],
                         hy_f_w3[0], hy_decay[0], tables)
    z5 = z.reshape(bsz, na, nbk, 3 * d_hy).transpose(0, 2, 1, 3).reshape(2, npair, nbk, na, 3 * d_hy)
    yh = _hyena(z5, hy_conv_w[0], hy_conv_b[0], hy_bias[0], kf, tables)
    yh = yh.reshape(bsz, nbk, na, d_hy).transpose(0, 2, 1, 3).reshape(bsz, n, d_hy)

    return _output_stage(x, pos, mod3, ys, yh, s5_glu_w[0].astype(BF16), s5_glu_b[0], mix_g_s5[0],
                         mix_g_hy[0], w_out[0].astype(BF16), norm2_g[0], mlp_w1[0].astype(BF16),
                         mlp_w2[0].astype(BF16), final_g)
```

```python
import functools
import math

import numpy as np
import jax
import jax.numpy as jnp
from jax import lax
from jax.experimental import pallas as pl
from jax.experimental.pallas import tpu as pltpu

F32 = jnp.float32
BF16 = jnp.bfloat16
HI = lax.Precision.HIGHEST

EPS = 1e-6
GRID_W = 64
POS_BASE = 10000.0
N_MOD = 6
S5_C = 16
S5_P = 64
S5_T = 16
HY_BANDS = 16
HY_ORDER = 2

FFT_N = 8192
FFT_NA = 128
FFT_NB = 64

LANES = 128
MXU_N = 256
SLAB = 16
RELAYOUT_ROWS = 256
VMEM_LIMIT = 56 * 1024 * 1024


def _cparams(*sem):
    return pltpu.CompilerParams(dimension_semantics=sem, vmem_limit_bytes=VMEM_LIMIT)


def _rms(x, g):
    return x * lax.rsqrt(jnp.mean(x * x, axis=-1, keepdims=True) + EPS) * g


def _mod_kernel(c_ref, w_ref, b_ref, o_ref):
    c = c_ref[...]
    a = c * jax.nn.sigmoid(c)
    o_ref[...] = jnp.dot(a.astype(BF16), w_ref[...].astype(BF16),
                         preferred_element_type=F32) + b_ref[...]


def _modulation(c_rows, ada_w, ada_b):
    rows, d = c_rows.shape
    n = ada_w.shape[1]
    bn = 1024
    return pl.pallas_call(
        _mod_kernel,
        grid=(n // bn,),
        in_specs=[pl.BlockSpec((rows, d), lambda j: (0, 0)),
                  pl.BlockSpec((d, bn), lambda j: (0, j)),
                  pl.BlockSpec((1, bn), lambda j: (0, j))],
        out_specs=pl.BlockSpec((rows, bn), lambda j: (0, j)),
        out_shape=jax.ShapeDtypeStruct((rows, n), F32),
        compiler_params=_cparams("arbitrary"),
        name="ada_mod",
    )(c_rows, ada_w, ada_b.reshape(1, n))


def _proj_kernel(*refs, d, use_pos, n_split):
    if use_pos:
        x_ref, pos_ref, mod_ref, g_ref, w_ref = refs[:5]
        outs = refs[5:]
        h = x_ref[0] + pos_ref[...]
    else:
        x_ref, mod_ref, g_ref, w_ref = refs[:4]
        outs = refs[4:]
        h = x_ref[0]
    m = mod_ref[0]
    hn = _rms(h, g_ref[...]) * (1.0 + m[:, d:2 * d]) + m[:, 0:d]
    proj = jnp.dot(hn.astype(BF16), w_ref[...], preferred_element_type=F32)
    off = 0
    for o_ref, width in zip(outs, n_split):
        o_ref[0] = proj[:, off:off + width].astype(o_ref.dtype)
        off += width


def _project(x, pos, mod3, mod_row_of_batch, g, w_bf16, n_split, tm):
    bsz, n, d = x.shape
    width = sum(n_split)
    use_pos = pos is not None
    in_specs = [pl.BlockSpec((1, tm, d), lambda i, b: (b, i, 0))]
    args = [x]
    if use_pos:
        in_specs.append(pl.BlockSpec((tm, d), lambda i, b: (i, 0)))
        args.append(pos)
    in_specs += [pl.BlockSpec((1, 1, mod3.shape[-1]), lambda i, b: (mod_row_of_batch(b), 0, 0)),
                 pl.BlockSpec((1, d), lambda i, b: (0, 0)),
                 pl.BlockSpec((d, width), lambda i, b: (0, 0))]
    args += [mod3, g.reshape(1, d), w_bf16]
    return pl.pallas_call(
        functools.partial(_proj_kernel, d=d, use_pos=use_pos, n_split=n_split),
        grid=(n // tm, bsz),
        in_specs=in_specs,
        out_specs=[pl.BlockSpec((1, tm, wd), lambda i, b: (b, i, 0)) for wd in n_split],
        out_shape=[jax.ShapeDtypeStruct((bsz, n, wd), BF16) for wd in n_split],
        compiler_params=_cparams("arbitrary", "arbitrary"),
        name="norm_proj",
    )(*args)


def _s5_tables(a_re, a_im, log_step, b_re, b_im, c_re, c_im, d_skip):
    T, C, P = S5_T, S5_C, S5_P
    G = a_re.shape[1]
    gpb = LANES // C
    slots = np.arange(T)
    tok = np.stack([8 * (slots // 8) + (slots % 8 - o) % 8 for o in range(gpb)])
    tok_g = tok[np.arange(G) % gpb]

    step = jnp.exp(log_step.astype(F32))[..., None]
    ar = a_re.astype(F32) * step
    ai = a_im.astype(F32) * step

    def powers(d, expo):
        e = jnp.asarray(np.broadcast_to(expo, (G, expo.shape[-1])).astype(np.float32))[:, :, None]
        mag = jnp.exp(ar[d][:, None, :] * e)
        return mag * jnp.cos(ai[d][:, None, :] * e), mag * jnp.sin(ai[d][:, None, :] * e)

    lr, li = a_re.astype(F32), a_im.astype(F32)
    first = [powers(d, np.ones(1)) for d in range(2)]
    nr = jnp.stack([first[d][0][:, 0] for d in range(2)]) - 1.0
    ni = jnp.stack([first[d][1][:, 0] for d in range(2)])
    den = lr * lr + li * li
    qr = (nr * lr + ni * li) / den
    qi = (ni * lr - nr * li) / den
    bbr = qr[..., None] * b_re - qi[..., None] * b_im
    bbi = qr[..., None] * b_im + qi[..., None] * b_re
    cr, ci = c_re.astype(F32), c_im.astype(F32)

    kerns = []
    for d in range(2):
        pr, pi = powers(d, np.arange(T))
        wr = cr[d][:, None] * pr[:, :, None, :] - ci[d][:, None] * pi[:, :, None, :]
        wi = cr[d][:, None] * pi[:, :, None, :] + ci[d][:, None] * pr[:, :, None, :]
        kerns.append(jnp.einsum('gtcp,gpe->gtce', wr, bbr[d], precision=HI)
                     - jnp.einsum('gtcp,gpe->gtce', wi, bbi[d], precision=HI))
    kcat = jnp.concatenate(kerns, axis=1)
    kpad = jnp.pad(kcat.reshape(G, 2 * T * C, C), ((0, 0), (0, 0), (0, LANES - C)))
    lag = np.repeat(tok[:, :, None] - tok[:, None, :], C, axis=2).astype(np.int32)
    d_lanes = jnp.tile(d_skip.astype(F32).reshape(G, 1, C), (1, 1, T))
    m1 = pl.pallas_call(
        _s5_intra_kernel,
        grid=(G,),
        in_specs=[pl.BlockSpec((1, 2 * T * C, LANES), lambda g: (g, 0, 0)),
                  pl.BlockSpec((1, T, T * C), lambda g: (g % gpb, 0, 0)),
                  pl.BlockSpec((1, 1, T * C), lambda g: (g, 0, 0))],
        out_specs=pl.BlockSpec((1, T * C, T * C), lambda g: (g, 0, 0)),
        out_shape=jax.ShapeDtypeStruct((G, T * C, T * C), BF16),
        compiler_params=_cparams("arbitrary"),
        name="s5_intra_table",
    )(kpad, jnp.asarray(lag), d_lanes)

    def in_mat(d, expo):
        p_r, p_i = powers(d, expo)
        br, bi = bbr[d].transpose(0, 2, 1)[:, None], bbi[d].transpose(0, 2, 1)[:, None]
        er = p_r[:, :, None, :] * br - p_i[:, :, None, :] * bi
        ei = p_r[:, :, None, :] * bi + p_i[:, :, None, :] * br
        return er.reshape(G, T * C, P), ei.reshape(G, T * C, P)

    efr, efi = in_mat(0, T - 1 - tok_g)
    ebr, ebi = in_mat(1, tok_g)
    m2 = jnp.concatenate([efr, ebr, efi, ebi], axis=-1)

    def out_mat(d, expo):
        p_r, p_i = powers(d, expo)
        crt = cr[d].transpose(0, 2, 1)[:, :, None, :]
        cit = ci[d].transpose(0, 2, 1)[:, :, None, :]
        prt = p_r.transpose(0, 2, 1)[:, :, :, None]
        pit = p_i.transpose(0, 2, 1)[:, :, :, None]
        vr = crt * prt - cit * pit
        vi = crt * pit + cit * prt
        return vr.reshape(G, P, T * C), -vi.reshape(G, P, T * C)

    vfr, vfi = out_mat(0, tok_g + 1)
    vbr, vbi = out_mat(1, T - tok_g)
    m3 = jnp.concatenate([vfr, vbr, vfi, vbi], axis=1)

    last = [powers(d, np.full(1, T)) for d in range(2)]
    lam = jnp.stack([jnp.concatenate([last[0][c][:, 0], last[1][c][:, 0]], axis=-1)
                     for c in range(2)], axis=1)
    is_fwd = jnp.asarray((np.arange(4 * P) // P) % 2 == 0)[None, :, None]
    m3f = jnp.where(is_fwd, m3, 0.0).astype(BF16)
    m3b = jnp.where(is_fwd, 0.0, m3).astype(BF16)
    return m1, m2.astype(BF16), m3f, m3b, lam


def _s5_intra_kernel(k_ref, lag_ref, d_ref, o_ref):
    T, C = S5_T, S5_C
    k = k_ref[0]
    shift = C
    while shift < LANES:
        k = k + pltpu.roll(k, shift, axis=1)
        shift *= 2
    k = jnp.concatenate([k, k], axis=1)
    block = lambda x: k[x * C:(x + 1) * C]
    row = lax.broadcasted_iota(jnp.int32, (C, T * C), 0)
    lane_c = lax.broadcasted_iota(jnp.int32, (C, T * C), 1) % C
    centre = block(0) + block(T) + jnp.where(row == lane_c, d_ref[0], 0.0)
    block_rows = []
    for l in range(T):
        lag = lag_ref[0, l:l + 1, :]
        acc = centre
        for x in range(1, T):
            acc = jnp.where(lag == x, block(x), acc)
            acc = jnp.where(lag == -x, block(T + x), acc)
        block_rows.append(acc)
    o_ref[0] = jnp.concatenate(block_rows, axis=0).T.astype(BF16)


def _gelu_tanh(x):
    return 0.5 * x * (1.0 + jnp.tanh(math.sqrt(2.0 / math.pi) * (x + 0.044715 * (x * x * x))))


def _s5_kernel(u_ref, uc_ref, m1_ref, m2_ref, m3f_ref, m3b_ref, lam_ref, o_ref,
               z_ref, e_ref, sf_ref, sb_ref, *, nbatch, nchunk, nctx):
    groups = LANES // S5_C
    rows = nbatch * nchunk
    lane = lax.broadcasted_iota(jnp.int32, (1, LANES), 1)
    slot_bits = [(((lane // S5_C) >> j) & 1) == 1 for j in range(3)]

    def skew(xs):
        cur = [xs[(-k) % groups] for k in range(groups)]
        for j, bit in enumerate(slot_bits):
            cur = [jnp.where(bit, cur[(i - (1 << j)) % groups], cur[i]) for i in range(groups)]
        return cur

    def to_chunk_rows(read_t, nrows, row0):
        for h in range(2):
            rolled = []
            for t8 in range(groups):
                v = read_t(8 * h + t8).astype(F32)
                rolled.append(pltpu.roll(v, t8 * S5_C, axis=1) if t8 else v)
            for g, zg in enumerate(skew(rolled)):
                z_ref[g, row0:row0 + nrows, h * LANES:(h + 1) * LANES] = zg.astype(BF16)

    for r0 in range(0, rows, RELAYOUT_ROWS):
        to_chunk_rows(lambda t, r0=r0: u_ref[t, r0:r0 + RELAYOUT_ROWS], RELAYOUT_ROWS, r0)
    to_chunk_rows(lambda t: uc_ref[t], nbatch * nctx, rows)

    wide = 2 * LANES
    fwd_lane = (lax.broadcasted_iota(jnp.int32, (1, wide), 1) % LANES) < (LANES // 2)
    tile = lambda k: pl.ds(pl.multiple_of(k * nbatch, nbatch), nbatch)

    for g in range(groups):
        zg = z_ref[g]
        e_ref[...] = jnp.dot(zg, m2_ref[g], preferred_element_type=F32)
        lam_r = lam_ref[g, 0:1, :]
        lam_i = lam_ref[g, 1:2, :]

        def advance(sr, si, kf, kb):
            e2 = jnp.where(fwd_lane, e_ref[tile(kf), :], e_ref[tile(kb), :])
            return (lam_r * sr - lam_i * si + e2[:, :LANES], lam_r * si + lam_i * sr + e2[:, LANES:])

        sr = jnp.zeros((nbatch, LANES), F32)
        si = jnp.zeros((nbatch, LANES), F32)
        for i in range(nctx):
            sr, si = advance(sr, si, nchunk + i, nchunk + nctx - 1 - i)

        def body(i, carry):
            sr, si = carry
            kf, kb = i, nchunk - 1 - i
            s = jnp.concatenate([sr, si], axis=-1)
            sf_ref[tile(kf), :] = s
            sb_ref[tile(kb), :] = s
            return advance(sr, si, kf, kb)

        lax.fori_loop(0, nchunk, body, (sr, si), unroll=2)

        y = (jnp.dot(zg[:rows], m1_ref[g], preferred_element_type=F32)
             + jnp.dot(sf_ref[...].astype(BF16), m3f_ref[g], preferred_element_type=F32)
             + jnp.dot(sb_ref[...].astype(BF16), m3b_ref[g], preferred_element_type=F32))
        z_ref[g, 0:rows, :] = _gelu_tanh(y).astype(BF16)

    for r0 in range(0, rows, RELAYOUT_ROWS):
        for h in range(2):
            ys = [z_ref[g, r0:r0 + RELAYOUT_ROWS, h * LANES:(h + 1) * LANES].astype(F32)
                  for g in range(groups)]
            for t8, acc in enumerate(skew(ys)):
                if t8:
                    acc = pltpu.roll(acc, LANES - t8 * S5_C, axis=1)
                o_ref[8 * h + t8, r0:r0 + RELAYOUT_ROWS] = acc.astype(BF16)


def _s5_scan(u_t, uc_t, bsz, m1, m2, m3f, m3b, lam):
    T, rows, width = u_t.shape
    ctx_rows = uc_t.shape[1]
    gpb = LANES // S5_C
    kern = functools.partial(_s5_kernel, nbatch=bsz, nchunk=rows // bsz, nctx=ctx_rows // bsz)
    one = pl.Buffered(1)
    mat = pl.BlockSpec((gpb, 2 * LANES, 2 * LANES), lambda j: (j, 0, 0))
    return pl.pallas_call(
        kern,
        grid=(width // LANES,),
        in_specs=[pl.BlockSpec((T, rows, LANES), lambda j: (0, 0, j), pipeline_mode=one),
                  pl.BlockSpec((T, ctx_rows, LANES), lambda j: (0, 0, j)),
                  mat, mat, mat, mat,
                  pl.BlockSpec((gpb, 2, LANES), lambda j: (j, 0, 0))],
        out_specs=pl.BlockSpec((T, rows, LANES), lambda j: (0, 0, j), pipeline_mode=one),
        out_shape=jax.ShapeDtypeStruct(u_t.shape, BF16),
        scratch_shapes=[pltpu.VMEM((gpb, rows + ctx_rows, 2 * LANES), BF16),
                        pltpu.VMEM((rows + ctx_rows, 2 * LANES), F32),
                        pltpu.VMEM((rows, 2 * LANES), F32),
                        pltpu.VMEM((rows, 2 * LANES), F32)],
        compiler_params=_cparams("arbitrary"),
        name="s5_scan",
    )(u_t, uc_t, m1, m2, m3f, m3b, lam)


def _dft_tables(n_seq):
    na_sig = n_seq // FFT_NB
    ka = np.arange(FFT_NA)[:, None]
    nb = np.arange(FFT_NB)[:, None, None]

    def stage1(n_in):
        na = np.arange(n_in)[None, :]
        ang = -2.0 * np.pi * (na * ka / FFT_NA)[None] - 2.0 * np.pi * (nb * ka[None]) / FFT_N
        return np.cos(ang), np.sin(ang)

    c, s = stage1(na_sig)
    f1 = np.concatenate([np.concatenate([c, -s], axis=2), np.concatenate([s, c], axis=2)], axis=1)
    c, s = stage1(FFT_NA)
    f1_real = np.concatenate([c, s], axis=1)
    kb = np.arange(FFT_NB)[:, None]
    nbv = np.arange(FFT_NB)[None, :]
    ang = -2.0 * np.pi * kb * nbv / FFT_NB
    c, s = np.cos(ang), np.sin(ang)
    f2 = np.block([[c, -s], [s, c]])
    f2_inv = np.block([[c, s], [-s, c]]) / FFT_NB
    nap = np.arange(na_sig)[:, None]
    kav = np.arange(FFT_NA)[None, :]
    ang = 2.0 * np.pi * (nap * kav / FFT_NA)[None] + 2.0 * np.pi * (nb * kav[None]) / FFT_N
    c, s = np.cos(ang) / FFT_NA, np.sin(ang) / FFT_NA
    f3 = np.concatenate([np.concatenate([c, -s], axis=2), np.concatenate([s, c], axis=2)], axis=1)
    as_bf16 = lambda a: jnp.asarray(a.astype(np.float32)).astype(BF16)
    return as_bf16(f1), as_bf16(f1_real), as_bf16(f2), as_bf16(f2_inv), as_bf16(f3)


def _conv3_slab(z_ref, plane, nb0, w, b, nbk):
    def rows(i):
        return z_ref[plane, 0, i].astype(F32)

    mid = [rows(nb0 + i) for i in range(SLAB)]
    na = mid[0].shape[0]
    ridx = lax.broadcasted_iota(jnp.int32, mid[0].shape, 0)
    wrapped = rows((nb0 + nbk - 1) % nbk)
    before = jnp.where(nb0 == 0, jnp.where(ridx == 0, 0.0, pltpu.roll(wrapped, 1, axis=0)), wrapped)
    wrapped = rows((nb0 + SLAB) % nbk)
    after = jnp.where(nb0 + SLAB == nbk, jnp.where(ridx == na - 1, 0.0, pltpu.roll(wrapped, na - 1, axis=0)),
                      wrapped)
    ext = [before] + mid + [after]
    return [ext[i] * w[0:1] + ext[i + 1] * w[1:2] + ext[i + 2] * w[2:3] + b for i in range(SLAB)]


def _for_row_slabs(nbk, fn):
    lax.fori_loop(0, nbk // SLAB, lambda j, c: (fn(pl.multiple_of(j * SLAB, SLAB)), c)[1], 0)


def _regroup(x):
    return jnp.swapaxes(x, 0, 1)


def _hyena_kernel(zv_ref, zg1_ref, zg2_ref, wv_ref, bv_ref, wg1_ref, bg1_ref, wg2_ref, bg2_ref,
                  kf_ref, f1_ref, f2_ref, f2i_ref, f3_ref, o_ref, s_ref, z1_ref,
                  *, nbk, na, ngrp, kag):
    step = pl.program_id(2)
    hb = FFT_NB

    def stage1(src_ref, conv):
        def slab(nb0):
            if conv:
                planes = [[v.astype(BF16) for v in _conv3_slab(src_ref, pln, nb0, wv_ref[...], bv_ref[...], nbk)]
                          for pln in range(2)]
            tiles = []
            for i in range(SLAB):
                nb = nb0 + i
                parts = [planes[0][i], planes[1][i]] if conv else [src_ref[0, 0, nb], src_ref[1, 0, nb]]
                rhs = jnp.concatenate(parts, axis=0)
                tiles.append(jnp.dot(f1_ref[nb], rhs, preferred_element_type=F32).astype(BF16))
            s_ref[:, pl.ds(nb0, SLAB), :] = _regroup(jnp.stack(tiles, axis=0))
        _for_row_slabs(nbk, slab)

    def stage3(gate_ref, wg, bg, dst_ref):
        def slab(nb0):
            rows = _regroup(s_ref[:, pl.ds(nb0, SLAB), :])
            gates = [_conv3_slab(gate_ref, pln, nb0, wg, bg, nbk) for pln in range(2)]
            for i in range(SLAB):
                nb = nb0 + i
                y = jnp.dot(f3_ref[nb], rows[i], preferred_element_type=F32)
                for pln in range(2):
                    dst_ref[pln, 0, nb] = (gates[pln][i] * y[pln * na:(pln + 1) * na]).astype(dst_ref.dtype)
        _for_row_slabs(nbk, slab)

    @pl.when(step == 0)
    def _():
        stage1(zv_ref, True)

    @pl.when(step == ngrp)
    def _():
        stage1(z1_ref, False)

    base = (step % ngrp) * kag
    loaded = [jnp.concatenate([s_ref[base + k], s_ref[FFT_NA + base + k]], axis=0) for k in range(kag)]
    results = []
    for k, rhs in enumerate(loaded):
        spec = jnp.dot(f2_ref[...], rhs, preferred_element_type=F32).astype(BF16)
        kf = kf_ref[0, k]
        sr, si = spec[:hb], spec[hb:]
        kr, ki = kf[:hb], kf[hb:]
        prod = jnp.concatenate([sr * kr - si * ki, sr * ki + si * kr], axis=0)
        results.append(jnp.dot(f2i_ref[...], prod, preferred_element_type=F32).astype(BF16))
    for k, back in enumerate(results):
        s_ref[base + k] = back[:hb]
        s_ref[FFT_NA + base + k] = back[hb:]

    @pl.when(step == ngrp - 1)
    def _():
        stage3(zg1_ref, wg1_ref[...], bg1_ref[...], z1_ref)

    @pl.when(step == 2 * ngrp - 1)
    def _():
        stage3(zg2_ref, wg2_ref[...], bg2_ref[...], o_ref)


def _hyena(z5, conv_w, conv_b, kf, tables, cb=MXU_N, kag=2 * SLAB):
    f1, _, f2, f2i, f3 = tables
    _, npair, nbk, na, _ = z5.shape
    c = kf.shape[-1]
    ncb = c // cb
    ngrp = FFT_NA // kag
    one = pl.Buffered(1)
    zspec = lambda grp: pl.BlockSpec((2, 1, nbk, na, cb), lambda p, j, s: (0, p, 0, 0, grp * ncb + j),
                                     pipeline_mode=one)
    wspec = lambda grp: pl.BlockSpec((3, cb), lambda p, j, s: (0, grp * ncb + j))
    bspec = lambda grp: pl.BlockSpec((1, cb), lambda p, j, s: (0, grp * ncb + j))
    const = lambda a: pl.BlockSpec(a.shape, lambda p, j, s: tuple(0 for _ in a.shape), pipeline_mode=one)
    cw = conv_w.astype(F32)
    cbias = conv_b.astype(F32).reshape(1, -1)
    return pl.pallas_call(
        functools.partial(_hyena_kernel, nbk=nbk, na=na, ngrp=ngrp, kag=kag),
        grid=(npair, ncb, 2 * ngrp),
        in_specs=[zspec(0), zspec(1), zspec(2), wspec(0), bspec(0), wspec(1), bspec(1), wspec(2), bspec(2),
                  pl.BlockSpec((1, kag, 2 * FFT_NB, cb), lambda p, j, s: (s // ngrp, s % ngrp, 0, j)),
                  const(f1), const(f2), const(f2i), const(f3)],
        out_specs=pl.BlockSpec((2, 1, nbk, na, cb), lambda p, j, s: (0, p, 0, 0, j), pipeline_mode=one),
        out_shape=jax.ShapeDtypeStruct((2, npair, nbk, na, c), BF16),
        scratch_shapes=[pltpu.VMEM((2 * FFT_NA, nbk, cb), BF16),
                        pltpu.VMEM((2, 1, nbk, na, cb), BF16)],
        compiler_params=_cparams("arbitrary", "arbitrary", "arbitrary"),
        name="hyena_conv",
    )(z5, z5, z5, cw, cbias, cw, cbias, cw, cbias, kf, f1, f2, f2i, f3)


def _filter_kernel(embt_ref, tv_ref, w1t_ref, b1_ref, w2t_ref, b2_ref, fr_ref, w3_ref, dec_ref, bias_ref,
                   f1_ref, f2_ref, k_ref, s_ref, h_ref, *, nbk, kag):
    first = (pl.program_id(0) == 0) & (pl.program_id(1) == 0)
    step = pl.program_id(2)
    half = FFT_NA // 2
    lanes_per_pass = 8 * FFT_NA

    @pl.when(first & (step == 0))
    def _():
        fr = fr_ref[...]
        for i in range(embt_ref.shape[1] // lanes_per_pass):
            cols = slice(i * lanes_per_pass, (i + 1) * lanes_per_pass)
            h = jnp.sin(fr * (jnp.dot(w1t_ref[...], embt_ref[:, cols], precision=HI,
                                      preferred_element_type=F32) + b1_ref[...]))
            h_ref[:, cols] = jnp.sin(fr * (jnp.dot(w2t_ref[...], h, precision=HI,
                                                   preferred_element_type=F32) + b2_ref[...]))

    @pl.when(step == 0)
    def _():
        dec = dec_ref[0]

        def slab(nb0):
            tiles = []
            for i in range(SLAB):
                nb = nb0 + i
                h = h_ref[:, pl.ds(pl.multiple_of(nb * FFT_NA, FFT_NA), FFT_NA)].T.astype(BF16)
                fwd = jnp.dot(h[:half], w3_ref[0, 0], preferred_element_type=F32)
                bwd = jnp.dot(h[half:], w3_ref[0, 1], preferred_element_type=F32)
                tv = tv_ref[nb]
                window = jnp.exp(-tv[:, 0:1] * dec) * tv[:, 1:2]
                filt = jnp.concatenate([fwd, bwd], axis=0) * window
                tiles.append(jnp.dot(f1_ref[nb], filt.astype(BF16), preferred_element_type=F32).astype(BF16))
            s_ref[:, pl.ds(nb0, SLAB), :] = _regroup(jnp.stack(tiles, axis=0))
        _for_row_slabs(nbk, slab)

    base = step * kag
    bias = bias_ref[0]
    for k in range(kag):
        rhs = jnp.concatenate([s_ref[base + k], s_ref[FFT_NA + base + k]], axis=0)
        spec = jnp.dot(f2_ref[...], rhs, preferred_element_type=F32)
        k_ref[0, k] = jnp.concatenate([spec[:FFT_NB] + bias, spec[FFT_NB:]], axis=0).astype(BF16)


def _filter_spectra(n, w1, b1, w2, b2, freq, w3, decay, bias, tables, cb=MXU_N, kag=16):
    _, f1_real, f2, _, _ = tables
    hid = w2.shape[0]
    c = decay.shape[-1]
    t = np.linspace(0.0, 1.0, n, dtype=np.float32)[:, None]
    w = (2.0 * np.pi * np.arange(n, dtype=np.float32) / n).astype(np.float32)
    bands = np.linspace(1e-4, HY_BANDS - 1, HY_BANDS, dtype=np.float32)
    ang = w[:, None] * bands[None, :]
    emb = np.concatenate([t, np.cos(ang), -np.sin(ang)], axis=-1).astype(np.float32)
    kpad = 64
    idx = FFT_NB * np.arange(FFT_NA)[None, :] + np.arange(FFT_NB)[:, None]
    lagi = np.minimum(np.where(idx < n, idx, FFT_N - idx), n - 1)
    embt = np.zeros((kpad, FFT_N), np.float32)
    embt[:emb.shape[1]] = emb[lagi.reshape(-1)].T
    tv = np.stack([t[lagi, 0], (idx != n).astype(np.float32)], axis=-1)
    col = lambda a: a.reshape(hid, 1).astype(F32)
    w1t = jnp.pad(w1.astype(F32), ((0, kpad - w1.shape[0]), (0, 0))).T
    w3r = w3.astype(BF16).reshape(hid, HY_ORDER, 2, c).transpose(1, 2, 0, 3)
    dec = jnp.abs(decay.astype(F32)).reshape(HY_ORDER, 1, c)
    one = pl.Buffered(1)
    const = lambda shape: pl.BlockSpec(shape, lambda o, j, s: tuple(0 for _ in shape), pipeline_mode=one)
    per_channel = pl.BlockSpec((1, 1, cb), lambda o, j, s: (o, 0, j))
    return pl.pallas_call(
        functools.partial(_filter_kernel, nbk=FFT_NB, kag=kag),
        grid=(HY_ORDER, c // cb, FFT_NA // kag),
        in_specs=[const(embt.shape), const(tv.shape), const((hid, kpad)), const((hid, 1)),
                  const((hid, hid)), const((hid, 1)), const((hid, 1)),
                  pl.BlockSpec((1, 2, hid, cb), lambda o, j, s: (o, 0, 0, j)),
                  per_channel, per_channel,
                  const(f1_real.shape), const(f2.shape)],
        out_specs=pl.BlockSpec((1, kag, 2 * FFT_NB, cb), lambda o, j, s: (o, s, 0, j)),
        out_shape=jax.ShapeDtypeStruct((HY_ORDER, FFT_NA, 2 * FFT_NB, c), BF16),
        scratch_shapes=[pltpu.VMEM((2 * FFT_NA, FFT_NB, cb), BF16),
                        pltpu.VMEM((hid, FFT_N), F32)],
        compiler_params=_cparams("arbitrary", "arbitrary", "arbitrary"),
        name="hyena_filter",
    )(jnp.asarray(embt), jnp.asarray(tv), w1t, col(b1), w2.astype(F32).T, col(b2), col(freq), w3r, dec,
      bias.astype(F32).reshape(HY_ORDER, 1, c), f1_real, f2)


def _out_kernel(x_ref, pos_ref, mod_ref, ys_ref, yh_ref, gw_ref, gb_ref, g5_ref, gh_ref, wo_ref,
                g2_ref, w1_ref, w2_ref, gf_ref, o_ref, *, d):
    m = mod_ref[0]
    gate1, shift2 = m[:, 2 * d:3 * d], m[:, 3 * d:4 * d]
    scale2, gate2 = m[:, 4 * d:5 * d], m[:, 5 * d:6 * d]
    h = x_ref[0] + pos_ref[...]
    ab = jnp.dot(ys_ref[0], gw_ref[...], preferred_element_type=F32) + gb_ref[...]
    half = ab.shape[-1] // 2
    y5 = ab[:, :half] * jax.nn.sigmoid(ab[:, half:])
    mix = jnp.concatenate([_rms(y5, g5_ref[...]), _rms(yh_ref[0].astype(F32), gh_ref[...])], axis=-1)
    h = h + gate1 * jnp.dot(mix.astype(BF16), wo_ref[...], preferred_element_type=F32)
    hn = _rms(h, g2_ref[...]) * (1.0 + scale2) + shift2
    hid = jnp.dot(hn.astype(BF16), w1_ref[...], preferred_element_type=F32)
    hid = jnp.square(jnp.maximum(hid, 0.0))
    h = h + gate2 * jnp.dot(hid.astype(BF16), w2_ref[...], preferred_element_type=F32)
    o_ref[0] = _rms(h, gf_ref[...])


def _output_stage(x, pos, mod3, ys, yh, glu_w, glu_b, g5, gh, w_out, g2, w1, w2, gf, tm=512):
    bsz, n, d = x.shape
    dh = ys.shape[-1]
    one = pl.Buffered(1)
    const = lambda a: pl.BlockSpec(a.shape, lambda i, b: tuple(0 for _ in a.shape), pipeline_mode=one)
    row = lambda a: a.reshape(1, -1).astype(F32)
    glu_b, g5, gh, g2, gf = row(glu_b), row(g5), row(gh), row(g2), row(gf)
    return pl.pallas_call(
        functools.partial(_out_kernel, d=d),
        grid=(n // tm, bsz),
        in_specs=[pl.BlockSpec((1, tm, d), lambda i, b: (b, i, 0)),
                  pl.BlockSpec((tm, d), lambda i, b: (i, 0)),
                  pl.BlockSpec((1, 1, mod3.shape[-1]), lambda i, b: (b, 0, 0)),
                  pl.BlockSpec((1, tm, dh), lambda i, b: (b, i, 0)),
                  pl.BlockSpec((1, tm, dh), lambda i, b: (b, i, 0)),
                  const(glu_w), const(glu_b), const(g5), const(gh), const(w_out), const(g2),
                  const(w1), const(w2), const(gf)],
        out_specs=pl.BlockSpec((1, tm, d), lambda i, b: (b, i, 0)),
        out_shape=jax.ShapeDtypeStruct((bsz, n, d), F32),
        compiler_params=_cparams("arbitrary", "arbitrary"),
        name="mix_mlp_out",
    )(x, pos, mod3, ys, yh, glu_w, glu_b, g5, gh, w_out, g2, w1, w2, gf)


def _pos_table(n, d):
    rows = n // GRID_W
    row = np.repeat(np.arange(rows, dtype=np.float32), GRID_W)
    col = np.tile(np.arange(GRID_W, dtype=np.float32), rows)
    quarter = d // 4
    omega = (1.0 / (POS_BASE ** (np.arange(quarter, dtype=np.float32) / quarter))).astype(np.float32)

    def enc(p):
        ang = p[:, None] * omega[None, :]
        return np.concatenate([np.sin(ang), np.cos(ang)], axis=-1)

    return jnp.asarray(np.concatenate([enc(row), enc(col)], axis=-1).astype(np.float32))


def kernel(x, c, ctx, c_ctx, ada_w, ada_b, norm1_g, w_in, s5_a_re, s5_a_im, s5_log_step, s5_b_re,
           s5_b_im, s5_c_re, s5_c_im, s5_d, s5_glu_w, s5_glu_b, hy_conv_w, hy_conv_b, hy_f_w1,
           hy_f_b1, hy_f_w2, hy_f_b2, hy_f_freq, hy_f_w3, hy_decay, hy_bias, mix_g_s5, mix_g_hy,
           w_out, norm2_g, mlp_w1, mlp_w2, final_g):
    bsz, n, d = x.shape
    depth = ada_w.shape[0]
    d_s5 = s5_d.shape[-1]
    d_hy = hy_bias.shape[-1]
    nctx = ctx.shape[1]
    npair = bsz // 2
    nbk, na = FFT_NB, n // FFT_NB
    pos = _pos_table(n, d)
    tables = _dft_tables(n)

    assert depth == 1 and bsz % 2 == 0 and n % (FFT_NB * 8) == 0 and 2 * n == FFT_N
    mod_rows = 16
    c_rows = jnp.concatenate([c, c_ctx[None], jnp.zeros((mod_rows - bsz - 1, d), c.dtype)], axis=0)
    mod3 = _modulation(c_rows, ada_w[0], ada_b[0]).reshape(mod_rows, 1, N_MOD * d)

    w_in_b = w_in[0].astype(BF16)
    u, z = _project(x, pos, mod3, lambda b: b, norm1_g[0], w_in_b, (d_s5, 3 * d_hy), tm=1024)
    (uc,) = _project(ctx, None, mod3, lambda b: bsz, norm1_g[0], w_in_b, (d_s5,), tm=nctx)

    s5_mats = _s5_tables(s5_a_re[0], s5_a_im[0], s5_log_step[0], s5_b_re[0], s5_b_im[0],
                         s5_c_re[0], s5_c_im[0], s5_d[0])
    u_t = u.reshape(bsz, n // S5_T, S5_T, d_s5).transpose(2, 1, 0, 3).reshape(S5_T, -1, d_s5)
    uc_t = uc.reshape(bsz, nctx // S5_T, S5_T, d_s5).transpose(2, 1, 0, 3).reshape(S5_T, -1, d_s5)
    ys = _s5_scan(u_t, uc_t, bsz, *s5_mats)
    ys = ys.reshape(S5_T, n // S5_T, bsz, d_s5).transpose(2, 1, 0, 3).reshape(bsz, n, d_s5)

    kf = _filter_spectra(n, hy_f_w1[0], hy_f_b1[0], hy_f_w2[0], hy_f_b2[0], hy_f_freq[0],
                         hy_f_w3[0], hy_decay[0], hy_bias[0], tables)
    z5 = z.reshape(bsz, na, nbk, 3 * d_hy).transpose(0, 2, 1, 3).reshape(2, npair, nbk, na, 3 * d_hy)
    yh = _hyena(z5, hy_conv_w[0], hy_conv_b[0], kf, tables)
    yh = yh.reshape(bsz, nbk, na, d_hy).transpose(0, 2, 1, 3).reshape(bsz, n, d_hy)

    return _output_stage(x, pos, mod3, ys, yh, s5_glu_w[0].astype(BF16), s5_glu_b[0], mix_g_s5[0],
                         mix_g_hy[0], w_out[0].astype(BF16), norm2_g[0], mlp_w1[0].astype(BF16),
                         mlp_w2[0].astype(BF16), final_g)
```

```python
import functools
import math

import numpy as np
import jax
import jax.numpy as jnp
from jax import lax
from jax.experimental import pallas as pl
from jax.experimental.pallas import tpu as pltpu

F32 = jnp.float32
BF16 = jnp.bfloat16
HI = lax.Precision.HIGHEST

EPS = 1e-6
GRID_W = 64
POS_BASE = 10000.0
N_MOD = 6
S5_C = 16
S5_P = 64
S5_T = 16
HY_BANDS = 16
HY_ORDER = 2

FFT_N = 8192
FFT_NA = 128
FFT_NB = 64

LANES = 128
MXU_N = 256
SLAB = 16
RELAYOUT_ROWS = 256
VMEM_LIMIT = 56 * 1024 * 1024


def _cparams(*sem):
    return pltpu.CompilerParams(dimension_semantics=sem, vmem_limit_bytes=VMEM_LIMIT)


def _rms(x, g):
    return x * lax.rsqrt(jnp.mean(x * x, axis=-1, keepdims=True) + EPS) * g


def _mod_kernel(c_ref, w_ref, b_ref, o_ref):
    c = c_ref[...]
    a = c * jax.nn.sigmoid(c)
    o_ref[...] = jnp.dot(a.astype(BF16), w_ref[...].astype(BF16),
                         preferred_element_type=F32) + b_ref[...]


def _modulation(c_rows, ada_w, ada_b):
    rows, d = c_rows.shape
    n = ada_w.shape[1]
    bn = 1024
    return pl.pallas_call(
        _mod_kernel,
        grid=(n // bn,),
        in_specs=[pl.BlockSpec((rows, d), lambda j: (0, 0)),
                  pl.BlockSpec((d, bn), lambda j: (0, j)),
                  pl.BlockSpec((1, bn), lambda j: (0, j))],
        out_specs=pl.BlockSpec((rows, bn), lambda j: (0, j)),
        out_shape=jax.ShapeDtypeStruct((rows, n), F32),
        compiler_params=_cparams("arbitrary"),
        name="ada_mod",
    )(c_rows, ada_w, ada_b.reshape(1, n))


def _to_slot_major(v, nbatch):
    tokens = v.shape[0] // nbatch
    by_token = _regroup(v.reshape(nbatch, tokens, v.shape[1]))
    return [jnp.concatenate([by_token[k * S5_T + t] for k in range(tokens // S5_T)], axis=0)
            for t in range(S5_T)]


def _from_slot_major(slots, nbatch):
    nchunk = slots[0].shape[0] // nbatch
    by_token = jnp.stack([slots[t][k * nbatch:(k + 1) * nbatch] for k in range(nchunk) for t in range(S5_T)],
                         axis=0)
    return _regroup(by_token).reshape(nbatch * nchunk * S5_T, -1)


def _proj_kernel(*refs, d, use_pos, d_s5):
    if use_pos:
        x_ref, pos_ref, mod_ref, g_ref, w_ref = refs[:5]
        outs = refs[5:]
        h = x_ref[...] + pos_ref[...][None]
    else:
        x_ref, mod_ref, g_ref, w_ref = refs[:4]
        outs = refs[4:]
        h = x_ref[...]
    nbatch, tokens, _ = h.shape
    m = mod_ref[...]
    hn = _rms(h, g_ref[...][None]) * (1.0 + m[:, :, d:2 * d]) + m[:, :, 0:d]
    proj = jnp.dot(hn.reshape(nbatch * tokens, d).astype(BF16), w_ref[...], preferred_element_type=F32)
    for t, rows in enumerate(_to_slot_major(proj[:, :d_s5], nbatch)):
        outs[0][t] = rows.astype(BF16)
    if len(outs) > 1:
        outs[1][...] = proj[:, d_s5:].reshape(nbatch, tokens, -1).astype(BF16)


def _project(x, pos, mod3, mod_row, g, w_bf16, d_s5, d_rest, tt):
    bsz, n, d = x.shape
    use_pos = pos is not None
    in_specs = [pl.BlockSpec((bsz, tt, d), lambda i: (0, i, 0))]
    args = [x]
    if use_pos:
        in_specs.append(pl.BlockSpec((tt, d), lambda i: (i, 0)))
        args.append(pos)
    mod_rows = bsz if mod_row is None else 1
    in_specs += [pl.BlockSpec((mod_rows, 1, mod3.shape[-1]), lambda i: (0 if mod_row is None else mod_row, 0, 0)),
                 pl.BlockSpec((1, d), lambda i: (0, 0)),
                 pl.BlockSpec((d, d_s5 + d_rest), lambda i: (0, 0))]
    args += [mod3, g.reshape(1, d), w_bf16]
    rows = tt // S5_T * bsz
    out_specs = [pl.BlockSpec((S5_T, rows, d_s5), lambda i: (0, i, 0))]
    out_shape = [jax.ShapeDtypeStruct((S5_T, n // S5_T * bsz, d_s5), BF16)]
    if d_rest:
        out_specs.append(pl.BlockSpec((bsz, tt, d_rest), lambda i: (0, i, 0)))
        out_shape.append(jax.ShapeDtypeStruct((bsz, n, d_rest), BF16))
    return pl.pallas_call(
        functools.partial(_proj_kernel, d=d, use_pos=use_pos, d_s5=d_s5),
        grid=(n // tt,),
        in_specs=in_specs,
        out_specs=out_specs,
        out_shape=out_shape,
        compiler_params=_cparams("arbitrary"),
        name="norm_proj",
    )(*args)


def _s5_tables(a_re, a_im, log_step, b_re, b_im, c_re, c_im, d_skip):
    T, C, P = S5_T, S5_C, S5_P
    G = a_re.shape[1]
    gpb = LANES // C
    slots = np.arange(T)
    tok = np.stack([8 * (slots // 8) + (slots % 8 - o) % 8 for o in range(gpb)])
    tok_g = tok[np.arange(G) % gpb]

    step = jnp.exp(log_step.astype(F32))[..., None]
    ar = a_re.astype(F32) * step
    ai = a_im.astype(F32) * step

    def powers(d, expo):
        e = jnp.asarray(np.broadcast_to(expo, (G, expo.shape[-1])).astype(np.float32))[:, :, None]
        mag = jnp.exp(ar[d][:, None, :] * e)
        return mag * jnp.cos(ai[d][:, None, :] * e), mag * jnp.sin(ai[d][:, None, :] * e)

    lr, li = a_re.astype(F32), a_im.astype(F32)
    first = [powers(d, np.ones(1)) for d in range(2)]
    nr = jnp.stack([first[d][0][:, 0] for d in range(2)]) - 1.0
    ni = jnp.stack([first[d][1][:, 0] for d in range(2)])
    den = lr * lr + li * li
    qr = (nr * lr + ni * li) / den
    qi = (ni * lr - nr * li) / den
    bbr = qr[..., None] * b_re - qi[..., None] * b_im
    bbi = qr[..., None] * b_im + qi[..., None] * b_re
    cr, ci = c_re.astype(F32), c_im.astype(F32)

    kerns = []
    for d in range(2):
        pr, pi = powers(d, np.arange(T))
        wr = cr[d][:, None] * pr[:, :, None, :] - ci[d][:, None] * pi[:, :, None, :]
        wi = cr[d][:, None] * pi[:, :, None, :] + ci[d][:, None] * pr[:, :, None, :]
        kerns.append(jnp.einsum('gtcp,gpe->gtce', wr, bbr[d], precision=HI)
                     - jnp.einsum('gtcp,gpe->gtce', wi, bbi[d], precision=HI))
    kcat = jnp.concatenate(kerns, axis=1)
    kpad = jnp.pad(kcat.reshape(G, 2 * T * C, C), ((0, 0), (0, 0), (0, LANES - C)))
    lag = np.repeat(tok[:, :, None] - tok[:, None, :], C, axis=2).astype(np.int32)
    d_lanes = jnp.tile(d_skip.astype(F32).reshape(G, 1, C), (1, 1, T))
    m1 = pl.pallas_call(
        _s5_intra_kernel,
        grid=(G,),
        in_specs=[pl.BlockSpec((1, 2 * T * C, LANES), lambda g: (g, 0, 0)),
                  pl.BlockSpec((1, T, T * C), lambda g: (g % gpb, 0, 0)),
                  pl.BlockSpec((1, 1, T * C), lambda g: (g, 0, 0))],
        out_specs=pl.BlockSpec((1, T * C, T * C), lambda g: (g, 0, 0)),
        out_shape=jax.ShapeDtypeStruct((G, T * C, T * C), BF16),
        compiler_params=_cparams("arbitrary"),
        name="s5_intra_table",
    )(kpad, jnp.asarray(lag), d_lanes)

    def in_mat(d, expo):
        p_r, p_i = powers(d, expo)
        br, bi = bbr[d].transpose(0, 2, 1)[:, None], bbi[d].transpose(0, 2, 1)[:, None]
        er = p_r[:, :, None, :] * br - p_i[:, :, None, :] * bi
        ei = p_r[:, :, None, :] * bi + p_i[:, :, None, :] * br
        return er.reshape(G, T * C, P), ei.reshape(G, T * C, P)

    efr, efi = in_mat(0, T - 1 - tok_g)
    ebr, ebi = in_mat(1, tok_g)
    m2 = jnp.concatenate([efr, ebr, efi, ebi], axis=-1)

    def out_mat(d, expo):
        p_r, p_i = powers(d, expo)
        crt = cr[d].transpose(0, 2, 1)[:, :, None, :]
        cit = ci[d].transpose(0, 2, 1)[:, :, None, :]
        prt = p_r.transpose(0, 2, 1)[:, :, :, None]
        pit = p_i.transpose(0, 2, 1)[:, :, :, None]
        vr = crt * prt - cit * pit
        vi = crt * pit + cit * prt
        return vr.reshape(G, P, T * C), -vi.reshape(G, P, T * C)

    vfr, vfi = out_mat(0, tok_g + 1)
    vbr, vbi = out_mat(1, T - tok_g)
    m3 = jnp.concatenate([vfr, vbr, vfi, vbi], axis=1)

    last = [powers(d, np.full(1, T)) for d in range(2)]
    lam = jnp.stack([jnp.concatenate([last[0][c][:, 0], last[1][c][:, 0]], axis=-1)
                     for c in range(2)], axis=1)
    is_fwd = jnp.asarray((np.arange(4 * P) // P) % 2 == 0)[None, :, None]
    m3f = jnp.where(is_fwd, m3, 0.0).astype(BF16)
    m3b = jnp.where(is_fwd, 0.0, m3).astype(BF16)
    return m1, m2.astype(BF16), m3f, m3b, lam


def _s5_intra_kernel(k_ref, lag_ref, d_ref, o_ref):
    T, C = S5_T, S5_C
    k = k_ref[0]
    shift = C
    while shift < LANES:
        k = k + pltpu.roll(k, shift, axis=1)
        shift *= 2
    k = jnp.concatenate([k, k], axis=1)
    block = lambda x: k[x * C:(x + 1) * C]
    row = lax.broadcasted_iota(jnp.int32, (C, T * C), 0)
    lane_c = lax.broadcasted_iota(jnp.int32, (C, T * C), 1) % C
    centre = block(0) + block(T) + jnp.where(row == lane_c, d_ref[0], 0.0)
    block_rows = []
    for l in range(T):
        lag = lag_ref[0, l:l + 1, :]
        acc = centre
        for x in range(1, T):
            acc = jnp.where(lag == x, block(x), acc)
            acc = jnp.where(lag == -x, block(T + x), acc)
        block_rows.append(acc)
    o_ref[0] = jnp.concatenate(block_rows, axis=0).T.astype(BF16)


def _gelu_tanh(x):
    return 0.5 * x * (1.0 + jnp.tanh(math.sqrt(2.0 / math.pi) * (x + 0.044715 * (x * x * x))))


def _s5_kernel(u_ref, uc_ref, m1_ref, m2_ref, m3f_ref, m3b_ref, lam_ref, o_ref,
               z_ref, e_ref, sf_ref, sb_ref, *, nbatch, nchunk, nctx):
    groups = LANES // S5_C
    rows = nbatch * nchunk
    lane = lax.broadcasted_iota(jnp.int32, (1, LANES), 1)
    slot_bits = [(((lane // S5_C) >> j) & 1) == 1 for j in range(3)]

    def skew(xs):
        cur = [xs[(-k) % groups] for k in range(groups)]
        for j, bit in enumerate(slot_bits):
            cur = [jnp.where(bit, cur[(i - (1 << j)) % groups], cur[i]) for i in range(groups)]
        return cur

    def to_chunk_rows(read_t, nrows, row0):
        for h in range(2):
            rolled = []
            for t8 in range(groups):
                v = read_t(8 * h + t8).astype(F32)
                rolled.append(pltpu.roll(v, t8 * S5_C, axis=1) if t8 else v)
            for g, zg in enumerate(skew(rolled)):
                z_ref[g, row0:row0 + nrows, h * LANES:(h + 1) * LANES] = zg.astype(BF16)

    for r0 in range(0, rows, RELAYOUT_ROWS):
        to_chunk_rows(lambda t, r0=r0: u_ref[t, r0:r0 + RELAYOUT_ROWS], RELAYOUT_ROWS, r0)
    to_chunk_rows(lambda t: uc_ref[t], nbatch * nctx, rows)

    wide = 2 * LANES
    fwd_lane = (lax.broadcasted_iota(jnp.int32, (1, wide), 1) % LANES) < (LANES // 2)
    tile = lambda k: pl.ds(pl.multiple_of(k * nbatch, nbatch), nbatch)

    for g in range(groups):
        zg = z_ref[g]
        e_ref[...] = jnp.dot(zg, m2_ref[g], preferred_element_type=F32)
        lam_r = lam_ref[g, 0:1, :]
        lam_i = lam_ref[g, 1:2, :]

        def advance(sr, si, kf, kb):
            e2 = jnp.where(fwd_lane, e_ref[tile(kf), :], e_ref[tile(kb), :])
            return (lam_r * sr - lam_i * si + e2[:, :LANES], lam_r * si + lam_i * sr + e2[:, LANES:])

        sr = jnp.zeros((nbatch, LANES), F32)
        si = jnp.zeros((nbatch, LANES), F32)
        for i in range(nctx):
            sr, si = advance(sr, si, nchunk + i, nchunk + nctx - 1 - i)

        def body(i, carry):
            sr, si = carry
            kf, kb = i, nchunk - 1 - i
            s = jnp.concatenate([sr, si], axis=-1)
            sf_ref[tile(kf), :] = s
            sb_ref[tile(kb), :] = s
            return advance(sr, si, kf, kb)

        lax.fori_loop(0, nchunk, body, (sr, si), unroll=2)

        y = (jnp.dot(zg[:rows], m1_ref[g], preferred_element_type=F32)
             + jnp.dot(sf_ref[...].astype(BF16), m3f_ref[g], preferred_element_type=F32)
             + jnp.dot(sb_ref[...].astype(BF16), m3b_ref[g], preferred_element_type=F32))
        z_ref[g, 0:rows, :] = _gelu_tanh(y).astype(BF16)

    for r0 in range(0, rows, RELAYOUT_ROWS):
        for h in range(2):
            ys = [z_ref[g, r0:r0 + RELAYOUT_ROWS, h * LANES:(h + 1) * LANES].astype(F32)
                  for g in range(groups)]
            for t8, acc in enumerate(skew(ys)):
                if t8:
                    acc = pltpu.roll(acc, LANES - t8 * S5_C, axis=1)
                o_ref[8 * h + t8, r0:r0 + RELAYOUT_ROWS] = acc.astype(BF16)


def _s5_scan(u_t, uc_t, bsz, m1, m2, m3f, m3b, lam):
    T, rows, width = u_t.shape
    ctx_rows = uc_t.shape[1]
    gpb = LANES // S5_C
    kern = functools.partial(_s5_kernel, nbatch=bsz, nchunk=rows // bsz, nctx=ctx_rows // bsz)
    one = pl.Buffered(1)
    mat = pl.BlockSpec((gpb, 2 * LANES, 2 * LANES), lambda j: (j, 0, 0))
    return pl.pallas_call(
        kern,
        grid=(width // LANES,),
        in_specs=[pl.BlockSpec((T, rows, LANES), lambda j: (0, 0, j), pipeline_mode=one),
                  pl.BlockSpec((T, ctx_rows, LANES), lambda j: (0, 0, j)),
                  mat, mat, mat, mat,
                  pl.BlockSpec((gpb, 2, LANES), lambda j: (j, 0, 0))],
        out_specs=pl.BlockSpec((T, rows, LANES), lambda j: (0, 0, j), pipeline_mode=one),
        out_shape=jax.ShapeDtypeStruct(u_t.shape, BF16),
        scratch_shapes=[pltpu.VMEM((gpb, rows + ctx_rows, 2 * LANES), BF16),
                        pltpu.VMEM((rows + ctx_rows, 2 * LANES), F32),
                        pltpu.VMEM((rows, 2 * LANES), F32),
                        pltpu.VMEM((rows, 2 * LANES), F32)],
        compiler_params=_cparams("arbitrary"),
        name="s5_scan",
    )(u_t, uc_t, m1, m2, m3f, m3b, lam)


def _dft_tables(n_seq):
    na_sig = n_seq // FFT_NB
    ka = np.arange(FFT_NA)[:, None]
    nb = np.arange(FFT_NB)[:, None, None]

    def stage1(n_in):
        na = np.arange(n_in)[None, :]
        ang = -2.0 * np.pi * (na * ka / FFT_NA)[None] - 2.0 * np.pi * (nb * ka[None]) / FFT_N
        return np.cos(ang), np.sin(ang)

    c, s = stage1(na_sig)
    f1 = np.concatenate([np.concatenate([c, -s], axis=2), np.concatenate([s, c], axis=2)], axis=1)
    c, s = stage1(FFT_NA)
    f1_real = np.concatenate([c, s], axis=1)
    kb = np.arange(FFT_NB)[:, None]
    nbv = np.arange(FFT_NB)[None, :]
    ang = -2.0 * np.pi * kb * nbv / FFT_NB
    c, s = np.cos(ang), np.sin(ang)
    f2 = np.block([[c, -s], [s, c]])
    f2_inv = np.block([[c, s], [-s, c]]) / FFT_NB
    nap = np.arange(na_sig)[:, None]
    kav = np.arange(FFT_NA)[None, :]
    ang = 2.0 * np.pi * (nap * kav / FFT_NA)[None] + 2.0 * np.pi * (nb * kav[None]) / FFT_N
    c, s = np.cos(ang) / FFT_NA, np.sin(ang) / FFT_NA
    f3 = np.concatenate([np.concatenate([c, -s], axis=2), np.concatenate([s, c], axis=2)], axis=1)
    as_bf16 = lambda a: jnp.asarray(a.astype(np.float32)).astype(BF16)
    return as_bf16(f1), as_bf16(f1_real), as_bf16(f2), as_bf16(f2_inv), as_bf16(f3)


def _conv3_slab(z_ref, plane, nb0, w, b, nbk):
    def rows(i):
        return z_ref[plane, 0, i].astype(F32)

    mid = [rows(nb0 + i) for i in range(SLAB)]
    na = mid[0].shape[0]
    ridx = lax.broadcasted_iota(jnp.int32, mid[0].shape, 0)
    wrapped = rows((nb0 + nbk - 1) % nbk)
    before = jnp.where(nb0 == 0, jnp.where(ridx == 0, 0.0, pltpu.roll(wrapped, 1, axis=0)), wrapped)
    wrapped = rows((nb0 + SLAB) % nbk)
    after = jnp.where(nb0 + SLAB == nbk, jnp.where(ridx == na - 1, 0.0, pltpu.roll(wrapped, na - 1, axis=0)),
                      wrapped)
    ext = [before] + mid + [after]
    return [ext[i] * w[0:1] + ext[i + 1] * w[1:2] + ext[i + 2] * w[2:3] + b for i in range(SLAB)]


def _for_row_slabs(nbk, fn):
    lax.fori_loop(0, nbk // SLAB, lambda j, c: (fn(pl.multiple_of(j * SLAB, SLAB)), c)[1], 0)


def _regroup(x):
    return jnp.swapaxes(x, 0, 1)


def _hyena_kernel(zv_ref, zg1_ref, zg2_ref, wv_ref, bv_ref, wg1_ref, bg1_ref, wg2_ref, bg2_ref,
                  kf_ref, f1_ref, f2_ref, f2i_ref, f3_ref, o_ref, s_ref, z1_ref,
                  *, nbk, na, ngrp, kag):
    step = pl.program_id(2)
    hb = FFT_NB

    def stage1(src_ref, conv):
        def slab(nb0):
            if conv:
                planes = [[v.astype(BF16) for v in _conv3_slab(src_ref, pln, nb0, wv_ref[...], bv_ref[...], nbk)]
                          for pln in range(2)]
            tiles = []
            for i in range(SLAB):
                nb = nb0 + i
                parts = [planes[0][i], planes[1][i]] if conv else [src_ref[0, 0, nb], src_ref[1, 0, nb]]
                rhs = jnp.concatenate(parts, axis=0)
                tiles.append(jnp.dot(f1_ref[nb], rhs, preferred_element_type=F32).astype(BF16))
            s_ref[:, pl.ds(nb0, SLAB), :] = _regroup(jnp.stack(tiles, axis=0))
        _for_row_slabs(nbk, slab)

    def stage3(gate_ref, wg, bg, dst_ref):
        def slab(nb0):
            rows = _regroup(s_ref[:, pl.ds(nb0, SLAB), :])
            gates = [_conv3_slab(gate_ref, pln, nb0, wg, bg, nbk) for pln in range(2)]
            for i in range(SLAB):
                nb = nb0 + i
                y = jnp.dot(f3_ref[nb], rows[i], preferred_element_type=F32)
                for pln in range(2):
                    dst_ref[pln, 0, nb] = (gates[pln][i] * y[pln * na:(pln + 1) * na]).astype(dst_ref.dtype)
        _for_row_slabs(nbk, slab)

    @pl.when(step == 0)
    def _():
        stage1(zv_ref, True)

    @pl.when(step == ngrp)
    def _():
        stage1(z1_ref, False)

    base = (step % ngrp) * kag
    loaded = [jnp.concatenate([s_ref[base + k], s_ref[FFT_NA + base + k]], axis=0) for k in range(kag)]
    results = []
    for k, rhs in enumerate(loaded):
        spec = jnp.dot(f2_ref[...], rhs, preferred_element_type=F32).astype(BF16)
        kf = kf_ref[0, k]
        sr, si = spec[:hb], spec[hb:]
        kr, ki = kf[:hb], kf[hb:]
        prod = jnp.concatenate([sr * kr - si * ki, sr * ki + si * kr], axis=0)
        results.append(jnp.dot(f2i_ref[...], prod, preferred_element_type=F32).astype(BF16))
    for k, back in enumerate(results):
        s_ref[base + k] = back[:hb]
        s_ref[FFT_NA + base + k] = back[hb:]

    @pl.when(step == ngrp - 1)
    def _():
        stage3(zg1_ref, wg1_ref[...], bg1_ref[...], z1_ref)

    @pl.when(step == 2 * ngrp - 1)
    def _():
        stage3(zg2_ref, wg2_ref[...], bg2_ref[...], o_ref)


def _hyena(z5, conv_w, conv_b, kf, tables, cb=MXU_N, kag=2 * SLAB):
    f1, _, f2, f2i, f3 = tables
    _, npair, nbk, na, _ = z5.shape
    c = kf.shape[-1]
    ncb = c // cb
    ngrp = FFT_NA // kag
    one = pl.Buffered(1)
    zspec = lambda grp: pl.BlockSpec((2, 1, nbk, na, cb), lambda p, j, s: (0, p, 0, 0, grp * ncb + j),
                                     pipeline_mode=one)
    wspec = lambda grp: pl.BlockSpec((3, cb), lambda p, j, s: (0, grp * ncb + j))
    bspec = lambda grp: pl.BlockSpec((1, cb), lambda p, j, s: (0, grp * ncb + j))
    const = lambda a: pl.BlockSpec(a.shape, lambda p, j, s: tuple(0 for _ in a.shape), pipeline_mode=one)
    cw = conv_w.astype(F32)
    cbias = conv_b.astype(F32).reshape(1, -1)
    return pl.pallas_call(
        functools.partial(_hyena_kernel, nbk=nbk, na=na, ngrp=ngrp, kag=kag),
        grid=(npair, ncb, 2 * ngrp),
        in_specs=[zspec(0), zspec(1), zspec(2), wspec(0), bspec(0), wspec(1), bspec(1), wspec(2), bspec(2),
                  pl.BlockSpec((1, kag, 2 * FFT_NB, cb), lambda p, j, s: (s // ngrp, s % ngrp, 0, j)),
                  const(f1), const(f2), const(f2i), const(f3)],
        out_specs=pl.BlockSpec((2, 1, nbk, na, cb), lambda p, j, s: (0, p, 0, 0, j), pipeline_mode=one),
        out_shape=jax.ShapeDtypeStruct((2, npair, nbk, na, c), BF16),
        scratch_shapes=[pltpu.VMEM((2 * FFT_NA, nbk, cb), BF16),
                        pltpu.VMEM((2, 1, nbk, na, cb), BF16)],
        compiler_params=_cparams("arbitrary", "arbitrary", "arbitrary"),
        name="hyena_conv",
    )(z5, z5, z5, cw, cbias, cw, cbias, cw, cbias, kf, f1, f2, f2i, f3)


def _filter_kernel(embt_ref, tv_ref, w1t_ref, b1_ref, w2t_ref, b2_ref, fr_ref, w3_ref, dec_ref, bias_ref,
                   f1_ref, f2_ref, k_ref, s_ref, h_ref, *, nbk, kag):
    first = (pl.program_id(0) == 0) & (pl.program_id(1) == 0)
    step = pl.program_id(2)
    half = FFT_NA // 2
    lanes_per_pass = 8 * FFT_NA

    @pl.when(first & (step == 0))
    def _():
        fr = fr_ref[...]
        for i in range(embt_ref.shape[1] // lanes_per_pass):
            cols = slice(i * lanes_per_pass, (i + 1) * lanes_per_pass)
            h = jnp.sin(fr * (jnp.dot(w1t_ref[...], embt_ref[:, cols], precision=HI,
                                      preferred_element_type=F32) + b1_ref[...]))
            h_ref[:, cols] = jnp.sin(fr * (jnp.dot(w2t_ref[...], h, precision=HI,
                                                   preferred_element_type=F32) + b2_ref[...]))

    @pl.when(step == 0)
    def _():
        dec = dec_ref[0]

        def slab(nb0):
            tiles = []
            for i in range(SLAB):
                nb = nb0 + i
                h = h_ref[:, pl.ds(pl.multiple_of(nb * FFT_NA, FFT_NA), FFT_NA)].T.astype(BF16)
                fwd = jnp.dot(h[:half], w3_ref[0, 0], preferred_element_type=F32)
                bwd = jnp.dot(h[half:], w3_ref[0, 1], preferred_element_type=F32)
                tv = tv_ref[nb]
                window = jnp.exp(-tv[:, 0:1] * dec) * tv[:, 1:2]
                filt = jnp.concatenate([fwd, bwd], axis=0) * window
                tiles.append(jnp.dot(f1_ref[nb], filt.astype(BF16), preferred_element_type=F32).astype(BF16))
            s_ref[:, pl.ds(nb0, SLAB), :] = _regroup(jnp.stack(tiles, axis=0))
        _for_row_slabs(nbk, slab)

    base = step * kag
    bias = bias_ref[0]
    for k in range(kag):
        rhs = jnp.concatenate([s_ref[base + k], s_ref[FFT_NA + base + k]], axis=0)
        spec = jnp.dot(f2_ref[...], rhs, preferred_element_type=F32)
        k_ref[0, k] = jnp.concatenate([spec[:FFT_NB] + bias, spec[FFT_NB:]], axis=0).astype(BF16)


def _filter_spectra(n, w1, b1, w2, b2, freq, w3, decay, bias, tables, cb=MXU_N, kag=16):
    _, f1_real, f2, _, _ = tables
    hid = w2.shape[0]
    c = decay.shape[-1]
    t = np.linspace(0.0, 1.0, n, dtype=np.float32)[:, None]
    w = (2.0 * np.pi * np.arange(n, dtype=np.float32) / n).astype(np.float32)
    bands = np.linspace(1e-4, HY_BANDS - 1, HY_BANDS, dtype=np.float32)
    ang = w[:, None] * bands[None, :]
    emb = np.concatenate([t, np.cos(ang), -np.sin(ang)], axis=-1).astype(np.float32)
    kpad = 64
    idx = FFT_NB * np.arange(FFT_NA)[None, :] + np.arange(FFT_NB)[:, None]
    lagi = np.minimum(np.where(idx < n, idx, FFT_N - idx), n - 1)
    embt = np.zeros((kpad, FFT_N), np.float32)
    embt[:emb.shape[1]] = emb[lagi.reshape(-1)].T
    tv = np.stack([t[lagi, 0], (idx != n).astype(np.float32)], axis=-1)
    col = lambda a: a.reshape(hid, 1).astype(F32)
    w1t = jnp.pad(w1.astype(F32), ((0, kpad - w1.shape[0]), (0, 0))).T
    w3r = w3.astype(BF16).reshape(hid, HY_ORDER, 2, c).transpose(1, 2, 0, 3)
    dec = jnp.abs(decay.astype(F32)).reshape(HY_ORDER, 1, c)
    one = pl.Buffered(1)
    const = lambda shape: pl.BlockSpec(shape, lambda o, j, s: tuple(0 for _ in shape), pipeline_mode=one)
    per_channel = pl.BlockSpec((1, 1, cb), lambda o, j, s: (o, 0, j))
    return pl.pallas_call(
        functools.partial(_filter_kernel, nbk=FFT_NB, kag=kag),
        grid=(HY_ORDER, c // cb, FFT_NA // kag),
        in_specs=[const(embt.shape), const(tv.shape), const((hid, kpad)), const((hid, 1)),
                  const((hid, hid)), const((hid, 1)), const((hid, 1)),
                  pl.BlockSpec((1, 2, hid, cb), lambda o, j, s: (o, 0, 0, j)),
                  per_channel, per_channel,
                  const(f1_real.shape), const(f2.shape)],
        out_specs=pl.BlockSpec((1, kag, 2 * FFT_NB, cb), lambda o, j, s: (o, s, 0, j)),
        out_shape=jax.ShapeDtypeStruct((HY_ORDER, FFT_NA, 2 * FFT_NB, c), BF16),
        scratch_shapes=[pltpu.VMEM((2 * FFT_NA, FFT_NB, cb), BF16),
                        pltpu.VMEM((hid, FFT_N), F32)],
        compiler_params=_cparams("arbitrary", "arbitrary", "arbitrary"),
        name="hyena_filter",
    )(jnp.asarray(embt), jnp.asarray(tv), w1t, col(b1), w2.astype(F32).T, col(b2), col(freq), w3r, dec,
      bias.astype(F32).reshape(HY_ORDER, 1, c), f1_real, f2)


def _out_kernel(x_ref, pos_ref, mod_ref, ys_ref, yh_ref, gw_ref, gb_ref, g5_ref, gh_ref, wo_ref,
                g2_ref, w1_ref, w2_ref, gf_ref, o_ref, *, d):
    nbatch, tokens, _ = x_ref.shape
    rows = nbatch * tokens
    m = mod_ref[...]
    gate1, shift2 = m[:, :, 2 * d:3 * d], m[:, :, 3 * d:4 * d]
    scale2, gate2 = m[:, :, 4 * d:5 * d], m[:, :, 5 * d:6 * d]
    per_batch = lambda v: v.reshape(nbatch, tokens, d)
    h = x_ref[...] + pos_ref[...][None]
    ys = _from_slot_major([ys_ref[t].astype(F32) for t in range(S5_T)], nbatch)
    ab = jnp.dot(ys.astype(BF16), gw_ref[...], preferred_element_type=F32) + gb_ref[...]
    half = ab.shape[-1] // 2
    y5 = ab[:, :half] * jax.nn.sigmoid(ab[:, half:])
    yh = yh_ref[...].reshape(rows, -1).astype(F32)
    mix = jnp.concatenate([_rms(y5, g5_ref[...]), _rms(yh, gh_ref[...])], axis=-1)
    h = h + gate1 * per_batch(jnp.dot(mix.astype(BF16), wo_ref[...], preferred_element_type=F32))
    hn = _rms(h, g2_ref[...][None]) * (1.0 + scale2) + shift2
    hid = jnp.dot(hn.reshape(rows, d).astype(BF16), w1_ref[...], preferred_element_type=F32)
    hid = jnp.square(jnp.maximum(hid, 0.0))
    h = h + gate2 * per_batch(jnp.dot(hid.astype(BF16), w2_ref[...], preferred_element_type=F32))
    o_ref[...] = _rms(h, gf_ref[...][None])


def _output_stage(x, pos, mod3, ys_t, yh, glu_w, glu_b, g5, gh, w_out, g2, w1, w2, gf, tt=64):
    bsz, n, d = x.shape
    dh = ys_t.shape[-1]
    one = pl.Buffered(1)
    const = lambda a: pl.BlockSpec(a.shape, lambda i: tuple(0 for _ in a.shape), pipeline_mode=one)
    row = lambda a: a.reshape(1, -1).astype(F32)
    glu_b, g5, gh, g2, gf = row(glu_b), row(g5), row(gh), row(g2), row(gf)
    return pl.pallas_call(
        functools.partial(_out_kernel, d=d),
        grid=(n // tt,),
        in_specs=[pl.BlockSpec((bsz, tt, d), lambda i: (0, i, 0)),
                  pl.BlockSpec((tt, d), lambda i: (i, 0)),
                  pl.BlockSpec((bsz, 1, mod3.shape[-1]), lambda i: (0, 0, 0)),
                  pl.BlockSpec((S5_T, tt // S5_T * bsz, dh), lambda i: (0, i, 0)),
                  pl.BlockSpec((bsz, tt, yh.shape[-1]), lambda i: (0, i, 0)),
                  const(glu_w), const(glu_b), const(g5), const(gh), const(w_out), const(g2),
                  const(w1), const(w2), const(gf)],
        out_specs=pl.BlockSpec((bsz, tt, d), lambda i: (0, i, 0)),
        out_shape=jax.ShapeDtypeStruct((bsz, n, d), F32),
        compiler_params=_cparams("arbitrary"),
        name="mix_mlp_out",
    )(x, pos, mod3, ys_t, yh, glu_w, glu_b, g5, gh, w_out, g2, w1, w2, gf)


def _pos_table(n, d):
    rows = n // GRID_W
    row = np.repeat(np.arange(rows, dtype=np.float32), GRID_W)
    col = np.tile(np.arange(GRID_W, dtype=np.float32), rows)
    quarter = d // 4
    omega = (1.0 / (POS_BASE ** (np.arange(quarter, dtype=np.float32) / quarter))).astype(np.float32)

    def enc(p):
        ang = p[:, None] * omega[None, :]
        return np.concatenate([np.sin(ang), np.cos(ang)], axis=-1)

    return jnp.asarray(np.concatenate([enc(row), enc(col)], axis=-1).astype(np.float32))


def kernel(x, c, ctx, c_ctx, ada_w, ada_b, norm1_g, w_in, s5_a_re, s5_a_im, s5_log_step, s5_b_re,
           s5_b_im, s5_c_re, s5_c_im, s5_d, s5_glu_w, s5_glu_b, hy_conv_w, hy_conv_b, hy_f_w1,
           hy_f_b1, hy_f_w2, hy_f_b2, hy_f_freq, hy_f_w3, hy_decay, hy_bias, mix_g_s5, mix_g_hy,
           w_out, norm2_g, mlp_w1, mlp_w2, final_g):
    bsz, n, d = x.shape
    depth = ada_w.shape[0]
    d_s5 = s5_d.shape[-1]
    d_hy = hy_bias.shape[-1]
    nctx = ctx.shape[1]
    npair = bsz // 2
    nbk, na = FFT_NB, n // FFT_NB
    pos = _pos_table(n, d)
    tables = _dft_tables(n)

    assert depth == 1 and bsz % 2 == 0 and n % (FFT_NB * 8) == 0 and 2 * n == FFT_N
    mod_rows = 16
    c_rows = jnp.concatenate([c, c_ctx[None], jnp.zeros((mod_rows - bsz - 1, d), c.dtype)], axis=0)
    mod3 = _modulation(c_rows, ada_w[0], ada_b[0]).reshape(mod_rows, 1, N_MOD * d)

    w_in_b = w_in[0].astype(BF16)
    u_t, z = _project(x, pos, mod3, None, norm1_g[0], w_in_b, d_s5, 3 * d_hy, tt=128)
    (uc_t,) = _project(ctx, None, mod3, bsz, norm1_g[0], w_in_b, d_s5, 0, tt=128)

    s5_mats = _s5_tables(s5_a_re[0], s5_a_im[0], s5_log_step[0], s5_b_re[0], s5_b_im[0],
                         s5_c_re[0], s5_c_im[0], s5_d[0])
    ys_t = _s5_scan(u_t, uc_t, bsz, *s5_mats)

    kf = _filter_spectra(n, hy_f_w1[0], hy_f_b1[0], hy_f_w2[0], hy_f_b2[0], hy_f_freq[0],
                         hy_f_w3[0], hy_decay[0], hy_bias[0], tables)
    z5 = z.reshape(bsz, na, nbk, 3 * d_hy).transpose(0, 2, 1, 3).reshape(2, npair, nbk, na, 3 * d_hy)
    yh = _hyena(z5, hy_conv_w[0], hy_conv_b[0], kf, tables)
    yh = yh.reshape(bsz, nbk, na, d_hy).transpose(0, 2, 1, 3).reshape(bsz, n, d_hy)

    return _output_stage(x, pos, mod3, ys_t, yh, s5_glu_w[0].astype(BF16), s5_glu_b[0], mix_g_s5[0],
                         mix_g_hy[0], w_out[0].astype(BF16), norm2_g[0], mlp_w1[0].astype(BF16),
                         mlp_w2[0].astype(BF16), final_g)
```

```python
import functools
import math

import numpy as np
import jax
import jax.numpy as jnp
from jax import lax
from jax.experimental import pallas as pl
from jax.experimental.pallas import tpu as pltpu

F32 = jnp.float32
BF16 = jnp.bfloat16
HI = lax.Precision.HIGHEST

EPS = 1e-6
GRID_W = 64
POS_BASE = 10000.0
N_MOD = 6
S5_C = 16
S5_P = 64
S5_T = 16
HY_BANDS = 16
HY_ORDER = 2

FFT_N = 8192
FFT_NA = 128
FFT_NB = 64

LANES = 128
MXU_N = 256
SLAB = 16
RELAYOUT_ROWS = 256
VMEM_LIMIT = 56 * 1024 * 1024


def _cparams(*sem):
    return pltpu.CompilerParams(dimension_semantics=sem, vmem_limit_bytes=VMEM_LIMIT)


def _rms(x, g):
    return x * lax.rsqrt(jnp.mean(x * x, axis=-1, keepdims=True) + EPS) * g


def _mod_kernel(c_ref, w_ref, b_ref, o_ref):
    c = c_ref[...]
    a = c * jax.nn.sigmoid(c)
    o_ref[...] = jnp.dot(a.astype(BF16), w_ref[...].astype(BF16),
                         preferred_element_type=F32) + b_ref[...]


def _modulation(c_rows, ada_w, ada_b):
    rows, d = c_rows.shape
    n = ada_w.shape[1]
    bn = 1024
    return pl.pallas_call(
        _mod_kernel,
        grid=(n // bn,),
        in_specs=[pl.BlockSpec((rows, d), lambda j: (0, 0)),
                  pl.BlockSpec((d, bn), lambda j: (0, j)),
                  pl.BlockSpec((1, bn), lambda j: (0, j))],
        out_specs=pl.BlockSpec((rows, bn), lambda j: (0, j)),
        out_shape=jax.ShapeDtypeStruct((rows, n), F32),
        compiler_params=_cparams("arbitrary"),
        name="ada_mod",
    )(c_rows, ada_w, ada_b.reshape(1, n))


def _to_slot_major(v, nbatch):
    tokens = v.shape[0] // nbatch
    by_token = _regroup(v.reshape(nbatch, tokens, v.shape[1]))
    return [jnp.concatenate([by_token[k * S5_T + t] for k in range(tokens // S5_T)], axis=0)
            for t in range(S5_T)]


def _from_slot_major(slots, nbatch):
    nchunk = slots[0].shape[0] // nbatch
    by_token = jnp.stack([slots[t][k * nbatch:(k + 1) * nbatch] for k in range(nchunk) for t in range(S5_T)],
                         axis=0)
    return _regroup(by_token).reshape(nbatch * nchunk * S5_T, -1)


def _proj_kernel(*refs, d, use_pos, d_s5):
    if use_pos:
        x_ref, pos_ref, mod_ref, g_ref, w_ref = refs[:5]
        outs = refs[5:]
        h = x_ref[...] + pos_ref[...][None]
    else:
        x_ref, mod_ref, g_ref, w_ref = refs[:4]
        outs = refs[4:]
        h = x_ref[...]
    nbatch, tokens, _ = h.shape
    m = mod_ref[...]
    hn = _rms(h, g_ref[...][None]) * (1.0 + m[:, :, d:2 * d]) + m[:, :, 0:d]
    proj = jnp.dot(hn.reshape(nbatch * tokens, d).astype(BF16), w_ref[...], preferred_element_type=F32)
    for t, rows in enumerate(_to_slot_major(proj[:, :d_s5], nbatch)):
        outs[0][t] = rows.astype(BF16)
    if len(outs) > 1:
        outs[1][...] = proj[:, d_s5:].reshape(nbatch, tokens, -1).astype(BF16)


def _project(x, pos, mod3, mod_row, g, w_bf16, d_s5, d_rest, tt):
    bsz, n, d = x.shape
    use_pos = pos is not None
    in_specs = [pl.BlockSpec((bsz, tt, d), lambda i: (0, i, 0))]
    args = [x]
    if use_pos:
        in_specs.append(pl.BlockSpec((tt, d), lambda i: (i, 0)))
        args.append(pos)
    mod_rows = bsz if mod_row is None else 1
    in_specs += [pl.BlockSpec((mod_rows, 1, mod3.shape[-1]), lambda i: (0 if mod_row is None else mod_row, 0, 0)),
                 pl.BlockSpec((1, d), lambda i: (0, 0)),
                 pl.BlockSpec((d, d_s5 + d_rest), lambda i: (0, 0))]
    args += [mod3, g.reshape(1, d), w_bf16]
    rows = tt // S5_T * bsz
    out_specs = [pl.BlockSpec((S5_T, rows, d_s5), lambda i: (0, i, 0))]
    out_shape = [jax.ShapeDtypeStruct((S5_T, n // S5_T * bsz, d_s5), BF16)]
    if d_rest:
        out_specs.append(pl.BlockSpec((bsz, tt, d_rest), lambda i: (0, i, 0)))
        out_shape.append(jax.ShapeDtypeStruct((bsz, n, d_rest), BF16))
    return pl.pallas_call(
        functools.partial(_proj_kernel, d=d, use_pos=use_pos, d_s5=d_s5),
        grid=(n // tt,),
        in_specs=in_specs,
        out_specs=out_specs,
        out_shape=out_shape,
        compiler_params=_cparams("arbitrary"),
        name="norm_proj",
    )(*args)


def _s5_tables(a_re, a_im, log_step, b_re, b_im, c_re, c_im, d_skip):
    T, C, P = S5_T, S5_C, S5_P
    G = a_re.shape[1]
    gpb = LANES // C
    slots = np.arange(T)
    tok = np.stack([8 * (slots // 8) + (slots % 8 - o) % 8 for o in range(gpb)])
    tok_g = tok[np.arange(G) % gpb]

    step = jnp.exp(log_step.astype(F32))[..., None]
    ar = a_re.astype(F32) * step
    ai = a_im.astype(F32) * step

    def powers(d, expo):
        e = jnp.asarray(np.broadcast_to(expo, (G, expo.shape[-1])).astype(np.float32))[:, :, None]
        mag = jnp.exp(ar[d][:, None, :] * e)
        return mag * jnp.cos(ai[d][:, None, :] * e), mag * jnp.sin(ai[d][:, None, :] * e)

    lr, li = a_re.astype(F32), a_im.astype(F32)
    first = [powers(d, np.ones(1)) for d in range(2)]
    nr = jnp.stack([first[d][0][:, 0] for d in range(2)]) - 1.0
    ni = jnp.stack([first[d][1][:, 0] for d in range(2)])
    den = lr * lr + li * li
    qr = (nr * lr + ni * li) / den
    qi = (ni * lr - nr * li) / den
    bbr = qr[..., None] * b_re - qi[..., None] * b_im
    bbi = qr[..., None] * b_im + qi[..., None] * b_re
    cr, ci = c_re.astype(F32), c_im.astype(F32)

    kerns = []
    for d in range(2):
        pr, pi = powers(d, np.arange(T))
        wr = cr[d][:, None] * pr[:, :, None, :] - ci[d][:, None] * pi[:, :, None, :]
        wi = cr[d][:, None] * pi[:, :, None, :] + ci[d][:, None] * pr[:, :, None, :]
        kerns.append(jnp.einsum('gtcp,gpe->gtce', wr, bbr[d], precision=HI)
                     - jnp.einsum('gtcp,gpe->gtce', wi, bbi[d], precision=HI))
    kcat = jnp.concatenate(kerns, axis=1)
    kpad = jnp.pad(kcat.reshape(G, 2 * T * C, C), ((0, 0), (0, 0), (0, LANES - C)))
    lag = np.repeat(tok[:, :, None] - tok[:, None, :], C, axis=2).astype(np.int32)
    d_lanes = jnp.tile(d_skip.astype(F32).reshape(G, 1, C), (1, 1, T))
    m1 = pl.pallas_call(
        _s5_intra_kernel,
        grid=(G,),
        in_specs=[pl.BlockSpec((1, 2 * T * C, LANES), lambda g: (g, 0, 0)),
                  pl.BlockSpec((1, T, T * C), lambda g: (g % gpb, 0, 0)),
                  pl.BlockSpec((1, 1, T * C), lambda g: (g, 0, 0))],
        out_specs=pl.BlockSpec((1, T * C, T * C), lambda g: (g, 0, 0)),
        out_shape=jax.ShapeDtypeStruct((G, T * C, T * C), BF16),
        compiler_params=_cparams("arbitrary"),
        name="s5_intra_table",
    )(kpad, jnp.asarray(lag), d_lanes)

    def in_mat(d, expo):
        p_r, p_i = powers(d, expo)
        br, bi = bbr[d].transpose(0, 2, 1)[:, None], bbi[d].transpose(0, 2, 1)[:, None]
        er = p_r[:, :, None, :] * br - p_i[:, :, None, :] * bi
        ei = p_r[:, :, None, :] * bi + p_i[:, :, None, :] * br
        return er.reshape(G, T * C, P), ei.reshape(G, T * C, P)

    efr, efi = in_mat(0, T - 1 - tok_g)
    ebr, ebi = in_mat(1, tok_g)
    m2 = jnp.concatenate([efr, ebr, efi, ebi], axis=-1)

    def out_mat(d, expo):
        p_r, p_i = powers(d, expo)
        crt = cr[d].transpose(0, 2, 1)[:, :, None, :]
        cit = ci[d].transpose(0, 2, 1)[:, :, None, :]
        prt = p_r.transpose(0, 2, 1)[:, :, :, None]
        pit = p_i.transpose(0, 2, 1)[:, :, :, None]
        vr = crt * prt - cit * pit
        vi = crt * pit + cit * prt
        return vr.reshape(G, P, T * C), -vi.reshape(G, P, T * C)

    vfr, vfi = out_mat(0, tok_g + 1)
    vbr, vbi = out_mat(1, T - tok_g)
    m3 = jnp.concatenate([vfr, vbr, vfi, vbi], axis=1)

    last = [powers(d, np.full(1, T)) for d in range(2)]
    lam = jnp.stack([jnp.concatenate([last[0][c][:, 0], last[1][c][:, 0]], axis=-1)
                     for c in range(2)], axis=1)
    is_fwd = jnp.asarray((np.arange(4 * P) // P) % 2 == 0)[None, :, None]
    m3f = jnp.where(is_fwd, m3, 0.0).astype(BF16)
    m3b = jnp.where(is_fwd, 0.0, m3).astype(BF16)
    return m1, m2.astype(BF16), m3f, m3b, lam


def _s5_intra_kernel(k_ref, lag_ref, d_ref, o_ref):
    T, C = S5_T, S5_C
    k = k_ref[0]
    shift = C
    while shift < LANES:
        k = k + pltpu.roll(k, shift, axis=1)
        shift *= 2
    k = jnp.concatenate([k, k], axis=1)
    block = lambda x: k[x * C:(x + 1) * C]
    row = lax.broadcasted_iota(jnp.int32, (C, T * C), 0)
    lane_c = lax.broadcasted_iota(jnp.int32, (C, T * C), 1) % C
    centre = block(0) + block(T) + jnp.where(row == lane_c, d_ref[0], 0.0)
    block_rows = []
    for l in range(T):
        lag = lag_ref[0, l:l + 1, :]
        acc = centre
        for x in range(1, T):
            acc = jnp.where(lag == x, block(x), acc)
            acc = jnp.where(lag == -x, block(T + x), acc)
        block_rows.append(acc)
    o_ref[0] = jnp.concatenate(block_rows, axis=0).T.astype(BF16)


def _gelu_tanh(x):
    return 0.5 * x * (1.0 + jnp.tanh(math.sqrt(2.0 / math.pi) * (x + 0.044715 * (x * x * x))))


def _s5_kernel(u_ref, uc_ref, m1_ref, m2_ref, m3f_ref, m3b_ref, lam_ref, o_ref,
               z_ref, e_ref, sf_ref, sb_ref, *, nbatch, nchunk, nctx):
    groups = LANES // S5_C
    rows = nbatch * nchunk
    lane = lax.broadcasted_iota(jnp.int32, (1, LANES), 1)
    slot_bits = [(((lane // S5_C) >> j) & 1) == 1 for j in range(3)]

    def skew(xs):
        cur = [xs[(-k) % groups] for k in range(groups)]
        for j, bit in enumerate(slot_bits):
            cur = [jnp.where(bit, cur[(i - (1 << j)) % groups], cur[i]) for i in range(groups)]
        return cur

    def to_chunk_rows(read_t, nrows, row0):
        for h in range(2):
            rolled = []
            for t8 in range(groups):
                v = read_t(8 * h + t8).astype(F32)
                rolled.append(pltpu.roll(v, t8 * S5_C, axis=1) if t8 else v)
            for g, zg in enumerate(skew(rolled)):
                z_ref[g, row0:row0 + nrows, h * LANES:(h + 1) * LANES] = zg.astype(BF16)

    for r0 in range(0, rows, RELAYOUT_ROWS):
        to_chunk_rows(lambda t, r0=r0: u_ref[t, r0:r0 + RELAYOUT_ROWS], RELAYOUT_ROWS, r0)
    to_chunk_rows(lambda t: uc_ref[t], nbatch * nctx, rows)

    wide = 2 * LANES
    fwd_lane = (lax.broadcasted_iota(jnp.int32, (1, wide), 1) % LANES) < (LANES // 2)
    tile = lambda k: pl.ds(pl.multiple_of(k * nbatch, nbatch), nbatch)

    for g in range(groups):
        zg = z_ref[g]
        e_ref[...] = jnp.dot(zg, m2_ref[g], preferred_element_type=F32)
        lam_r = lam_ref[g, 0:1, :]
        lam_i = lam_ref[g, 1:2, :]

        def advance(sr, si, kf, kb):
            e2 = jnp.where(fwd_lane, e_ref[tile(kf), :], e_ref[tile(kb), :])
            return (lam_r * sr - lam_i * si + e2[:, :LANES], lam_r * si + lam_i * sr + e2[:, LANES:])

        sr = jnp.zeros((nbatch, LANES), F32)
        si = jnp.zeros((nbatch, LANES), F32)
        for i in range(nctx):
            sr, si = advance(sr, si, nchunk + i, nchunk + nctx - 1 - i)

        def body(i, carry):
            sr, si = carry
            kf, kb = i, nchunk - 1 - i
            s = jnp.concatenate([sr, si], axis=-1)
            sf_ref[tile(kf), :] = s
            sb_ref[tile(kb), :] = s
            return advance(sr, si, kf, kb)

        lax.fori_loop(0, nchunk, body, (sr, si), unroll=2)

        y = (jnp.dot(zg[:rows], m1_ref[g], preferred_element_type=F32)
             + jnp.dot(sf_ref[...].astype(BF16), m3f_ref[g], preferred_element_type=F32)
             + jnp.dot(sb_ref[...].astype(BF16), m3b_ref[g], preferred_element_type=F32))
        z_ref[g, 0:rows, :] = _gelu_tanh(y).astype(BF16)

    for r0 in range(0, rows, RELAYOUT_ROWS):
        for h in range(2):
            ys = [z_ref[g, r0:r0 + RELAYOUT_ROWS, h * LANES:(h + 1) * LANES].astype(F32)
                  for g in range(groups)]
            for t8, acc in enumerate(skew(ys)):
                if t8:
                    acc = pltpu.roll(acc, LANES - t8 * S5_C, axis=1)
                o_ref[8 * h + t8, r0:r0 + RELAYOUT_ROWS] = acc.astype(BF16)


def _s5_scan(u_t, uc_t, bsz, m1, m2, m3f, m3b, lam):
    T, rows, width = u_t.shape
    ctx_rows = uc_t.shape[1]
    gpb = LANES // S5_C
    kern = functools.partial(_s5_kernel, nbatch=bsz, nchunk=rows // bsz, nctx=ctx_rows // bsz)
    one = pl.Buffered(1)
    mat = pl.BlockSpec((gpb, 2 * LANES, 2 * LANES), lambda j: (j, 0, 0))
    return pl.pallas_call(
        kern,
        grid=(width // LANES,),
        in_specs=[pl.BlockSpec((T, rows, LANES), lambda j: (0, 0, j), pipeline_mode=one),
                  pl.BlockSpec((T, ctx_rows, LANES), lambda j: (0, 0, j)),
                  mat, mat, mat, mat,
                  pl.BlockSpec((gpb, 2, LANES), lambda j: (j, 0, 0))],
        out_specs=pl.BlockSpec((T, rows, LANES), lambda j: (0, 0, j), pipeline_mode=one),
        out_shape=jax.ShapeDtypeStruct(u_t.shape, BF16),
        scratch_shapes=[pltpu.VMEM((gpb, rows + ctx_rows, 2 * LANES), BF16),
                        pltpu.VMEM((rows + ctx_rows, 2 * LANES), F32),
                        pltpu.VMEM((rows, 2 * LANES), F32),
                        pltpu.VMEM((rows, 2 * LANES), F32)],
        compiler_params=_cparams("arbitrary"),
        name="s5_scan",
    )(u_t, uc_t, m1, m2, m3f, m3b, lam)


def _dft_tables(n_seq):
    na_sig = n_seq // FFT_NB
    ka = np.arange(FFT_NA)[:, None]
    nb = np.arange(FFT_NB)[:, None, None]

    def stage1(n_in):
        na = np.arange(n_in)[None, :]
        ang = -2.0 * np.pi * (na * ka / FFT_NA)[None] - 2.0 * np.pi * (nb * ka[None]) / FFT_N
        return np.cos(ang), np.sin(ang)

    c, s = stage1(na_sig)
    f1 = np.concatenate([np.concatenate([c, -s], axis=2), np.concatenate([s, c], axis=2)], axis=1)
    c, s = stage1(FFT_NA)
    f1_real = np.concatenate([c, s], axis=1)
    kb = np.arange(FFT_NB)[:, None]
    nbv = np.arange(FFT_NB)[None, :]
    ang = -2.0 * np.pi * kb * nbv / FFT_NB
    c, s = np.cos(ang), np.sin(ang)
    f2 = np.block([[c, -s], [s, c]])
    f2_inv = np.block([[c, s], [-s, c]]) / FFT_NB
    nap = np.arange(na_sig)[:, None]
    kav = np.arange(FFT_NA)[None, :]
    ang = 2.0 * np.pi * (nap * kav / FFT_NA)[None] + 2.0 * np.pi * (nb * kav[None]) / FFT_N
    c, s = np.cos(ang) / FFT_NA, np.sin(ang) / FFT_NA
    f3 = np.concatenate([np.concatenate([c, -s], axis=2), np.concatenate([s, c], axis=2)], axis=1)
    as_bf16 = lambda a: jnp.asarray(a.astype(np.float32)).astype(BF16)
    return as_bf16(f1), as_bf16(f1_real), as_bf16(f2), as_bf16(f2_inv), as_bf16(f3)


def _conv3_slab(z_ref, plane, nb0, w, b, nbk):
    def rows(i):
        return z_ref[plane, 0, i].astype(F32)

    mid = [rows(nb0 + i) for i in range(SLAB)]
    na = mid[0].shape[0]
    ridx = lax.broadcasted_iota(jnp.int32, mid[0].shape, 0)
    wrapped = rows((nb0 + nbk - 1) % nbk)
    before = jnp.where(nb0 == 0, jnp.where(ridx == 0, 0.0, pltpu.roll(wrapped, 1, axis=0)), wrapped)
    wrapped = rows((nb0 + SLAB) % nbk)
    after = jnp.where(nb0 + SLAB == nbk, jnp.where(ridx == na - 1, 0.0, pltpu.roll(wrapped, na - 1, axis=0)),
                      wrapped)
    ext = [before] + mid + [after]
    return [ext[i] * w[0:1] + ext[i + 1] * w[1:2] + ext[i + 2] * w[2:3] + b for i in range(SLAB)]


def _for_row_slabs(nbk, fn):
    lax.fori_loop(0, nbk // SLAB, lambda j, c: (fn(pl.multiple_of(j * SLAB, SLAB)), c)[1], 0)


def _regroup(x):
    return jnp.swapaxes(x, 0, 1)


def _hyena_kernel(zv_ref, zg1_ref, zg2_ref, wv_ref, bv_ref, wg1_ref, bg1_ref, wg2_ref, bg2_ref,
                  kf_ref, f1_ref, f2_ref, f2i_ref, f3_ref, o_ref, s_ref, z1_ref,
                  *, nbk, na, ngrp, kag):
    step = pl.program_id(2)
    hb = FFT_NB

    def stage1(src_ref, conv):
        def slab(nb0):
            if conv:
                planes = [[v.astype(BF16) for v in _conv3_slab(src_ref, pln, nb0, wv_ref[...], bv_ref[...], nbk)]
                          for pln in range(2)]
            tiles = []
            for i in range(SLAB):
                nb = nb0 + i
                parts = [planes[0][i], planes[1][i]] if conv else [src_ref[0, 0, nb], src_ref[1, 0, nb]]
                rhs = jnp.concatenate(parts, axis=0)
                tiles.append(jnp.dot(f1_ref[nb], rhs, preferred_element_type=F32).astype(BF16))
            s_ref[:, pl.ds(nb0, SLAB), :] = _regroup(jnp.stack(tiles, axis=0))
        _for_row_slabs(nbk, slab)

    def stage3(gate_ref, wg, bg, dst_ref):
        def slab(nb0):
            rows = _regroup(s_ref[:, pl.ds(nb0, SLAB), :])
            gates = [_conv3_slab(gate_ref, pln, nb0, wg, bg, nbk) for pln in range(2)]
            for i in range(SLAB):
                nb = nb0 + i
                y = jnp.dot(f3_ref[nb], rows[i], preferred_element_type=F32)
                for pln in range(2):
                    dst_ref[pln, 0, nb] = (gates[pln][i] * y[pln * na:(pln + 1) * na]).astype(dst_ref.dtype)
        _for_row_slabs(nbk, slab)

    @pl.when(step == 0)
    def _():
        stage1(zv_ref, True)

    @pl.when(step == ngrp)
    def _():
        stage1(z1_ref, False)

    base = (step % ngrp) * kag
    loaded = [jnp.concatenate([s_ref[base + k], s_ref[FFT_NA + base + k]], axis=0) for k in range(kag)]
    results = []
    for k, rhs in enumerate(loaded):
        spec = jnp.dot(f2_ref[...], rhs, preferred_element_type=F32).astype(BF16)
        kf = kf_ref[0, k]
        sr, si = spec[:hb], spec[hb:]
        kr, ki = kf[:hb], kf[hb:]
        prod = jnp.concatenate([sr * kr - si * ki, sr * ki + si * kr], axis=0)
        results.append(jnp.dot(f2i_ref[...], prod, preferred_element_type=F32).astype(BF16))
    for k, back in enumerate(results):
        s_ref[base + k] = back[:hb]
        s_ref[FFT_NA + base + k] = back[hb:]

    @pl.when(step == ngrp - 1)
    def _():
        stage3(zg1_ref, wg1_ref[...], bg1_ref[...], z1_ref)

    @pl.when(step == 2 * ngrp - 1)
    def _():
        stage3(zg2_ref, wg2_ref[...], bg2_ref[...], o_ref)


def _hyena(z5, conv_w, conv_b, kf, tables, cb=MXU_N, kag=2 * SLAB):
    f1, _, f2, f2i, f3 = tables
    _, npair, nbk, na, _ = z5.shape
    c = kf.shape[-1]
    ncb = c // cb
    ngrp = FFT_NA // kag
    one = pl.Buffered(1)
    zspec = lambda grp: pl.BlockSpec((2, 1, nbk, na, cb), lambda p, j, s: (0, p, 0, 0, grp * ncb + j))
    wspec = lambda grp: pl.BlockSpec((3, cb), lambda p, j, s: (0, grp * ncb + j))
    bspec = lambda grp: pl.BlockSpec((1, cb), lambda p, j, s: (0, grp * ncb + j))
    const = lambda a: pl.BlockSpec(a.shape, lambda p, j, s: tuple(0 for _ in a.shape), pipeline_mode=one)
    cw = conv_w.astype(F32)
    cbias = conv_b.astype(F32).reshape(1, -1)
    return pl.pallas_call(
        functools.partial(_hyena_kernel, nbk=nbk, na=na, ngrp=ngrp, kag=kag),
        grid=(npair, ncb, 2 * ngrp),
        in_specs=[zspec(0), zspec(1), zspec(2), wspec(0), bspec(0), wspec(1), bspec(1), wspec(2), bspec(2),
                  pl.BlockSpec((1, kag, 2 * FFT_NB, cb), lambda p, j, s: (s // ngrp, s % ngrp, 0, j)),
                  const(f1), const(f2), const(f2i), const(f3)],
        out_specs=pl.BlockSpec((2, 1, nbk, na, cb), lambda p, j, s: (0, p, 0, 0, j), pipeline_mode=one),
        out_shape=jax.ShapeDtypeStruct((2, npair, nbk, na, c), BF16),
        scratch_shapes=[pltpu.VMEM((2 * FFT_NA, nbk, cb), BF16),
                        pltpu.VMEM((2, 1, nbk, na, cb), BF16)],
        compiler_params=_cparams("arbitrary", "arbitrary", "arbitrary"),
        name="hyena_conv",
    )(z5, z5, z5, cw, cbias, cw, cbias, cw, cbias, kf, f1, f2, f2i, f3)


def _filter_kernel(embt_ref, tv_ref, w1t_ref, b1_ref, w2t_ref, b2_ref, fr_ref, w3_ref, dec_ref, bias_ref,
                   f1_ref, f2_ref, k_ref, s_ref, h_ref, *, nbk, kag):
    first = (pl.program_id(0) == 0) & (pl.program_id(1) == 0)
    step = pl.program_id(2)
    half = FFT_NA // 2
    lanes_per_pass = 8 * FFT_NA

    @pl.when(first & (step == 0))
    def _():
        fr = fr_ref[...]
        for i in range(embt_ref.shape[1] // lanes_per_pass):
            cols = slice(i * lanes_per_pass, (i + 1) * lanes_per_pass)
            h = jnp.sin(fr * (jnp.dot(w1t_ref[...], embt_ref[:, cols], precision=HI,
                                      preferred_element_type=F32) + b1_ref[...]))
            h_ref[:, cols] = jnp.sin(fr * (jnp.dot(w2t_ref[...], h, precision=HI,
                                                   preferred_element_type=F32) + b2_ref[...]))

    @pl.when(step == 0)
    def _():
        dec = dec_ref[0]

        def slab(nb0):
            tiles = []
            for i in range(SLAB):
                nb = nb0 + i
                h = h_ref[:, pl.ds(pl.multiple_of(nb * FFT_NA, FFT_NA), FFT_NA)].T.astype(BF16)
                fwd = jnp.dot(h[:half], w3_ref[0, 0], preferred_element_type=F32)
                bwd = jnp.dot(h[half:], w3_ref[0, 1], preferred_element_type=F32)
                tv = tv_ref[nb]
                window = jnp.exp(-tv[:, 0:1] * dec) * tv[:, 1:2]
                filt = jnp.concatenate([fwd, bwd], axis=0) * window
                tiles.append(jnp.dot(f1_ref[nb], filt.astype(BF16), preferred_element_type=F32).astype(BF16))
            s_ref[:, pl.ds(nb0, SLAB), :] = _regroup(jnp.stack(tiles, axis=0))
        _for_row_slabs(nbk, slab)

    base = step * kag
    bias = bias_ref[0]
    for k in range(kag):
        rhs = jnp.concatenate([s_ref[base + k], s_ref[FFT_NA + base + k]], axis=0)
        spec = jnp.dot(f2_ref[...], rhs, preferred_element_type=F32)
        k_ref[0, k] = jnp.concatenate([spec[:FFT_NB] + bias, spec[FFT_NB:]], axis=0).astype(BF16)


def _filter_spectra(n, w1, b1, w2, b2, freq, w3, decay, bias, tables, cb=MXU_N, kag=16):
    _, f1_real, f2, _, _ = tables
    hid = w2.shape[0]
    c = decay.shape[-1]
    t = np.linspace(0.0, 1.0, n, dtype=np.float32)[:, None]
    w = (2.0 * np.pi * np.arange(n, dtype=np.float32) / n).astype(np.float32)
    bands = np.linspace(1e-4, HY_BANDS - 1, HY_BANDS, dtype=np.float32)
    ang = w[:, None] * bands[None, :]
    emb = np.concatenate([t, np.cos(ang), -np.sin(ang)], axis=-1).astype(np.float32)
    kpad = 64
    idx = FFT_NB * np.arange(FFT_NA)[None, :] + np.arange(FFT_NB)[:, None]
    lagi = np.minimum(np.where(idx < n, idx, FFT_N - idx), n - 1)
    embt = np.zeros((kpad, FFT_N), np.float32)
    embt[:emb.shape[1]] = emb[lagi.reshape(-1)].T
    tv = np.stack([t[lagi, 0], (idx != n).astype(np.float32)], axis=-1)
    col = lambda a: a.reshape(hid, 1).astype(F32)
    w1t = jnp.pad(w1.astype(F32), ((0, kpad - w1.shape[0]), (0, 0))).T
    w3r = w3.astype(BF16).reshape(hid, HY_ORDER, 2, c).transpose(1, 2, 0, 3)
    dec = jnp.abs(decay.astype(F32)).reshape(HY_ORDER, 1, c)
    one = pl.Buffered(1)
    const = lambda shape: pl.BlockSpec(shape, lambda o, j, s: tuple(0 for _ in shape), pipeline_mode=one)
    per_channel = pl.BlockSpec((1, 1, cb), lambda o, j, s: (o, 0, j))
    return pl.pallas_call(
        functools.partial(_filter_kernel, nbk=FFT_NB, kag=kag),
        grid=(HY_ORDER, c // cb, FFT_NA // kag),
        in_specs=[const(embt.shape), const(tv.shape), const((hid, kpad)), const((hid, 1)),
                  const((hid, hid)), const((hid, 1)), const((hid, 1)),
                  pl.BlockSpec((1, 2, hid, cb), lambda o, j, s: (o, 0, 0, j)),
                  per_channel, per_channel,
                  const(f1_real.shape), const(f2.shape)],
        out_specs=pl.BlockSpec((1, kag, 2 * FFT_NB, cb), lambda o, j, s: (o, s, 0, j)),
        out_shape=jax.ShapeDtypeStruct((HY_ORDER, FFT_NA, 2 * FFT_NB, c), BF16),
        scratch_shapes=[pltpu.VMEM((2 * FFT_NA, FFT_NB, cb), BF16),
                        pltpu.VMEM((hid, FFT_N), F32)],
        compiler_params=_cparams("arbitrary", "arbitrary", "arbitrary"),
        name="hyena_filter",
    )(jnp.asarray(embt), jnp.asarray(tv), w1t, col(b1), w2.astype(F32).T, col(b2), col(freq), w3r, dec,
      bias.astype(F32).reshape(HY_ORDER, 1, c), f1_real, f2)


def _out_kernel(x_ref, pos_ref, mod_ref, ys_ref, yh_ref, gw_ref, gb_ref, g5_ref, gh_ref, wo_ref,
                g2_ref, w1_ref, w2_ref, gf_ref, o_ref, *, d):
    nbatch, tokens, _ = x_ref.shape
    rows = nbatch * tokens
    m = mod_ref[...]
    gate1, shift2 = m[:, :, 2 * d:3 * d], m[:, :, 3 * d:4 * d]
    scale2, gate2 = m[:, :, 4 * d:5 * d], m[:, :, 5 * d:6 * d]
    per_batch = lambda v: v.reshape(nbatch, tokens, d)
    h = x_ref[...] + pos_ref[...][None]
    ys = _from_slot_major([ys_ref[t].astype(F32) for t in range(S5_T)], nbatch)
    ab = jnp.dot(ys.astype(BF16), gw_ref[...], preferred_element_type=F32) + gb_ref[...]
    half = ab.shape[-1] // 2
    y5 = ab[:, :half] * jax.nn.sigmoid(ab[:, half:])
    yh = yh_ref[...].reshape(rows, -1).astype(F32)
    mix = jnp.concatenate([_rms(y5, g5_ref[...]), _rms(yh, gh_ref[...])], axis=-1)
    h = h + gate1 * per_batch(jnp.dot(mix.astype(BF16), wo_ref[...], preferred_element_type=F32))
    hn = _rms(h, g2_ref[...][None]) * (1.0 + scale2) + shift2
    hid = jnp.dot(hn.reshape(rows, d).astype(BF16), w1_ref[...], preferred_element_type=F32)
    hid = jnp.square(jnp.maximum(hid, 0.0))
    h = h + gate2 * per_batch(jnp.dot(hid.astype(BF16), w2_ref[...], preferred_element_type=F32))
    o_ref[...] = _rms(h, gf_ref[...][None])


def _output_stage(x, pos, mod3, ys_t, yh, glu_w, glu_b, g5, gh, w_out, g2, w1, w2, gf, tt=64):
    bsz, n, d = x.shape
    dh = ys_t.shape[-1]
    one = pl.Buffered(1)
    const = lambda a: pl.BlockSpec(a.shape, lambda i: tuple(0 for _ in a.shape), pipeline_mode=one)
    row = lambda a: a.reshape(1, -1).astype(F32)
    glu_b, g5, gh, g2, gf = row(glu_b), row(g5), row(gh), row(g2), row(gf)
    return pl.pallas_call(
        functools.partial(_out_kernel, d=d),
        grid=(n // tt,),
        in_specs=[pl.BlockSpec((bsz, tt, d), lambda i: (0, i, 0)),
                  pl.BlockSpec((tt, d), lambda i: (i, 0)),
                  pl.BlockSpec((bsz, 1, mod3.shape[-1]), lambda i: (0, 0, 0)),
                  pl.BlockSpec((S5_T, tt // S5_T * bsz, dh), lambda i: (0, i, 0)),
                  pl.BlockSpec((bsz, tt, yh.shape[-1]), lambda i: (0, i, 0)),
                  const(glu_w), const(glu_b), const(g5), const(gh), const(w_out), const(g2),
                  const(w1), const(w2), const(gf)],
        out_specs=pl.BlockSpec((bsz, tt, d), lambda i: (0, i, 0)),
        out_shape=jax.ShapeDtypeStruct((bsz, n, d), F32),
        compiler_params=_cparams("arbitrary"),
        name="mix_mlp_out",
    )(x, pos, mod3, ys_t, yh, glu_w, glu_b, g5, gh, w_out, g2, w1, w2, gf)


def _pos_table(n, d):
    rows = n // GRID_W
    row = np.repeat(np.arange(rows, dtype=np.float32), GRID_W)
    col = np.tile(np.arange(GRID_W, dtype=np.float32), rows)
    quarter = d // 4
    omega = (1.0 / (POS_BASE ** (np.arange(quarter, dtype=np.float32) / quarter))).astype(np.float32)

    def enc(p):
        ang = p[:, None] * omega[None, :]
        return np.concatenate([np.sin(ang), np.cos(ang)], axis=-1)

    return jnp.asarray(np.concatenate([enc(row), enc(col)], axis=-1).astype(np.float32))


def kernel(x, c, ctx, c_ctx, ada_w, ada_b, norm1_g, w_in, s5_a_re, s5_a_im, s5_log_step, s5_b_re,
           s5_b_im, s5_c_re, s5_c_im, s5_d, s5_glu_w, s5_glu_b, hy_conv_w, hy_conv_b, hy_f_w1,
           hy_f_b1, hy_f_w2, hy_f_b2, hy_f_freq, hy_f_w3, hy_decay, hy_bias, mix_g_s5, mix_g_hy,
           w_out, norm2_g, mlp_w1, mlp_w2, final_g):
    bsz, n, d = x.shape
    depth = ada_w.shape[0]
    d_s5 = s5_d.shape[-1]
    d_hy = hy_bias.shape[-1]
    nctx = ctx.shape[1]
    npair = bsz // 2
    nbk, na = FFT_NB, n // FFT_NB
    pos = _pos_table(n, d)
    tables = _dft_tables(n)

    assert depth == 1 and bsz % 2 == 0 and n % (FFT_NB * 8) == 0 and 2 * n == FFT_N
    mod_rows = 16
    c_rows = jnp.concatenate([c, c_ctx[None], jnp.zeros((mod_rows - bsz - 1, d), c.dtype)], axis=0)
    mod3 = _modulation(c_rows, ada_w[0], ada_b[0]).reshape(mod_rows, 1, N_MOD * d)

    w_in_b = w_in[0].astype(BF16)
    u_t, z = _project(x, pos, mod3, None, norm1_g[0], w_in_b, d_s5, 3 * d_hy, tt=128)
    (uc_t,) = _project(ctx, None, mod3, bsz, norm1_g[0], w_in_b, d_s5, 0, tt=128)

    s5_mats = _s5_tables(s5_a_re[0], s5_a_im[0], s5_log_step[0], s5_b_re[0], s5_b_im[0],
                         s5_c_re[0], s5_c_im[0], s5_d[0])
    ys_t = _s5_scan(u_t, uc_t, bsz, *s5_mats)

    kf = _filter_spectra(n, hy_f_w1[0], hy_f_b1[0], hy_f_w2[0], hy_f_b2[0], hy_f_freq[0],
                         hy_f_w3[0], hy_decay[0], hy_bias[0], tables)
    z5 = z.reshape(bsz, na, nbk, 3 * d_hy).transpose(0, 2, 1, 3).reshape(2, npair, nbk, na, 3 * d_hy)
    yh = _hyena(z5, hy_conv_w[0], hy_conv_b[0], kf, tables)
    yh = yh.reshape(bsz, nbk, na, d_hy).transpose(0, 2, 1, 3).reshape(bsz, n, d_hy)

    return _output_stage(x, pos, mod3, ys_t, yh, s5_glu_w[0].astype(BF16), s5_glu_b[0], mix_g_s5[0],
                         mix_g_hy[0], w_out[0].astype(BF16), norm2_g[0], mlp_w1[0].astype(BF16),
                         mlp_w2[0].astype(BF16), final_g)
```

```python
import functools
import math

import numpy as np
import jax
import jax.numpy as jnp
from jax import lax
from jax.experimental import pallas as pl
from jax.experimental.pallas import tpu as pltpu

F32 = jnp.float32
BF16 = jnp.bfloat16
HI = lax.Precision.HIGHEST

EPS = 1e-6
GRID_W = 64
POS_BASE = 10000.0
N_MOD = 6
S5_C = 16
S5_P = 64
S5_T = 16
HY_BANDS = 16
HY_ORDER = 2

FFT_N = 8192
FFT_NA = 128
FFT_NB = 64

LANES = 128
MXU_N = 256
SLAB = 16
RELAYOUT_ROWS = 256
SCAN_GROUPS = 2
VMEM_LIMIT = 56 * 1024 * 1024


def _cparams(*sem):
    return pltpu.CompilerParams(dimension_semantics=sem, vmem_limit_bytes=VMEM_LIMIT)


def _rms(x, g):
    return x * lax.rsqrt(jnp.mean(x * x, axis=-1, keepdims=True) + EPS) * g


def _mod_kernel(c_ref, w_ref, b_ref, o_ref):
    c = c_ref[...]
    a = c * jax.nn.sigmoid(c)
    o_ref[...] = jnp.dot(a.astype(BF16), w_ref[...].astype(BF16),
                         preferred_element_type=F32) + b_ref[...]


def _modulation(c_rows, ada_w, ada_b):
    rows, d = c_rows.shape
    n = ada_w.shape[1]
    bn = 1024
    return pl.pallas_call(
        _mod_kernel,
        grid=(n // bn,),
        in_specs=[pl.BlockSpec((rows, d), lambda j: (0, 0)),
                  pl.BlockSpec((d, bn), lambda j: (0, j)),
                  pl.BlockSpec((1, bn), lambda j: (0, j))],
        out_specs=pl.BlockSpec((rows, bn), lambda j: (0, j)),
        out_shape=jax.ShapeDtypeStruct((rows, n), F32),
        compiler_params=_cparams("arbitrary"),
        name="ada_mod",
    )(c_rows, ada_w, ada_b.reshape(1, n))


def _to_slot_major(v, nbatch):
    tokens = v.shape[0] // nbatch
    by_token = _regroup(v.reshape(nbatch, tokens, v.shape[1]))
    return [jnp.concatenate([by_token[k * S5_T + t] for k in range(tokens // S5_T)], axis=0)
            for t in range(S5_T)]


def _from_slot_major(slots, nbatch):
    nchunk = slots[0].shape[0] // nbatch
    by_token = jnp.stack([slots[t][k * nbatch:(k + 1) * nbatch] for k in range(nchunk) for t in range(S5_T)],
                         axis=0)
    return _regroup(by_token).reshape(nbatch * nchunk * S5_T, -1)


def _proj_kernel(*refs, d, use_pos, d_s5):
    if use_pos:
        x_ref, pos_ref, mod_ref, g_ref, w_ref = refs[:5]
        outs = refs[5:]
        h = x_ref[...] + pos_ref[...][None]
    else:
        x_ref, mod_ref, g_ref, w_ref = refs[:4]
        outs = refs[4:]
        h = x_ref[...]
    nbatch, tokens, _ = h.shape
    m = mod_ref[...]
    hn = _rms(h, g_ref[...][None]) * (1.0 + m[:, :, d:2 * d]) + m[:, :, 0:d]
    proj = jnp.dot(hn.reshape(nbatch * tokens, d).astype(BF16), w_ref[...], preferred_element_type=F32)
    for t, rows in enumerate(_to_slot_major(proj[:, :d_s5], nbatch)):
        outs[0][t] = rows.astype(BF16)
    if len(outs) > 1:
        outs[1][...] = proj[:, d_s5:].reshape(nbatch, tokens, -1).astype(BF16)


def _project(x, pos, mod3, mod_row, g, w_bf16, d_s5, d_rest, tt):
    bsz, n, d = x.shape
    use_pos = pos is not None
    in_specs = [pl.BlockSpec((bsz, tt, d), lambda i: (0, i, 0))]
    args = [x]
    if use_pos:
        in_specs.append(pl.BlockSpec((tt, d), lambda i: (i, 0)))
        args.append(pos)
    mod_rows = bsz if mod_row is None else 1
    in_specs += [pl.BlockSpec((mod_rows, 1, mod3.shape[-1]), lambda i: (0 if mod_row is None else mod_row, 0, 0)),
                 pl.BlockSpec((1, d), lambda i: (0, 0)),
                 pl.BlockSpec((d, d_s5 + d_rest), lambda i: (0, 0))]
    args += [mod3, g.reshape(1, d), w_bf16]
    rows = tt // S5_T * bsz
    out_specs = [pl.BlockSpec((S5_T, rows, d_s5), lambda i: (0, i, 0))]
    out_shape = [jax.ShapeDtypeStruct((S5_T, n // S5_T * bsz, d_s5), BF16)]
    if d_rest:
        out_specs.append(pl.BlockSpec((bsz, tt, d_rest), lambda i: (0, i, 0)))
        out_shape.append(jax.ShapeDtypeStruct((bsz, n, d_rest), BF16))
    return pl.pallas_call(
        functools.partial(_proj_kernel, d=d, use_pos=use_pos, d_s5=d_s5),
        grid=(n // tt,),
        in_specs=in_specs,
        out_specs=out_specs,
        out_shape=out_shape,
        compiler_params=_cparams("arbitrary"),
        name="norm_proj",
    )(*args)


def _s5_tables(a_re, a_im, log_step, b_re, b_im, c_re, c_im, d_skip):
    T, C, P = S5_T, S5_C, S5_P
    G = a_re.shape[1]
    gpb = LANES // C
    slots = np.arange(T)
    tok = np.stack([8 * (slots // 8) + (slots % 8 - o) % 8 for o in range(gpb)])
    tok_g = tok[np.arange(G) % gpb]

    step = jnp.exp(log_step.astype(F32))[..., None]
    ar = a_re.astype(F32) * step
    ai = a_im.astype(F32) * step

    def powers(d, expo):
        e = jnp.asarray(np.broadcast_to(expo, (G, expo.shape[-1])).astype(np.float32))[:, :, None]
        mag = jnp.exp(ar[d][:, None, :] * e)
        return mag * jnp.cos(ai[d][:, None, :] * e), mag * jnp.sin(ai[d][:, None, :] * e)

    lr, li = a_re.astype(F32), a_im.astype(F32)
    first = [powers(d, np.ones(1)) for d in range(2)]
    nr = jnp.stack([first[d][0][:, 0] for d in range(2)]) - 1.0
    ni = jnp.stack([first[d][1][:, 0] for d in range(2)])
    den = lr * lr + li * li
    qr = (nr * lr + ni * li) / den
    qi = (ni * lr - nr * li) / den
    bbr = qr[..., None] * b_re - qi[..., None] * b_im
    bbi = qr[..., None] * b_im + qi[..., None] * b_re
    cr, ci = c_re.astype(F32), c_im.astype(F32)

    kerns = []
    for d in range(2):
        pr, pi = powers(d, np.arange(T))
        wr = cr[d][:, None] * pr[:, :, None, :] - ci[d][:, None] * pi[:, :, None, :]
        wi = cr[d][:, None] * pi[:, :, None, :] + ci[d][:, None] * pr[:, :, None, :]
        kerns.append(jnp.einsum('gtcp,gpe->gtce', wr, bbr[d], precision=HI)
                     - jnp.einsum('gtcp,gpe->gtce', wi, bbi[d], precision=HI))
    kcat = jnp.concatenate(kerns, axis=1)
    kpad = jnp.pad(kcat.reshape(G, 2 * T * C, C), ((0, 0), (0, 0), (0, LANES - C)))
    lag = np.repeat(tok[:, :, None] - tok[:, None, :], C, axis=2).astype(np.int32)
    d_lanes = jnp.tile(d_skip.astype(F32).reshape(G, 1, C), (1, 1, T))
    m1 = pl.pallas_call(
        _s5_intra_kernel,
        grid=(G,),
        in_specs=[pl.BlockSpec((1, 2 * T * C, LANES), lambda g: (g, 0, 0)),
                  pl.BlockSpec((1, T, T * C), lambda g: (g % gpb, 0, 0)),
                  pl.BlockSpec((1, 1, T * C), lambda g: (g, 0, 0))],
        out_specs=pl.BlockSpec((1, T * C, T * C), lambda g: (g, 0, 0)),
        out_shape=jax.ShapeDtypeStruct((G, T * C, T * C), BF16),
        compiler_params=_cparams("arbitrary"),
        name="s5_intra_table",
    )(kpad, jnp.asarray(lag), d_lanes)

    def in_mat(d, expo):
        p_r, p_i = powers(d, expo)
        br, bi = bbr[d].transpose(0, 2, 1)[:, None], bbi[d].transpose(0, 2, 1)[:, None]
        er = p_r[:, :, None, :] * br - p_i[:, :, None, :] * bi
        ei = p_r[:, :, None, :] * bi + p_i[:, :, None, :] * br
        return er.reshape(G, T * C, P), ei.reshape(G, T * C, P)

    efr, efi = in_mat(0, T - 1 - tok_g)
    ebr, ebi = in_mat(1, tok_g)
    m2 = jnp.concatenate([efr, ebr, efi, ebi], axis=-1)

    def out_mat(d, expo):
        p_r, p_i = powers(d, expo)
        crt = cr[d].transpose(0, 2, 1)[:, :, None, :]
        cit = ci[d].transpose(0, 2, 1)[:, :, None, :]
        prt = p_r.transpose(0, 2, 1)[:, :, :, None]
        pit = p_i.transpose(0, 2, 1)[:, :, :, None]
        vr = crt * prt - cit * pit
        vi = crt * pit + cit * prt
        return vr.reshape(G, P, T * C), -vi.reshape(G, P, T * C)

    vfr, vfi = out_mat(0, tok_g + 1)
    vbr, vbi = out_mat(1, T - tok_g)
    m3 = jnp.concatenate([vfr, vbr, vfi, vbi], axis=1)

    last = [powers(d, np.full(1, T)) for d in range(2)]
    lam = jnp.stack([jnp.concatenate([last[0][c][:, 0], last[1][c][:, 0]], axis=-1)
                     for c in range(2)], axis=1)
    is_fwd = jnp.asarray((np.arange(4 * P) // P) % 2 == 0)[None, :, None]
    m3f = jnp.where(is_fwd, m3, 0.0).astype(BF16)
    m3b = jnp.where(is_fwd, 0.0, m3).astype(BF16)
    return m1, m2.astype(BF16), m3f, m3b, lam


def _s5_intra_kernel(k_ref, lag_ref, d_ref, o_ref):
    T, C = S5_T, S5_C
    k = k_ref[0]
    shift = C
    while shift < LANES:
        k = k + pltpu.roll(k, shift, axis=1)
        shift *= 2
    k = jnp.concatenate([k, k], axis=1)
    block = lambda x: k[x * C:(x + 1) * C]
    row = lax.broadcasted_iota(jnp.int32, (C, T * C), 0)
    lane_c = lax.broadcasted_iota(jnp.int32, (C, T * C), 1) % C
    centre = block(0) + block(T) + jnp.where(row == lane_c, d_ref[0], 0.0)
    block_rows = []
    for l in range(T):
        lag = lag_ref[0, l:l + 1, :]
        acc = centre
        for x in range(1, T):
            acc = jnp.where(lag == x, block(x), acc)
            acc = jnp.where(lag == -x, block(T + x), acc)
        block_rows.append(acc)
    o_ref[0] = jnp.concatenate(block_rows, axis=0).T.astype(BF16)


def _gelu_tanh(x):
    return 0.5 * x * (1.0 + jnp.tanh(math.sqrt(2.0 / math.pi) * (x + 0.044715 * (x * x * x))))


def _s5_kernel(u_ref, uc_ref, m1_ref, m2_ref, m3f_ref, m3b_ref, lam_ref, o_ref,
               z_ref, e_ref, sf_ref, sb_ref, *, nbatch, nchunk, nctx):
    groups = LANES // S5_C
    rows = nbatch * nchunk
    lane = lax.broadcasted_iota(jnp.int32, (1, LANES), 1)
    slot_bits = [(((lane // S5_C) >> j) & 1) == 1 for j in range(3)]

    def skew(xs):
        cur = [xs[(-k) % groups] for k in range(groups)]
        for j, bit in enumerate(slot_bits):
            cur = [jnp.where(bit, cur[(i - (1 << j)) % groups], cur[i]) for i in range(groups)]
        return cur

    def to_chunk_rows(read_t, nrows, row0):
        for h in range(2):
            rolled = []
            for t8 in range(groups):
                v = read_t(8 * h + t8).astype(F32)
                rolled.append(pltpu.roll(v, t8 * S5_C, axis=1) if t8 else v)
            for g, zg in enumerate(skew(rolled)):
                z_ref[g, row0:row0 + nrows, h * LANES:(h + 1) * LANES] = zg.astype(BF16)

    for r0 in range(0, rows, RELAYOUT_ROWS):
        to_chunk_rows(lambda t, r0=r0: u_ref[t, r0:r0 + RELAYOUT_ROWS], RELAYOUT_ROWS, r0)
    to_chunk_rows(lambda t: uc_ref[t], nbatch * nctx, rows)

    wide = 2 * LANES
    fwd_lane = (lax.broadcasted_iota(jnp.int32, (1, wide), 1) % LANES) < (LANES // 2)
    tile = lambda k: pl.ds(pl.multiple_of(k * nbatch, nbatch), nbatch)

    npar = e_ref.shape[0]
    for g0 in range(0, groups, npar):
        gs = range(g0, g0 + npar)
        for j, g in enumerate(gs):
            e_ref[j] = jnp.dot(z_ref[g], m2_ref[g], preferred_element_type=F32)
        lam_r = [lam_ref[g, 0:1, :] for g in gs]
        lam_i = [lam_ref[g, 1:2, :] for g in gs]

        def advance(state, kf, kb):
            out = []
            for j, (sr, si) in enumerate(state):
                e2 = jnp.where(fwd_lane, e_ref[j, tile(kf), :], e_ref[j, tile(kb), :])
                out.append((lam_r[j] * sr - lam_i[j] * si + e2[:, :LANES],
                            lam_r[j] * si + lam_i[j] * sr + e2[:, LANES:]))
            return tuple(out)

        zero = jnp.zeros((nbatch, LANES), F32)
        state = tuple((zero, zero) for _ in gs)
        for i in range(nctx):
            state = advance(state, nchunk + i, nchunk + nctx - 1 - i)

        def body(i, state):
            kf, kb = i, nchunk - 1 - i
            for j, (sr, si) in enumerate(state):
                s = jnp.concatenate([sr, si], axis=-1)
                sf_ref[j, tile(kf), :] = s
                sb_ref[j, tile(kb), :] = s
            return advance(state, kf, kb)

        lax.fori_loop(0, nchunk, body, state, unroll=2)

        for j, g in enumerate(gs):
            y = (jnp.dot(z_ref[g, 0:rows, :], m1_ref[g], preferred_element_type=F32)
                 + jnp.dot(sf_ref[j].astype(BF16), m3f_ref[g], preferred_element_type=F32)
                 + jnp.dot(sb_ref[j].astype(BF16), m3b_ref[g], preferred_element_type=F32))
            z_ref[g, 0:rows, :] = _gelu_tanh(y).astype(BF16)

    for r0 in range(0, rows, RELAYOUT_ROWS):
        for h in range(2):
            ys = [z_ref[g, r0:r0 + RELAYOUT_ROWS, h * LANES:(h + 1) * LANES].astype(F32)
                  for g in range(groups)]
            for t8, acc in enumerate(skew(ys)):
                if t8:
                    acc = pltpu.roll(acc, LANES - t8 * S5_C, axis=1)
                o_ref[8 * h + t8, r0:r0 + RELAYOUT_ROWS] = acc.astype(BF16)


def _s5_scan(u_t, uc_t, bsz, m1, m2, m3f, m3b, lam):
    T, rows, width = u_t.shape
    ctx_rows = uc_t.shape[1]
    gpb = LANES // S5_C
    kern = functools.partial(_s5_kernel, nbatch=bsz, nchunk=rows // bsz, nctx=ctx_rows // bsz)
    one = pl.Buffered(1)
    mat = pl.BlockSpec((gpb, 2 * LANES, 2 * LANES), lambda j: (j, 0, 0), pipeline_mode=one)
    return pl.pallas_call(
        kern,
        grid=(width // LANES,),
        in_specs=[pl.BlockSpec((T, rows, LANES), lambda j: (0, 0, j)),
                  pl.BlockSpec((T, ctx_rows, LANES), lambda j: (0, 0, j)),
                  mat, mat, mat, mat,
                  pl.BlockSpec((gpb, 2, LANES), lambda j: (j, 0, 0))],
        out_specs=pl.BlockSpec((T, rows, LANES), lambda j: (0, 0, j), pipeline_mode=one),
        out_shape=jax.ShapeDtypeStruct(u_t.shape, BF16),
        scratch_shapes=[pltpu.VMEM((gpb, rows + ctx_rows, 2 * LANES), BF16),
                        pltpu.VMEM((SCAN_GROUPS, rows + ctx_rows, 2 * LANES), F32),
                        pltpu.VMEM((SCAN_GROUPS, rows, 2 * LANES), F32),
                        pltpu.VMEM((SCAN_GROUPS, rows, 2 * LANES), F32)],
        compiler_params=_cparams("arbitrary"),
        name="s5_scan",
    )(u_t, uc_t, m1, m2, m3f, m3b, lam)


def _dft_tables(n_seq):
    na_sig = n_seq // FFT_NB
    ka = np.arange(FFT_NA)[:, None]
    nb = np.arange(FFT_NB)[:, None, None]

    def stage1(n_in):
        na = np.arange(n_in)[None, :]
        ang = -2.0 * np.pi * (na * ka / FFT_NA)[None] - 2.0 * np.pi * (nb * ka[None]) / FFT_N
        return np.cos(ang), np.sin(ang)

    c, s = stage1(na_sig)
    f1 = np.concatenate([np.concatenate([c, -s], axis=2), np.concatenate([s, c], axis=2)], axis=1)
    c, s = stage1(FFT_NA)
    f1_real = np.concatenate([c, s], axis=1)
    kb = np.arange(FFT_NB)[:, None]
    nbv = np.arange(FFT_NB)[None, :]
    ang = -2.0 * np.pi * kb * nbv / FFT_NB
    c, s = np.cos(ang), np.sin(ang)
    f2 = np.block([[c, -s], [s, c]])
    f2_inv = np.block([[c, s], [-s, c]]) / FFT_NB
    nap = np.arange(na_sig)[:, None]
    kav = np.arange(FFT_NA)[None, :]
    ang = 2.0 * np.pi * (nap * kav / FFT_NA)[None] + 2.0 * np.pi * (nb * kav[None]) / FFT_N
    c, s = np.cos(ang) / FFT_NA, np.sin(ang) / FFT_NA
    f3 = np.concatenate([np.concatenate([c, -s], axis=2), np.concatenate([s, c], axis=2)], axis=1)
    as_bf16 = lambda a: jnp.asarray(a.astype(np.float32)).astype(BF16)
    return as_bf16(f1), as_bf16(f1_real), as_bf16(f2), as_bf16(f2_inv), as_bf16(f3)


def _conv3_slab(z_ref, plane, nb0, w, b, nbk):
    def rows(i):
        return z_ref[plane, 0, i].astype(F32)

    mid = [rows(nb0 + i) for i in range(SLAB)]
    na = mid[0].shape[0]
    ridx = lax.broadcasted_iota(jnp.int32, mid[0].shape, 0)
    wrapped = rows((nb0 + nbk - 1) % nbk)
    before = jnp.where(nb0 == 0, jnp.where(ridx == 0, 0.0, pltpu.roll(wrapped, 1, axis=0)), wrapped)
    wrapped = rows((nb0 + SLAB) % nbk)
    after = jnp.where(nb0 + SLAB == nbk, jnp.where(ridx == na - 1, 0.0, pltpu.roll(wrapped, na - 1, axis=0)),
                      wrapped)
    ext = [before] + mid + [after]
    return [ext[i] * w[0:1] + ext[i + 1] * w[1:2] + ext[i + 2] * w[2:3] + b for i in range(SLAB)]


def _for_row_slabs(nbk, fn):
    lax.fori_loop(0, nbk // SLAB, lambda j, c: (fn(pl.multiple_of(j * SLAB, SLAB)), c)[1], 0)


def _regroup(x):
    return jnp.swapaxes(x, 0, 1)


def _hyena_kernel(zv_ref, zg1_ref, zg2_ref, wv_ref, bv_ref, wg1_ref, bg1_ref, wg2_ref, bg2_ref,
                  kf_ref, f1_ref, f2_ref, f2i_ref, f3_ref, o_ref, s_ref, z1_ref,
                  *, nbk, na, ngrp, kag):
    step = pl.program_id(2)
    hb = FFT_NB

    def stage1(src_ref, conv):
        def slab(nb0):
            if conv:
                planes = [[v.astype(BF16) for v in _conv3_slab(src_ref, pln, nb0, wv_ref[...], bv_ref[...], nbk)]
                          for pln in range(2)]
            tiles = []
            for i in range(SLAB):
                nb = nb0 + i
                parts = [planes[0][i], planes[1][i]] if conv else [src_ref[0, 0, nb], src_ref[1, 0, nb]]
                rhs = jnp.concatenate(parts, axis=0)
                tiles.append(jnp.dot(f1_ref[nb], rhs, preferred_element_type=F32).astype(BF16))
            s_ref[:, pl.ds(nb0, SLAB), :] = _regroup(jnp.stack(tiles, axis=0))
        _for_row_slabs(nbk, slab)

    def stage3(gate_ref, wg, bg, dst_ref):
        def slab(nb0):
            rows = _regroup(s_ref[:, pl.ds(nb0, SLAB), :])
            gates = [_conv3_slab(gate_ref, pln, nb0, wg, bg, nbk) for pln in range(2)]
            for i in range(SLAB):
                nb = nb0 + i
                y = jnp.dot(f3_ref[nb], rows[i], preferred_element_type=F32)
                for pln in range(2):
                    dst_ref[pln, 0, nb] = (gates[pln][i] * y[pln * na:(pln + 1) * na]).astype(dst_ref.dtype)
        _for_row_slabs(nbk, slab)

    @pl.when(step == 0)
    def _():
        stage1(zv_ref, True)

    @pl.when(step == ngrp)
    def _():
        stage1(z1_ref, False)

    base = (step % ngrp) * kag
    loaded = [jnp.concatenate([s_ref[base + k], s_ref[FFT_NA + base + k]], axis=0) for k in range(kag)]
    results = []
    for k, rhs in enumerate(loaded):
        spec = jnp.dot(f2_ref[...], rhs, preferred_element_type=F32).astype(BF16)
        kf = kf_ref[0, k]
        sr, si = spec[:hb], spec[hb:]
        kr, ki = kf[:hb], kf[hb:]
        prod = jnp.concatenate([sr * kr - si * ki, sr * ki + si * kr], axis=0)
        results.append(jnp.dot(f2i_ref[...], prod, preferred_element_type=F32).astype(BF16))
    for k, back in enumerate(results):
        s_ref[base + k] = back[:hb]
        s_ref[FFT_NA + base + k] = back[hb:]

    @pl.when(step == ngrp - 1)
    def _():
        stage3(zg1_ref, wg1_ref[...], bg1_ref[...], z1_ref)

    @pl.when(step == 2 * ngrp - 1)
    def _():
        stage3(zg2_ref, wg2_ref[...], bg2_ref[...], o_ref)


def _hyena(z5, conv_w, conv_b, kf, tables, cb=MXU_N, kag=2 * SLAB):
    f1, _, f2, f2i, f3 = tables
    _, npair, nbk, na, _ = z5.shape
    c = kf.shape[-1]
    ncb = c // cb
    ngrp = FFT_NA // kag
    one = pl.Buffered(1)
    zspec = lambda grp: pl.BlockSpec((2, 1, nbk, na, cb), lambda p, j, s: (0, p, 0, 0, grp * ncb + j))
    wspec = lambda grp: pl.BlockSpec((3, cb), lambda p, j, s: (0, grp * ncb + j))
    bspec = lambda grp: pl.BlockSpec((1, cb), lambda p, j, s: (0, grp * ncb + j))
    const = lambda a: pl.BlockSpec(a.shape, lambda p, j, s: tuple(0 for _ in a.shape), pipeline_mode=one)
    cw = conv_w.astype(F32)
    cbias = conv_b.astype(F32).reshape(1, -1)
    return pl.pallas_call(
        functools.partial(_hyena_kernel, nbk=nbk, na=na, ngrp=ngrp, kag=kag),
        grid=(npair, ncb, 2 * ngrp),
        in_specs=[zspec(0), zspec(1), zspec(2), wspec(0), bspec(0), wspec(1), bspec(1), wspec(2), bspec(2),
                  pl.BlockSpec((1, kag, 2 * FFT_NB, cb), lambda p, j, s: (s // ngrp, s % ngrp, 0, j)),
                  const(f1), const(f2), const(f2i), const(f3)],
        out_specs=pl.BlockSpec((2, 1, nbk, na, cb), lambda p, j, s: (0, p, 0, 0, j), pipeline_mode=one),
        out_shape=jax.ShapeDtypeStruct((2, npair, nbk, na, c), BF16),
        scratch_shapes=[pltpu.VMEM((2 * FFT_NA, nbk, cb), BF16),
                        pltpu.VMEM((2, 1, nbk, na, cb), BF16)],
        compiler_params=_cparams("arbitrary", "arbitrary", "arbitrary"),
        name="hyena_conv",
    )(z5, z5, z5, cw, cbias, cw, cbias, cw, cbias, kf, f1, f2, f2i, f3)


def _filter_kernel(embt_ref, tv_ref, w1t_ref, b1_ref, w2t_ref, b2_ref, fr_ref, w3_ref, dec_ref, bias_ref,
                   f1_ref, f2_ref, k_ref, s_ref, h_ref, *, nbk, kag):
    first = (pl.program_id(0) == 0) & (pl.program_id(1) == 0)
    step = pl.program_id(2)
    half = FFT_NA // 2
    lanes_per_pass = 8 * FFT_NA

    @pl.when(first & (step == 0))
    def _():
        fr = fr_ref[...]
        for i in range(embt_ref.shape[1] // lanes_per_pass):
            cols = slice(i * lanes_per_pass, (i + 1) * lanes_per_pass)
            h = jnp.sin(fr * (jnp.dot(w1t_ref[...], embt_ref[:, cols], precision=HI,
                                      preferred_element_type=F32) + b1_ref[...]))
            h_ref[:, cols] = jnp.sin(fr * (jnp.dot(w2t_ref[...], h, precision=HI,
                                                   preferred_element_type=F32) + b2_ref[...]))

    @pl.when(step == 0)
    def _():
        dec = dec_ref[0]

        def slab(nb0):
            tiles = []
            for i in range(SLAB):
                nb = nb0 + i
                h = h_ref[:, pl.ds(pl.multiple_of(nb * FFT_NA, FFT_NA), FFT_NA)].T.astype(BF16)
                fwd = jnp.dot(h[:half], w3_ref[0, 0], preferred_element_type=F32)
                bwd = jnp.dot(h[half:], w3_ref[0, 1], preferred_element_type=F32)
                tv = tv_ref[nb]
                window = jnp.exp(-tv[:, 0:1] * dec) * tv[:, 1:2]
                filt = jnp.concatenate([fwd, bwd], axis=0) * window
                tiles.append(jnp.dot(f1_ref[nb], filt.astype(BF16), preferred_element_type=F32).astype(BF16))
            s_ref[:, pl.ds(nb0, SLAB), :] = _regroup(jnp.stack(tiles, axis=0))
        _for_row_slabs(nbk, slab)

    base = step * kag
    bias = bias_ref[0]
    for k in range(kag):
        rhs = jnp.concatenate([s_ref[base + k], s_ref[FFT_NA + base + k]], axis=0)
        spec = jnp.dot(f2_ref[...], rhs, preferred_element_type=F32)
        k_ref[0, k] = jnp.concatenate([spec[:FFT_NB] + bias, spec[FFT_NB:]], axis=0).astype(BF16)


def _filter_spectra(n, w1, b1, w2, b2, freq, w3, decay, bias, tables, cb=MXU_N, kag=16):
    _, f1_real, f2, _, _ = tables
    hid = w2.shape[0]
    c = decay.shape[-1]
    t = np.linspace(0.0, 1.0, n, dtype=np.float32)[:, None]
    w = (2.0 * np.pi * np.arange(n, dtype=np.float32) / n).astype(np.float32)
    bands = np.linspace(1e-4, HY_BANDS - 1, HY_BANDS, dtype=np.float32)
    ang = w[:, None] * bands[None, :]
    emb = np.concatenate([t, np.cos(ang), -np.sin(ang)], axis=-1).astype(np.float32)
    kpad = 64
    idx = FFT_NB * np.arange(FFT_NA)[None, :] + np.arange(FFT_NB)[:, None]
    lagi = np.minimum(np.where(idx < n, idx, FFT_N - idx), n - 1)
    embt = np.zeros((kpad, FFT_N), np.float32)
    embt[:emb.shape[1]] = emb[lagi.reshape(-1)].T
    tv = np.stack([t[lagi, 0], (idx != n).astype(np.float32)], axis=-1)
    col = lambda a: a.reshape(hid, 1).astype(F32)
    w1t = jnp.pad(w1.astype(F32), ((0, kpad - w1.shape[0]), (0, 0))).T
    w3r = w3.astype(BF16).reshape(hid, HY_ORDER, 2, c).transpose(1, 2, 0, 3)
    dec = jnp.abs(decay.astype(F32)).reshape(HY_ORDER, 1, c)
    one = pl.Buffered(1)
    const = lambda shape: pl.BlockSpec(shape, lambda o, j, s: tuple(0 for _ in shape), pipeline_mode=one)
    per_channel = pl.BlockSpec((1, 1, cb), lambda o, j, s: (o, 0, j))
    return pl.pallas_call(
        functools.partial(_filter_kernel, nbk=FFT_NB, kag=kag),
        grid=(HY_ORDER, c // cb, FFT_NA // kag),
        in_specs=[const(embt.shape), const(tv.shape), const((hid, kpad)), const((hid, 1)),
                  const((hid, hid)), const((hid, 1)), const((hid, 1)),
                  pl.BlockSpec((1, 2, hid, cb), lambda o, j, s: (o, 0, 0, j)),
                  per_channel, per_channel,
                  const(f1_real.shape), const(f2.shape)],
        out_specs=pl.BlockSpec((1, kag, 2 * FFT_NB, cb), lambda o, j, s: (o, s, 0, j)),
        out_shape=jax.ShapeDtypeStruct((HY_ORDER, FFT_NA, 2 * FFT_NB, c), BF16),
        scratch_shapes=[pltpu.VMEM((2 * FFT_NA, FFT_NB, cb), BF16),
                        pltpu.VMEM((hid, FFT_N), F32)],
        compiler_params=_cparams("arbitrary", "arbitrary", "arbitrary"),
        name="hyena_filter",
    )(jnp.asarray(embt), jnp.asarray(tv), w1t, col(b1), w2.astype(F32).T, col(b2), col(freq), w3r, dec,
      bias.astype(F32).reshape(HY_ORDER, 1, c), f1_real, f2)


def _out_kernel(x_ref, pos_ref, mod_ref, ys_ref, yh_ref, gw_ref, gb_ref, g5_ref, gh_ref, wo_ref,
                g2_ref, w1_ref, w2_ref, gf_ref, o_ref, *, d):
    nbatch, tokens, _ = x_ref.shape
    rows = nbatch * tokens
    m = mod_ref[...]
    gate1, shift2 = m[:, :, 2 * d:3 * d], m[:, :, 3 * d:4 * d]
    scale2, gate2 = m[:, :, 4 * d:5 * d], m[:, :, 5 * d:6 * d]
    per_batch = lambda v: v.reshape(nbatch, tokens, d)
    h = x_ref[...] + pos_ref[...][None]
    ys = _from_slot_major([ys_ref[t].astype(F32) for t in range(S5_T)], nbatch)
    ab = jnp.dot(ys.astype(BF16), gw_ref[...], preferred_element_type=F32) + gb_ref[...]
    half = ab.shape[-1] // 2
    y5 = ab[:, :half] * jax.nn.sigmoid(ab[:, half:])
    yh = yh_ref[...].reshape(rows, -1).astype(F32)
    mix = jnp.concatenate([_rms(y5, g5_ref[...]), _rms(yh, gh_ref[...])], axis=-1)
    h = h + gate1 * per_batch(jnp.dot(mix.astype(BF16), wo_ref[...], preferred_element_type=F32))
    hn = _rms(h, g2_ref[...][None]) * (1.0 + scale2) + shift2
    hid = jnp.dot(hn.reshape(rows, d).astype(BF16), w1_ref[...], preferred_element_type=F32)
    hid = jnp.square(jnp.maximum(hid, 0.0))
    h = h + gate2 * per_batch(jnp.dot(hid.astype(BF16), w2_ref[...], preferred_element_type=F32))
    o_ref[...] = _rms(h, gf_ref[...][None])


def _output_stage(x, pos, mod3, ys_t, yh, glu_w, glu_b, g5, gh, w_out, g2, w1, w2, gf, tt=64):
    bsz, n, d = x.shape
    dh = ys_t.shape[-1]
    one = pl.Buffered(1)
    const = lambda a: pl.BlockSpec(a.shape, lambda i: tuple(0 for _ in a.shape), pipeline_mode=one)
    row = lambda a: a.reshape(1, -1).astype(F32)
    glu_b, g5, gh, g2, gf = row(glu_b), row(g5), row(gh), row(g2), row(gf)
    return pl.pallas_call(
        functools.partial(_out_kernel, d=d),
        grid=(n // tt,),
        in_specs=[pl.BlockSpec((bsz, tt, d), lambda i: (0, i, 0)),
                  pl.BlockSpec((tt, d), lambda i: (i, 0)),
                  pl.BlockSpec((bsz, 1, mod3.shape[-1]), lambda i: (0, 0, 0)),
                  pl.BlockSpec((S5_T, tt // S5_T * bsz, dh), lambda i: (0, i, 0)),
                  pl.BlockSpec((bsz, tt, yh.shape[-1]), lambda i: (0, i, 0)),
                  const(glu_w), const(glu_b), const(g5), const(gh), const(w_out), const(g2),
                  const(w1), const(w2), const(gf)],
        out_specs=pl.BlockSpec((bsz, tt, d), lambda i: (0, i, 0)),
        out_shape=jax.ShapeDtypeStruct((bsz, n, d), F32),
        compiler_params=_cparams("arbitrary"),
        name="mix_mlp_out",
    )(x, pos, mod3, ys_t, yh, glu_w, glu_b, g5, gh, w_out, g2, w1, w2, gf)


def _pos_table(n, d):
    rows = n // GRID_W
    row = np.repeat(np.arange(rows, dtype=np.float32), GRID_W)
    col = np.tile(np.arange(GRID_W, dtype=np.float32), rows)
    quarter = d // 4
    omega = (1.0 / (POS_BASE ** (np.arange(quarter, dtype=np.float32) / quarter))).astype(np.float32)

    def enc(p):
        ang = p[:, None] * omega[None, :]
        return np.concatenate([np.sin(ang), np.cos(ang)], axis=-1)

    return jnp.asarray(np.concatenate([enc(row), enc(col)], axis=-1).astype(np.float32))


def kernel(x, c, ctx, c_ctx, ada_w, ada_b, norm1_g, w_in, s5_a_re, s5_a_im, s5_log_step, s5_b_re,
           s5_b_im, s5_c_re, s5_c_im, s5_d, s5_glu_w, s5_glu_b, hy_conv_w, hy_conv_b, hy_f_w1,
           hy_f_b1, hy_f_w2, hy_f_b2, hy_f_freq, hy_f_w3, hy_decay, hy_bias, mix_g_s5, mix_g_hy,
           w_out, norm2_g, mlp_w1, mlp_w2, final_g):
    bsz, n, d = x.shape
    depth = ada_w.shape[0]
    d_s5 = s5_d.shape[-1]
    d_hy = hy_bias.shape[-1]
    nctx = ctx.shape[1]
    npair = bsz // 2
    nbk, na = FFT_NB, n // FFT_NB
    pos = _pos_table(n, d)
    tables = _dft_tables(n)

    assert depth == 1 and bsz % 2 == 0 and n % (FFT_NB * 8) == 0 and 2 * n == FFT_N
    mod_rows = 16
    c_rows = jnp.concatenate([c, c_ctx[None], jnp.zeros((mod_rows - bsz - 1, d), c.dtype)], axis=0)
    mod3 = _modulation(c_rows, ada_w[0], ada_b[0]).reshape(mod_rows, 1, N_MOD * d)

    w_in_b = w_in[0].astype(BF16)
    u_t, z = _project(x, pos, mod3, None, norm1_g[0], w_in_b, d_s5, 3 * d_hy, tt=128)
    (uc_t,) = _project(ctx, None, mod3, bsz, norm1_g[0], w_in_b, d_s5, 0, tt=128)

    s5_mats = _s5_tables(s5_a_re[0], s5_a_im[0], s5_log_step[0], s5_b_re[0], s5_b_im[0],
                         s5_c_re[0], s5_c_im[0], s5_d[0])
    ys_t = _s5_scan(u_t, uc_t, bsz, *s5_mats)

    kf = _filter_spectra(n, hy_f_w1[0], hy_f_b1[0], hy_f_w2[0], hy_f_b2[0], hy_f_freq[0],
                         hy_f_w3[0], hy_decay[0], hy_bias[0], tables)
    z5 = z.reshape(bsz, na, nbk, 3 * d_hy).transpose(0, 2, 1, 3).reshape(2, npair, nbk, na, 3 * d_hy)
    yh = _hyena(z5, hy_conv_w[0], hy_conv_b[0], kf, tables)
    yh = yh.reshape(bsz, nbk, na, d_hy).transpose(0, 2, 1, 3).reshape(bsz, n, d_hy)

    return _output_stage(x, pos, mod3, ys_t, yh, s5_glu_w[0].astype(BF16), s5_glu_b[0], mix_g_s5[0],
                         mix_g_hy[0], w_out[0].astype(BF16), norm2_g[0], mlp_w1[0].astype(BF16),
                         mlp_w2[0].astype(BF16), final_g)
```

```python
import functools
import math

import numpy as np
import jax
import jax.numpy as jnp
from jax import lax
from jax.experimental import pallas as pl
from jax.experimental.pallas import tpu as pltpu

F32 = jnp.float32
BF16 = jnp.bfloat16
HI = lax.Precision.HIGHEST

EPS = 1e-6
GRID_W = 64
POS_BASE = 10000.0
N_MOD = 6
S5_C = 16
S5_P = 64
S5_T = 16
HY_BANDS = 16
HY_ORDER = 2

FFT_N = 8192
FFT_NA = 128
FFT_NB = 64

LANES = 128
MXU_N = 256
SLAB = 16
RELAYOUT_ROWS = 256
SCAN_GROUPS = 2
INTRA_GROUPS_PER_STEP = 4
VMEM_LIMIT = 56 * 1024 * 1024


def _cparams(*sem):
    return pltpu.CompilerParams(dimension_semantics=sem, vmem_limit_bytes=VMEM_LIMIT)


def _rms(x, g):
    return x * lax.rsqrt(jnp.mean(x * x, axis=-1, keepdims=True) + EPS) * g


def _mod_kernel(c_ref, w_ref, b_ref, o_ref):
    c = c_ref[...]
    a = c * jax.nn.sigmoid(c)
    o_ref[...] = jnp.dot(a.astype(BF16), w_ref[...].astype(BF16),
                         preferred_element_type=F32) + b_ref[...]


def _modulation(c_rows, ada_w, ada_b):
    rows, d = c_rows.shape
    n = ada_w.shape[1]
    bn = 1024
    return pl.pallas_call(
        _mod_kernel,
        grid=(n // bn,),
        in_specs=[pl.BlockSpec((rows, d), lambda j: (0, 0)),
                  pl.BlockSpec((d, bn), lambda j: (0, j)),
                  pl.BlockSpec((1, bn), lambda j: (0, j))],
        out_specs=pl.BlockSpec((rows, bn), lambda j: (0, j)),
        out_shape=jax.ShapeDtypeStruct((rows, n), F32),
        compiler_params=_cparams("arbitrary"),
        name="ada_mod",
    )(c_rows, ada_w, ada_b.reshape(1, n))


def _to_slot_major(v, nbatch):
    tokens = v.shape[0] // nbatch
    by_token = _regroup(v.reshape(nbatch, tokens, v.shape[1]))
    return [jnp.concatenate([by_token[k * S5_T + t] for k in range(tokens // S5_T)], axis=0)
            for t in range(S5_T)]


def _from_slot_major(slots, nbatch):
    nchunk = slots[0].shape[0] // nbatch
    by_token = jnp.stack([slots[t][k * nbatch:(k + 1) * nbatch] for k in range(nchunk) for t in range(S5_T)],
                         axis=0)
    return _regroup(by_token).reshape(nbatch * nchunk * S5_T, -1)


def _proj_kernel(*refs, d, use_pos, d_s5):
    if use_pos:
        x_ref, pos_ref, mod_ref, g_ref, w_ref = refs[:5]
        outs = refs[5:]
        h = x_ref[...] + pos_ref[...][None]
    else:
        x_ref, mod_ref, g_ref, w_ref = refs[:4]
        outs = refs[4:]
        h = x_ref[...]
    nbatch, tokens, _ = h.shape
    m = mod_ref[...]
    hn = _rms(h, g_ref[...][None]) * (1.0 + m[:, :, d:2 * d]) + m[:, :, 0:d]
    proj = jnp.dot(hn.reshape(nbatch * tokens, d).astype(BF16), w_ref[...], preferred_element_type=F32)
    for t, rows in enumerate(_to_slot_major(proj[:, :d_s5], nbatch)):
        outs[0][t] = rows.astype(BF16)
    if len(outs) > 1:
        outs[1][...] = proj[:, d_s5:].reshape(nbatch, tokens, -1).astype(BF16)


def _project(x, pos, mod3, mod_row, g, w_bf16, d_s5, d_rest, tt):
    bsz, n, d = x.shape
    use_pos = pos is not None
    in_specs = [pl.BlockSpec((bsz, tt, d), lambda i: (0, i, 0))]
    args = [x]
    if use_pos:
        in_specs.append(pl.BlockSpec((tt, d), lambda i: (i, 0)))
        args.append(pos)
    mod_rows = bsz if mod_row is None else 1
    in_specs += [pl.BlockSpec((mod_rows, 1, mod3.shape[-1]), lambda i: (0 if mod_row is None else mod_row, 0, 0)),
                 pl.BlockSpec((1, d), lambda i: (0, 0)),
                 pl.BlockSpec((d, d_s5 + d_rest), lambda i: (0, 0))]
    args += [mod3, g.reshape(1, d), w_bf16]
    rows = tt // S5_T * bsz
    out_specs = [pl.BlockSpec((S5_T, rows, d_s5), lambda i: (0, i, 0))]
    out_shape = [jax.ShapeDtypeStruct((S5_T, n // S5_T * bsz, d_s5), BF16)]
    if d_rest:
        out_specs.append(pl.BlockSpec((bsz, tt, d_rest), lambda i: (0, i, 0)))
        out_shape.append(jax.ShapeDtypeStruct((bsz, n, d_rest), BF16))
    return pl.pallas_call(
        functools.partial(_proj_kernel, d=d, use_pos=use_pos, d_s5=d_s5),
        grid=(n // tt,),
        in_specs=in_specs,
        out_specs=out_specs,
        out_shape=out_shape,
        compiler_params=_cparams("arbitrary"),
        name="norm_proj",
    )(*args)


def _s5_tables(a_re, a_im, log_step, b_re, b_im, c_re, c_im, d_skip):
    T, C, P = S5_T, S5_C, S5_P
    G = a_re.shape[1]
    gpb = LANES // C
    slots = np.arange(T)
    tok = np.stack([8 * (slots // 8) + (slots % 8 - o) % 8 for o in range(gpb)])
    tok_g = tok[np.arange(G) % gpb]

    step = jnp.exp(log_step.astype(F32))[..., None]
    ar = a_re.astype(F32) * step
    ai = a_im.astype(F32) * step

    def powers(d, expo):
        e = jnp.asarray(np.broadcast_to(expo, (G, expo.shape[-1])).astype(np.float32))[:, :, None]
        mag = jnp.exp(ar[d][:, None, :] * e)
        return mag * jnp.cos(ai[d][:, None, :] * e), mag * jnp.sin(ai[d][:, None, :] * e)

    lr, li = a_re.astype(F32), a_im.astype(F32)
    first = [powers(d, np.ones(1)) for d in range(2)]
    nr = jnp.stack([first[d][0][:, 0] for d in range(2)]) - 1.0
    ni = jnp.stack([first[d][1][:, 0] for d in range(2)])
    den = lr * lr + li * li
    qr = (nr * lr + ni * li) / den
    qi = (ni * lr - nr * li) / den
    bbr = qr[..., None] * b_re - qi[..., None] * b_im
    bbi = qr[..., None] * b_im + qi[..., None] * b_re
    cr, ci = c_re.astype(F32), c_im.astype(F32)

    kerns = []
    for d in range(2):
        pr, pi = powers(d, np.arange(T))
        wr = cr[d][:, None] * pr[:, :, None, :] - ci[d][:, None] * pi[:, :, None, :]
        wi = cr[d][:, None] * pi[:, :, None, :] + ci[d][:, None] * pr[:, :, None, :]
        kerns.append(jnp.einsum('gtcp,gpe->gtce', wr, bbr[d], precision=HI)
                     - jnp.einsum('gtcp,gpe->gtce', wi, bbi[d], precision=HI))
    kcat = jnp.concatenate(kerns, axis=1)
    kpad = jnp.pad(kcat.reshape(G, 2 * T * C, C), ((0, 0), (0, 0), (0, LANES - C)))
    lag = np.repeat(tok[:, :, None] - tok[:, None, :], C, axis=2).astype(np.int32)
    d_lanes = jnp.tile(d_skip.astype(F32).reshape(G, 1, C), (1, 1, T))
    gps = INTRA_GROUPS_PER_STEP
    m1 = pl.pallas_call(
        _s5_intra_kernel,
        grid=(G // gps,),
        in_specs=[pl.BlockSpec((gps, 2 * T * C, LANES), lambda s: (s, 0, 0)),
                  pl.BlockSpec((gps, T, T * C), lambda s: (s % (gpb // gps), 0, 0)),
                  pl.BlockSpec((gps, 1, T * C), lambda s: (s, 0, 0))],
        out_specs=pl.BlockSpec((gps, T * C, T * C), lambda s: (s, 0, 0)),
        out_shape=jax.ShapeDtypeStruct((G, T * C, T * C), BF16),
        compiler_params=_cparams("arbitrary"),
        name="s5_intra_table",
    )(kpad, jnp.asarray(lag), d_lanes)

    def in_mat(d, expo):
        p_r, p_i = powers(d, expo)
        br, bi = bbr[d].transpose(0, 2, 1)[:, None], bbi[d].transpose(0, 2, 1)[:, None]
        er = p_r[:, :, None, :] * br - p_i[:, :, None, :] * bi
        ei = p_r[:, :, None, :] * bi + p_i[:, :, None, :] * br
        return er.reshape(G, T * C, P), ei.reshape(G, T * C, P)

    efr, efi = in_mat(0, T - 1 - tok_g)
    ebr, ebi = in_mat(1, tok_g)
    m2 = jnp.concatenate([efr, ebr, efi, ebi], axis=-1)

    def out_mat(d, expo):
        p_r, p_i = powers(d, expo)
        crt = cr[d].transpose(0, 2, 1)[:, :, None, :]
        cit = ci[d].transpose(0, 2, 1)[:, :, None, :]
        prt = p_r.transpose(0, 2, 1)[:, :, :, None]
        pit = p_i.transpose(0, 2, 1)[:, :, :, None]
        vr = crt * prt - cit * pit
        vi = crt * pit + cit * prt
        return vr.reshape(G, P, T * C), -vi.reshape(G, P, T * C)

    vfr, vfi = out_mat(0, tok_g + 1)
    vbr, vbi = out_mat(1, T - tok_g)
    m3 = jnp.concatenate([vfr, vbr, vfi, vbi], axis=1)

    last = [powers(d, np.full(1, T)) for d in range(2)]
    lam = jnp.stack([jnp.concatenate([last[0][c][:, 0], last[1][c][:, 0]], axis=-1)
                     for c in range(2)], axis=1)
    is_fwd = jnp.asarray((np.arange(4 * P) // P) % 2 == 0)[None, :, None]
    m3f = jnp.where(is_fwd, m3, 0.0).astype(BF16)
    m3b = jnp.where(is_fwd, 0.0, m3).astype(BF16)
    return m1, m2.astype(BF16), m3f, m3b, lam


def _s5_intra_kernel(k_ref, lag_ref, d_ref, o_ref):
    T, C = S5_T, S5_C
    row = lax.broadcasted_iota(jnp.int32, (C, T * C), 0)
    lane_c = lax.broadcasted_iota(jnp.int32, (C, T * C), 1) % C
    for g in range(k_ref.shape[0]):
        k = k_ref[g]
        shift = C
        while shift < LANES:
            k = k + pltpu.roll(k, shift, axis=1)
            shift *= 2
        k = jnp.concatenate([k, k], axis=1)
        block = lambda x, k=k: k[x * C:(x + 1) * C]
        centre = block(0) + block(T) + jnp.where(row == lane_c, d_ref[g], 0.0)
        block_rows = []
        for l in range(T):
            lag = lag_ref[g, l:l + 1, :]
            acc = centre
            for x in range(1, T):
                acc = jnp.where(lag == x, block(x), acc)
                acc = jnp.where(lag == -x, block(T + x), acc)
            block_rows.append(acc)
        o_ref[g] = jnp.concatenate(block_rows, axis=0).T.astype(BF16)


def _gelu_tanh(x):
    return 0.5 * x * (1.0 + jnp.tanh(math.sqrt(2.0 / math.pi) * (x + 0.044715 * (x * x * x))))


def _s5_kernel(u_ref, uc_ref, m1_ref, m2_ref, m3f_ref, m3b_ref, lam_ref, o_ref,
               z_ref, e_ref, sf_ref, sb_ref, *, nbatch, nchunk, nctx):
    groups = LANES // S5_C
    rows = nbatch * nchunk
    lane = lax.broadcasted_iota(jnp.int32, (1, LANES), 1)
    slot_bits = [(((lane // S5_C) >> j) & 1) == 1 for j in range(3)]

    def skew(xs):
        cur = [xs[(-k) % groups] for k in range(groups)]
        for j, bit in enumerate(slot_bits):
            cur = [jnp.where(bit, cur[(i - (1 << j)) % groups], cur[i]) for i in range(groups)]
        return cur

    def to_chunk_rows(read_t, nrows, row0):
        for h in range(2):
            rolled = []
            for t8 in range(groups):
                v = read_t(8 * h + t8).astype(F32)
                rolled.append(pltpu.roll(v, t8 * S5_C, axis=1) if t8 else v)
            for g, zg in enumerate(skew(rolled)):
                z_ref[g, row0:row0 + nrows, h * LANES:(h + 1) * LANES] = zg.astype(BF16)

    for r0 in range(0, rows, RELAYOUT_ROWS):
        to_chunk_rows(lambda t, r0=r0: u_ref[t, r0:r0 + RELAYOUT_ROWS], RELAYOUT_ROWS, r0)
    to_chunk_rows(lambda t: uc_ref[t], nbatch * nctx, rows)

    wide = 2 * LANES
    fwd_lane = (lax.broadcasted_iota(jnp.int32, (1, wide), 1) % LANES) < (LANES // 2)
    tile = lambda k: pl.ds(pl.multiple_of(k * nbatch, nbatch), nbatch)

    npar = e_ref.shape[0]
    for g0 in range(0, groups, npar):
        gs = range(g0, g0 + npar)
        for j, g in enumerate(gs):
            e_ref[j] = jnp.dot(z_ref[g], m2_ref[g], preferred_element_type=F32)
        lam_r = [lam_ref[g, 0:1, :] for g in gs]
        lam_i = [lam_ref[g, 1:2, :] for g in gs]

        def advance(state, kf, kb):
            out = []
            for j, (sr, si) in enumerate(state):
                e2 = jnp.where(fwd_lane, e_ref[j, tile(kf), :], e_ref[j, tile(kb), :])
                out.append((lam_r[j] * sr - lam_i[j] * si + e2[:, :LANES],
                            lam_r[j] * si + lam_i[j] * sr + e2[:, LANES:]))
            return tuple(out)

        zero = jnp.zeros((nbatch, LANES), F32)
        state = tuple((zero, zero) for _ in gs)
        for i in range(nctx):
            state = advance(state, nchunk + i, nchunk + nctx - 1 - i)

        def body(i, state):
            kf, kb = i, nchunk - 1 - i
            for j, (sr, si) in enumerate(state):
                s = jnp.concatenate([sr, si], axis=-1)
                sf_ref[j, tile(kf), :] = s
                sb_ref[j, tile(kb), :] = s
            return advance(state, kf, kb)

        lax.fori_loop(0, nchunk, body, state, unroll=2)

        for j, g in enumerate(gs):
            y = (jnp.dot(z_ref[g, 0:rows, :], m1_ref[g], preferred_element_type=F32)
                 + jnp.dot(sf_ref[j].astype(BF16), m3f_ref[g], preferred_element_type=F32)
                 + jnp.dot(sb_ref[j].astype(BF16), m3b_ref[g], preferred_element_type=F32))
            z_ref[g, 0:rows, :] = _gelu_tanh(y).astype(BF16)

    for r0 in range(0, rows, RELAYOUT_ROWS):
        for h in range(2):
            ys = [z_ref[g, r0:r0 + RELAYOUT_ROWS, h * LANES:(h + 1) * LANES].astype(F32)
                  for g in range(groups)]
            for t8, acc in enumerate(skew(ys)):
                if t8:
                    acc = pltpu.roll(acc, LANES - t8 * S5_C, axis=1)
                o_ref[8 * h + t8, r0:r0 + RELAYOUT_ROWS] = acc.astype(BF16)


def _s5_scan(u_t, uc_t, bsz, m1, m2, m3f, m3b, lam):
    T, rows, width = u_t.shape
    ctx_rows = uc_t.shape[1]
    gpb = LANES // S5_C
    kern = functools.partial(_s5_kernel, nbatch=bsz, nchunk=rows // bsz, nctx=ctx_rows // bsz)
    one = pl.Buffered(1)
    mat = pl.BlockSpec((gpb, 2 * LANES, 2 * LANES), lambda j: (j, 0, 0), pipeline_mode=one)
    return pl.pallas_call(
        kern,
        grid=(width // LANES,),
        in_specs=[pl.BlockSpec((T, rows, LANES), lambda j: (0, 0, j)),
                  pl.BlockSpec((T, ctx_rows, LANES), lambda j: (0, 0, j)),
                  mat, mat, mat, mat,
                  pl.BlockSpec((gpb, 2, LANES), lambda j: (j, 0, 0))],
        out_specs=pl.BlockSpec((T, rows, LANES), lambda j: (0, 0, j), pipeline_mode=one),
        out_shape=jax.ShapeDtypeStruct(u_t.shape, BF16),
        scratch_shapes=[pltpu.VMEM((gpb, rows + ctx_rows, 2 * LANES), BF16),
                        pltpu.VMEM((SCAN_GROUPS, rows + ctx_rows, 2 * LANES), F32),
                        pltpu.VMEM((SCAN_GROUPS, rows, 2 * LANES), F32),
                        pltpu.VMEM((SCAN_GROUPS, rows, 2 * LANES), F32)],
        compiler_params=_cparams("arbitrary"),
        name="s5_scan",
    )(u_t, uc_t, m1, m2, m3f, m3b, lam)


def _dft_tables(n_seq):
    na_sig = n_seq // FFT_NB
    ka = np.arange(FFT_NA)[:, None]
    nb = np.arange(FFT_NB)[:, None, None]

    def stage1(n_in):
        na = np.arange(n_in)[None, :]
        ang = -2.0 * np.pi * (na * ka / FFT_NA)[None] - 2.0 * np.pi * (nb * ka[None]) / FFT_N
        return np.cos(ang), np.sin(ang)

    c, s = stage1(na_sig)
    f1 = np.concatenate([np.concatenate([c, -s], axis=2), np.concatenate([s, c], axis=2)], axis=1)
    c, s = stage1(FFT_NA)
    f1_real = np.concatenate([c, s], axis=1)
    kb = np.arange(FFT_NB)[:, None]
    nbv = np.arange(FFT_NB)[None, :]
    ang = -2.0 * np.pi * kb * nbv / FFT_NB
    c, s = np.cos(ang), np.sin(ang)
    f2 = np.block([[c, -s], [s, c]])
    f2_inv = np.block([[c, s], [-s, c]]) / FFT_NB
    nap = np.arange(na_sig)[:, None]
    kav = np.arange(FFT_NA)[None, :]
    ang = 2.0 * np.pi * (nap * kav / FFT_NA)[None] + 2.0 * np.pi * (nb * kav[None]) / FFT_N
    c, s = np.cos(ang) / FFT_NA, np.sin(ang) / FFT_NA
    f3 = np.concatenate([np.concatenate([c, -s], axis=2), np.concatenate([s, c], axis=2)], axis=1)
    as_bf16 = lambda a: jnp.asarray(a.astype(np.float32)).astype(BF16)
    return as_bf16(f1), as_bf16(f1_real), as_bf16(f2), as_bf16(f2_inv), as_bf16(f3)


def _conv3_slab(z_ref, plane, nb0, w, b, nbk):
    def rows(i):
        return z_ref[plane, 0, i].astype(F32)

    mid = [rows(nb0 + i) for i in range(SLAB)]
    na = mid[0].shape[0]
    ridx = lax.broadcasted_iota(jnp.int32, mid[0].shape, 0)
    wrapped = rows((nb0 + nbk - 1) % nbk)
    before = jnp.where(nb0 == 0, jnp.where(ridx == 0, 0.0, pltpu.roll(wrapped, 1, axis=0)), wrapped)
    wrapped = rows((nb0 + SLAB) % nbk)
    after = jnp.where(nb0 + SLAB == nbk, jnp.where(ridx == na - 1, 0.0, pltpu.roll(wrapped, na - 1, axis=0)),
                      wrapped)
    ext = [before] + mid + [after]
    return [ext[i] * w[0:1] + ext[i + 1] * w[1:2] + ext[i + 2] * w[2:3] + b for i in range(SLAB)]


def _for_row_slabs(nbk, fn):
    lax.fori_loop(0, nbk // SLAB, lambda j, c: (fn(pl.multiple_of(j * SLAB, SLAB)), c)[1], 0)


def _regroup(x):
    return jnp.swapaxes(x, 0, 1)


def _hyena_kernel(zv_ref, zg1_ref, zg2_ref, wv_ref, bv_ref, wg1_ref, bg1_ref, wg2_ref, bg2_ref,
                  kf_ref, f1_ref, f2_ref, f2i_ref, f3_ref, o_ref, s_ref, z1_ref,
                  *, nbk, na, ngrp, kag):
    step = pl.program_id(2)
    hb = FFT_NB

    def stage1(src_ref, conv):
        def slab(nb0):
            if conv:
                planes = [[v.astype(BF16) for v in _conv3_slab(src_ref, pln, nb0, wv_ref[...], bv_ref[...], nbk)]
                          for pln in range(2)]
            tiles = []
            for i in range(SLAB):
                nb = nb0 + i
                parts = [planes[0][i], planes[1][i]] if conv else [src_ref[0, 0, nb], src_ref[1, 0, nb]]
                rhs = jnp.concatenate(parts, axis=0)
                tiles.append(jnp.dot(f1_ref[nb], rhs, preferred_element_type=F32).astype(BF16))
            s_ref[:, pl.ds(nb0, SLAB), :] = _regroup(jnp.stack(tiles, axis=0))
        _for_row_slabs(nbk, slab)

    def stage3(gate_ref, wg, bg, dst_ref):
        def slab(nb0):
            rows = _regroup(s_ref[:, pl.ds(nb0, SLAB), :])
            gates = [_conv3_slab(gate_ref, pln, nb0, wg, bg, nbk) for pln in range(2)]
            for i in range(SLAB):
                nb = nb0 + i
                y = jnp.dot(f3_ref[nb], rows[i], preferred_element_type=F32)
                for pln in range(2):
                    dst_ref[pln, 0, nb] = (gates[pln][i] * y[pln * na:(pln + 1) * na]).astype(dst_ref.dtype)
        _for_row_slabs(nbk, slab)

    @pl.when(step == 0)
    def _():
        stage1(zv_ref, True)

    @pl.when(step == ngrp)
    def _():
        stage1(z1_ref, False)

    base = (step % ngrp) * kag
    loaded = [jnp.concatenate([s_ref[base + k], s_ref[FFT_NA + base + k]], axis=0) for k in range(kag)]
    results = []
    for k, rhs in enumerate(loaded):
        spec = jnp.dot(f2_ref[...], rhs, preferred_element_type=F32).astype(BF16)
        kf = kf_ref[0, k]
        sr, si = spec[:hb], spec[hb:]
        kr, ki = kf[:hb], kf[hb:]
        prod = jnp.concatenate([sr * kr - si * ki, sr * ki + si * kr], axis=0)
        results.append(jnp.dot(f2i_ref[...], prod, preferred_element_type=F32).astype(BF16))
    for k, back in enumerate(results):
        s_ref[base + k] = back[:hb]
        s_ref[FFT_NA + base + k] = back[hb:]

    @pl.when(step == ngrp - 1)
    def _():
        stage3(zg1_ref, wg1_ref[...], bg1_ref[...], z1_ref)

    @pl.when(step == 2 * ngrp - 1)
    def _():
        stage3(zg2_ref, wg2_ref[...], bg2_ref[...], o_ref)


def _hyena(z5, conv_w, conv_b, kf, tables, cb=MXU_N, kag=2 * SLAB):
    f1, _, f2, f2i, f3 = tables
    _, npair, nbk, na, _ = z5.shape
    c = kf.shape[-1]
    ncb = c // cb
    ngrp = FFT_NA // kag
    one = pl.Buffered(1)
    zspec = lambda grp: pl.BlockSpec((2, 1, nbk, na, cb), lambda p, j, s: (0, p, 0, 0, grp * ncb + j))
    wspec = lambda grp: pl.BlockSpec((3, cb), lambda p, j, s: (0, grp * ncb + j))
    bspec = lambda grp: pl.BlockSpec((1, cb), lambda p, j, s: (0, grp * ncb + j))
    const = lambda a: pl.BlockSpec(a.shape, lambda p, j, s: tuple(0 for _ in a.shape), pipeline_mode=one)
    cw = conv_w.astype(F32)
    cbias = conv_b.astype(F32).reshape(1, -1)
    return pl.pallas_call(
        functools.partial(_hyena_kernel, nbk=nbk, na=na, ngrp=ngrp, kag=kag),
        grid=(npair, ncb, 2 * ngrp),
        in_specs=[zspec(0), zspec(1), zspec(2), wspec(0), bspec(0), wspec(1), bspec(1), wspec(2), bspec(2),
                  pl.BlockSpec((1, kag, 2 * FFT_NB, cb), lambda p, j, s: (s // ngrp, s % ngrp, 0, j)),
                  const(f1), const(f2), const(f2i), const(f3)],
        out_specs=pl.BlockSpec((2, 1, nbk, na, cb), lambda p, j, s: (0, p, 0, 0, j), pipeline_mode=one),
        out_shape=jax.ShapeDtypeStruct((2, npair, nbk, na, c), BF16),
        scratch_shapes=[pltpu.VMEM((2 * FFT_NA, nbk, cb), BF16),
                        pltpu.VMEM((2, 1, nbk, na, cb), BF16)],
        compiler_params=_cparams("arbitrary", "arbitrary", "arbitrary"),
        name="hyena_conv",
    )(z5, z5, z5, cw, cbias, cw, cbias, cw, cbias, kf, f1, f2, f2i, f3)


def _filter_kernel(embt_ref, tv_ref, w1t_ref, b1_ref, w2t_ref, b2_ref, fr_ref, w3_ref, dec_ref, bias_ref,
                   f1_ref, f2_ref, k_ref, s_ref, h_ref, *, nbk, kag):
    first = (pl.program_id(0) == 0) & (pl.program_id(1) == 0)
    step = pl.program_id(2)
    half = FFT_NA // 2
    lanes_per_pass = 8 * FFT_NA

    @pl.when(first & (step == 0))
    def _():
        fr = fr_ref[...]
        for i in range(embt_ref.shape[1] // lanes_per_pass):
            cols = slice(i * lanes_per_pass, (i + 1) * lanes_per_pass)
            h = jnp.sin(fr * (jnp.dot(w1t_ref[...], embt_ref[:, cols], precision=HI,
                                      preferred_element_type=F32) + b1_ref[...]))
            h_ref[:, cols] = jnp.sin(fr * (jnp.dot(w2t_ref[...], h, precision=HI,
                                                   preferred_element_type=F32) + b2_ref[...]))

    @pl.when(step == 0)
    def _():
        dec = dec_ref[0]

        def slab(nb0):
            tiles = []
            for i in range(SLAB):
                nb = nb0 + i
                h = h_ref[:, pl.ds(pl.multiple_of(nb * FFT_NA, FFT_NA), FFT_NA)].T.astype(BF16)
                fwd = jnp.dot(h[:half], w3_ref[0, 0], preferred_element_type=F32)
                bwd = jnp.dot(h[half:], w3_ref[0, 1], preferred_element_type=F32)
                tv = tv_ref[nb]
                window = jnp.exp(-tv[:, 0:1] * dec) * tv[:, 1:2]
                filt = jnp.concatenate([fwd, bwd], axis=0) * window
                tiles.append(jnp.dot(f1_ref[nb], filt.astype(BF16), preferred_element_type=F32).astype(BF16))
            s_ref[:, pl.ds(nb0, SLAB), :] = _regroup(jnp.stack(tiles, axis=0))
        _for_row_slabs(nbk, slab)

    base = step * kag
    bias = bias_ref[0]
    for k in range(kag):
        rhs = jnp.concatenate([s_ref[base + k], s_ref[FFT_NA + base + k]], axis=0)
        spec = jnp.dot(f2_ref[...], rhs, preferred_element_type=F32)
        k_ref[0, k] = jnp.concatenate([spec[:FFT_NB] + bias, spec[FFT_NB:]], axis=0).astype(BF16)


def _filter_spectra(n, w1, b1, w2, b2, freq, w3, decay, bias, tables, cb=MXU_N, kag=2 * SLAB):
    _, f1_real, f2, _, _ = tables
    hid = w2.shape[0]
    c = decay.shape[-1]
    t = np.linspace(0.0, 1.0, n, dtype=np.float32)[:, None]
    w = (2.0 * np.pi * np.arange(n, dtype=np.float32) / n).astype(np.float32)
    bands = np.linspace(1e-4, HY_BANDS - 1, HY_BANDS, dtype=np.float32)
    ang = w[:, None] * bands[None, :]
    emb = np.concatenate([t, np.cos(ang), -np.sin(ang)], axis=-1).astype(np.float32)
    kpad = 64
    idx = FFT_NB * np.arange(FFT_NA)[None, :] + np.arange(FFT_NB)[:, None]
    lagi = np.minimum(np.where(idx < n, idx, FFT_N - idx), n - 1)
    embt = np.zeros((kpad, FFT_N), np.float32)
    embt[:emb.shape[1]] = emb[lagi.reshape(-1)].T
    tv = np.stack([t[lagi, 0], (idx != n).astype(np.float32)], axis=-1)
    col = lambda a: a.reshape(hid, 1).astype(F32)
    w1t = jnp.pad(w1.astype(F32), ((0, kpad - w1.shape[0]), (0, 0))).T
    w3r = w3.astype(BF16).reshape(hid, HY_ORDER, 2, c).transpose(1, 2, 0, 3)
    dec = jnp.abs(decay.astype(F32)).reshape(HY_ORDER, 1, c)
    one = pl.Buffered(1)
    const = lambda shape: pl.BlockSpec(shape, lambda o, j, s: tuple(0 for _ in shape), pipeline_mode=one)
    per_channel = pl.BlockSpec((1, 1, cb), lambda o, j, s: (o, 0, j))
    return pl.pallas_call(
        functools.partial(_filter_kernel, nbk=FFT_NB, kag=kag),
        grid=(HY_ORDER, c // cb, FFT_NA // kag),
        in_specs=[const(embt.shape), const(tv.shape), const((hid, kpad)), const((hid, 1)),
                  const((hid, hid)), const((hid, 1)), const((hid, 1)),
                  pl.BlockSpec((1, 2, hid, cb), lambda o, j, s: (o, 0, 0, j)),
                  per_channel, per_channel,
                  const(f1_real.shape), const(f2.shape)],
        out_specs=pl.BlockSpec((1, kag, 2 * FFT_NB, cb), lambda o, j, s: (o, s, 0, j)),
        out_shape=jax.ShapeDtypeStruct((HY_ORDER, FFT_NA, 2 * FFT_NB, c), BF16),
        scratch_shapes=[pltpu.VMEM((2 * FFT_NA, FFT_NB, cb), BF16),
                        pltpu.VMEM((hid, FFT_N), F32)],
        compiler_params=_cparams("arbitrary", "arbitrary", "arbitrary"),
        name="hyena_filter",
    )(jnp.asarray(embt), jnp.asarray(tv), w1t, col(b1), w2.astype(F32).T, col(b2), col(freq), w3r, dec,
      bias.astype(F32).reshape(HY_ORDER, 1, c), f1_real, f2)


def _out_kernel(x_ref, pos_ref, mod_ref, ys_ref, yh_ref, gw_ref, gb_ref, g5_ref, gh_ref, wo_ref,
                g2_ref, w1_ref, w2_ref, gf_ref, o_ref, *, d):
    nbatch, tokens, _ = x_ref.shape
    rows = nbatch * tokens
    m = mod_ref[...]
    gate1, shift2 = m[:, :, 2 * d:3 * d], m[:, :, 3 * d:4 * d]
    scale2, gate2 = m[:, :, 4 * d:5 * d], m[:, :, 5 * d:6 * d]
    per_batch = lambda v: v.reshape(nbatch, tokens, d)
    h = x_ref[...] + pos_ref[...][None]
    ys = _from_slot_major([ys_ref[t].astype(F32) for t in range(S5_T)], nbatch)
    ab = jnp.dot(ys.astype(BF16), gw_ref[...], preferred_element_type=F32) + gb_ref[...]
    half = ab.shape[-1] // 2
    y5 = ab[:, :half] * jax.nn.sigmoid(ab[:, half:])
    yh = yh_ref[...].reshape(rows, -1).astype(F32)
    mix = jnp.concatenate([_rms(y5, g5_ref[...]), _rms(yh, gh_ref[...])], axis=-1)
    h = h + gate1 * per_batch(jnp.dot(mix.astype(BF16), wo_ref[...], preferred_element_type=F32))
    hn = _rms(h, g2_ref[...][None]) * (1.0 + scale2) + shift2
    hid = jnp.dot(hn.reshape(rows, d).astype(BF16), w1_ref[...], preferred_element_type=F32)
    hid = jnp.square(jnp.maximum(hid, 0.0))
    h = h + gate2 * per_batch(jnp.dot(hid.astype(BF16), w2_ref[...], preferred_element_type=F32))
    o_ref[...] = _rms(h, gf_ref[...][None])


def _output_stage(x, pos, mod3, ys_t, yh, glu_w, glu_b, g5, gh, w_out, g2, w1, w2, gf, tt=64):
    bsz, n, d = x.shape
    dh = ys_t.shape[-1]
    one = pl.Buffered(1)
    const = lambda a: pl.BlockSpec(a.shape, lambda i: tuple(0 for _ in a.shape), pipeline_mode=one)
    row = lambda a: a.reshape(1, -1).astype(F32)
    glu_b, g5, gh, g2, gf = row(glu_b), row(g5), row(gh), row(g2), row(gf)
    return pl.pallas_call(
        functools.partial(_out_kernel, d=d),
        grid=(n // tt,),
        in_specs=[pl.BlockSpec((bsz, tt, d), lambda i: (0, i, 0)),
                  pl.BlockSpec((tt, d), lambda i: (i, 0)),
                  pl.BlockSpec((bsz, 1, mod3.shape[-1]), lambda i: (0, 0, 0)),
                  pl.BlockSpec((S5_T, tt // S5_T * bsz, dh), lambda i: (0, i, 0)),
                  pl.BlockSpec((bsz, tt, yh.shape[-1]), lambda i: (0, i, 0)),
                  const(glu_w), const(glu_b), const(g5), const(gh), const(w_out), const(g2),
                  const(w1), const(w2), const(gf)],
        out_specs=pl.BlockSpec((bsz, tt, d), lambda i: (0, i, 0)),
        out_shape=jax.ShapeDtypeStruct((bsz, n, d), F32),
        compiler_params=_cparams("arbitrary"),
        name="mix_mlp_out",
    )(x, pos, mod3, ys_t, yh, glu_w, glu_b, g5, gh, w_out, g2, w1, w2, gf)


def _pos_table(n, d):
    rows = n // GRID_W
    row = np.repeat(np.arange(rows, dtype=np.float32), GRID_W)
    col = np.tile(np.arange(GRID_W, dtype=np.float32), rows)
    quarter = d // 4
    omega = (1.0 / (POS_BASE ** (np.arange(quarter, dtype=np.float32) / quarter))).astype(np.float32)

    def enc(p):
        ang = p[:, None] * omega[None, :]
        return np.concatenate([np.sin(ang), np.cos(ang)], axis=-1)

    return jnp.asarray(np.concatenate([enc(row), enc(col)], axis=-1).astype(np.float32))


def kernel(x, c, ctx, c_ctx, ada_w, ada_b, norm1_g, w_in, s5_a_re, s5_a_im, s5_log_step, s5_b_re,
           s5_b_im, s5_c_re, s5_c_im, s5_d, s5_glu_w, s5_glu_b, hy_conv_w, hy_conv_b, hy_f_w1,
           hy_f_b1, hy_f_w2, hy_f_b2, hy_f_freq, hy_f_w3, hy_decay, hy_bias, mix_g_s5, mix_g_hy,
           w_out, norm2_g, mlp_w1, mlp_w2, final_g):
    bsz, n, d = x.shape
    depth = ada_w.shape[0]
    d_s5 = s5_d.shape[-1]
    d_hy = hy_bias.shape[-1]
    nctx = ctx.shape[1]
    npair = bsz // 2
    nbk, na = FFT_NB, n // FFT_NB
    pos = _pos_table(n, d)
    tables = _dft_tables(n)

    assert depth == 1 and bsz % 2 == 0 and n % (FFT_NB * 8) == 0 and 2 * n == FFT_N
    mod_rows = 16
    c_rows = jnp.concatenate([c, c_ctx[None], jnp.zeros((mod_rows - bsz - 1, d), c.dtype)], axis=0)
    mod3 = _modulation(c_rows, ada_w[0], ada_b[0]).reshape(mod_rows, 1, N_MOD * d)

    w_in_b = w_in[0].astype(BF16)
    u_t, z = _project(x, pos, mod3, None, norm1_g[0], w_in_b, d_s5, 3 * d_hy, tt=128)
    (uc_t,) = _project(ctx, None, mod3, bsz, norm1_g[0], w_in_b, d_s5, 0, tt=128)

    s5_mats = _s5_tables(s5_a_re[0], s5_a_im[0], s5_log_step[0], s5_b_re[0], s5_b_im[0],
                         s5_c_re[0], s5_c_im[0], s5_d[0])
    ys_t = _s5_scan(u_t, uc_t, bsz, *s5_mats)

    kf = _filter_spectra(n, hy_f_w1[0], hy_f_b1[0], hy_f_w2[0], hy_f_b2[0], hy_f_freq[0],
                         hy_f_w3[0], hy_decay[0], hy_bias[0], tables)
    z5 = z.reshape(bsz, na, nbk, 3 * d_hy).transpose(0, 2, 1, 3).reshape(2, npair, nbk, na, 3 * d_hy)
    yh = _hyena(z5, hy_conv_w[0], hy_conv_b[0], kf, tables)
    yh = yh.reshape(bsz, nbk, na, d_hy).transpose(0, 2, 1, 3).reshape(bsz, n, d_hy)

    return _output_stage(x, pos, mod3, ys_t, yh, s5_glu_w[0].astype(BF16), s5_glu_b[0], mix_g_s5[0],
                         mix_g_hy[0], w_out[0].astype(BF16), norm2_g[0], mlp_w1[0].astype(BF16),
                         mlp_w2[0].astype(BF16), final_g)
```

```python
import functools
import math

import numpy as np
import jax
import jax.numpy as jnp
from jax import lax
from jax.experimental import pallas as pl
from jax.experimental.pallas import tpu as pltpu

F32 = jnp.float32
BF16 = jnp.bfloat16
HI = lax.Precision.HIGHEST

EPS = 1e-6
GRID_W = 64
POS_BASE = 10000.0
N_MOD = 6
S5_C = 16
S5_P = 64
S5_T = 16
HY_BANDS = 16
HY_ORDER = 2

FFT_N = 8192
FFT_NA = 128
FFT_NB = 64

LANES = 128
MXU_N = 256
SLAB = 16
RELAYOUT_ROWS = 256
SCAN_GROUPS = 2
INTRA_GROUPS_PER_STEP = 4
VMEM_LIMIT = 56 * 1024 * 1024


def _cparams(*sem):
    return pltpu.CompilerParams(dimension_semantics=sem, vmem_limit_bytes=VMEM_LIMIT)


def _rms(x, g):
    return x * lax.rsqrt(jnp.mean(x * x, axis=-1, keepdims=True) + EPS) * g


def _mod_kernel(c_ref, w_ref, b_ref, o_ref):
    c = c_ref[...]
    a = c * jax.nn.sigmoid(c)
    o_ref[...] = jnp.dot(a.astype(BF16), w_ref[...].astype(BF16),
                         preferred_element_type=F32) + b_ref[...]


def _modulation(c_rows, ada_w, ada_b):
    rows, d = c_rows.shape
    n = ada_w.shape[1]
    bn = 1024
    return pl.pallas_call(
        _mod_kernel,
        grid=(n // bn,),
        in_specs=[pl.BlockSpec((rows, d), lambda j: (0, 0)),
                  pl.BlockSpec((d, bn), lambda j: (0, j)),
                  pl.BlockSpec((1, bn), lambda j: (0, j))],
        out_specs=pl.BlockSpec((rows, bn), lambda j: (0, j)),
        out_shape=jax.ShapeDtypeStruct((rows, n), F32),
        compiler_params=_cparams("arbitrary"),
        name="ada_mod",
    )(c_rows, ada_w, ada_b.reshape(1, n))


def _to_slot_major(v, nbatch):
    nchunk = v.shape[0] // (nbatch * S5_T)
    pieces = v.reshape(nbatch * nchunk, S5_T, v.shape[1])
    by_chunk = jnp.stack([pieces[b * nchunk + k] for k in range(nchunk) for b in range(nbatch)], axis=0)
    return _regroup(by_chunk)


def _from_slot_major(slots, nbatch):
    nchunk = slots.shape[1] // nbatch
    by_chunk = _regroup(slots)
    pieces = jnp.stack([by_chunk[k * nbatch + b] for b in range(nbatch) for k in range(nchunk)], axis=0)
    return pieces.reshape(nbatch * nchunk * S5_T, slots.shape[2])


def _proj_kernel(*refs, d, use_pos, d_s5):
    if use_pos:
        x_ref, pos_ref, mod_ref, g_ref, w_ref = refs[:5]
        outs = refs[5:]
        h = x_ref[...] + pos_ref[...][None]
    else:
        x_ref, mod_ref, g_ref, w_ref = refs[:4]
        outs = refs[4:]
        h = x_ref[...]
    nbatch, tokens, _ = h.shape
    m = mod_ref[...]
    hn = _rms(h, g_ref[...][None]) * (1.0 + m[:, :, d:2 * d]) + m[:, :, 0:d]
    proj = jnp.dot(hn.reshape(nbatch * tokens, d).astype(BF16), w_ref[...], preferred_element_type=F32)
    outs[0][...] = _to_slot_major(proj[:, :d_s5].astype(BF16), nbatch)
    if len(outs) > 1:
        outs[1][...] = proj[:, d_s5:].reshape(nbatch, tokens, -1).astype(BF16)


def _project(x, pos, mod3, mod_row, g, w_bf16, d_s5, d_rest, tt):
    bsz, n, d = x.shape
    use_pos = pos is not None
    in_specs = [pl.BlockSpec((bsz, tt, d), lambda i: (0, i, 0))]
    args = [x]
    if use_pos:
        in_specs.append(pl.BlockSpec((tt, d), lambda i: (i, 0)))
        args.append(pos)
    mod_rows = bsz if mod_row is None else 1
    in_specs += [pl.BlockSpec((mod_rows, 1, mod3.shape[-1]), lambda i: (0 if mod_row is None else mod_row, 0, 0)),
                 pl.BlockSpec((1, d), lambda i: (0, 0)),
                 pl.BlockSpec((d, d_s5 + d_rest), lambda i: (0, 0))]
    args += [mod3, g.reshape(1, d), w_bf16]
    rows = tt // S5_T * bsz
    out_specs = [pl.BlockSpec((S5_T, rows, d_s5), lambda i: (0, i, 0))]
    out_shape = [jax.ShapeDtypeStruct((S5_T, n // S5_T * bsz, d_s5), BF16)]
    if d_rest:
        out_specs.append(pl.BlockSpec((bsz, tt, d_rest), lambda i: (0, i, 0)))
        out_shape.append(jax.ShapeDtypeStruct((bsz, n, d_rest), BF16))
    return pl.pallas_call(
        functools.partial(_proj_kernel, d=d, use_pos=use_pos, d_s5=d_s5),
        grid=(n // tt,),
        in_specs=in_specs,
        out_specs=out_specs,
        out_shape=out_shape,
        compiler_params=_cparams("arbitrary"),
        name="norm_proj",
    )(*args)


def _s5_tables(a_re, a_im, log_step, b_re, b_im, c_re, c_im, d_skip):
    T, C, P = S5_T, S5_C, S5_P
    G = a_re.shape[1]
    gpb = LANES // C
    slots = np.arange(T)
    tok = np.stack([8 * (slots // 8) + (slots % 8 - o) % 8 for o in range(gpb)])
    tok_g = tok[np.arange(G) % gpb]

    step = jnp.exp(log_step.astype(F32))[..., None]
    ar = a_re.astype(F32) * step
    ai = a_im.astype(F32) * step

    def powers(d, expo):
        e = jnp.asarray(np.broadcast_to(expo, (G, expo.shape[-1])).astype(np.float32))[:, :, None]
        mag = jnp.exp(ar[d][:, None, :] * e)
        return mag * jnp.cos(ai[d][:, None, :] * e), mag * jnp.sin(ai[d][:, None, :] * e)

    lr, li = a_re.astype(F32), a_im.astype(F32)
    first = [powers(d, np.ones(1)) for d in range(2)]
    nr = jnp.stack([first[d][0][:, 0] for d in range(2)]) - 1.0
    ni = jnp.stack([first[d][1][:, 0] for d in range(2)])
    den = lr * lr + li * li
    qr = (nr * lr + ni * li) / den
    qi = (ni * lr - nr * li) / den
    bbr = qr[..., None] * b_re - qi[..., None] * b_im
    bbi = qr[..., None] * b_im + qi[..., None] * b_re
    cr, ci = c_re.astype(F32), c_im.astype(F32)

    kerns = []
    for d in range(2):
        pr, pi = powers(d, np.arange(T))
        wr = cr[d][:, None] * pr[:, :, None, :] - ci[d][:, None] * pi[:, :, None, :]
        wi = cr[d][:, None] * pi[:, :, None, :] + ci[d][:, None] * pr[:, :, None, :]
        kerns.append(jnp.einsum('gtcp,gpe->gtce', wr, bbr[d], precision=HI)
                     - jnp.einsum('gtcp,gpe->gtce', wi, bbi[d], precision=HI))
    kcat = jnp.concatenate(kerns, axis=1)
    kpad = jnp.pad(kcat.reshape(G, 2 * T * C, C), ((0, 0), (0, 0), (0, LANES - C)))
    lag = np.repeat(tok[:, :, None] - tok[:, None, :], C, axis=2).astype(np.int32)
    d_lanes = jnp.tile(d_skip.astype(F32).reshape(G, 1, C), (1, 1, T))
    gps = INTRA_GROUPS_PER_STEP
    m1 = pl.pallas_call(
        _s5_intra_kernel,
        grid=(G // gps,),
        in_specs=[pl.BlockSpec((gps, 2 * T * C, LANES), lambda s: (s, 0, 0)),
                  pl.BlockSpec((gps, T, T * C), lambda s: (s % (gpb // gps), 0, 0)),
                  pl.BlockSpec((gps, 1, T * C), lambda s: (s, 0, 0))],
        out_specs=pl.BlockSpec((gps, T * C, T * C), lambda s: (s, 0, 0)),
        out_shape=jax.ShapeDtypeStruct((G, T * C, T * C), BF16),
        compiler_params=_cparams("arbitrary"),
        name="s5_intra_table",
    )(kpad, jnp.asarray(lag), d_lanes)

    def in_mat(d, expo):
        p_r, p_i = powers(d, expo)
        br, bi = bbr[d].transpose(0, 2, 1)[:, None], bbi[d].transpose(0, 2, 1)[:, None]
        er = p_r[:, :, None, :] * br - p_i[:, :, None, :] * bi
        ei = p_r[:, :, None, :] * bi + p_i[:, :, None, :] * br
        return er.reshape(G, T * C, P), ei.reshape(G, T * C, P)

    efr, efi = in_mat(0, T - 1 - tok_g)
    ebr, ebi = in_mat(1, tok_g)
    m2 = jnp.concatenate([efr, ebr, efi, ebi], axis=-1)

    def out_mat(d, expo):
        p_r, p_i = powers(d, expo)
        crt = cr[d].transpose(0, 2, 1)[:, :, None, :]
        cit = ci[d].transpose(0, 2, 1)[:, :, None, :]
        prt = p_r.transpose(0, 2, 1)[:, :, :, None]
        pit = p_i.transpose(0, 2, 1)[:, :, :, None]
        vr = crt * prt - cit * pit
        vi = crt * pit + cit * prt
        return vr.reshape(G, P, T * C), -vi.reshape(G, P, T * C)

    vfr, vfi = out_mat(0, tok_g + 1)
    vbr, vbi = out_mat(1, T - tok_g)
    m3 = jnp.concatenate([vfr, vbr, vfi, vbi], axis=1)

    last = [powers(d, np.full(1, T)) for d in range(2)]
    lam = jnp.stack([jnp.concatenate([last[0][c][:, 0], last[1][c][:, 0]], axis=-1)
                     for c in range(2)], axis=1)
    is_fwd = jnp.asarray((np.arange(4 * P) // P) % 2 == 0)[None, :, None]
    m3f = jnp.where(is_fwd, m3, 0.0).astype(BF16)
    m3b = jnp.where(is_fwd, 0.0, m3).astype(BF16)
    return m1, m2.astype(BF16), m3f, m3b, lam


def _s5_intra_kernel(k_ref, lag_ref, d_ref, o_ref):
    T, C = S5_T, S5_C
    row = lax.broadcasted_iota(jnp.int32, (C, T * C), 0)
    lane_c = lax.broadcasted_iota(jnp.int32, (C, T * C), 1) % C
    for g in range(k_ref.shape[0]):
        k = k_ref[g]
        shift = C
        while shift < LANES:
            k = k + pltpu.roll(k, shift, axis=1)
            shift *= 2
        k = jnp.concatenate([k, k], axis=1)
        block = lambda x, k=k: k[x * C:(x + 1) * C]
        centre = block(0) + block(T) + jnp.where(row == lane_c, d_ref[g], 0.0)
        block_rows = []
        for l in range(T):
            lag = lag_ref[g, l:l + 1, :]
            acc = centre
            for x in range(1, T):
                acc = jnp.where(lag == x, block(x), acc)
                acc = jnp.where(lag == -x, block(T + x), acc)
            block_rows.append(acc)
        o_ref[g] = jnp.concatenate(block_rows, axis=0).T.astype(BF16)


def _gelu_tanh(x):
    return 0.5 * x * (1.0 + jnp.tanh(math.sqrt(2.0 / math.pi) * (x + 0.044715 * (x * x * x))))


def _s5_kernel(u_ref, uc_ref, m1_ref, m2_ref, m3f_ref, m3b_ref, lam_ref, o_ref,
               z_ref, e_ref, sf_ref, sb_ref, *, nbatch, nchunk, nctx):
    groups = LANES // S5_C
    rows = nbatch * nchunk
    lane = lax.broadcasted_iota(jnp.int32, (1, LANES), 1)
    slot_bits = [(((lane // S5_C) >> j) & 1) == 1 for j in range(3)]

    def skew(xs):
        cur = [xs[(-k) % groups] for k in range(groups)]
        for j, bit in enumerate(slot_bits):
            cur = [jnp.where(bit, cur[(i - (1 << j)) % groups], cur[i]) for i in range(groups)]
        return cur

    def lane_roll(v, shift):
        if v.dtype != BF16:
            return pltpu.roll(v, shift, axis=1)
        return pltpu.bitcast(pltpu.roll(pltpu.bitcast(v, jnp.uint32), shift, axis=1), BF16)

    def to_chunk_rows(read_t, nrows, row0):
        for h in range(2):
            rolled = []
            for t8 in range(groups):
                v = read_t(8 * h + t8)
                rolled.append(lane_roll(v, t8 * S5_C) if t8 else v)
            for g, zg in enumerate(skew(rolled)):
                z_ref[g, row0:row0 + nrows, h * LANES:(h + 1) * LANES] = zg

    for r0 in range(0, rows, RELAYOUT_ROWS):
        to_chunk_rows(lambda t, r0=r0: u_ref[t, r0:r0 + RELAYOUT_ROWS], RELAYOUT_ROWS, r0)
    to_chunk_rows(lambda t: uc_ref[t], nbatch * nctx, rows)

    wide = 2 * LANES
    fwd_lane = (lax.broadcasted_iota(jnp.int32, (1, wide), 1) % LANES) < (LANES // 2)
    tile = lambda k: pl.ds(pl.multiple_of(k * nbatch, nbatch), nbatch)

    npar = e_ref.shape[0]
    for g0 in range(0, groups, npar):
        gs = range(g0, g0 + npar)
        for j, g in enumerate(gs):
            e_ref[j] = jnp.dot(z_ref[g], m2_ref[g], preferred_element_type=F32)
        lam_r = [lam_ref[g, 0:1, :] for g in gs]
        lam_i = [lam_ref[g, 1:2, :] for g in gs]

        def advance(state, kf, kb):
            out = []
            for j, (sr, si) in enumerate(state):
                e2 = jnp.where(fwd_lane, e_ref[j, tile(kf), :], e_ref[j, tile(kb), :])
                out.append((lam_r[j] * sr - lam_i[j] * si + e2[:, :LANES],
                            lam_r[j] * si + lam_i[j] * sr + e2[:, LANES:]))
            return tuple(out)

        zero = jnp.zeros((nbatch, LANES), F32)
        state = tuple((zero, zero) for _ in gs)
        for i in range(nctx):
            state = advance(state, nchunk + i, nchunk + nctx - 1 - i)

        def body(i, state):
            kf, kb = i, nchunk - 1 - i
            for j, (sr, si) in enumerate(state):
                s = jnp.concatenate([sr, si], axis=-1)
                sf_ref[j, tile(kf), :] = s
                sb_ref[j, tile(kb), :] = s
            return advance(state, kf, kb)

        lax.fori_loop(0, nchunk, body, state, unroll=2)

        for j, g in enumerate(gs):
            y = (jnp.dot(z_ref[g, 0:rows, :], m1_ref[g], preferred_element_type=F32)
                 + jnp.dot(sf_ref[j].astype(BF16), m3f_ref[g], preferred_element_type=F32)
                 + jnp.dot(sb_ref[j].astype(BF16), m3b_ref[g], preferred_element_type=F32))
            z_ref[g, 0:rows, :] = _gelu_tanh(y).astype(BF16)

    for r0 in range(0, rows, RELAYOUT_ROWS):
        for h in range(2):
            ys = [z_ref[g, r0:r0 + RELAYOUT_ROWS, h * LANES:(h + 1) * LANES] for g in range(groups)]
            for t8, acc in enumerate(skew(ys)):
                if t8:
                    acc = lane_roll(acc, LANES - t8 * S5_C)
                o_ref[8 * h + t8, r0:r0 + RELAYOUT_ROWS] = acc


def _s5_scan(u_t, uc_t, bsz, m1, m2, m3f, m3b, lam):
    T, rows, width = u_t.shape
    ctx_rows = uc_t.shape[1]
    gpb = LANES // S5_C
    kern = functools.partial(_s5_kernel, nbatch=bsz, nchunk=rows // bsz, nctx=ctx_rows // bsz)
    one = pl.Buffered(1)
    mat = pl.BlockSpec((gpb, 2 * LANES, 2 * LANES), lambda j: (j, 0, 0), pipeline_mode=one)
    return pl.pallas_call(
        kern,
        grid=(width // LANES,),
        in_specs=[pl.BlockSpec((T, rows, LANES), lambda j: (0, 0, j)),
                  pl.BlockSpec((T, ctx_rows, LANES), lambda j: (0, 0, j)),
                  mat, mat, mat, mat,
                  pl.BlockSpec((gpb, 2, LANES), lambda j: (j, 0, 0))],
        out_specs=pl.BlockSpec((T, rows, LANES), lambda j: (0, 0, j), pipeline_mode=one),
        out_shape=jax.ShapeDtypeStruct(u_t.shape, BF16),
        scratch_shapes=[pltpu.VMEM((gpb, rows + ctx_rows, 2 * LANES), BF16),
                        pltpu.VMEM((SCAN_GROUPS, rows + ctx_rows, 2 * LANES), F32),
                        pltpu.VMEM((SCAN_GROUPS, rows, 2 * LANES), F32),
                        pltpu.VMEM((SCAN_GROUPS, rows, 2 * LANES), F32)],
        compiler_params=_cparams("arbitrary"),
        name="s5_scan",
    )(u_t, uc_t, m1, m2, m3f, m3b, lam)


def _dft_tables(n_seq):
    na_sig = n_seq // FFT_NB
    ka = np.arange(FFT_NA)[:, None]
    nb = np.arange(FFT_NB)[:, None, None]

    def stage1(n_in):
        na = np.arange(n_in)[None, :]
        ang = -2.0 * np.pi * (na * ka / FFT_NA)[None] - 2.0 * np.pi * (nb * ka[None]) / FFT_N
        return np.cos(ang), np.sin(ang)

    c, s = stage1(na_sig)
    f1 = np.concatenate([np.concatenate([c, -s], axis=2), np.concatenate([s, c], axis=2)], axis=1)
    c, s = stage1(FFT_NA)
    f1_real = np.concatenate([c, s], axis=1)
    kb = np.arange(FFT_NB)[:, None]
    nbv = np.arange(FFT_NB)[None, :]
    ang = -2.0 * np.pi * kb * nbv / FFT_NB
    c, s = np.cos(ang), np.sin(ang)
    f2 = np.block([[c, -s], [s, c]])
    f2_inv = np.block([[c, s], [-s, c]]) / FFT_NB
    nap = np.arange(na_sig)[:, None]
    kav = np.arange(FFT_NA)[None, :]
    ang = 2.0 * np.pi * (nap * kav / FFT_NA)[None] + 2.0 * np.pi * (nb * kav[None]) / FFT_N
    c, s = np.cos(ang) / FFT_NA, np.sin(ang) / FFT_NA
    f3 = np.concatenate([np.concatenate([c, -s], axis=2), np.concatenate([s, c], axis=2)], axis=1)
    as_bf16 = lambda a: jnp.asarray(a.astype(np.float32)).astype(BF16)
    return as_bf16(f1), as_bf16(f1_real), as_bf16(f2), as_bf16(f2_inv), as_bf16(f3)


def _conv3_slab(z_ref, plane, nb0, w, b, nbk):
    def rows(i):
        return z_ref[plane, 0, i].astype(F32)

    mid = [rows(nb0 + i) for i in range(SLAB)]
    na = mid[0].shape[0]
    ridx = lax.broadcasted_iota(jnp.int32, mid[0].shape, 0)
    wrapped = rows((nb0 + nbk - 1) % nbk)
    before = jnp.where(nb0 == 0, jnp.where(ridx == 0, 0.0, pltpu.roll(wrapped, 1, axis=0)), wrapped)
    wrapped = rows((nb0 + SLAB) % nbk)
    after = jnp.where(nb0 + SLAB == nbk, jnp.where(ridx == na - 1, 0.0, pltpu.roll(wrapped, na - 1, axis=0)),
                      wrapped)
    ext = [before] + mid + [after]
    return [ext[i] * w[0:1] + ext[i + 1] * w[1:2] + ext[i + 2] * w[2:3] + b for i in range(SLAB)]


def _for_row_slabs(nbk, fn):
    lax.fori_loop(0, nbk // SLAB, lambda j, c: (fn(pl.multiple_of(j * SLAB, SLAB)), c)[1], 0)


def _regroup(x):
    return jnp.swapaxes(x, 0, 1)


def _hyena_kernel(zv_ref, zg1_ref, zg2_ref, wv_ref, bv_ref, wg1_ref, bg1_ref, wg2_ref, bg2_ref,
                  kf_ref, f1_ref, f2_ref, f2i_ref, f3_ref, o_ref, s_ref, z1_ref,
                  *, nbk, na, ngrp, kag):
    step = pl.program_id(2)
    hb = FFT_NB

    def stage1(src_ref, conv):
        def slab(nb0):
            if conv:
                planes = [[v.astype(BF16) for v in _conv3_slab(src_ref, pln, nb0, wv_ref[...], bv_ref[...], nbk)]
                          for pln in range(2)]
            tiles = []
            for i in range(SLAB):
                nb = nb0 + i
                parts = [planes[0][i], planes[1][i]] if conv else [src_ref[0, 0, nb], src_ref[1, 0, nb]]
                rhs = jnp.concatenate(parts, axis=0)
                tiles.append(jnp.dot(f1_ref[nb], rhs, preferred_element_type=F32).astype(BF16))
            s_ref[:, pl.ds(nb0, SLAB), :] = _regroup(jnp.stack(tiles, axis=0))
        _for_row_slabs(nbk, slab)

    def stage3(gate_ref, wg, bg, dst_ref):
        def slab(nb0):
            rows = _regroup(s_ref[:, pl.ds(nb0, SLAB), :])
            gates = [_conv3_slab(gate_ref, pln, nb0, wg, bg, nbk) for pln in range(2)]
            for i in range(SLAB):
                nb = nb0 + i
                y = jnp.dot(f3_ref[nb], rows[i], preferred_element_type=F32)
                for pln in range(2):
                    dst_ref[pln, 0, nb] = (gates[pln][i] * y[pln * na:(pln + 1) * na]).astype(dst_ref.dtype)
        _for_row_slabs(nbk, slab)

    @pl.when(step == 0)
    def _():
        stage1(zv_ref, True)

    @pl.when(step == ngrp)
    def _():
        stage1(z1_ref, False)

    base = (step % ngrp) * kag
    loaded = [jnp.concatenate([s_ref[base + k], s_ref[FFT_NA + base + k]], axis=0) for k in range(kag)]
    results = []
    for k, rhs in enumerate(loaded):
        spec = jnp.dot(f2_ref[...], rhs, preferred_element_type=F32).astype(BF16)
        kf = kf_ref[0, k]
        sr, si = spec[:hb], spec[hb:]
        kr, ki = kf[:hb], kf[hb:]
        prod = jnp.concatenate([sr * kr - si * ki, sr * ki + si * kr], axis=0)
        results.append(jnp.dot(f2i_ref[...], prod, preferred_element_type=F32).astype(BF16))
    for k, back in enumerate(results):
        s_ref[base + k] = back[:hb]
        s_ref[FFT_NA + base + k] = back[hb:]

    @pl.when(step == ngrp - 1)
    def _():
        stage3(zg1_ref, wg1_ref[...], bg1_ref[...], z1_ref)

    @pl.when(step == 2 * ngrp - 1)
    def _():
        stage3(zg2_ref, wg2_ref[...], bg2_ref[...], o_ref)


def _hyena(z5, conv_w, conv_b, kf, tables, cb=MXU_N, kag=2 * SLAB):
    f1, _, f2, f2i, f3 = tables
    _, npair, nbk, na, _ = z5.shape
    c = kf.shape[-1]
    ncb = c // cb
    ngrp = FFT_NA // kag
    one = pl.Buffered(1)
    zspec = lambda grp: pl.BlockSpec((2, 1, nbk, na, cb), lambda p, j, s: (0, p, 0, 0, grp * ncb + j))
    wspec = lambda grp: pl.BlockSpec((3, cb), lambda p, j, s: (0, grp * ncb + j))
    bspec = lambda grp: pl.BlockSpec((1, cb), lambda p, j, s: (0, grp * ncb + j))
    const = lambda a: pl.BlockSpec(a.shape, lambda p, j, s: tuple(0 for _ in a.shape), pipeline_mode=one)
    cw = conv_w.astype(F32)
    cbias = conv_b.astype(F32).reshape(1, -1)
    return pl.pallas_call(
        functools.partial(_hyena_kernel, nbk=nbk, na=na, ngrp=ngrp, kag=kag),
        grid=(npair, ncb, 2 * ngrp),
        in_specs=[zspec(0), zspec(1), zspec(2), wspec(0), bspec(0), wspec(1), bspec(1), wspec(2), bspec(2),
                  pl.BlockSpec((1, kag, 2 * FFT_NB, cb), lambda p, j, s: (s // ngrp, s % ngrp, 0, j)),
                  const(f1), const(f2), const(f2i), const(f3)],
        out_specs=pl.BlockSpec((2, 1, nbk, na, cb), lambda p, j, s: (0, p, 0, 0, j), pipeline_mode=one),
        out_shape=jax.ShapeDtypeStruct((2, npair, nbk, na, c), BF16),
        scratch_shapes=[pltpu.VMEM((2 * FFT_NA, nbk, cb), BF16),
                        pltpu.VMEM((2, 1, nbk, na, cb), BF16)],
        compiler_params=_cparams("arbitrary", "arbitrary", "arbitrary"),
        name="hyena_conv",
    )(z5, z5, z5, cw, cbias, cw, cbias, cw, cbias, kf, f1, f2, f2i, f3)


def _filter_kernel(embt_ref, tv_ref, w1t_ref, b1_ref, w2t_ref, b2_ref, fr_ref, w3_ref, dec_ref, bias_ref,
                   f1_ref, f2_ref, k_ref, s_ref, h_ref, *, nbk, kag):
    first = (pl.program_id(0) == 0) & (pl.program_id(1) == 0)
    step = pl.program_id(2)
    half = FFT_NA // 2
    lanes_per_pass = 8 * FFT_NA

    @pl.when(first & (step == 0))
    def _():
        fr = fr_ref[...]
        for i in range(embt_ref.shape[1] // lanes_per_pass):
            cols = slice(i * lanes_per_pass, (i + 1) * lanes_per_pass)
            h = jnp.sin(fr * (jnp.dot(w1t_ref[...], embt_ref[:, cols], precision=HI,
                                      preferred_element_type=F32) + b1_ref[...]))
            h_ref[:, cols] = jnp.sin(fr * (jnp.dot(w2t_ref[...], h, precision=HI,
                                                   preferred_element_type=F32) + b2_ref[...]))

    @pl.when(step == 0)
    def _():
        dec = dec_ref[0]

        def slab(nb0):
            tiles = []
            for i in range(SLAB):
                nb = nb0 + i
                h = h_ref[:, pl.ds(pl.multiple_of(nb * FFT_NA, FFT_NA), FFT_NA)].T.astype(BF16)
                fwd = jnp.dot(h[:half], w3_ref[0, 0], preferred_element_type=F32)
                bwd = jnp.dot(h[half:], w3_ref[0, 1], preferred_element_type=F32)
                tv = tv_ref[nb]
                window = jnp.exp(-tv[:, 0:1] * dec) * tv[:, 1:2]
                filt = jnp.concatenate([fwd, bwd], axis=0) * window
                tiles.append(jnp.dot(f1_ref[nb], filt.astype(BF16), preferred_element_type=F32).astype(BF16))
            s_ref[:, pl.ds(nb0, SLAB), :] = _regroup(jnp.stack(tiles, axis=0))
        _for_row_slabs(nbk, slab)

    base = step * kag
    bias = bias_ref[0]
    for k in range(kag):
        rhs = jnp.concatenate([s_ref[base + k], s_ref[FFT_NA + base + k]], axis=0)
        spec = jnp.dot(f2_ref[...], rhs, preferred_element_type=F32)
        k_ref[0, k] = jnp.concatenate([spec[:FFT_NB] + bias, spec[FFT_NB:]], axis=0).astype(BF16)


def _filter_spectra(n, w1, b1, w2, b2, freq, w3, decay, bias, tables, cb=MXU_N, kag=2 * SLAB):
    _, f1_real, f2, _, _ = tables
    hid = w2.shape[0]
    c = decay.shape[-1]
    t = np.linspace(0.0, 1.0, n, dtype=np.float32)[:, None]
    w = (2.0 * np.pi * np.arange(n, dtype=np.float32) / n).astype(np.float32)
    bands = np.linspace(1e-4, HY_BANDS - 1, HY_BANDS, dtype=np.float32)
    ang = w[:, None] * bands[None, :]
    emb = np.concatenate([t, np.cos(ang), -np.sin(ang)], axis=-1).astype(np.float32)
    kpad = 64
    idx = FFT_NB * np.arange(FFT_NA)[None, :] + np.arange(FFT_NB)[:, None]
    lagi = np.minimum(np.where(idx < n, idx, FFT_N - idx), n - 1)
    embt = np.zeros((kpad, FFT_N), np.float32)
    embt[:emb.shape[1]] = emb[lagi.reshape(-1)].T
    tv = np.stack([t[lagi, 0], (idx != n).astype(np.float32)], axis=-1)
    col = lambda a: a.reshape(hid, 1).astype(F32)
    w1t = jnp.pad(w1.astype(F32), ((0, kpad - w1.shape[0]), (0, 0))).T
    w3r = w3.astype(BF16).reshape(hid, HY_ORDER, 2, c).transpose(1, 2, 0, 3)
    dec = jnp.abs(decay.astype(F32)).reshape(HY_ORDER, 1, c)
    one = pl.Buffered(1)
    const = lambda shape: pl.BlockSpec(shape, lambda o, j, s: tuple(0 for _ in shape), pipeline_mode=one)
    per_channel = pl.BlockSpec((1, 1, cb), lambda o, j, s: (o, 0, j))
    return pl.pallas_call(
        functools.partial(_filter_kernel, nbk=FFT_NB, kag=kag),
        grid=(HY_ORDER, c // cb, FFT_NA // kag),
        in_specs=[const(embt.shape), const(tv.shape), const((hid, kpad)), const((hid, 1)),
                  const((hid, hid)), const((hid, 1)), const((hid, 1)),
                  pl.BlockSpec((1, 2, hid, cb), lambda o, j, s: (o, 0, 0, j)),
                  per_channel, per_channel,
                  const(f1_real.shape), const(f2.shape)],
        out_specs=pl.BlockSpec((1, kag, 2 * FFT_NB, cb), lambda o, j, s: (o, s, 0, j)),
        out_shape=jax.ShapeDtypeStruct((HY_ORDER, FFT_NA, 2 * FFT_NB, c), BF16),
        scratch_shapes=[pltpu.VMEM((2 * FFT_NA, FFT_NB, cb), BF16),
                        pltpu.VMEM((hid, FFT_N), F32)],
        compiler_params=_cparams("arbitrary", "arbitrary", "arbitrary"),
        name="hyena_filter",
    )(jnp.asarray(embt), jnp.asarray(tv), w1t, col(b1), w2.astype(F32).T, col(b2), col(freq), w3r, dec,
      bias.astype(F32).reshape(HY_ORDER, 1, c), f1_real, f2)


def _out_kernel(x_ref, pos_ref, mod_ref, ys_ref, yh_ref, gw_ref, gb_ref, g5_ref, gh_ref, wo_ref,
                g2_ref, w1_ref, w2_ref, gf_ref, o_ref, *, d):
    nbatch, tokens, _ = x_ref.shape
    rows = nbatch * tokens
    m = mod_ref[...]
    gate1, shift2 = m[:, :, 2 * d:3 * d], m[:, :, 3 * d:4 * d]
    scale2, gate2 = m[:, :, 4 * d:5 * d], m[:, :, 5 * d:6 * d]
    per_batch = lambda v: v.reshape(nbatch, tokens, d)
    h = x_ref[...] + pos_ref[...][None]
    ys = _from_slot_major(ys_ref[...], nbatch)
    ab = jnp.dot(ys, gw_ref[...], preferred_element_type=F32) + gb_ref[...]
    half = ab.shape[-1] // 2
    y5 = ab[:, :half] * jax.nn.sigmoid(ab[:, half:])
    yh = yh_ref[...].reshape(rows, -1).astype(F32)
    mix = jnp.concatenate([_rms(y5, g5_ref[...]), _rms(yh, gh_ref[...])], axis=-1)
    h = h + gate1 * per_batch(jnp.dot(mix.astype(BF16), wo_ref[...], preferred_element_type=F32))
    hn = _rms(h, g2_ref[...][None]) * (1.0 + scale2) + shift2
    hid = jnp.dot(hn.reshape(rows, d).astype(BF16), w1_ref[...], preferred_element_type=F32)
    hid = jnp.square(jnp.maximum(hid, 0.0))
    h = h + gate2 * per_batch(jnp.dot(hid.astype(BF16), w2_ref[...], preferred_element_type=F32))
    o_ref[...] = _rms(h, gf_ref[...][None])


def _output_stage(x, pos, mod3, ys_t, yh, glu_w, glu_b, g5, gh, w_out, g2, w1, w2, gf, tt=64):
    bsz, n, d = x.shape
    dh = ys_t.shape[-1]
    one = pl.Buffered(1)
    const = lambda a: pl.BlockSpec(a.shape, lambda i: tuple(0 for _ in a.shape), pipeline_mode=one)
    row = lambda a: a.reshape(1, -1).astype(F32)
    glu_b, g5, gh, g2, gf = row(glu_b), row(g5), row(gh), row(g2), row(gf)
    return pl.pallas_call(
        functools.partial(_out_kernel, d=d),
        grid=(n // tt,),
        in_specs=[pl.BlockSpec((bsz, tt, d), lambda i: (0, i, 0)),
                  pl.BlockSpec((tt, d), lambda i: (i, 0)),
                  pl.BlockSpec((bsz, 1, mod3.shape[-1]), lambda i: (0, 0, 0)),
                  pl.BlockSpec((S5_T, tt // S5_T * bsz, dh), lambda i: (0, i, 0)),
                  pl.BlockSpec((bsz, tt, yh.shape[-1]), lambda i: (0, i, 0)),
                  const(glu_w), const(glu_b), const(g5), const(gh), const(w_out), const(g2),
                  const(w1), const(w2), const(gf)],
        out_specs=pl.BlockSpec((bsz, tt, d), lambda i: (0, i, 0)),
        out_shape=jax.ShapeDtypeStruct((bsz, n, d), F32),
        compiler_params=_cparams("arbitrary"),
        name="mix_mlp_out",
    )(x, pos, mod3, ys_t, yh, glu_w, glu_b, g5, gh, w_out, g2, w1, w2, gf)


def _pos_table(n, d):
    rows = n // GRID_W
    row = np.repeat(np.arange(rows, dtype=np.float32), GRID_W)
    col = np.tile(np.arange(GRID_W, dtype=np.float32), rows)
    quarter = d // 4
    omega = (1.0 / (POS_BASE ** (np.arange(quarter, dtype=np.float32) / quarter))).astype(np.float32)

    def enc(p):
        ang = p[:, None] * omega[None, :]
        return np.concatenate([np.sin(ang), np.cos(ang)], axis=-1)

    return jnp.asarray(np.concatenate([enc(row), enc(col)], axis=-1).astype(np.float32))


def kernel(x, c, ctx, c_ctx, ada_w, ada_b, norm1_g, w_in, s5_a_re, s5_a_im, s5_log_step, s5_b_re,
           s5_b_im, s5_c_re, s5_c_im, s5_d, s5_glu_w, s5_glu_b, hy_conv_w, hy_conv_b, hy_f_w1,
           hy_f_b1, hy_f_w2, hy_f_b2, hy_f_freq, hy_f_w3, hy_decay, hy_bias, mix_g_s5, mix_g_hy,
           w_out, norm2_g, mlp_w1, mlp_w2, final_g):
    bsz, n, d = x.shape
    depth = ada_w.shape[0]
    d_s5 = s5_d.shape[-1]
    d_hy = hy_bias.shape[-1]
    nctx = ctx.shape[1]
    npair = bsz // 2
    nbk, na = FFT_NB, n // FFT_NB
    pos = _pos_table(n, d)
    tables = _dft_tables(n)

    assert depth == 1 and bsz % 2 == 0 and n % (FFT_NB * 8) == 0 and 2 * n == FFT_N
    mod_rows = 16
    c_rows = jnp.concatenate([c, c_ctx[None], jnp.zeros((mod_rows - bsz - 1, d), c.dtype)], axis=0)
    mod3 = _modulation(c_rows, ada_w[0], ada_b[0]).reshape(mod_rows, 1, N_MOD * d)

    w_in_b = w_in[0].astype(BF16)
    u_t, z = _project(x, pos, mod3, None, norm1_g[0], w_in_b, d_s5, 3 * d_hy, tt=128)
    (uc_t,) = _project(ctx, None, mod3, bsz, norm1_g[0], w_in_b, d_s5, 0, tt=128)

    s5_mats = _s5_tables(s5_a_re[0], s5_a_im[0], s5_log_step[0], s5_b_re[0], s5_b_im[0],
                         s5_c_re[0], s5_c_im[0], s5_d[0])
    ys_t = _s5_scan(u_t, uc_t, bsz, *s5_mats)

    kf = _filter_spectra(n, hy_f_w1[0], hy_f_b1[0], hy_f_w2[0], hy_f_b2[0], hy_f_freq[0],
                         hy_f_w3[0], hy_decay[0], hy_bias[0], tables)
    z5 = z.reshape(bsz, na, nbk, 3 * d_hy).transpose(0, 2, 1, 3).reshape(2, npair, nbk, na, 3 * d_hy)
    yh = _hyena(z5, hy_conv_w[0], hy_conv_b[0], kf, tables)
    yh = yh.reshape(bsz, nbk, na, d_hy).transpose(0, 2, 1, 3).reshape(bsz, n, d_hy)

    return _output_stage(x, pos, mod3, ys_t, yh, s5_glu_w[0].astype(BF16), s5_glu_b[0], mix_g_s5[0],
                         mix_g_hy[0], w_out[0].astype(BF16), norm2_g[0], mlp_w1[0].astype(BF16),
                         mlp_w2[0].astype(BF16), final_g)
```

```python
import functools
import math

import numpy as np
import jax
import jax.numpy as jnp
from jax import lax
from jax.experimental import pallas as pl
from jax.experimental.pallas import tpu as pltpu

F32 = jnp.float32
BF16 = jnp.bfloat16
HI = lax.Precision.HIGHEST

EPS = 1e-6
GRID_W = 64
POS_BASE = 10000.0
N_MOD = 6
S5_C = 16
S5_P = 64
S5_T = 16
HY_BANDS = 16
HY_ORDER = 2

FFT_N = 8192
FFT_NA = 128
FFT_NB = 64

LANES = 128
MXU_N = 256
SLAB = 16
RELAYOUT_ROWS = 256
SCAN_GROUPS = 2
VMEM_LIMIT = 56 * 1024 * 1024


def _cparams(*sem):
    return pltpu.CompilerParams(dimension_semantics=sem, vmem_limit_bytes=VMEM_LIMIT)


def _rms(x, g):
    return x * lax.rsqrt(jnp.mean(x * x, axis=-1, keepdims=True) + EPS) * g


def _mod_kernel(c_ref, w_ref, b_ref, o_ref):
    c = c_ref[...]
    a = c * jax.nn.sigmoid(c)
    o_ref[...] = jnp.dot(a.astype(BF16), w_ref[...].astype(BF16),
                         preferred_element_type=F32) + b_ref[...]


def _modulation(c_rows, ada_w, ada_b):
    rows, d = c_rows.shape
    n = ada_w.shape[1]
    bn = 1024
    return pl.pallas_call(
        _mod_kernel,
        grid=(n // bn,),
        in_specs=[pl.BlockSpec((rows, d), lambda j: (0, 0)),
                  pl.BlockSpec((d, bn), lambda j: (0, j)),
                  pl.BlockSpec((1, bn), lambda j: (0, j))],
        out_specs=pl.BlockSpec((rows, bn), lambda j: (0, j)),
        out_shape=jax.ShapeDtypeStruct((rows, n), F32),
        compiler_params=_cparams("arbitrary"),
        name="ada_mod",
    )(c_rows, ada_w, ada_b.reshape(1, n))


def _to_slot_major(v, nbatch):
    nchunk = v.shape[0] // (nbatch * S5_T)
    pieces = v.reshape(nbatch * nchunk, S5_T, v.shape[1])
    by_chunk = jnp.stack([pieces[b * nchunk + k] for k in range(nchunk) for b in range(nbatch)], axis=0)
    return _regroup(by_chunk)


def _from_slot_major(slots, nbatch):
    nchunk = slots.shape[1] // nbatch
    by_chunk = _regroup(slots)
    pieces = jnp.stack([by_chunk[k * nbatch + b] for b in range(nbatch) for k in range(nchunk)], axis=0)
    return pieces.reshape(nbatch * nchunk * S5_T, slots.shape[2])


def _proj_kernel(*refs, d, use_pos, d_s5):
    if use_pos:
        x_ref, pos_ref, mod_ref, g_ref, w_ref = refs[:5]
        outs = refs[5:]
        h = x_ref[...] + pos_ref[...][None]
    else:
        x_ref, mod_ref, g_ref, w_ref = refs[:4]
        outs = refs[4:]
        h = x_ref[...]
    nbatch, tokens, _ = h.shape
    m = mod_ref[...]
    hn = _rms(h, g_ref[...][None]) * (1.0 + m[:, :, d:2 * d]) + m[:, :, 0:d]
    proj = jnp.dot(hn.reshape(nbatch * tokens, d).astype(BF16), w_ref[...], preferred_element_type=F32)
    outs[0][...] = _to_slot_major(proj[:, :d_s5].astype(BF16), nbatch)
    if len(outs) > 1:
        outs[1][...] = proj[:, d_s5:].reshape(nbatch, tokens, -1).astype(BF16)


def _project(x, pos, mod3, mod_row, g, w_bf16, d_s5, d_rest, tt):
    bsz, n, d = x.shape
    use_pos = pos is not None
    in_specs = [pl.BlockSpec((bsz, tt, d), lambda i: (0, i, 0))]
    args = [x]
    if use_pos:
        in_specs.append(pl.BlockSpec((tt, d), lambda i: (i, 0)))
        args.append(pos)
    mod_rows = bsz if mod_row is None else 1
    in_specs += [pl.BlockSpec((mod_rows, 1, mod3.shape[-1]), lambda i: (0 if mod_row is None else mod_row, 0, 0)),
                 pl.BlockSpec((1, d), lambda i: (0, 0)),
                 pl.BlockSpec((d, d_s5 + d_rest), lambda i: (0, 0))]
    args += [mod3, g.reshape(1, d), w_bf16]
    rows = tt // S5_T * bsz
    out_specs = [pl.BlockSpec((S5_T, rows, d_s5), lambda i: (0, i, 0))]
    out_shape = [jax.ShapeDtypeStruct((S5_T, n // S5_T * bsz, d_s5), BF16)]
    if d_rest:
        out_specs.append(pl.BlockSpec((bsz, tt, d_rest), lambda i: (0, i, 0)))
        out_shape.append(jax.ShapeDtypeStruct((bsz, n, d_rest), BF16))
    return pl.pallas_call(
        functools.partial(_proj_kernel, d=d, use_pos=use_pos, d_s5=d_s5),
        grid=(n // tt,),
        in_specs=in_specs,
        out_specs=out_specs,
        out_shape=out_shape,
        compiler_params=_cparams("arbitrary"),
        name="norm_proj",
    )(*args)


def _s5_tables(a_re, a_im, log_step, b_re, b_im, c_re, c_im, d_skip):
    T, C, P = S5_T, S5_C, S5_P
    G = a_re.shape[1]
    gpb = LANES // C
    slots = np.arange(T)
    tok = np.stack([8 * (slots // 8) + (slots % 8 - o) % 8 for o in range(gpb)])
    tok_g = tok[np.arange(G) % gpb]

    step = jnp.exp(log_step.astype(F32))[..., None]
    ar = a_re.astype(F32) * step
    ai = a_im.astype(F32) * step

    def powers(d, expo):
        e = jnp.asarray(np.broadcast_to(expo, (G, expo.shape[-1])).astype(np.float32))[:, :, None]
        mag = jnp.exp(ar[d][:, None, :] * e)
        return mag * jnp.cos(ai[d][:, None, :] * e), mag * jnp.sin(ai[d][:, None, :] * e)

    lr, li = a_re.astype(F32), a_im.astype(F32)
    first = [powers(d, np.ones(1)) for d in range(2)]
    nr = jnp.stack([first[d][0][:, 0] for d in range(2)]) - 1.0
    ni = jnp.stack([first[d][1][:, 0] for d in range(2)])
    den = lr * lr + li * li
    qr = (nr * lr + ni * li) / den
    qi = (ni * lr - nr * li) / den
    bbr = qr[..., None] * b_re - qi[..., None] * b_im
    bbi = qr[..., None] * b_im + qi[..., None] * b_re
    cr, ci = c_re.astype(F32), c_im.astype(F32)

    kerns = []
    for d in range(2):
        pr, pi = powers(d, np.arange(T))
        wr = cr[d][:, None] * pr[:, :, None, :] - ci[d][:, None] * pi[:, :, None, :]
        wi = cr[d][:, None] * pi[:, :, None, :] + ci[d][:, None] * pr[:, :, None, :]
        kerns.append(jnp.einsum('gtcp,gpe->gtce', wr, bbr[d], precision=HI)
                     - jnp.einsum('gtcp,gpe->gtce', wi, bbi[d], precision=HI))
    kcat = jnp.concatenate(kerns, axis=1)
    kpad = jnp.pad(kcat.reshape(G, 2 * T * C, C), ((0, 0), (0, 0), (0, LANES - C)))
    d_lanes = jnp.tile(d_skip.astype(F32).reshape(G, 1, C), (1, 1, 2 * T))
    m1 = pl.pallas_call(
        _s5_intra_kernel,
        grid=(G // gpb,),
        in_specs=[pl.BlockSpec((gpb, 2 * T * C, LANES), lambda s: (s, 0, 0)),
                  pl.BlockSpec((gpb, 1, 2 * T * C), lambda s: (s, 0, 0))],
        out_specs=pl.BlockSpec((gpb, T * C, T * C), lambda s: (s, 0, 0)),
        out_shape=jax.ShapeDtypeStruct((G, T * C, T * C), BF16),
        compiler_params=_cparams("arbitrary"),
        name="s5_intra_table",
    )(kpad, d_lanes)

    def in_mat(d, expo):
        p_r, p_i = powers(d, expo)
        br, bi = bbr[d].transpose(0, 2, 1)[:, None], bbi[d].transpose(0, 2, 1)[:, None]
        er = p_r[:, :, None, :] * br - p_i[:, :, None, :] * bi
        ei = p_r[:, :, None, :] * bi + p_i[:, :, None, :] * br
        return er.reshape(G, T * C, P), ei.reshape(G, T * C, P)

    efr, efi = in_mat(0, T - 1 - tok_g)
    ebr, ebi = in_mat(1, tok_g)
    m2 = jnp.concatenate([efr, ebr, efi, ebi], axis=-1)

    def out_mat(d, expo):
        p_r, p_i = powers(d, expo)
        crt = cr[d].transpose(0, 2, 1)[:, :, None, :]
        cit = ci[d].transpose(0, 2, 1)[:, :, None, :]
        prt = p_r.transpose(0, 2, 1)[:, :, :, None]
        pit = p_i.transpose(0, 2, 1)[:, :, :, None]
        vr = crt * prt - cit * pit
        vi = crt * pit + cit * prt
        return vr.reshape(G, P, T * C), -vi.reshape(G, P, T * C)

    vfr, vfi = out_mat(0, tok_g + 1)
    vbr, vbi = out_mat(1, T - tok_g)
    m3 = jnp.concatenate([vfr, vbr, vfi, vbi], axis=1)

    last = [powers(d, np.full(1, T)) for d in range(2)]
    lam = jnp.stack([jnp.concatenate([last[0][c][:, 0], last[1][c][:, 0]], axis=-1)
                     for c in range(2)], axis=1)
    is_fwd = jnp.asarray((np.arange(4 * P) // P) % 2 == 0)[None, :, None]
    m3f = jnp.where(is_fwd, m3, 0.0).astype(BF16)
    m3b = jnp.where(is_fwd, 0.0, m3).astype(BF16)
    return m1, m2.astype(BF16), m3f, m3b, lam


def _s5_intra_kernel(k_ref, d_ref, o_ref):
    T, C = S5_T, S5_C
    wide = 2 * T * C
    row = lax.broadcasted_iota(jnp.int32, (C, wide), 0)
    lane = lax.broadcasted_iota(jnp.int32, (C, wide), 1)
    for g in range(k_ref.shape[0]):
        k = k_ref[g]
        shift = C
        while shift < LANES:
            k = k + pltpu.roll(k, shift, axis=1)
            shift *= 2
        k = jnp.concatenate([k] * (wide // LANES), axis=1)
        block = lambda x, k=k: k[x * C:(x + 1) * C]
        strip = block(0) + block(T) + jnp.where(row == lane % C, d_ref[g], 0.0)
        for j in range(2 * T - 1):
            if j != T - 1:
                x = T - 1 - j if j < T - 1 else T + (j - (T - 1))
                strip = jnp.where(lane // C == j, block(x), strip)
        block_rows = []
        for l in range(T):
            t = 8 * (l // 8) + (l % 8 - g) % 8
            start = (T - 1 - t) * C
            window = pltpu.roll(strip, (wide - start) % wide, axis=1) if start else strip
            block_rows.append(window[:, :T * C])
        m = jnp.concatenate(block_rows, axis=0)
        if g:
            m = jnp.concatenate([pltpu.roll(m[:, h * LANES:(h + 1) * LANES], g * C, axis=1) for h in range(2)],
                                axis=1)
        o_ref[g] = m.T.astype(BF16)


def _gelu_tanh(x):
    return 0.5 * x * (1.0 + jnp.tanh(math.sqrt(2.0 / math.pi) * (x + 0.044715 * (x * x * x))))


def _s5_kernel(u_ref, uc_ref, m1_ref, m2_ref, m3f_ref, m3b_ref, lam_ref, o_ref,
               z_ref, e_ref, sf_ref, sb_ref, *, nbatch, nchunk, nctx):
    groups = LANES // S5_C
    rows = nbatch * nchunk
    lane = lax.broadcasted_iota(jnp.int32, (1, LANES), 1)
    slot_bits = [(((lane // S5_C) >> j) & 1) == 1 for j in range(3)]

    def skew(xs):
        cur = [xs[(-k) % groups] for k in range(groups)]
        for j, bit in enumerate(slot_bits):
            cur = [jnp.where(bit, cur[(i - (1 << j)) % groups], cur[i]) for i in range(groups)]
        return cur

    def lane_roll(v, shift):
        if v.dtype != BF16:
            return pltpu.roll(v, shift, axis=1)
        return pltpu.bitcast(pltpu.roll(pltpu.bitcast(v, jnp.uint32), shift, axis=1), BF16)

    def to_chunk_rows(read_t, nrows, row0):
        for h in range(2):
            rolled = []
            for t8 in range(groups):
                v = read_t(8 * h + t8)
                rolled.append(lane_roll(v, t8 * S5_C) if t8 else v)
            for g, zg in enumerate(skew(rolled)):
                z_ref[g, row0:row0 + nrows, h * LANES:(h + 1) * LANES] = zg

    for r0 in range(0, rows, RELAYOUT_ROWS):
        to_chunk_rows(lambda t, r0=r0: u_ref[t, r0:r0 + RELAYOUT_ROWS], RELAYOUT_ROWS, r0)
    to_chunk_rows(lambda t: uc_ref[t], nbatch * nctx, rows)

    wide = 2 * LANES
    fwd_lane = (lax.broadcasted_iota(jnp.int32, (1, wide), 1) % LANES) < (LANES // 2)
    tile = lambda k: pl.ds(pl.multiple_of(k * nbatch, nbatch), nbatch)

    npar = e_ref.shape[0]
    for g0 in range(0, groups, npar):
        gs = range(g0, g0 + npar)
        for j, g in enumerate(gs):
            e_ref[j] = jnp.dot(z_ref[g], m2_ref[g], preferred_element_type=F32)
        lam_r = [lam_ref[g, 0:1, :] for g in gs]
        lam_i = [lam_ref[g, 1:2, :] for g in gs]

        def advance(state, kf, kb):
            out = []
            for j, (sr, si) in enumerate(state):
                e2 = jnp.where(fwd_lane, e_ref[j, tile(kf), :], e_ref[j, tile(kb), :])
                out.append((lam_r[j] * sr - lam_i[j] * si + e2[:, :LANES],
                            lam_r[j] * si + lam_i[j] * sr + e2[:, LANES:]))
            return tuple(out)

        zero = jnp.zeros((nbatch, LANES), F32)
        state = tuple((zero, zero) for _ in gs)
        for i in range(nctx):
            state = advance(state, nchunk + i, nchunk + nctx - 1 - i)

        def body(i, state):
            kf, kb = i, nchunk - 1 - i
            for j, (sr, si) in enumerate(state):
                s = jnp.concatenate([sr, si], axis=-1)
                sf_ref[j, tile(kf), :] = s
                sb_ref[j, tile(kb), :] = s
            return advance(state, kf, kb)

        lax.fori_loop(0, nchunk, body, state, unroll=2)

        for j, g in enumerate(gs):
            y = (jnp.dot(z_ref[g, 0:rows, :], m1_ref[g], preferred_element_type=F32)
                 + jnp.dot(sf_ref[j].astype(BF16), m3f_ref[g], preferred_element_type=F32)
                 + jnp.dot(sb_ref[j].astype(BF16), m3b_ref[g], preferred_element_type=F32))
            z_ref[g, 0:rows, :] = _gelu_tanh(y).astype(BF16)

    for r0 in range(0, rows, RELAYOUT_ROWS):
        for h in range(2):
            ys = [z_ref[g, r0:r0 + RELAYOUT_ROWS, h * LANES:(h + 1) * LANES] for g in range(groups)]
            for t8, acc in enumerate(skew(ys)):
                if t8:
                    acc = lane_roll(acc, LANES - t8 * S5_C)
                o_ref[8 * h + t8, r0:r0 + RELAYOUT_ROWS] = acc


def _s5_scan(u_t, uc_t, bsz, m1, m2, m3f, m3b, lam):
    T, rows, width = u_t.shape
    ctx_rows = uc_t.shape[1]
    gpb = LANES // S5_C
    kern = functools.partial(_s5_kernel, nbatch=bsz, nchunk=rows // bsz, nctx=ctx_rows // bsz)
    one = pl.Buffered(1)
    mat = pl.BlockSpec((gpb, 2 * LANES, 2 * LANES), lambda j: (j, 0, 0), pipeline_mode=one)
    return pl.pallas_call(
        kern,
        grid=(width // LANES,),
        in_specs=[pl.BlockSpec((T, rows, LANES), lambda j: (0, 0, j)),
                  pl.BlockSpec((T, ctx_rows, LANES), lambda j: (0, 0, j)),
                  mat, mat, mat, mat,
                  pl.BlockSpec((gpb, 2, LANES), lambda j: (j, 0, 0))],
        out_specs=pl.BlockSpec((T, rows, LANES), lambda j: (0, 0, j), pipeline_mode=one),
        out_shape=jax.ShapeDtypeStruct(u_t.shape, BF16),
        scratch_shapes=[pltpu.VMEM((gpb, rows + ctx_rows, 2 * LANES), BF16),
                        pltpu.VMEM((SCAN_GROUPS, rows + ctx_rows, 2 * LANES), F32),
                        pltpu.VMEM((SCAN_GROUPS, rows, 2 * LANES), F32),
                        pltpu.VMEM((SCAN_GROUPS, rows, 2 * LANES), F32)],
        compiler_params=_cparams("arbitrary"),
        name="s5_scan",
    )(u_t, uc_t, m1, m2, m3f, m3b, lam)


def _dft_tables(n_seq):
    na_sig = n_seq // FFT_NB
    ka = np.arange(FFT_NA)[:, None]
    nb = np.arange(FFT_NB)[:, None, None]

    def stage1(n_in):
        na = np.arange(n_in)[None, :]
        ang = -2.0 * np.pi * (na * ka / FFT_NA)[None] - 2.0 * np.pi * (nb * ka[None]) / FFT_N
        return np.cos(ang), np.sin(ang)

    c, s = stage1(na_sig)
    f1 = np.concatenate([np.concatenate([c, -s], axis=2), np.concatenate([s, c], axis=2)], axis=1)
    c, s = stage1(FFT_NA)
    f1_real = np.concatenate([c, s], axis=1)
    kb = np.arange(FFT_NB)[:, None]
    nbv = np.arange(FFT_NB)[None, :]
    ang = -2.0 * np.pi * kb * nbv / FFT_NB
    c, s = np.cos(ang), np.sin(ang)
    f2 = np.block([[c, -s], [s, c]])
    f2_inv = np.block([[c, s], [-s, c]]) / FFT_NB
    nap = np.arange(na_sig)[:, None]
    kav = np.arange(FFT_NA)[None, :]
    ang = 2.0 * np.pi * (nap * kav / FFT_NA)[None] + 2.0 * np.pi * (nb * kav[None]) / FFT_N
    c, s = np.cos(ang) / FFT_NA, np.sin(ang) / FFT_NA
    f3 = np.concatenate([np.concatenate([c, -s], axis=2), np.concatenate([s, c], axis=2)], axis=1)
    as_bf16 = lambda a: jnp.asarray(a.astype(np.float32)).astype(BF16)
    return as_bf16(f1), as_bf16(f1_real), as_bf16(f2), as_bf16(f2_inv), as_bf16(f3)


def _conv3_slab(z_ref, plane, nb0, w, b, nbk):
    def rows(i):
        return z_ref[plane, 0, i].astype(F32)

    mid = [rows(nb0 + i) for i in range(SLAB)]
    na = mid[0].shape[0]
    ridx = lax.broadcasted_iota(jnp.int32, mid[0].shape, 0)
    wrapped = rows((nb0 + nbk - 1) % nbk)
    before = jnp.where(nb0 == 0, jnp.where(ridx == 0, 0.0, pltpu.roll(wrapped, 1, axis=0)), wrapped)
    wrapped = rows((nb0 + SLAB) % nbk)
    after = jnp.where(nb0 + SLAB == nbk, jnp.where(ridx == na - 1, 0.0, pltpu.roll(wrapped, na - 1, axis=0)),
                      wrapped)
    ext = [before] + mid + [after]
    return [ext[i] * w[0:1] + ext[i + 1] * w[1:2] + ext[i + 2] * w[2:3] + b for i in range(SLAB)]


def _for_row_slabs(nbk, fn):
    lax.fori_loop(0, nbk // SLAB, lambda j, c: (fn(pl.multiple_of(j * SLAB, SLAB)), c)[1], 0)


def _regroup(x):
    return jnp.swapaxes(x, 0, 1)


def _hyena_kernel(zv_ref, zg1_ref, zg2_ref, wv_ref, bv_ref, wg1_ref, bg1_ref, wg2_ref, bg2_ref,
                  kf_ref, f1_ref, f2_ref, f2i_ref, f3_ref, o_ref, s_ref, z1_ref,
                  *, nbk, na, ngrp, kag):
    step = pl.program_id(2)
    hb = FFT_NB

    def stage1(src_ref, conv):
        def slab(nb0):
            if conv:
                planes = [[v.astype(BF16) for v in _conv3_slab(src_ref, pln, nb0, wv_ref[...], bv_ref[...], nbk)]
                          for pln in range(2)]
            tiles = []
            for i in range(SLAB):
                nb = nb0 + i
                parts = [planes[0][i], planes[1][i]] if conv else [src_ref[0, 0, nb], src_ref[1, 0, nb]]
                rhs = jnp.concatenate(parts, axis=0)
                tiles.append(jnp.dot(f1_ref[nb], rhs, preferred_element_type=F32).astype(BF16))
            s_ref[:, pl.ds(nb0, SLAB), :] = _regroup(jnp.stack(tiles, axis=0))
        _for_row_slabs(nbk, slab)

    def stage3(gate_ref, wg, bg, dst_ref):
        def slab(nb0):
            rows = _regroup(s_ref[:, pl.ds(nb0, SLAB), :])
            gates = [_conv3_slab(gate_ref, pln, nb0, wg, bg, nbk) for pln in range(2)]
            for i in range(SLAB):
                nb = nb0 + i
                y = jnp.dot(f3_ref[nb], rows[i], preferred_element_type=F32)
                for pln in range(2):
                    dst_ref[pln, 0, nb] = (gates[pln][i] * y[pln * na:(pln + 1) * na]).astype(dst_ref.dtype)
        _for_row_slabs(nbk, slab)

    @pl.when(step == 0)
    def _():
        stage1(zv_ref, True)

    @pl.when(step == ngrp)
    def _():
        stage1(z1_ref, False)

    base = (step % ngrp) * kag
    loaded = [jnp.concatenate([s_ref[base + k], s_ref[FFT_NA + base + k]], axis=0) for k in range(kag)]
    results = []
    for k, rhs in enumerate(loaded):
        spec = jnp.dot(f2_ref[...], rhs, preferred_element_type=F32).astype(BF16)
        kf = kf_ref[0, k]
        sr, si = spec[:hb], spec[hb:]
        kr, ki = kf[:hb], kf[hb:]
        prod = jnp.concatenate([sr * kr - si * ki, sr * ki + si * kr], axis=0)
        results.append(jnp.dot(f2i_ref[...], prod, preferred_element_type=F32).astype(BF16))
    for k, back in enumerate(results):
        s_ref[base + k] = back[:hb]
        s_ref[FFT_NA + base + k] = back[hb:]

    @pl.when(step == ngrp - 1)
    def _():
        stage3(zg1_ref, wg1_ref[...], bg1_ref[...], z1_ref)

    @pl.when(step == 2 * ngrp - 1)
    def _():
        stage3(zg2_ref, wg2_ref[...], bg2_ref[...], o_ref)


def _hyena(z5, conv_w, conv_b, kf, tables, cb=MXU_N, kag=2 * SLAB):
    f1, _, f2, f2i, f3 = tables
    _, npair, nbk, na, _ = z5.shape
    c = kf.shape[-1]
    ncb = c // cb
    ngrp = FFT_NA // kag
    one = pl.Buffered(1)
    zspec = lambda grp: pl.BlockSpec((2, 1, nbk, na, cb), lambda p, j, s: (0, p, 0, 0, grp * ncb + j))
    wspec = lambda grp: pl.BlockSpec((3, cb), lambda p, j, s: (0, grp * ncb + j))
    bspec = lambda grp: pl.BlockSpec((1, cb), lambda p, j, s: (0, grp * ncb + j))
    const = lambda a: pl.BlockSpec(a.shape, lambda p, j, s: tuple(0 for _ in a.shape), pipeline_mode=one)
    cw = conv_w.astype(F32)
    cbias = conv_b.astype(F32).reshape(1, -1)
    return pl.pallas_call(
        functools.partial(_hyena_kernel, nbk=nbk, na=na, ngrp=ngrp, kag=kag),
        grid=(npair, ncb, 2 * ngrp),
        in_specs=[zspec(0), zspec(1), zspec(2), wspec(0), bspec(0), wspec(1), bspec(1), wspec(2), bspec(2),
                  pl.BlockSpec((1, kag, 2 * FFT_NB, cb), lambda p, j, s: (s // ngrp, s % ngrp, 0, j)),
                  const(f1), const(f2), const(f2i), const(f3)],
        out_specs=pl.BlockSpec((2, 1, nbk, na, cb), lambda p, j, s: (0, p, 0, 0, j), pipeline_mode=one),
        out_shape=jax.ShapeDtypeStruct((2, npair, nbk, na, c), BF16),
        scratch_shapes=[pltpu.VMEM((2 * FFT_NA, nbk, cb), BF16),
                        pltpu.VMEM((2, 1, nbk, na, cb), BF16)],
        compiler_params=_cparams("arbitrary", "arbitrary", "arbitrary"),
        name="hyena_conv",
    )(z5, z5, z5, cw, cbias, cw, cbias, cw, cbias, kf, f1, f2, f2i, f3)


def _filter_kernel(embt_ref, tv_ref, w1t_ref, b1_ref, w2t_ref, b2_ref, fr_ref, w3_ref, dec_ref, bias_ref,
                   f1_ref, f2_ref, k_ref, s_ref, h_ref, *, nbk, kag):
    first = (pl.program_id(0) == 0) & (pl.program_id(1) == 0)
    step = pl.program_id(2)
    half = FFT_NA // 2
    lanes_per_pass = 8 * FFT_NA

    @pl.when(first & (step == 0))
    def _():
        fr = fr_ref[...]
        for i in range(embt_ref.shape[1] // lanes_per_pass):
            cols = slice(i * lanes_per_pass, (i + 1) * lanes_per_pass)
            h = jnp.sin(fr * (jnp.dot(w1t_ref[...], embt_ref[:, cols], precision=HI,
                                      preferred_element_type=F32) + b1_ref[...]))
            h_ref[:, cols] = jnp.sin(fr * (jnp.dot(w2t_ref[...], h, precision=HI,
                                                   preferred_element_type=F32) + b2_ref[...]))

    @pl.when(step == 0)
    def _():
        dec = dec_ref[0]

        def slab(nb0):
            tiles = []
            for i in range(SLAB):
                nb = nb0 + i
                h = h_ref[:, pl.ds(pl.multiple_of(nb * FFT_NA, FFT_NA), FFT_NA)].T.astype(BF16)
                fwd = jnp.dot(h[:half], w3_ref[0, 0], preferred_element_type=F32)
                bwd = jnp.dot(h[half:], w3_ref[0, 1], preferred_element_type=F32)
                tv = tv_ref[nb]
                window = jnp.exp(-tv[:, 0:1] * dec) * tv[:, 1:2]
                filt = jnp.concatenate([fwd, bwd], axis=0) * window
                tiles.append(jnp.dot(f1_ref[nb], filt.astype(BF16), preferred_element_type=F32).astype(BF16))
            s_ref[:, pl.ds(nb0, SLAB), :] = _regroup(jnp.stack(tiles, axis=0))
        _for_row_slabs(nbk, slab)

    base = step * kag
    bias = bias_ref[0]
    for k in range(kag):
        rhs = jnp.concatenate([s_ref[base + k], s_ref[FFT_NA + base + k]], axis=0)
        spec = jnp.dot(f2_ref[...], rhs, preferred_element_type=F32)
        k_ref[0, k] = jnp.concatenate([spec[:FFT_NB] + bias, spec[FFT_NB:]], axis=0).astype(BF16)


def _filter_spectra(n, w1, b1, w2, b2, freq, w3, decay, bias, tables, cb=MXU_N, kag=2 * SLAB):
    _, f1_real, f2, _, _ = tables
    hid = w2.shape[0]
    c = decay.shape[-1]
    t = np.linspace(0.0, 1.0, n, dtype=np.float32)[:, None]
    w = (2.0 * np.pi * np.arange(n, dtype=np.float32) / n).astype(np.float32)
    bands = np.linspace(1e-4, HY_BANDS - 1, HY_BANDS, dtype=np.float32)
    ang = w[:, None] * bands[None, :]
    emb = np.concatenate([t, np.cos(ang), -np.sin(ang)], axis=-1).astype(np.float32)
    kpad = 64
    idx = FFT_NB * np.arange(FFT_NA)[None, :] + np.arange(FFT_NB)[:, None]
    lagi = np.minimum(np.where(idx < n, idx, FFT_N - idx), n - 1)
    embt = np.zeros((kpad, FFT_N), np.float32)
    embt[:emb.shape[1]] = emb[lagi.reshape(-1)].T
    tv = np.stack([t[lagi, 0], (idx != n).astype(np.float32)], axis=-1)
    col = lambda a: a.reshape(hid, 1).astype(F32)
    w1t = jnp.pad(w1.astype(F32), ((0, kpad - w1.shape[0]), (0, 0))).T
    w3r = w3.astype(BF16).reshape(hid, HY_ORDER, 2, c).transpose(1, 2, 0, 3)
    dec = jnp.abs(decay.astype(F32)).reshape(HY_ORDER, 1, c)
    one = pl.Buffered(1)
    const = lambda shape: pl.BlockSpec(shape, lambda o, j, s: tuple(0 for _ in shape), pipeline_mode=one)
    per_channel = pl.BlockSpec((1, 1, cb), lambda o, j, s: (o, 0, j))
    return pl.pallas_call(
        functools.partial(_filter_kernel, nbk=FFT_NB, kag=kag),
        grid=(HY_ORDER, c // cb, FFT_NA // kag),
        in_specs=[const(embt.shape), const(tv.shape), const((hid, kpad)), const((hid, 1)),
                  const((hid, hid)), const((hid, 1)), const((hid, 1)),
                  pl.BlockSpec((1, 2, hid, cb), lambda o, j, s: (o, 0, 0, j)),
                  per_channel, per_channel,
                  const(f1_real.shape), const(f2.shape)],
        out_specs=pl.BlockSpec((1, kag, 2 * FFT_NB, cb), lambda o, j, s: (o, s, 0, j)),
        out_shape=jax.ShapeDtypeStruct((HY_ORDER, FFT_NA, 2 * FFT_NB, c), BF16),
        scratch_shapes=[pltpu.VMEM((2 * FFT_NA, FFT_NB, cb), BF16),
                        pltpu.VMEM((hid, FFT_N), F32)],
        compiler_params=_cparams("arbitrary", "arbitrary", "arbitrary"),
        name="hyena_filter",
    )(jnp.asarray(embt), jnp.asarray(tv), w1t, col(b1), w2.astype(F32).T, col(b2), col(freq), w3r, dec,
      bias.astype(F32).reshape(HY_ORDER, 1, c), f1_real, f2)


def _out_kernel(x_ref, pos_ref, mod_ref, ys_ref, yh_ref, gw_ref, gb_ref, g5_ref, gh_ref, wo_ref,
                g2_ref, w1_ref, w2_ref, gf_ref, o_ref, *, d):
    nbatch, tokens, _ = x_ref.shape
    rows = nbatch * tokens
    m = mod_ref[...]
    gate1, shift2 = m[:, :, 2 * d:3 * d], m[:, :, 3 * d:4 * d]
    scale2, gate2 = m[:, :, 4 * d:5 * d], m[:, :, 5 * d:6 * d]
    per_batch = lambda v: v.reshape(nbatch, tokens, d)
    h = x_ref[...] + pos_ref[...][None]
    ys = _from_slot_major(ys_ref[...], nbatch)
    ab = jnp.dot(ys, gw_ref[...], preferred_element_type=F32) + gb_ref[...]
    half = ab.shape[-1] // 2
    y5 = ab[:, :half] * jax.nn.sigmoid(ab[:, half:])
    yh = yh_ref[...].reshape(rows, -1).astype(F32)
    mix = jnp.concatenate([_rms(y5, g5_ref[...]), _rms(yh, gh_ref[...])], axis=-1)
    h = h + gate1 * per_batch(jnp.dot(mix.astype(BF16), wo_ref[...], preferred_element_type=F32))
    hn = _rms(h, g2_ref[...][None]) * (1.0 + scale2) + shift2
    hid = jnp.dot(hn.reshape(rows, d).astype(BF16), w1_ref[...], preferred_element_type=F32)
    hid = jnp.square(jnp.maximum(hid, 0.0))
    h = h + gate2 * per_batch(jnp.dot(hid.astype(BF16), w2_ref[...], preferred_element_type=F32))
    o_ref[...] = _rms(h, gf_ref[...][None])


def _output_stage(x, pos, mod3, ys_t, yh, glu_w, glu_b, g5, gh, w_out, g2, w1, w2, gf, tt=64):
    bsz, n, d = x.shape
    dh = ys_t.shape[-1]
    one = pl.Buffered(1)
    const = lambda a: pl.BlockSpec(a.shape, lambda i: tuple(0 for _ in a.shape), pipeline_mode=one)
    row = lambda a: a.reshape(1, -1).astype(F32)
    glu_b, g5, gh, g2, gf = row(glu_b), row(g5), row(gh), row(g2), row(gf)
    return pl.pallas_call(
        functools.partial(_out_kernel, d=d),
        grid=(n // tt,),
        in_specs=[pl.BlockSpec((bsz, tt, d), lambda i: (0, i, 0)),
                  pl.BlockSpec((tt, d), lambda i: (i, 0)),
                  pl.BlockSpec((bsz, 1, mod3.shape[-1]), lambda i: (0, 0, 0)),
                  pl.BlockSpec((S5_T, tt // S5_T * bsz, dh), lambda i: (0, i, 0)),
                  pl.BlockSpec((bsz, tt, yh.shape[-1]), lambda i: (0, i, 0)),
                  const(glu_w), const(glu_b), const(g5), const(gh), const(w_out), const(g2),
                  const(w1), const(w2), const(gf)],
        out_specs=pl.BlockSpec((bsz, tt, d), lambda i: (0, i, 0)),
        out_shape=jax.ShapeDtypeStruct((bsz, n, d), F32),
        compiler_params=_cparams("arbitrary"),
        name="mix_mlp_out",
    )(x, pos, mod3, ys_t, yh, glu_w, glu_b, g5, gh, w_out, g2, w1, w2, gf)


def _pos_table(n, d):
    rows = n // GRID_W
    row = np.repeat(np.arange(rows, dtype=np.float32), GRID_W)
    col = np.tile(np.arange(GRID_W, dtype=np.float32), rows)
    quarter = d // 4
    omega = (1.0 / (POS_BASE ** (np.arange(quarter, dtype=np.float32) / quarter))).astype(np.float32)

    def enc(p):
        ang = p[:, None] * omega[None, :]
        return np.concatenate([np.sin(ang), np.cos(ang)], axis=-1)

    return jnp.asarray(np.concatenate([enc(row), enc(col)], axis=-1).astype(np.float32))


def kernel(x, c, ctx, c_ctx, ada_w, ada_b, norm1_g, w_in, s5_a_re, s5_a_im, s5_log_step, s5_b_re,
           s5_b_im, s5_c_re, s5_c_im, s5_d, s5_glu_w, s5_glu_b, hy_conv_w, hy_conv_b, hy_f_w1,
           hy_f_b1, hy_f_w2, hy_f_b2, hy_f_freq, hy_f_w3, hy_decay, hy_bias, mix_g_s5, mix_g_hy,
           w_out, norm2_g, mlp_w1, mlp_w2, final_g):
    bsz, n, d = x.shape
    depth = ada_w.shape[0]
    d_s5 = s5_d.shape[-1]
    d_hy = hy_bias.shape[-1]
    nctx = ctx.shape[1]
    npair = bsz // 2
    nbk, na = FFT_NB, n // FFT_NB
    pos = _pos_table(n, d)
    tables = _dft_tables(n)

    assert depth == 1 and bsz % 2 == 0 and n % (FFT_NB * 8) == 0 and 2 * n == FFT_N
    mod_rows = 16
    c_rows = jnp.concatenate([c, c_ctx[None], jnp.zeros((mod_rows - bsz - 1, d), c.dtype)], axis=0)
    mod3 = _modulation(c_rows, ada_w[0], ada_b[0]).reshape(mod_rows, 1, N_MOD * d)

    w_in_b = w_in[0].astype(BF16)
    u_t, z = _project(x, pos, mod3, None, norm1_g[0], w_in_b, d_s5, 3 * d_hy, tt=128)
    (uc_t,) = _project(ctx, None, mod3, bsz, norm1_g[0], w_in_b, d_s5, 0, tt=128)

    s5_mats = _s5_tables(s5_a_re[0], s5_a_im[0], s5_log_step[0], s5_b_re[0], s5_b_im[0],
                         s5_c_re[0], s5_c_im[0], s5_d[0])
    ys_t = _s5_scan(u_t, uc_t, bsz, *s5_mats)

    kf = _filter_spectra(n, hy_f_w1[0], hy_f_b1[0], hy_f_w2[0], hy_f_b2[0], hy_f_freq[0],
                         hy_f_w3[0], hy_decay[0], hy_bias[0], tables)
    z5 = z.reshape(bsz, na, nbk, 3 * d_hy).transpose(0, 2, 1, 3).reshape(2, npair, nbk, na, 3 * d_hy)
    yh = _hyena(z5, hy_conv_w[0], hy_conv_b[0], kf, tables)
    yh = yh.reshape(bsz, nbk, na, d_hy).transpose(0, 2, 1, 3).reshape(bsz, n, d_hy)

    return _output_stage(x, pos, mod3, ys_t, yh, s5_glu_w[0].astype(BF16), s5_glu_b[0], mix_g_s5[0],
                         mix_g_hy[0], w_out[0].astype(BF16), norm2_g[0], mlp_w1[0].astype(BF16),
                         mlp_w2[0].astype(BF16), final_g)
```

```python
import functools
import math

import numpy as np
import jax
import jax.numpy as jnp
from jax import lax
from jax.experimental import pallas as pl
from jax.experimental.pallas import tpu as pltpu

F32 = jnp.float32
BF16 = jnp.bfloat16
HI = lax.Precision.HIGHEST

EPS = 1e-6
GRID_W = 64
POS_BASE = 10000.0
N_MOD = 6
S5_C = 16
S5_P = 64
S5_T = 16
HY_BANDS = 16
HY_ORDER = 2

FFT_N = 8192
FFT_NA = 128
FFT_NB = 64

LANES = 128
MXU_N = 256
SLAB = 16
RELAYOUT_ROWS = 256
SCAN_GROUPS = 2
VMEM_LIMIT = 56 * 1024 * 1024


def _cparams(*sem):
    return pltpu.CompilerParams(dimension_semantics=sem, vmem_limit_bytes=VMEM_LIMIT)


def _rms(x, g):
    return x * lax.rsqrt(jnp.mean(x * x, axis=-1, keepdims=True) + EPS) * g


def _mod_kernel(c_ref, w_ref, b_ref, o_ref):
    c = c_ref[...]
    a = c * jax.nn.sigmoid(c)
    o_ref[...] = jnp.dot(a.astype(BF16), w_ref[...].astype(BF16),
                         preferred_element_type=F32) + b_ref[...]


def _modulation(c_rows, ada_w, ada_b):
    rows, d = c_rows.shape
    n = ada_w.shape[1]
    bn = 1024
    return pl.pallas_call(
        _mod_kernel,
        grid=(n // bn,),
        in_specs=[pl.BlockSpec((rows, d), lambda j: (0, 0)),
                  pl.BlockSpec((d, bn), lambda j: (0, j)),
                  pl.BlockSpec((1, bn), lambda j: (0, j))],
        out_specs=pl.BlockSpec((rows, bn), lambda j: (0, j)),
        out_shape=jax.ShapeDtypeStruct((rows, n), F32),
        compiler_params=_cparams("arbitrary"),
        name="ada_mod",
    )(c_rows, ada_w, ada_b.reshape(1, n))


def _to_slot_major(v, nbatch):
    nchunk = v.shape[0] // (nbatch * S5_T)
    pieces = v.reshape(nbatch * nchunk, S5_T, v.shape[1])
    by_chunk = jnp.stack([pieces[b * nchunk + k] for k in range(nchunk) for b in range(nbatch)], axis=0)
    return _regroup(by_chunk)


def _from_slot_major(slots, nbatch):
    nchunk = slots.shape[1] // nbatch
    by_chunk = _regroup(slots)
    pieces = jnp.stack([by_chunk[k * nbatch + b] for b in range(nbatch) for k in range(nchunk)], axis=0)
    return pieces.reshape(nbatch * nchunk * S5_T, slots.shape[2])


def _proj_kernel(*refs, d, use_pos, d_s5):
    if use_pos:
        x_ref, pos_ref, mod_ref, g_ref, w_ref = refs[:5]
        outs = refs[5:]
        h = x_ref[...] + pos_ref[...][None]
    else:
        x_ref, mod_ref, g_ref, w_ref = refs[:4]
        outs = refs[4:]
        h = x_ref[...]
    nbatch, tokens, _ = h.shape
    m = mod_ref[...]
    hn = _rms(h, g_ref[...][None]) * (1.0 + m[:, :, d:2 * d]) + m[:, :, 0:d]
    proj = jnp.dot(hn.reshape(nbatch * tokens, d).astype(BF16), w_ref[...], preferred_element_type=F32)
    outs[0][...] = _to_slot_major(proj[:, :d_s5].astype(BF16), nbatch)
    if len(outs) > 1:
        outs[1][...] = proj[:, d_s5:].reshape(nbatch, tokens, -1).astype(BF16)


def _project(x, pos, mod3, mod_row, g, w_bf16, d_s5, d_rest, tt):
    bsz, n, d = x.shape
    use_pos = pos is not None
    in_specs = [pl.BlockSpec((bsz, tt, d), lambda i: (0, i, 0))]
    args = [x]
    if use_pos:
        in_specs.append(pl.BlockSpec((tt, d), lambda i: (i, 0)))
        args.append(pos)
    mod_rows = bsz if mod_row is None else 1
    in_specs += [pl.BlockSpec((mod_rows, 1, mod3.shape[-1]), lambda i: (0 if mod_row is None else mod_row, 0, 0)),
                 pl.BlockSpec((1, d), lambda i: (0, 0)),
                 pl.BlockSpec((d, d_s5 + d_rest), lambda i: (0, 0))]
    args += [mod3, g.reshape(1, d), w_bf16]
    rows = tt // S5_T * bsz
    out_specs = [pl.BlockSpec((S5_T, rows, d_s5), lambda i: (0, i, 0))]
    out_shape = [jax.ShapeDtypeStruct((S5_T, n // S5_T * bsz, d_s5), BF16)]
    if d_rest:
        out_specs.append(pl.BlockSpec((bsz, tt, d_rest), lambda i: (0, i, 0)))
        out_shape.append(jax.ShapeDtypeStruct((bsz, n, d_rest), BF16))
    return pl.pallas_call(
        functools.partial(_proj_kernel, d=d, use_pos=use_pos, d_s5=d_s5),
        grid=(n // tt,),
        in_specs=in_specs,
        out_specs=out_specs,
        out_shape=out_shape,
        compiler_params=_cparams("arbitrary"),
        name="norm_proj",
    )(*args)


def _s5_tables(a_re, a_im, log_step, b_re, b_im, c_re, c_im, d_skip):
    T, C, P = S5_T, S5_C, S5_P
    G = a_re.shape[1]
    gpb = LANES // C
    slots = np.arange(T)
    tok = np.stack([8 * (slots // 8) + (slots % 8 - o) % 8 for o in range(gpb)])
    tok_g = tok[np.arange(G) % gpb]

    step = jnp.exp(log_step.astype(F32))[..., None]
    ar = a_re.astype(F32) * step
    ai = a_im.astype(F32) * step

    def powers(d, expo):
        e = jnp.asarray(np.broadcast_to(expo, (G, expo.shape[-1])).astype(np.float32))[:, :, None]
        mag = jnp.exp(ar[d][:, None, :] * e)
        return mag * jnp.cos(ai[d][:, None, :] * e), mag * jnp.sin(ai[d][:, None, :] * e)

    lr, li = a_re.astype(F32), a_im.astype(F32)
    first = [powers(d, np.ones(1)) for d in range(2)]
    nr = jnp.stack([first[d][0][:, 0] for d in range(2)]) - 1.0
    ni = jnp.stack([first[d][1][:, 0] for d in range(2)])
    den = lr * lr + li * li
    qr = (nr * lr + ni * li) / den
    qi = (ni * lr - nr * li) / den
    bbr = qr[..., None] * b_re - qi[..., None] * b_im
    bbi = qr[..., None] * b_im + qi[..., None] * b_re
    cr, ci = c_re.astype(F32), c_im.astype(F32)

    kerns = []
    for d in range(2):
        pr, pi = powers(d, np.arange(T))
        wr = cr[d][:, None] * pr[:, :, None, :] - ci[d][:, None] * pi[:, :, None, :]
        wi = cr[d][:, None] * pi[:, :, None, :] + ci[d][:, None] * pr[:, :, None, :]
        kerns.append(jnp.einsum('gtcp,gpe->gtce', wr, bbr[d], precision=HI)
                     - jnp.einsum('gtcp,gpe->gtce', wi, bbi[d], precision=HI))
    kcat = jnp.concatenate(kerns, axis=1)
    kpad = jnp.pad(kcat.reshape(G, 2 * T * C, C), ((0, 0), (0, 0), (0, LANES - C)))
    d_lanes = jnp.tile(d_skip.astype(F32).reshape(G, 1, C), (1, 1, 2 * T))
    m1 = pl.pallas_call(
        _s5_intra_kernel,
        grid=(G // gpb,),
        in_specs=[pl.BlockSpec((gpb, 2 * T * C, LANES), lambda s: (s, 0, 0)),
                  pl.BlockSpec((gpb, 1, 2 * T * C), lambda s: (s, 0, 0))],
        out_specs=pl.BlockSpec((gpb, T * C, T * C), lambda s: (s, 0, 0)),
        out_shape=jax.ShapeDtypeStruct((G, T * C, T * C), BF16),
        compiler_params=_cparams("arbitrary"),
        name="s5_intra_table",
    )(kpad, d_lanes)

    def in_mat(d, expo):
        p_r, p_i = powers(d, expo)
        br, bi = bbr[d].transpose(0, 2, 1)[:, None], bbi[d].transpose(0, 2, 1)[:, None]
        er = p_r[:, :, None, :] * br - p_i[:, :, None, :] * bi
        ei = p_r[:, :, None, :] * bi + p_i[:, :, None, :] * br
        return er.reshape(G, T * C, P), ei.reshape(G, T * C, P)

    efr, efi = in_mat(0, T - 1 - tok_g)
    ebr, ebi = in_mat(1, tok_g)
    m2 = jnp.concatenate([efr, ebr, efi, ebi], axis=-1)

    def out_mat(d, expo):
        p_r, p_i = powers(d, expo)
        crt = cr[d].transpose(0, 2, 1)[:, :, None, :]
        cit = ci[d].transpose(0, 2, 1)[:, :, None, :]
        prt = p_r.transpose(0, 2, 1)[:, :, :, None]
        pit = p_i.transpose(0, 2, 1)[:, :, :, None]
        vr = crt * prt - cit * pit
        vi = crt * pit + cit * prt
        return vr.reshape(G, P, T * C), -vi.reshape(G, P, T * C)

    vfr, vfi = out_mat(0, tok_g + 1)
    vbr, vbi = out_mat(1, T - tok_g)
    m3 = jnp.concatenate([vfr, vbr, vfi, vbi], axis=1)

    last = [powers(d, np.full(1, T)) for d in range(2)]
    lam = jnp.stack([jnp.concatenate([last[0][c][:, 0], last[1][c][:, 0]], axis=-1)
                     for c in range(2)], axis=1)
    is_fwd = jnp.asarray((np.arange(4 * P) // P) % 2 == 0)[None, :, None]
    m3f = jnp.where(is_fwd, m3, 0.0).astype(BF16)
    m3b = jnp.where(is_fwd, 0.0, m3).astype(BF16)
    return m1, m2.astype(BF16), m3f, m3b, lam


def _s5_intra_kernel(k_ref, d_ref, o_ref):
    T, C = S5_T, S5_C
    wide = 2 * T * C
    row = lax.broadcasted_iota(jnp.int32, (C, wide), 0)
    lane = lax.broadcasted_iota(jnp.int32, (C, wide), 1)
    for g in range(k_ref.shape[0]):
        k = k_ref[g]
        shift = C
        while shift < LANES:
            k = k + pltpu.roll(k, shift, axis=1)
            shift *= 2
        k = jnp.concatenate([k] * (wide // LANES), axis=1)
        block = lambda x, k=k: k[x * C:(x + 1) * C]
        strip = block(0) + block(T) + jnp.where(row == lane % C, d_ref[g], 0.0)
        for j in range(2 * T - 1):
            if j != T - 1:
                x = T - 1 - j if j < T - 1 else T + (j - (T - 1))
                strip = jnp.where(lane // C == j, block(x), strip)
        block_rows = []
        for l in range(T):
            t = 8 * (l // 8) + (l % 8 - g) % 8
            start = (T - 1 - t) * C
            window = pltpu.roll(strip, (wide - start) % wide, axis=1) if start else strip
            block_rows.append(window[:, :T * C])
        m = jnp.concatenate(block_rows, axis=0)
        if g:
            m = jnp.concatenate([pltpu.roll(m[:, h * LANES:(h + 1) * LANES], g * C, axis=1) for h in range(2)],
                                axis=1)
        o_ref[g] = m.T.astype(BF16)


def _gelu_tanh(x):
    return 0.5 * x * (1.0 + jnp.tanh(math.sqrt(2.0 / math.pi) * (x + 0.044715 * (x * x * x))))


def _s5_kernel(u_ref, uc_ref, m1_ref, m2_ref, m3f_ref, m3b_ref, lam_ref, o_ref,
               z_ref, e_ref, sf_ref, sb_ref, *, nbatch, nchunk, nctx):
    groups = LANES // S5_C
    rows = nbatch * nchunk
    lane = lax.broadcasted_iota(jnp.int32, (1, LANES), 1)
    slot_bits = [(((lane // S5_C) >> j) & 1) == 1 for j in range(3)]

    def skew(xs):
        cur = [xs[(-k) % groups] for k in range(groups)]
        for j, bit in enumerate(slot_bits):
            cur = [jnp.where(bit, cur[(i - (1 << j)) % groups], cur[i]) for i in range(groups)]
        return cur

    def lane_roll(v, shift):
        if v.dtype != BF16:
            return pltpu.roll(v, shift, axis=1)
        return pltpu.bitcast(pltpu.roll(pltpu.bitcast(v, jnp.uint32), shift, axis=1), BF16)

    def to_chunk_rows(read_t, nrows, row0):
        for h in range(2):
            rolled = []
            for t8 in range(groups):
                v = read_t(8 * h + t8)
                rolled.append(lane_roll(v, t8 * S5_C) if t8 else v)
            for g, zg in enumerate(skew(rolled)):
                z_ref[g, row0:row0 + nrows, h * LANES:(h + 1) * LANES] = zg

    for r0 in range(0, rows, RELAYOUT_ROWS):
        to_chunk_rows(lambda t, r0=r0: u_ref[t, r0:r0 + RELAYOUT_ROWS], RELAYOUT_ROWS, r0)
    to_chunk_rows(lambda t: uc_ref[t], nbatch * nctx, rows)

    wide = 2 * LANES
    fwd_lane = (lax.broadcasted_iota(jnp.int32, (1, wide), 1) % LANES) < (LANES // 2)
    tile = lambda k: pl.ds(pl.multiple_of(k * nbatch, nbatch), nbatch)

    npar = e_ref.shape[0]
    for g0 in range(0, groups, npar):
        gs = range(g0, g0 + npar)
        for j, g in enumerate(gs):
            e_ref[j] = jnp.dot(z_ref[g], m2_ref[g], preferred_element_type=F32)
        lam_r = [lam_ref[g, 0:1, :] for g in gs]
        lam_i = [lam_ref[g, 1:2, :] for g in gs]

        def advance(state, kf, kb):
            out = []
            for j, (sr, si) in enumerate(state):
                e2 = jnp.where(fwd_lane, e_ref[j, tile(kf), :], e_ref[j, tile(kb), :])
                out.append((lam_r[j] * sr - lam_i[j] * si + e2[:, :LANES],
                            lam_r[j] * si + lam_i[j] * sr + e2[:, LANES:]))
            return tuple(out)

        zero = jnp.zeros((nbatch, LANES), F32)
        state = tuple((zero, zero) for _ in gs)
        for i in range(nctx):
            state = advance(state, nchunk + i, nchunk + nctx - 1 - i)

        def body(i, state):
            kf, kb = i, nchunk - 1 - i
            for j, (sr, si) in enumerate(state):
                s = jnp.concatenate([sr, si], axis=-1)
                sf_ref[j, tile(kf), :] = s
                sb_ref[j, tile(kb), :] = s
            return advance(state, kf, kb)

        lax.fori_loop(0, nchunk, body, state, unroll=2)

        for j, g in enumerate(gs):
            y = (jnp.dot(z_ref[g, 0:rows, :], m1_ref[g], preferred_element_type=F32)
                 + jnp.dot(sf_ref[j].astype(BF16), m3f_ref[g], preferred_element_type=F32)
                 + jnp.dot(sb_ref[j].astype(BF16), m3b_ref[g], preferred_element_type=F32))
            z_ref[g, 0:rows, :] = _gelu_tanh(y).astype(BF16)

    for r0 in range(0, rows, RELAYOUT_ROWS):
        for h in range(2):
            ys = [z_ref[g, r0:r0 + RELAYOUT_ROWS, h * LANES:(h + 1) * LANES] for g in range(groups)]
            for t8, acc in enumerate(skew(ys)):
                if t8:
                    acc = lane_roll(acc, LANES - t8 * S5_C)
                o_ref[8 * h + t8, r0:r0 + RELAYOUT_ROWS] = acc


def _s5_scan(u_t, uc_t, bsz, m1, m2, m3f, m3b, lam):
    T, rows, width = u_t.shape
    ctx_rows = uc_t.shape[1]
    gpb = LANES // S5_C
    kern = functools.partial(_s5_kernel, nbatch=bsz, nchunk=rows // bsz, nctx=ctx_rows // bsz)
    one = pl.Buffered(1)
    mat = pl.BlockSpec((gpb, 2 * LANES, 2 * LANES), lambda j: (j, 0, 0), pipeline_mode=one)
    return pl.pallas_call(
        kern,
        grid=(width // LANES,),
        in_specs=[pl.BlockSpec((T, rows, LANES), lambda j: (0, 0, j)),
                  pl.BlockSpec((T, ctx_rows, LANES), lambda j: (0, 0, j)),
                  mat, mat, mat, mat,
                  pl.BlockSpec((gpb, 2, LANES), lambda j: (j, 0, 0))],
        out_specs=pl.BlockSpec((T, rows, LANES), lambda j: (0, 0, j), pipeline_mode=one),
        out_shape=jax.ShapeDtypeStruct(u_t.shape, BF16),
        scratch_shapes=[pltpu.VMEM((gpb, rows + ctx_rows, 2 * LANES), BF16),
                        pltpu.VMEM((SCAN_GROUPS, rows + ctx_rows, 2 * LANES), F32),
                        pltpu.VMEM((SCAN_GROUPS, rows, 2 * LANES), F32),
                        pltpu.VMEM((SCAN_GROUPS, rows, 2 * LANES), F32)],
        compiler_params=_cparams("arbitrary"),
        name="s5_scan",
    )(u_t, uc_t, m1, m2, m3f, m3b, lam)


def _dft_tables(n_seq):
    na_sig = n_seq // FFT_NB
    ka = np.arange(FFT_NA)[:, None]
    nb = np.arange(FFT_NB)[:, None, None]

    def stage1(n_in):
        na = np.arange(n_in)[None, :]
        ang = -2.0 * np.pi * (na * ka / FFT_NA)[None] - 2.0 * np.pi * (nb * ka[None]) / FFT_N
        return np.cos(ang), np.sin(ang)

    c, s = stage1(na_sig)
    f1 = np.concatenate([np.concatenate([c, -s], axis=2), np.concatenate([s, c], axis=2)], axis=1)
    c, s = stage1(FFT_NA)
    f1_real = np.concatenate([c, s], axis=1)
    kb = np.arange(FFT_NB)[:, None]
    nbv = np.arange(FFT_NB)[None, :]
    ang = -2.0 * np.pi * kb * nbv / FFT_NB
    c, s = np.cos(ang), np.sin(ang)
    f2 = np.block([[c, -s], [s, c]])
    f2_inv = np.block([[c, s], [-s, c]]) / FFT_NB
    nap = np.arange(na_sig)[:, None]
    kav = np.arange(FFT_NA)[None, :]
    ang = 2.0 * np.pi * (nap * kav / FFT_NA)[None] + 2.0 * np.pi * (nb * kav[None]) / FFT_N
    c, s = np.cos(ang) / FFT_NA, np.sin(ang) / FFT_NA
    f3 = np.concatenate([np.concatenate([c, -s], axis=2), np.concatenate([s, c], axis=2)], axis=1)
    as_bf16 = lambda a: jnp.asarray(a.astype(np.float32)).astype(BF16)
    return as_bf16(f1), as_bf16(f1_real), as_bf16(f2), as_bf16(f2_inv), as_bf16(f3)


def _conv3_slab(z_ref, plane, nb0, w, b, nbk):
    def rows(i):
        return z_ref[plane, 0, i].astype(F32)

    mid = [rows(nb0 + i) for i in range(SLAB)]
    na = mid[0].shape[0]
    ridx = lax.broadcasted_iota(jnp.int32, mid[0].shape, 0)
    wrapped = rows((nb0 + nbk - 1) % nbk)
    before = jnp.where(nb0 == 0, jnp.where(ridx == 0, 0.0, pltpu.roll(wrapped, 1, axis=0)), wrapped)
    wrapped = rows((nb0 + SLAB) % nbk)
    after = jnp.where(nb0 + SLAB == nbk, jnp.where(ridx == na - 1, 0.0, pltpu.roll(wrapped, na - 1, axis=0)),
                      wrapped)
    ext = [before] + mid + [after]
    return [ext[i] * w[0:1] + ext[i + 1] * w[1:2] + ext[i + 2] * w[2:3] + b for i in range(SLAB)]


def _for_row_slabs(nbk, fn):
    lax.fori_loop(0, nbk // SLAB, lambda j, c: (fn(pl.multiple_of(j * SLAB, SLAB)), c)[1], 0)


def _regroup(x):
    return jnp.swapaxes(x, 0, 1)


def _hyena_kernel(zv_ref, zg1_ref, zg2_ref, wv_ref, bv_ref, wg1_ref, bg1_ref, wg2_ref, bg2_ref,
                  kf_ref, f1_ref, f2_ref, f2i_ref, f3_ref, o_ref, s_ref, z1_ref,
                  *, nbk, na, ngrp, kag):
    step = pl.program_id(2)
    hb = FFT_NB

    def stage1(src_ref, conv):
        def slab(nb0):
            if conv:
                planes = [[v.astype(BF16) for v in _conv3_slab(src_ref, pln, nb0, wv_ref[...], bv_ref[...], nbk)]
                          for pln in range(2)]
            tiles = []
            for i in range(SLAB):
                nb = nb0 + i
                parts = [planes[0][i], planes[1][i]] if conv else [src_ref[0, 0, nb], src_ref[1, 0, nb]]
                rhs = jnp.concatenate(parts, axis=0)
                tiles.append(jnp.dot(f1_ref[nb], rhs, preferred_element_type=F32).astype(BF16))
            s_ref[:, pl.ds(nb0, SLAB), :] = _regroup(jnp.stack(tiles, axis=0))
        _for_row_slabs(nbk, slab)

    def stage3(gate_ref, wg, bg, dst_ref):
        def slab(nb0):
            rows = _regroup(s_ref[:, pl.ds(nb0, SLAB), :])
            gates = [_conv3_slab(gate_ref, pln, nb0, wg, bg, nbk) for pln in range(2)]
            for i in range(SLAB):
                nb = nb0 + i
                y = jnp.dot(f3_ref[nb], rows[i], preferred_element_type=F32)
                for pln in range(2):
                    dst_ref[pln, 0, nb] = (gates[pln][i] * y[pln * na:(pln + 1) * na]).astype(dst_ref.dtype)
        _for_row_slabs(nbk, slab)

    @pl.when(step == 0)
    def _():
        stage1(zv_ref, True)

    @pl.when(step == ngrp)
    def _():
        stage1(z1_ref, False)

    base = (step % ngrp) * kag
    loaded = [jnp.concatenate([s_ref[base + k], s_ref[FFT_NA + base + k]], axis=0) for k in range(kag)]
    results = []
    for k, rhs in enumerate(loaded):
        spec = jnp.dot(f2_ref[...], rhs, preferred_element_type=F32).astype(BF16)
        kf = kf_ref[0, k]
        sr, si = spec[:hb], spec[hb:]
        kr, ki = kf[:hb], kf[hb:]
        prod = jnp.concatenate([sr * kr - si * ki, sr * ki + si * kr], axis=0)
        results.append(jnp.dot(f2i_ref[...], prod, preferred_element_type=F32).astype(BF16))
    for k, back in enumerate(results):
        s_ref[base + k] = back[:hb]
        s_ref[FFT_NA + base + k] = back[hb:]

    @pl.when(step == ngrp - 1)
    def _():
        stage3(zg1_ref, wg1_ref[...], bg1_ref[...], z1_ref)

    @pl.when(step == 2 * ngrp - 1)
    def _():
        stage3(zg2_ref, wg2_ref[...], bg2_ref[...], o_ref)


def _hyena(z5, conv_w, conv_b, kf, tables, cb=MXU_N, kag=2 * SLAB):
    f1, _, f2, f2i, f3 = tables
    _, npair, nbk, na, _ = z5.shape
    c = kf.shape[-1]
    ncb = c // cb
    ngrp = FFT_NA // kag
    one = pl.Buffered(1)
    zspec = lambda grp: pl.BlockSpec((2, 1, nbk, na, cb), lambda p, j, s: (0, p, 0, 0, grp * ncb + j))
    wspec = lambda grp: pl.BlockSpec((3, cb), lambda p, j, s: (0, grp * ncb + j))
    bspec = lambda grp: pl.BlockSpec((1, cb), lambda p, j, s: (0, grp * ncb + j))
    const = lambda a: pl.BlockSpec(a.shape, lambda p, j, s: tuple(0 for _ in a.shape), pipeline_mode=one)
    cw = conv_w.astype(F32)
    cbias = conv_b.astype(F32).reshape(1, -1)
    return pl.pallas_call(
        functools.partial(_hyena_kernel, nbk=nbk, na=na, ngrp=ngrp, kag=kag),
        grid=(npair, ncb, 2 * ngrp),
        in_specs=[zspec(0), zspec(1), zspec(2), wspec(0), bspec(0), wspec(1), bspec(1), wspec(2), bspec(2),
                  pl.BlockSpec((1, kag, 2 * FFT_NB, cb), lambda p, j, s: (s // ngrp, s % ngrp, 0, j)),
                  const(f1), const(f2), const(f2i), const(f3)],
        out_specs=pl.BlockSpec((2, 1, nbk, na, cb), lambda p, j, s: (0, p, 0, 0, j), pipeline_mode=one),
        out_shape=jax.ShapeDtypeStruct((2, npair, nbk, na, c), BF16),
        scratch_shapes=[pltpu.VMEM((2 * FFT_NA, nbk, cb), BF16),
                        pltpu.VMEM((2, 1, nbk, na, cb), BF16)],
        compiler_params=_cparams("arbitrary", "arbitrary", "arbitrary"),
        name="hyena_conv",
    )(z5, z5, z5, cw, cbias, cw, cbias, cw, cbias, kf, f1, f2, f2i, f3)


def _filter_kernel(embt_ref, tv_ref, w1t_ref, b1_ref, w2t_ref, b2_ref, fr_ref, w3_ref, dec_ref, bias_ref,
                   f1_ref, f2_ref, k_ref, s_ref, h_ref, *, nbk, kag):
    first = (pl.program_id(0) == 0) & (pl.program_id(1) == 0)
    step = pl.program_id(2)
    half = FFT_NA // 2
    lanes_per_pass = 8 * FFT_NA

    @pl.when(first & (step == 0))
    def _():
        fr = fr_ref[...]
        for i in range(embt_ref.shape[1] // lanes_per_pass):
            cols = slice(i * lanes_per_pass, (i + 1) * lanes_per_pass)
            h = jnp.sin(fr * (jnp.dot(w1t_ref[...], embt_ref[:, cols], precision=HI,
                                      preferred_element_type=F32) + b1_ref[...]))
            h = jnp.sin(fr * (jnp.dot(w2t_ref[...], h, precision=HI, preferred_element_type=F32) + b2_ref[...]))
            h_ref[i * lanes_per_pass:(i + 1) * lanes_per_pass, :] = h.T.astype(BF16)

    @pl.when(step == 0)
    def _():
        dec = dec_ref[0]

        def slab(nb0):
            rows = h_ref[pl.ds(pl.multiple_of(nb0 * FFT_NA, SLAB * FFT_NA), SLAB * FFT_NA), :]
            rows = rows.reshape(SLAB, FFT_NA, rows.shape[-1])
            fwd = jnp.dot(rows[:, :half].reshape(SLAB * half, -1), w3_ref[0, 0], preferred_element_type=F32)
            bwd = jnp.dot(rows[:, half:].reshape(SLAB * half, -1), w3_ref[0, 1], preferred_element_type=F32)
            tiles = []
            for i in range(SLAB):
                nb = nb0 + i
                tv = tv_ref[nb]
                window = jnp.exp(-tv[:, 0:1] * dec) * tv[:, 1:2]
                taps = slice(i * half, (i + 1) * half)
                filt = jnp.concatenate([fwd[taps], bwd[taps]], axis=0) * window
                tiles.append(jnp.dot(f1_ref[nb], filt.astype(BF16), preferred_element_type=F32).astype(BF16))
            s_ref[:, pl.ds(nb0, SLAB), :] = _regroup(jnp.stack(tiles, axis=0))
        _for_row_slabs(nbk, slab)

    base = step * kag
    bias = bias_ref[0]
    for k in range(kag):
        rhs = jnp.concatenate([s_ref[base + k], s_ref[FFT_NA + base + k]], axis=0)
        spec = jnp.dot(f2_ref[...], rhs, preferred_element_type=F32)
        k_ref[0, k] = jnp.concatenate([spec[:FFT_NB] + bias, spec[FFT_NB:]], axis=0).astype(BF16)


def _filter_spectra(n, w1, b1, w2, b2, freq, w3, decay, bias, tables, cb=MXU_N, kag=2 * SLAB):
    _, f1_real, f2, _, _ = tables
    hid = w2.shape[0]
    c = decay.shape[-1]
    t = np.linspace(0.0, 1.0, n, dtype=np.float32)[:, None]
    w = (2.0 * np.pi * np.arange(n, dtype=np.float32) / n).astype(np.float32)
    bands = np.linspace(1e-4, HY_BANDS - 1, HY_BANDS, dtype=np.float32)
    ang = w[:, None] * bands[None, :]
    emb = np.concatenate([t, np.cos(ang), -np.sin(ang)], axis=-1).astype(np.float32)
    kpad = 64
    idx = FFT_NB * np.arange(FFT_NA)[None, :] + np.arange(FFT_NB)[:, None]
    lagi = np.minimum(np.where(idx < n, idx, FFT_N - idx), n - 1)
    embt = np.zeros((kpad, FFT_N), np.float32)
    embt[:emb.shape[1]] = emb[lagi.reshape(-1)].T
    tv = np.stack([t[lagi, 0], (idx != n).astype(np.float32)], axis=-1)
    col = lambda a: a.reshape(hid, 1).astype(F32)
    w1t = jnp.pad(w1.astype(F32), ((0, kpad - w1.shape[0]), (0, 0))).T
    w3r = w3.astype(BF16).reshape(hid, HY_ORDER, 2, c).transpose(1, 2, 0, 3)
    dec = jnp.abs(decay.astype(F32)).reshape(HY_ORDER, 1, c)
    one = pl.Buffered(1)
    const = lambda shape: pl.BlockSpec(shape, lambda o, j, s: tuple(0 for _ in shape), pipeline_mode=one)
    per_channel = pl.BlockSpec((1, 1, cb), lambda o, j, s: (o, 0, j))
    return pl.pallas_call(
        functools.partial(_filter_kernel, nbk=FFT_NB, kag=kag),
        grid=(HY_ORDER, c // cb, FFT_NA // kag),
        in_specs=[const(embt.shape), const(tv.shape), const((hid, kpad)), const((hid, 1)),
                  const((hid, hid)), const((hid, 1)), const((hid, 1)),
                  pl.BlockSpec((1, 2, hid, cb), lambda o, j, s: (o, 0, 0, j)),
                  per_channel, per_channel,
                  const(f1_real.shape), const(f2.shape)],
        out_specs=pl.BlockSpec((1, kag, 2 * FFT_NB, cb), lambda o, j, s: (o, s, 0, j)),
        out_shape=jax.ShapeDtypeStruct((HY_ORDER, FFT_NA, 2 * FFT_NB, c), BF16),
        scratch_shapes=[pltpu.VMEM((2 * FFT_NA, FFT_NB, cb), BF16),
                        pltpu.VMEM((FFT_N, hid), BF16)],
        compiler_params=_cparams("arbitrary", "arbitrary", "arbitrary"),
        name="hyena_filter",
    )(jnp.asarray(embt), jnp.asarray(tv), w1t, col(b1), w2.astype(F32).T, col(b2), col(freq), w3r, dec,
      bias.astype(F32).reshape(HY_ORDER, 1, c), f1_real, f2)


def _out_kernel(x_ref, pos_ref, mod_ref, ys_ref, yh_ref, gw_ref, gb_ref, g5_ref, gh_ref, wo_ref,
                g2_ref, w1_ref, w2_ref, gf_ref, o_ref, *, d):
    nbatch, tokens, _ = x_ref.shape
    rows = nbatch * tokens
    m = mod_ref[...]
    gate1, shift2 = m[:, :, 2 * d:3 * d], m[:, :, 3 * d:4 * d]
    scale2, gate2 = m[:, :, 4 * d:5 * d], m[:, :, 5 * d:6 * d]
    per_batch = lambda v: v.reshape(nbatch, tokens, d)
    h = x_ref[...] + pos_ref[...][None]
    ys = _from_slot_major(ys_ref[...], nbatch)
    ab = jnp.dot(ys, gw_ref[...], preferred_element_type=F32) + gb_ref[...]
    half = ab.shape[-1] // 2
    y5 = ab[:, :half] * jax.nn.sigmoid(ab[:, half:])
    yh = yh_ref[...].reshape(rows, -1).astype(F32)
    mix = jnp.concatenate([_rms(y5, g5_ref[...]), _rms(yh, gh_ref[...])], axis=-1)
    h = h + gate1 * per_batch(jnp.dot(mix.astype(BF16), wo_ref[...], preferred_element_type=F32))
    hn = _rms(h, g2_ref[...][None]) * (1.0 + scale2) + shift2
    hid = jnp.dot(hn.reshape(rows, d).astype(BF16), w1_ref[...], preferred_element_type=F32)
    hid = jnp.square(jnp.maximum(hid, 0.0))
    h = h + gate2 * per_batch(jnp.dot(hid.astype(BF16), w2_ref[...], preferred_element_type=F32))
    o_ref[...] = _rms(h, gf_ref[...][None])


def _output_stage(x, pos, mod3, ys_t, yh, glu_w, glu_b, g5, gh, w_out, g2, w1, w2, gf, tt=64):
    bsz, n, d = x.shape
    dh = ys_t.shape[-1]
    one = pl.Buffered(1)
    const = lambda a: pl.BlockSpec(a.shape, lambda i: tuple(0 for _ in a.shape), pipeline_mode=one)
    row = lambda a: a.reshape(1, -1).astype(F32)
    glu_b, g5, gh, g2, gf = row(glu_b), row(g5), row(gh), row(g2), row(gf)
    return pl.pallas_call(
        functools.partial(_out_kernel, d=d),
        grid=(n // tt,),
        in_specs=[pl.BlockSpec((bsz, tt, d), lambda i: (0, i, 0)),
                  pl.BlockSpec((tt, d), lambda i: (i, 0)),
                  pl.BlockSpec((bsz, 1, mod3.shape[-1]), lambda i: (0, 0, 0)),
                  pl.BlockSpec((S5_T, tt // S5_T * bsz, dh), lambda i: (0, i, 0)),
                  pl.BlockSpec((bsz, tt, yh.shape[-1]), lambda i: (0, i, 0)),
                  const(glu_w), const(glu_b), const(g5), const(gh), const(w_out), const(g2),
                  const(w1), const(w2), const(gf)],
        out_specs=pl.BlockSpec((bsz, tt, d), lambda i: (0, i, 0)),
        out_shape=jax.ShapeDtypeStruct((bsz, n, d), F32),
        compiler_params=_cparams("arbitrary"),
        name="mix_mlp_out",
    )(x, pos, mod3, ys_t, yh, glu_w, glu_b, g5, gh, w_out, g2, w1, w2, gf)


def _pos_table(n, d):
    rows = n // GRID_W
    row = np.repeat(np.arange(rows, dtype=np.float32), GRID_W)
    col = np.tile(np.arange(GRID_W, dtype=np.float32), rows)
    quarter = d // 4
    omega = (1.0 / (POS_BASE ** (np.arange(quarter, dtype=np.float32) / quarter))).astype(np.float32)

    def enc(p):
        ang = p[:, None] * omega[None, :]
        return np.concatenate([np.sin(ang), np.cos(ang)], axis=-1)

    return jnp.asarray(np.concatenate([enc(row), enc(col)], axis=-1).astype(np.float32))


def kernel(x, c, ctx, c_ctx, ada_w, ada_b, norm1_g, w_in, s5_a_re, s5_a_im, s5_log_step, s5_b_re,
           s5_b_im, s5_c_re, s5_c_im, s5_d, s5_glu_w, s5_glu_b, hy_conv_w, hy_conv_b, hy_f_w1,
           hy_f_b1, hy_f_w2, hy_f_b2, hy_f_freq, hy_f_w3, hy_decay, hy_bias, mix_g_s5, mix_g_hy,
           w_out, norm2_g, mlp_w1, mlp_w2, final_g):
    bsz, n, d = x.shape
    depth = ada_w.shape[0]
    d_s5 = s5_d.shape[-1]
    d_hy = hy_bias.shape[-1]
    nctx = ctx.shape[1]
    npair = bsz // 2
    nbk, na = FFT_NB, n // FFT_NB
    pos = _pos_table(n, d)
    tables = _dft_tables(n)

    assert depth == 1 and bsz % 2 == 0 and n % (FFT_NB * 8) == 0 and 2 * n == FFT_N
    mod_rows = 16
    c_rows = jnp.concatenate([c, c_ctx[None], jnp.zeros((mod_rows - bsz - 1, d), c.dtype)], axis=0)
    mod3 = _modulation(c_rows, ada_w[0], ada_b[0]).reshape(mod_rows, 1, N_MOD * d)

    w_in_b = w_in[0].astype(BF16)
    u_t, z = _project(x, pos, mod3, None, norm1_g[0], w_in_b, d_s5, 3 * d_hy, tt=128)
    (uc_t,) = _project(ctx, None, mod3, bsz, norm1_g[0], w_in_b, d_s5, 0, tt=128)

    s5_mats = _s5_tables(s5_a_re[0], s5_a_im[0], s5_log_step[0], s5_b_re[0], s5_b_im[0],
                         s5_c_re[0], s5_c_im[0], s5_d[0])
    ys_t = _s5_scan(u_t, uc_t, bsz, *s5_mats)

    kf = _filter_spectra(n, hy_f_w1[0], hy_f_b1[0], hy_f_w2[0], hy_f_b2[0], hy_f_freq[0],
                         hy_f_w3[0], hy_decay[0], hy_bias[0], tables)
    z5 = z.reshape(bsz, na, nbk, 3 * d_hy).transpose(0, 2, 1, 3).reshape(2, npair, nbk, na, 3 * d_hy)
    yh = _hyena(z5, hy_conv_w[0], hy_conv_b[0], kf, tables)
    yh = yh.reshape(bsz, nbk, na, d_hy).transpose(0, 2, 1, 3).reshape(bsz, n, d_hy)

    return _output_stage(x, pos, mod3, ys_t, yh, s5_glu_w[0].astype(BF16), s5_glu_b[0], mix_g_s5[0],
                         mix_g_hy[0], w_out[0].astype(BF16), norm2_g[0], mlp_w1[0].astype(BF16),
                         mlp_w2[0].astype(BF16), final_g)
```

```python
import functools
import math

import numpy as np
import jax
import jax.numpy as jnp
from jax import lax
from jax.experimental import pallas as pl
from jax.experimental.pallas import tpu as pltpu

F32 = jnp.float32
BF16 = jnp.bfloat16
HI = lax.Precision.HIGHEST

EPS = 1e-6
GRID_W = 64
POS_BASE = 10000.0
N_MOD = 6
S5_C = 16
S5_P = 64
S5_T = 16
HY_BANDS = 16
HY_ORDER = 2

FFT_N = 8192
FFT_NA = 128
FFT_NB = 64

LANES = 128
MXU_N = 256
SLAB = 16
RELAYOUT_ROWS = 256
SCAN_GROUPS = 2
VMEM_LIMIT = 56 * 1024 * 1024


def _cparams(*sem):
    return pltpu.CompilerParams(dimension_semantics=sem, vmem_limit_bytes=VMEM_LIMIT)


def _rms(x, g):
    return x * lax.rsqrt(jnp.mean(x * x, axis=-1, keepdims=True) + EPS) * g


def _mod_kernel(c_ref, w_ref, b_ref, o_ref):
    c = c_ref[...]
    a = c * jax.nn.sigmoid(c)
    o_ref[...] = jnp.dot(a.astype(BF16), w_ref[...].astype(BF16),
                         preferred_element_type=F32) + b_ref[...]


def _modulation(c_rows, ada_w, ada_b):
    rows, d = c_rows.shape
    n = ada_w.shape[1]
    bn = 1024
    return pl.pallas_call(
        _mod_kernel,
        grid=(n // bn,),
        in_specs=[pl.BlockSpec((rows, d), lambda j: (0, 0)),
                  pl.BlockSpec((d, bn), lambda j: (0, j)),
                  pl.BlockSpec((1, bn), lambda j: (0, j))],
        out_specs=pl.BlockSpec((rows, bn), lambda j: (0, j)),
        out_shape=jax.ShapeDtypeStruct((rows, n), F32),
        compiler_params=_cparams("arbitrary"),
        name="ada_mod",
    )(c_rows, ada_w, ada_b.reshape(1, n))


def _to_slot_major(v, nbatch):
    nchunk = v.shape[0] // (nbatch * S5_T)
    pieces = v.reshape(nbatch * nchunk, S5_T, v.shape[1])
    by_chunk = jnp.stack([pieces[b * nchunk + k] for k in range(nchunk) for b in range(nbatch)], axis=0)
    return _regroup(by_chunk)


def _from_slot_major(slots, nbatch):
    nchunk = slots.shape[1] // nbatch
    by_chunk = _regroup(slots)
    pieces = jnp.stack([by_chunk[k * nbatch + b] for b in range(nbatch) for k in range(nchunk)], axis=0)
    return pieces.reshape(nbatch * nchunk * S5_T, slots.shape[2])


def _proj_kernel(*refs, d, use_pos, d_s5):
    if use_pos:
        x_ref, pos_ref, mod_ref, g_ref, w_ref = refs[:5]
        outs = refs[5:]
        h = x_ref[...] + pos_ref[...][None]
    else:
        x_ref, mod_ref, g_ref, w_ref = refs[:4]
        outs = refs[4:]
        h = x_ref[...]
    nbatch, tokens, _ = h.shape
    m = mod_ref[...]
    hn = _rms(h, g_ref[...][None]) * (1.0 + m[:, :, d:2 * d]) + m[:, :, 0:d]
    proj = jnp.dot(hn.reshape(nbatch * tokens, d).astype(BF16), w_ref[...], preferred_element_type=F32)
    outs[0][...] = _to_slot_major(proj[:, :d_s5].astype(BF16), nbatch)
    if len(outs) > 1:
        outs[1][...] = proj[:, d_s5:].reshape(nbatch, tokens, -1).astype(BF16)


def _project(x, pos, mod3, mod_row, g, w_bf16, d_s5, d_rest, tt):
    bsz, n, d = x.shape
    use_pos = pos is not None
    in_specs = [pl.BlockSpec((bsz, tt, d), lambda i: (0, i, 0))]
    args = [x]
    if use_pos:
        in_specs.append(pl.BlockSpec((tt, d), lambda i: (i, 0)))
        args.append(pos)
    mod_rows = bsz if mod_row is None else 1
    in_specs += [pl.BlockSpec((mod_rows, 1, mod3.shape[-1]), lambda i: (0 if mod_row is None else mod_row, 0, 0)),
                 pl.BlockSpec((1, d), lambda i: (0, 0)),
                 pl.BlockSpec((d, d_s5 + d_rest), lambda i: (0, 0))]
    args += [mod3, g.reshape(1, d), w_bf16]
    rows = tt // S5_T * bsz
    out_specs = [pl.BlockSpec((S5_T, rows, d_s5), lambda i: (0, i, 0))]
    out_shape = [jax.ShapeDtypeStruct((S5_T, n // S5_T * bsz, d_s5), BF16)]
    if d_rest:
        out_specs.append(pl.BlockSpec((bsz, tt, d_rest), lambda i: (0, i, 0)))
        out_shape.append(jax.ShapeDtypeStruct((bsz, n, d_rest), BF16))
    return pl.pallas_call(
        functools.partial(_proj_kernel, d=d, use_pos=use_pos, d_s5=d_s5),
        grid=(n // tt,),
        in_specs=in_specs,
        out_specs=out_specs,
        out_shape=out_shape,
        compiler_params=_cparams("arbitrary"),
        name="norm_proj",
    )(*args)


def _s5_tables(a_re, a_im, log_step, b_re, b_im, c_re, c_im, d_skip):
    T, C, P = S5_T, S5_C, S5_P
    G = a_re.shape[1]
    gpb = LANES // C
    slots = np.arange(T)
    tok = np.stack([8 * (slots // 8) + (slots % 8 - o) % 8 for o in range(gpb)])
    tok_g = tok[np.arange(G) % gpb]

    step = jnp.exp(log_step.astype(F32))[..., None]
    ar = a_re.astype(F32) * step
    ai = a_im.astype(F32) * step

    def powers(d, expo):
        e = jnp.asarray(np.broadcast_to(expo, (G, expo.shape[-1])).astype(np.float32))[:, :, None]
        mag = jnp.exp(ar[d][:, None, :] * e)
        return mag * jnp.cos(ai[d][:, None, :] * e), mag * jnp.sin(ai[d][:, None, :] * e)

    lr, li = a_re.astype(F32), a_im.astype(F32)
    first = [powers(d, np.ones(1)) for d in range(2)]
    nr = jnp.stack([first[d][0][:, 0] for d in range(2)]) - 1.0
    ni = jnp.stack([first[d][1][:, 0] for d in range(2)])
    den = lr * lr + li * li
    qr = (nr * lr + ni * li) / den
    qi = (ni * lr - nr * li) / den
    bbr = qr[..., None] * b_re - qi[..., None] * b_im
    bbi = qr[..., None] * b_im + qi[..., None] * b_re
    cr, ci = c_re.astype(F32), c_im.astype(F32)

    kerns = []
    for d in range(2):
        pr, pi = powers(d, np.arange(T))
        wr = cr[d][:, None] * pr[:, :, None, :] - ci[d][:, None] * pi[:, :, None, :]
        wi = cr[d][:, None] * pi[:, :, None, :] + ci[d][:, None] * pr[:, :, None, :]
        kerns.append(jnp.einsum('gtcp,gpe->gtce', wr, bbr[d], precision=HI)
                     - jnp.einsum('gtcp,gpe->gtce', wi, bbi[d], precision=HI))
    kcat = jnp.concatenate(kerns, axis=1)
    kpad = jnp.pad(kcat.reshape(G, 2 * T * C, C), ((0, 0), (0, 0), (0, LANES - C)))
    d_lanes = jnp.tile(d_skip.astype(F32).reshape(G, 1, C), (1, 1, 2 * T))
    m1 = pl.pallas_call(
        _s5_intra_kernel,
        grid=(G // gpb,),
        in_specs=[pl.BlockSpec((gpb, 2 * T * C, LANES), lambda s: (s, 0, 0)),
                  pl.BlockSpec((gpb, 1, 2 * T * C), lambda s: (s, 0, 0))],
        out_specs=pl.BlockSpec((gpb, T * C, T * C), lambda s: (s, 0, 0)),
        out_shape=jax.ShapeDtypeStruct((G, T * C, T * C), BF16),
        compiler_params=_cparams("arbitrary"),
        name="s5_intra_table",
    )(kpad, d_lanes)

    def in_mat(d, expo):
        p_r, p_i = powers(d, expo)
        br, bi = bbr[d].transpose(0, 2, 1)[:, None], bbi[d].transpose(0, 2, 1)[:, None]
        er = p_r[:, :, None, :] * br - p_i[:, :, None, :] * bi
        ei = p_r[:, :, None, :] * bi + p_i[:, :, None, :] * br
        return er.reshape(G, T * C, P), ei.reshape(G, T * C, P)

    efr, efi = in_mat(0, T - 1 - tok_g)
    ebr, ebi = in_mat(1, tok_g)
    m2 = jnp.concatenate([efr, ebr, efi, ebi], axis=-1)

    def out_mat(d, expo):
        p_r, p_i = powers(d, expo)
        crt = cr[d].transpose(0, 2, 1)[:, :, None, :]
        cit = ci[d].transpose(0, 2, 1)[:, :, None, :]
        prt = p_r.transpose(0, 2, 1)[:, :, :, None]
        pit = p_i.transpose(0, 2, 1)[:, :, :, None]
        vr = crt * prt - cit * pit
        vi = crt * pit + cit * prt
        return vr.reshape(G, P, T * C), -vi.reshape(G, P, T * C)

    vfr, vfi = out_mat(0, tok_g + 1)
    vbr, vbi = out_mat(1, T - tok_g)
    m3 = jnp.concatenate([vfr, vbr, vfi, vbi], axis=1)

    last = [powers(d, np.full(1, T)) for d in range(2)]
    lam = jnp.stack([jnp.concatenate([last[0][c][:, 0], last[1][c][:, 0]], axis=-1)
                     for c in range(2)], axis=1)
    is_fwd = jnp.asarray((np.arange(4 * P) // P) % 2 == 0)[None, :, None]
    m3f = jnp.where(is_fwd, m3, 0.0).astype(BF16)
    m3b = jnp.where(is_fwd, 0.0, m3).astype(BF16)
    return m1, m2.astype(BF16), m3f, m3b, lam


def _s5_intra_kernel(k_ref, d_ref, o_ref):
    T, C = S5_T, S5_C
    wide = 2 * T * C
    row = lax.broadcasted_iota(jnp.int32, (C, wide), 0)
    lane = lax.broadcasted_iota(jnp.int32, (C, wide), 1)
    for g in range(k_ref.shape[0]):
        k = k_ref[g]
        shift = C
        while shift < LANES:
            k = k + pltpu.roll(k, shift, axis=1)
            shift *= 2
        k = jnp.concatenate([k] * (wide // LANES), axis=1)
        block = lambda x, k=k: k[x * C:(x + 1) * C]
        strip = block(0) + block(T) + jnp.where(row == lane % C, d_ref[g], 0.0)
        for j in range(2 * T - 1):
            if j != T - 1:
                x = T - 1 - j if j < T - 1 else T + (j - (T - 1))
                strip = jnp.where(lane // C == j, block(x), strip)
        block_rows = []
        for l in range(T):
            t = 8 * (l // 8) + (l % 8 - g) % 8
            start = (T - 1 - t) * C
            window = pltpu.roll(strip, (wide - start) % wide, axis=1) if start else strip
            block_rows.append(window[:, :T * C])
        m = jnp.concatenate(block_rows, axis=0)
        if g:
            m = jnp.concatenate([pltpu.roll(m[:, h * LANES:(h + 1) * LANES], g * C, axis=1) for h in range(2)],
                                axis=1)
        o_ref[g] = m.T.astype(BF16)


def _gelu_tanh(x):
    return 0.5 * x * (1.0 + jnp.tanh(math.sqrt(2.0 / math.pi) * (x + 0.044715 * (x * x * x))))


def _s5_kernel(u_ref, uc_ref, m1_ref, m2_ref, m3f_ref, m3b_ref, lam_ref, o_ref,
               z_ref, e_ref, sf_ref, sb_ref, *, nbatch, nchunk, nctx):
    groups = LANES // S5_C
    rows = nbatch * nchunk
    lane = lax.broadcasted_iota(jnp.int32, (1, LANES), 1)
    slot_bits = [(((lane // S5_C) >> j) & 1) == 1 for j in range(3)]

    def skew(xs):
        cur = [xs[(-k) % groups] for k in range(groups)]
        for j, bit in enumerate(slot_bits):
            cur = [jnp.where(bit, cur[(i - (1 << j)) % groups], cur[i]) for i in range(groups)]
        return cur

    def lane_roll(v, shift):
        if v.dtype != BF16:
            return pltpu.roll(v, shift, axis=1)
        return pltpu.bitcast(pltpu.roll(pltpu.bitcast(v, jnp.uint32), shift, axis=1), BF16)

    def to_chunk_rows(read_t, nrows, row0):
        for h in range(2):
            rolled = []
            for t8 in range(groups):
                v = read_t(8 * h + t8)
                rolled.append(lane_roll(v, t8 * S5_C) if t8 else v)
            for g, zg in enumerate(skew(rolled)):
                z_ref[g, row0:row0 + nrows, h * LANES:(h + 1) * LANES] = zg

    for r0 in range(0, rows, RELAYOUT_ROWS):
        to_chunk_rows(lambda t, r0=r0: u_ref[t, r0:r0 + RELAYOUT_ROWS], RELAYOUT_ROWS, r0)
    to_chunk_rows(lambda t: uc_ref[t], nbatch * nctx, rows)

    wide = 2 * LANES
    fwd_lane = (lax.broadcasted_iota(jnp.int32, (1, wide), 1) % LANES) < (LANES // 2)
    tile = lambda k: pl.ds(pl.multiple_of(k * nbatch, nbatch), nbatch)

    npar = e_ref.shape[0]
    for g0 in range(0, groups, npar):
        gs = range(g0, g0 + npar)
        for j, g in enumerate(gs):
            e_ref[j] = jnp.dot(z_ref[g], m2_ref[g], preferred_element_type=F32)
        lam_r = [lam_ref[g, 0:1, :] for g in gs]
        lam_i = [lam_ref[g, 1:2, :] for g in gs]

        def advance(state, kf, kb):
            out = []
            for j, (sr, si) in enumerate(state):
                e2 = jnp.where(fwd_lane, e_ref[j, tile(kf), :], e_ref[j, tile(kb), :])
                out.append((lam_r[j] * sr - lam_i[j] * si + e2[:, :LANES],
                            lam_r[j] * si + lam_i[j] * sr + e2[:, LANES:]))
            return tuple(out)

        zero = jnp.zeros((nbatch, LANES), F32)
        state = tuple((zero, zero) for _ in gs)
        for i in range(nctx):
            state = advance(state, nchunk + i, nchunk + nctx - 1 - i)

        def body(i, state):
            kf, kb = i, nchunk - 1 - i
            for j, (sr, si) in enumerate(state):
                s = jnp.concatenate([sr, si], axis=-1)
                sf_ref[j, tile(kf), :] = s
                sb_ref[j, tile(kb), :] = s
            return advance(state, kf, kb)

        lax.fori_loop(0, nchunk, body, state, unroll=2)

        for j, g in enumerate(gs):
            y = (jnp.dot(z_ref[g, 0:rows, :], m1_ref[g], preferred_element_type=F32)
                 + jnp.dot(sf_ref[j].astype(BF16), m3f_ref[g], preferred_element_type=F32)
                 + jnp.dot(sb_ref[j].astype(BF16), m3b_ref[g], preferred_element_type=F32))
            z_ref[g, 0:rows, :] = _gelu_tanh(y).astype(BF16)

    for r0 in range(0, rows, RELAYOUT_ROWS):
        for h in range(2):
            ys = [z_ref[g, r0:r0 + RELAYOUT_ROWS, h * LANES:(h + 1) * LANES] for g in range(groups)]
            for t8, acc in enumerate(skew(ys)):
                if t8:
                    acc = lane_roll(acc, LANES - t8 * S5_C)
                o_ref[8 * h + t8, r0:r0 + RELAYOUT_ROWS] = acc


def _s5_scan(u_t, uc_t, bsz, m1, m2, m3f, m3b, lam):
    T, rows, width = u_t.shape
    ctx_rows = uc_t.shape[1]
    gpb = LANES // S5_C
    kern = functools.partial(_s5_kernel, nbatch=bsz, nchunk=rows // bsz, nctx=ctx_rows // bsz)
    one = pl.Buffered(1)
    mat = pl.BlockSpec((gpb, 2 * LANES, 2 * LANES), lambda j: (j, 0, 0), pipeline_mode=one)
    return pl.pallas_call(
        kern,
        grid=(width // LANES,),
        in_specs=[pl.BlockSpec((T, rows, LANES), lambda j: (0, 0, j)),
                  pl.BlockSpec((T, ctx_rows, LANES), lambda j: (0, 0, j)),
                  mat, mat, mat, mat,
                  pl.BlockSpec((gpb, 2, LANES), lambda j: (j, 0, 0))],
        out_specs=pl.BlockSpec((T, rows, LANES), lambda j: (0, 0, j), pipeline_mode=one),
        out_shape=jax.ShapeDtypeStruct(u_t.shape, BF16),
        scratch_shapes=[pltpu.VMEM((gpb, rows + ctx_rows, 2 * LANES), BF16),
                        pltpu.VMEM((SCAN_GROUPS, rows + ctx_rows, 2 * LANES), F32),
                        pltpu.VMEM((SCAN_GROUPS, rows, 2 * LANES), F32),
                        pltpu.VMEM((SCAN_GROUPS, rows, 2 * LANES), F32)],
        compiler_params=_cparams("arbitrary"),
        name="s5_scan",
    )(u_t, uc_t, m1, m2, m3f, m3b, lam)


def _dft_tables(n_seq):
    na_sig = n_seq // FFT_NB
    ka = np.arange(FFT_NA)[:, None]
    nb = np.arange(FFT_NB)[:, None, None]

    def stage1(n_in):
        na = np.arange(n_in)[None, :]
        ang = -2.0 * np.pi * (na * ka / FFT_NA)[None] - 2.0 * np.pi * (nb * ka[None]) / FFT_N
        return np.cos(ang), np.sin(ang)

    c, s = stage1(na_sig)
    f1 = np.concatenate([np.concatenate([c, -s], axis=2), np.concatenate([s, c], axis=2)], axis=1)
    c, s = stage1(FFT_NA)
    f1_real = np.concatenate([c, s], axis=1)
    kb = np.arange(FFT_NB)[:, None]
    nbv = np.arange(FFT_NB)[None, :]
    ang = -2.0 * np.pi * kb * nbv / FFT_NB
    c, s = np.cos(ang), np.sin(ang)
    f2 = np.block([[c, -s], [s, c]])
    f2_inv = np.block([[c, s], [-s, c]]) / FFT_NB
    nap = np.arange(na_sig)[:, None]
    kav = np.arange(FFT_NA)[None, :]
    ang = 2.0 * np.pi * (nap * kav / FFT_NA)[None] + 2.0 * np.pi * (nb * kav[None]) / FFT_N
    c, s = np.cos(ang) / FFT_NA, np.sin(ang) / FFT_NA
    f3 = np.concatenate([np.concatenate([c, -s], axis=2), np.concatenate([s, c], axis=2)], axis=1)
    as_bf16 = lambda a: jnp.asarray(a.astype(np.float32)).astype(BF16)
    return as_bf16(f1), as_bf16(f1_real), as_bf16(f2), as_bf16(f2_inv), as_bf16(f3)


def _conv3_slab(z_ref, plane, nb0, w, b, nbk):
    def rows(i):
        return z_ref[plane, 0, i].astype(F32)

    mid = [rows(nb0 + i) for i in range(SLAB)]
    na = mid[0].shape[0]
    ridx = lax.broadcasted_iota(jnp.int32, mid[0].shape, 0)
    wrapped = rows((nb0 + nbk - 1) % nbk)
    before = jnp.where(nb0 == 0, jnp.where(ridx == 0, 0.0, pltpu.roll(wrapped, 1, axis=0)), wrapped)
    wrapped = rows((nb0 + SLAB) % nbk)
    after = jnp.where(nb0 + SLAB == nbk, jnp.where(ridx == na - 1, 0.0, pltpu.roll(wrapped, na - 1, axis=0)),
                      wrapped)
    ext = [before] + mid + [after]
    return [ext[i] * w[0:1] + ext[i + 1] * w[1:2] + ext[i + 2] * w[2:3] + b for i in range(SLAB)]


def _for_row_slabs(nbk, fn):
    lax.fori_loop(0, nbk // SLAB, lambda j, c: (fn(pl.multiple_of(j * SLAB, SLAB)), c)[1], 0, unroll=2)


def _regroup(x):
    return jnp.swapaxes(x, 0, 1)


def _hyena_kernel(zv_ref, zg1_ref, zg2_ref, wv_ref, bv_ref, wg1_ref, bg1_ref, wg2_ref, bg2_ref,
                  kf_ref, f1_ref, f2_ref, f2i_ref, f3_ref, o_ref, s_ref, z1_ref,
                  *, nbk, na, ngrp, kag):
    step = pl.program_id(2)
    hb = FFT_NB

    def stage1(src_ref, conv):
        def slab(nb0):
            if conv:
                planes = [[v.astype(BF16) for v in _conv3_slab(src_ref, pln, nb0, wv_ref[...], bv_ref[...], nbk)]
                          for pln in range(2)]
            tiles = []
            for i in range(SLAB):
                nb = nb0 + i
                parts = [planes[0][i], planes[1][i]] if conv else [src_ref[0, 0, nb], src_ref[1, 0, nb]]
                rhs = jnp.concatenate(parts, axis=0)
                tiles.append(jnp.dot(f1_ref[nb], rhs, preferred_element_type=F32).astype(BF16))
            s_ref[:, pl.ds(nb0, SLAB), :] = _regroup(jnp.stack(tiles, axis=0))
        _for_row_slabs(nbk, slab)

    def stage3(gate_ref, wg, bg, dst_ref):
        def slab(nb0):
            rows = _regroup(s_ref[:, pl.ds(nb0, SLAB), :])
            gates = [_conv3_slab(gate_ref, pln, nb0, wg, bg, nbk) for pln in range(2)]
            for i in range(SLAB):
                nb = nb0 + i
                y = jnp.dot(f3_ref[nb], rows[i], preferred_element_type=F32)
                for pln in range(2):
                    dst_ref[pln, 0, nb] = (gates[pln][i] * y[pln * na:(pln + 1) * na]).astype(dst_ref.dtype)
        _for_row_slabs(nbk, slab)

    @pl.when(step == 0)
    def _():
        stage1(zv_ref, True)

    @pl.when(step == ngrp)
    def _():
        stage1(z1_ref, False)

    base = (step % ngrp) * kag
    loaded = [jnp.concatenate([s_ref[base + k], s_ref[FFT_NA + base + k]], axis=0) for k in range(kag)]
    results = []
    for k, rhs in enumerate(loaded):
        spec = jnp.dot(f2_ref[...], rhs, preferred_element_type=F32).astype(BF16)
        kf = kf_ref[0, k]
        sr, si = spec[:hb], spec[hb:]
        kr, ki = kf[:hb], kf[hb:]
        prod = jnp.concatenate([sr * kr - si * ki, sr * ki + si * kr], axis=0)
        results.append(jnp.dot(f2i_ref[...], prod, preferred_element_type=F32).astype(BF16))
    for k, back in enumerate(results):
        s_ref[base + k] = back[:hb]
        s_ref[FFT_NA + base + k] = back[hb:]

    @pl.when(step == ngrp - 1)
    def _():
        stage3(zg1_ref, wg1_ref[...], bg1_ref[...], z1_ref)

    @pl.when(step == 2 * ngrp - 1)
    def _():
        stage3(zg2_ref, wg2_ref[...], bg2_ref[...], o_ref)


def _hyena(z5, conv_w, conv_b, kf, tables, cb=MXU_N, kag=2 * SLAB):
    f1, _, f2, f2i, f3 = tables
    _, npair, nbk, na, _ = z5.shape
    c = kf.shape[-1]
    ncb = c // cb
    ngrp = FFT_NA // kag
    one = pl.Buffered(1)
    zspec = lambda grp: pl.BlockSpec((2, 1, nbk, na, cb), lambda p, j, s: (0, p, 0, 0, grp * ncb + j))
    wspec = lambda grp: pl.BlockSpec((3, cb), lambda p, j, s: (0, grp * ncb + j))
    bspec = lambda grp: pl.BlockSpec((1, cb), lambda p, j, s: (0, grp * ncb + j))
    const = lambda a: pl.BlockSpec(a.shape, lambda p, j, s: tuple(0 for _ in a.shape), pipeline_mode=one)
    cw = conv_w.astype(F32)
    cbias = conv_b.astype(F32).reshape(1, -1)
    return pl.pallas_call(
        functools.partial(_hyena_kernel, nbk=nbk, na=na, ngrp=ngrp, kag=kag),
        grid=(npair, ncb, 2 * ngrp),
        in_specs=[zspec(0), zspec(1), zspec(2), wspec(0), bspec(0), wspec(1), bspec(1), wspec(2), bspec(2),
                  pl.BlockSpec((1, kag, 2 * FFT_NB, cb), lambda p, j, s: (s // ngrp, s % ngrp, 0, j)),
                  const(f1), const(f2), const(f2i), const(f3)],
        out_specs=pl.BlockSpec((2, 1, nbk, na, cb), lambda p, j, s: (0, p, 0, 0, j), pipeline_mode=one),
        out_shape=jax.ShapeDtypeStruct((2, npair, nbk, na, c), BF16),
        scratch_shapes=[pltpu.VMEM((2 * FFT_NA, nbk, cb), BF16),
                        pltpu.VMEM((2, 1, nbk, na, cb), BF16)],
        compiler_params=_cparams("arbitrary", "arbitrary", "arbitrary"),
        name="hyena_conv",
    )(z5, z5, z5, cw, cbias, cw, cbias, cw, cbias, kf, f1, f2, f2i, f3)


def _filter_kernel(embt_ref, tv_ref, w1t_ref, b1_ref, w2t_ref, b2_ref, fr_ref, w3_ref, dec_ref, bias_ref,
                   f1_ref, f2_ref, k_ref, s_ref, h_ref, *, nbk, kag):
    first = (pl.program_id(0) == 0) & (pl.program_id(1) == 0)
    step = pl.program_id(2)
    half = FFT_NA // 2
    lanes_per_pass = 8 * FFT_NA

    @pl.when(first & (step == 0))
    def _():
        fr = fr_ref[...]
        for i in range(embt_ref.shape[1] // lanes_per_pass):
            cols = slice(i * lanes_per_pass, (i + 1) * lanes_per_pass)
            h = jnp.sin(fr * (jnp.dot(w1t_ref[...], embt_ref[:, cols], precision=HI,
                                      preferred_element_type=F32) + b1_ref[...]))
            h = jnp.sin(fr * (jnp.dot(w2t_ref[...], h, precision=HI, preferred_element_type=F32) + b2_ref[...]))
            h_ref[i * lanes_per_pass:(i + 1) * lanes_per_pass, :] = h.T.astype(BF16)

    @pl.when(step == 0)
    def _():
        dec = dec_ref[0]

        def slab(nb0):
            rows = h_ref[pl.ds(pl.multiple_of(nb0 * FFT_NA, SLAB * FFT_NA), SLAB * FFT_NA), :]
            rows = rows.reshape(SLAB, FFT_NA, rows.shape[-1])
            fwd = jnp.dot(rows[:, :half].reshape(SLAB * half, -1), w3_ref[0, 0], preferred_element_type=F32)
            bwd = jnp.dot(rows[:, half:].reshape(SLAB * half, -1), w3_ref[0, 1], preferred_element_type=F32)
            tiles = []
            for i in range(SLAB):
                nb = nb0 + i
                tv = tv_ref[nb]
                window = jnp.exp(-tv[:, 0:1] * dec) * tv[:, 1:2]
                taps = slice(i * half, (i + 1) * half)
                filt = jnp.concatenate([fwd[taps], bwd[taps]], axis=0) * window
                tiles.append(jnp.dot(f1_ref[nb], filt.astype(BF16), preferred_element_type=F32).astype(BF16))
            s_ref[:, pl.ds(nb0, SLAB), :] = _regroup(jnp.stack(tiles, axis=0))
        _for_row_slabs(nbk, slab)

    base = step * kag
    bias = bias_ref[0]
    for k in range(kag):
        rhs = jnp.concatenate([s_ref[base + k], s_ref[FFT_NA + base + k]], axis=0)
        spec = jnp.dot(f2_ref[...], rhs, preferred_element_type=F32)
        k_ref[0, k] = jnp.concatenate([spec[:FFT_NB] + bias, spec[FFT_NB:]], axis=0).astype(BF16)


def _filter_spectra(n, w1, b1, w2, b2, freq, w3, decay, bias, tables, cb=MXU_N, kag=2 * SLAB):
    _, f1_real, f2, _, _ = tables
    hid = w2.shape[0]
    c = decay.shape[-1]
    t = np.linspace(0.0, 1.0, n, dtype=np.float32)[:, None]
    w = (2.0 * np.pi * np.arange(n, dtype=np.float32) / n).astype(np.float32)
    bands = np.linspace(1e-4, HY_BANDS - 1, HY_BANDS, dtype=np.float32)
    ang = w[:, None] * bands[None, :]
    emb = np.concatenate([t, np.cos(ang), -np.sin(ang)], axis=-1).astype(np.float32)
    kpad = 64
    idx = FFT_NB * np.arange(FFT_NA)[None, :] + np.arange(FFT_NB)[:, None]
    lagi = np.minimum(np.where(idx < n, idx, FFT_N - idx), n - 1)
    embt = np.zeros((kpad, FFT_N), np.float32)
    embt[:emb.shape[1]] = emb[lagi.reshape(-1)].T
    tv = np.stack([t[lagi, 0], (idx != n).astype(np.float32)], axis=-1)
    col = lambda a: a.reshape(hid, 1).astype(F32)
    w1t = jnp.pad(w1.astype(F32), ((0, kpad - w1.shape[0]), (0, 0))).T
    w3r = w3.astype(BF16).reshape(hid, HY_ORDER, 2, c).transpose(1, 2, 0, 3)
    dec = jnp.abs(decay.astype(F32)).reshape(HY_ORDER, 1, c)
    one = pl.Buffered(1)
    const = lambda shape: pl.BlockSpec(shape, lambda o, j, s: tuple(0 for _ in shape), pipeline_mode=one)
    per_channel = pl.BlockSpec((1, 1, cb), lambda o, j, s: (o, 0, j))
    return pl.pallas_call(
        functools.partial(_filter_kernel, nbk=FFT_NB, kag=kag),
        grid=(HY_ORDER, c // cb, FFT_NA // kag),
        in_specs=[const(embt.shape), const(tv.shape), const((hid, kpad)), const((hid, 1)),
                  const((hid, hid)), const((hid, 1)), const((hid, 1)),
                  pl.BlockSpec((1, 2, hid, cb), lambda o, j, s: (o, 0, 0, j)),
                  per_channel, per_channel,
                  const(f1_real.shape), const(f2.shape)],
        out_specs=pl.BlockSpec((1, kag, 2 * FFT_NB, cb), lambda o, j, s: (o, s, 0, j)),
        out_shape=jax.ShapeDtypeStruct((HY_ORDER, FFT_NA, 2 * FFT_NB, c), BF16),
        scratch_shapes=[pltpu.VMEM((2 * FFT_NA, FFT_NB, cb), BF16),
                        pltpu.VMEM((FFT_N, hid), BF16)],
        compiler_params=_cparams("arbitrary", "arbitrary", "arbitrary"),
        name="hyena_filter",
    )(jnp.asarray(embt), jnp.asarray(tv), w1t, col(b1), w2.astype(F32).T, col(b2), col(freq), w3r, dec,
      bias.astype(F32).reshape(HY_ORDER, 1, c), f1_real, f2)


def _out_kernel(x_ref, pos_ref, mod_ref, ys_ref, yh_ref, gw_ref, gb_ref, g5_ref, gh_ref, wo_ref,
                g2_ref, w1_ref, w2_ref, gf_ref, o_ref, *, d):
    nbatch, tokens, _ = x_ref.shape
    rows = nbatch * tokens
    m = mod_ref[...]
    gate1, shift2 = m[:, :, 2 * d:3 * d], m[:, :, 3 * d:4 * d]
    scale2, gate2 = m[:, :, 4 * d:5 * d], m[:, :, 5 * d:6 * d]
    per_batch = lambda v: v.reshape(nbatch, tokens, d)
    h = x_ref[...] + pos_ref[...][None]
    ys = _from_slot_major(ys_ref[...], nbatch)
    ab = jnp.dot(ys, gw_ref[...], preferred_element_type=F32) + gb_ref[...]
    half = ab.shape[-1] // 2
    y5 = ab[:, :half] * jax.nn.sigmoid(ab[:, half:])
    yh = yh_ref[...].reshape(rows, -1).astype(F32)
    mix = jnp.concatenate([_rms(y5, g5_ref[...]), _rms(yh, gh_ref[...])], axis=-1)
    h = h + gate1 * per_batch(jnp.dot(mix.astype(BF16), wo_ref[...], preferred_element_type=F32))
    hn = _rms(h, g2_ref[...][None]) * (1.0 + scale2) + shift2
    hid = jnp.dot(hn.reshape(rows, d).astype(BF16), w1_ref[...], preferred_element_type=F32)
    hid = jnp.square(jnp.maximum(hid, 0.0))
    h = h + gate2 * per_batch(jnp.dot(hid.astype(BF16), w2_ref[...], preferred_element_type=F32))
    o_ref[...] = _rms(h, gf_ref[...][None])


def _output_stage(x, pos, mod3, ys_t, yh, glu_w, glu_b, g5, gh, w_out, g2, w1, w2, gf, tt=64):
    bsz, n, d = x.shape
    dh = ys_t.shape[-1]
    one = pl.Buffered(1)
    const = lambda a: pl.BlockSpec(a.shape, lambda i: tuple(0 for _ in a.shape), pipeline_mode=one)
    row = lambda a: a.reshape(1, -1).astype(F32)
    glu_b, g5, gh, g2, gf = row(glu_b), row(g5), row(gh), row(g2), row(gf)
    return pl.pallas_call(
        functools.partial(_out_kernel, d=d),
        grid=(n // tt,),
        in_specs=[pl.BlockSpec((bsz, tt, d), lambda i: (0, i, 0)),
                  pl.BlockSpec((tt, d), lambda i: (i, 0)),
                  pl.BlockSpec((bsz, 1, mod3.shape[-1]), lambda i: (0, 0, 0)),
                  pl.BlockSpec((S5_T, tt // S5_T * bsz, dh), lambda i: (0, i, 0)),
                  pl.BlockSpec((bsz, tt, yh.shape[-1]), lambda i: (0, i, 0)),
                  const(glu_w), const(glu_b), const(g5), const(gh), const(w_out), const(g2),
                  const(w1), const(w2), const(gf)],
        out_specs=pl.BlockSpec((bsz, tt, d), lambda i: (0, i, 0)),
        out_shape=jax.ShapeDtypeStruct((bsz, n, d), F32),
        compiler_params=_cparams("arbitrary"),
        name="mix_mlp_out",
    )(x, pos, mod3, ys_t, yh, glu_w, glu_b, g5, gh, w_out, g2, w1, w2, gf)


def _pos_table(n, d):
    rows = n // GRID_W
    row = np.repeat(np.arange(rows, dtype=np.float32), GRID_W)
    col = np.tile(np.arange(GRID_W, dtype=np.float32), rows)
    quarter = d // 4
    omega = (1.0 / (POS_BASE ** (np.arange(quarter, dtype=np.float32) / quarter))).astype(np.float32)

    def enc(p):
        ang = p[:, None] * omega[None, :]
        return np.concatenate([np.sin(ang), np.cos(ang)], axis=-1)

    return jnp.asarray(np.concatenate([enc(row), enc(col)], axis=-1).astype(np.float32))


def kernel(x, c, ctx, c_ctx, ada_w, ada_b, norm1_g, w_in, s5_a_re, s5_a_im, s5_log_step, s5_b_re,
           s5_b_im, s5_c_re, s5_c_im, s5_d, s5_glu_w, s5_glu_b, hy_conv_w, hy_conv_b, hy_f_w1,
           hy_f_b1, hy_f_w2, hy_f_b2, hy_f_freq, hy_f_w3, hy_decay, hy_bias, mix_g_s5, mix_g_hy,
           w_out, norm2_g, mlp_w1, mlp_w2, final_g):
    bsz, n, d = x.shape
    depth = ada_w.shape[0]
    d_s5 = s5_d.shape[-1]
    d_hy = hy_bias.shape[-1]
    nctx = ctx.shape[1]
    npair = bsz // 2
    nbk, na = FFT_NB, n // FFT_NB
    pos = _pos_table(n, d)
    tables = _dft_tables(n)

    assert depth == 1 and bsz % 2 == 0 and n % (FFT_NB * 8) == 0 and 2 * n == FFT_N
    mod_rows = 16
    c_rows = jnp.concatenate([c, c_ctx[None], jnp.zeros((mod_rows - bsz - 1, d), c.dtype)], axis=0)
    mod3 = _modulation(c_rows, ada_w[0], ada_b[0]).reshape(mod_rows, 1, N_MOD * d)

    w_in_b = w_in[0].astype(BF16)
    u_t, z = _project(x, pos, mod3, None, norm1_g[0], w_in_b, d_s5, 3 * d_hy, tt=128)
    (uc_t,) = _project(ctx, None, mod3, bsz, norm1_g[0], w_in_b, d_s5, 0, tt=128)

    s5_mats = _s5_tables(s5_a_re[0], s5_a_im[0], s5_log_step[0], s5_b_re[0], s5_b_im[0],
                         s5_c_re[0], s5_c_im[0], s5_d[0])
    ys_t = _s5_scan(u_t, uc_t, bsz, *s5_mats)

    kf = _filter_spectra(n, hy_f_w1[0], hy_f_b1[0], hy_f_w2[0], hy_f_b2[0], hy_f_freq[0],
                         hy_f_w3[0], hy_decay[0], hy_bias[0], tables)
    z5 = z.reshape(bsz, na, nbk, 3 * d_hy).transpose(0, 2, 1, 3).reshape(2, npair, nbk, na, 3 * d_hy)
    yh = _hyena(z5, hy_conv_w[0], hy_conv_b[0], kf, tables)
    yh = yh.reshape(bsz, nbk, na, d_hy).transpose(0, 2, 1, 3).reshape(bsz, n, d_hy)

    return _output_stage(x, pos, mod3, ys_t, yh, s5_glu_w[0].astype(BF16), s5_glu_b[0], mix_g_s5[0],
                         mix_g_hy[0], w_out[0].astype(BF16), norm2_g[0], mlp_w1[0].astype(BF16),
                         mlp_w2[0].astype(BF16), final_g)
```

```python
import functools
import math

import numpy as np
import jax
import jax.numpy as jnp
from jax import lax
from jax.experimental import pallas as pl
from jax.experimental.pallas import tpu as pltpu

F32 = jnp.float32
BF16 = jnp.bfloat16
HI = lax.Precision.HIGHEST

EPS = 1e-6
GRID_W = 64
POS_BASE = 10000.0
N_MOD = 6
S5_C = 16
S5_P = 64
S5_T = 16
HY_BANDS = 16
HY_ORDER = 2

FFT_N = 8192
FFT_NA = 128
FFT_NB = 64

LANES = 128
MXU_N = 256
SLAB = 16
RELAYOUT_ROWS = 256
SCAN_GROUPS = 2
VMEM_LIMIT = 56 * 1024 * 1024


def _cparams(*sem):
    return pltpu.CompilerParams(dimension_semantics=sem, vmem_limit_bytes=VMEM_LIMIT)


def _rms(x, g):
    return x * lax.rsqrt(jnp.mean(x * x, axis=-1, keepdims=True) + EPS) * g


def _mod_kernel(c_ref, w_ref, b_ref, o_ref):
    c = c_ref[...]
    a = c * jax.nn.sigmoid(c)
    o_ref[...] = jnp.dot(a.astype(BF16), w_ref[...].astype(BF16),
                         preferred_element_type=F32) + b_ref[...]


def _modulation(c_rows, ada_w, ada_b):
    rows, d = c_rows.shape
    n = ada_w.shape[1]
    bn = 1024
    return pl.pallas_call(
        _mod_kernel,
        grid=(n // bn,),
        in_specs=[pl.BlockSpec((rows, d), lambda j: (0, 0)),
                  pl.BlockSpec((d, bn), lambda j: (0, j)),
                  pl.BlockSpec((1, bn), lambda j: (0, j))],
        out_specs=pl.BlockSpec((rows, bn), lambda j: (0, j)),
        out_shape=jax.ShapeDtypeStruct((rows, n), F32),
        compiler_params=_cparams("arbitrary"),
        name="ada_mod",
    )(c_rows, ada_w, ada_b.reshape(1, n))


def _to_slot_major(v, nbatch):
    nchunk = v.shape[0] // (nbatch * S5_T)
    pieces = v.reshape(nbatch * nchunk, S5_T, v.shape[1])
    by_chunk = jnp.stack([pieces[b * nchunk + k] for k in range(nchunk) for b in range(nbatch)], axis=0)
    return _regroup(by_chunk)


def _from_slot_major(slots, nbatch):
    nchunk = slots.shape[1] // nbatch
    by_chunk = _regroup(slots)
    pieces = jnp.stack([by_chunk[k * nbatch + b] for b in range(nbatch) for k in range(nchunk)], axis=0)
    return pieces.reshape(nbatch * nchunk * S5_T, slots.shape[2])


def _proj_kernel(*refs, d, use_pos, d_s5):
    if use_pos:
        x_ref, pos_ref, mod_ref, g_ref, w_ref = refs[:5]
        outs = refs[5:]
        h = x_ref[...] + pos_ref[...][None]
    else:
        x_ref, mod_ref, g_ref, w_ref = refs[:4]
        outs = refs[4:]
        h = x_ref[...]
    nbatch, tokens, _ = h.shape
    m = mod_ref[...]
    hn = _rms(h, g_ref[...][None]) * (1.0 + m[:, :, d:2 * d]) + m[:, :, 0:d]
    proj = jnp.dot(hn.reshape(nbatch * tokens, d).astype(BF16), w_ref[...], preferred_element_type=F32)
    outs[0][...] = _to_slot_major(proj[:, :d_s5].astype(BF16), nbatch)
    if len(outs) > 1:
        outs[1][...] = proj[:, d_s5:].reshape(nbatch, tokens, -1).astype(BF16)


def _project(x, pos, mod3, mod_row, g, w_bf16, d_s5, d_rest, tt):
    bsz, n, d = x.shape
    use_pos = pos is not None
    in_specs = [pl.BlockSpec((bsz, tt, d), lambda i: (0, i, 0))]
    args = [x]
    if use_pos:
        in_specs.append(pl.BlockSpec((tt, d), lambda i: (i, 0)))
        args.append(pos)
    mod_rows = bsz if mod_row is None else 1
    in_specs += [pl.BlockSpec((mod_rows, 1, mod3.shape[-1]), lambda i: (0 if mod_row is None else mod_row, 0, 0)),
                 pl.BlockSpec((1, d), lambda i: (0, 0)),
                 pl.BlockSpec((d, d_s5 + d_rest), lambda i: (0, 0))]
    args += [mod3, g.reshape(1, d), w_bf16]
    rows = tt // S5_T * bsz
    out_specs = [pl.BlockSpec((S5_T, rows, d_s5), lambda i: (0, i, 0))]
    out_shape = [jax.ShapeDtypeStruct((S5_T, n // S5_T * bsz, d_s5), BF16)]
    if d_rest:
        out_specs.append(pl.BlockSpec((bsz, tt, d_rest), lambda i: (0, i, 0)))
        out_shape.append(jax.ShapeDtypeStruct((bsz, n, d_rest), BF16))
    return pl.pallas_call(
        functools.partial(_proj_kernel, d=d, use_pos=use_pos, d_s5=d_s5),
        grid=(n // tt,),
        in_specs=in_specs,
        out_specs=out_specs,
        out_shape=out_shape,
        compiler_params=_cparams("arbitrary"),
        name="norm_proj",
    )(*args)


def _s5_tables(a_re, a_im, log_step, b_re, b_im, c_re, c_im, d_skip):
    T, C, P = S5_T, S5_C, S5_P
    G = a_re.shape[1]
    gpb = LANES // C
    slots = np.arange(T)
    tok = np.stack([8 * (slots // 8) + (slots % 8 - o) % 8 for o in range(gpb)])
    tok_g = tok[np.arange(G) % gpb]

    step = jnp.exp(log_step.astype(F32))[..., None]
    ar = a_re.astype(F32) * step
    ai = a_im.astype(F32) * step

    def powers(d, expo):
        e = jnp.asarray(np.broadcast_to(expo, (G, expo.shape[-1])).astype(np.float32))[:, :, None]
        mag = jnp.exp(ar[d][:, None, :] * e)
        return mag * jnp.cos(ai[d][:, None, :] * e), mag * jnp.sin(ai[d][:, None, :] * e)

    lr, li = a_re.astype(F32), a_im.astype(F32)
    first = [powers(d, np.ones(1)) for d in range(2)]
    nr = jnp.stack([first[d][0][:, 0] for d in range(2)]) - 1.0
    ni = jnp.stack([first[d][1][:, 0] for d in range(2)])
    den = lr * lr + li * li
    qr = (nr * lr + ni * li) / den
    qi = (ni * lr - nr * li) / den
    bbr = qr[..., None] * b_re - qi[..., None] * b_im
    bbi = qr[..., None] * b_im + qi[..., None] * b_re
    cr, ci = c_re.astype(F32), c_im.astype(F32)

    kerns = []
    for d in range(2):
        pr, pi = powers(d, np.arange(T))
        wr = cr[d][:, None] * pr[:, :, None, :] - ci[d][:, None] * pi[:, :, None, :]
        wi = cr[d][:, None] * pi[:, :, None, :] + ci[d][:, None] * pr[:, :, None, :]
        kerns.append(jnp.einsum('gtcp,gpe->gtce', wr, bbr[d], precision=HI)
                     - jnp.einsum('gtcp,gpe->gtce', wi, bbi[d], precision=HI))
    kcat = jnp.concatenate(kerns, axis=1)
    kpad = jnp.pad(kcat.reshape(G, 2 * T * C, C), ((0, 0), (0, 0), (0, LANES - C)))
    d_lanes = jnp.tile(d_skip.astype(F32).reshape(G, 1, C), (1, 1, 2 * T))
    m1 = pl.pallas_call(
        _s5_intra_kernel,
        grid=(G // gpb,),
        in_specs=[pl.BlockSpec((gpb, 2 * T * C, LANES), lambda s: (s, 0, 0)),
                  pl.BlockSpec((gpb, 1, 2 * T * C), lambda s: (s, 0, 0))],
        out_specs=pl.BlockSpec((gpb, T * C, T * C), lambda s: (s, 0, 0)),
        out_shape=jax.ShapeDtypeStruct((G, T * C, T * C), BF16),
        compiler_params=_cparams("arbitrary"),
        name="s5_intra_table",
    )(kpad, d_lanes)

    def in_mat(d, expo):
        p_r, p_i = powers(d, expo)
        br, bi = bbr[d].transpose(0, 2, 1)[:, None], bbi[d].transpose(0, 2, 1)[:, None]
        er = p_r[:, :, None, :] * br - p_i[:, :, None, :] * bi
        ei = p_r[:, :, None, :] * bi + p_i[:, :, None, :] * br
        return er.reshape(G, T * C, P), ei.reshape(G, T * C, P)

    efr, efi = in_mat(0, T - 1 - tok_g)
    ebr, ebi = in_mat(1, tok_g)
    m2 = jnp.concatenate([efr, ebr, efi, ebi], axis=-1)

    def out_mat(d, expo):
        p_r, p_i = powers(d, expo)
        crt = cr[d].transpose(0, 2, 1)[:, :, None, :]
        cit = ci[d].transpose(0, 2, 1)[:, :, None, :]
        prt = p_r.transpose(0, 2, 1)[:, :, :, None]
        pit = p_i.transpose(0, 2, 1)[:, :, :, None]
        vr = crt * prt - cit * pit
        vi = crt * pit + cit * prt
        return vr.reshape(G, P, T * C), -vi.reshape(G, P, T * C)

    vfr, vfi = out_mat(0, tok_g + 1)
    vbr, vbi = out_mat(1, T - tok_g)
    m3 = jnp.concatenate([vfr, vbr, vfi, vbi], axis=1)

    last = [powers(d, np.full(1, T)) for d in range(2)]
    lam = jnp.stack([jnp.concatenate([last[0][c][:, 0], last[1][c][:, 0]], axis=-1)
                     for c in range(2)], axis=1)
    is_fwd = jnp.asarray((np.arange(4 * P) // P) % 2 == 0)[None, :, None]
    m3f = jnp.where(is_fwd, m3, 0.0).astype(BF16)
    m3b = jnp.where(is_fwd, 0.0, m3).astype(BF16)
    return m1, m2.astype(BF16), m3f, m3b, lam


def _s5_intra_kernel(k_ref, d_ref, o_ref):
    T, C = S5_T, S5_C
    wide = 2 * T * C
    row = lax.broadcasted_iota(jnp.int32, (C, wide), 0)
    lane = lax.broadcasted_iota(jnp.int32, (C, wide), 1)
    for g in range(k_ref.shape[0]):
        k = k_ref[g]
        shift = C
        while shift < LANES:
            k = k + pltpu.roll(k, shift, axis=1)
            shift *= 2
        k = jnp.concatenate([k] * (wide // LANES), axis=1)
        block = lambda x, k=k: k[x * C:(x + 1) * C]
        strip = block(0) + block(T) + jnp.where(row == lane % C, d_ref[g], 0.0)
        for j in range(2 * T - 1):
            if j != T - 1:
                x = T - 1 - j if j < T - 1 else T + (j - (T - 1))
                strip = jnp.where(lane // C == j, block(x), strip)
        block_rows = []
        for l in range(T):
            t = 8 * (l // 8) + (l % 8 - g) % 8
            start = (T - 1 - t) * C
            window = pltpu.roll(strip, (wide - start) % wide, axis=1) if start else strip
            block_rows.append(window[:, :T * C])
        m = jnp.concatenate(block_rows, axis=0)
        if g:
            m = jnp.concatenate([pltpu.roll(m[:, h * LANES:(h + 1) * LANES], g * C, axis=1) for h in range(2)],
                                axis=1)
        o_ref[g] = m.T.astype(BF16)


def _gelu_tanh(x):
    return 0.5 * x * (1.0 + jnp.tanh(math.sqrt(2.0 / math.pi) * (x + 0.044715 * (x * x * x))))


def _s5_kernel(u_ref, uc_ref, m1_ref, m2_ref, m3f_ref, m3b_ref, lam_ref, o_ref,
               z_ref, e_ref, sf_ref, sb_ref, *, nbatch, nchunk, nctx):
    groups = LANES // S5_C
    rows = nbatch * nchunk
    lane = lax.broadcasted_iota(jnp.int32, (1, LANES), 1)
    slot_bits = [(((lane // S5_C) >> j) & 1) == 1 for j in range(3)]

    def skew(xs):
        cur = [xs[(-k) % groups] for k in range(groups)]
        for j, bit in enumerate(slot_bits):
            cur = [jnp.where(bit, cur[(i - (1 << j)) % groups], cur[i]) for i in range(groups)]
        return cur

    def lane_roll(v, shift):
        if v.dtype != BF16:
            return pltpu.roll(v, shift, axis=1)
        return pltpu.bitcast(pltpu.roll(pltpu.bitcast(v, jnp.uint32), shift, axis=1), BF16)

    def to_chunk_rows(read_t, nrows, row0):
        for h in range(2):
            rolled = []
            for t8 in range(groups):
                v = read_t(8 * h + t8)
                rolled.append(lane_roll(v, t8 * S5_C) if t8 else v)
            for g, zg in enumerate(skew(rolled)):
                z_ref[g, row0:row0 + nrows, h * LANES:(h + 1) * LANES] = zg

    for r0 in range(0, rows, RELAYOUT_ROWS):
        to_chunk_rows(lambda t, r0=r0: u_ref[t, r0:r0 + RELAYOUT_ROWS], RELAYOUT_ROWS, r0)
    to_chunk_rows(lambda t: uc_ref[t], nbatch * nctx, rows)

    wide = 2 * LANES
    fwd_lane = (lax.broadcasted_iota(jnp.int32, (1, wide), 1) % LANES) < (LANES // 2)
    tile = lambda k: pl.ds(pl.multiple_of(k * nbatch, nbatch), nbatch)

    npar = e_ref.shape[0]
    for g0 in range(0, groups, npar):
        gs = range(g0, g0 + npar)
        for j, g in enumerate(gs):
            e_ref[j] = jnp.dot(z_ref[g], m2_ref[g], preferred_element_type=F32)
        lam_r = [lam_ref[g, 0:1, :] for g in gs]
        lam_i = [lam_ref[g, 1:2, :] for g in gs]

        def advance(state, kf, kb):
            out = []
            for j, (sr, si) in enumerate(state):
                e2 = jnp.where(fwd_lane, e_ref[j, tile(kf), :], e_ref[j, tile(kb), :])
                out.append((lam_r[j] * sr - lam_i[j] * si + e2[:, :LANES],
                            lam_r[j] * si + lam_i[j] * sr + e2[:, LANES:]))
            return tuple(out)

        zero = jnp.zeros((nbatch, LANES), F32)
        state = tuple((zero, zero) for _ in gs)
        for i in range(nctx):
            state = advance(state, nchunk + i, nchunk + nctx - 1 - i)

        def body(i, state):
            kf, kb = i, nchunk - 1 - i
            for j, (sr, si) in enumerate(state):
                s = jnp.concatenate([sr, si], axis=-1)
                sf_ref[j, tile(kf), :] = s
                sb_ref[j, tile(kb), :] = s
            return advance(state, kf, kb)

        lax.fori_loop(0, nchunk, body, state, unroll=4)

        for j, g in enumerate(gs):
            y = (jnp.dot(z_ref[g, 0:rows, :], m1_ref[g], preferred_element_type=F32)
                 + jnp.dot(sf_ref[j].astype(BF16), m3f_ref[g], preferred_element_type=F32)
                 + jnp.dot(sb_ref[j].astype(BF16), m3b_ref[g], preferred_element_type=F32))
            z_ref[g, 0:rows, :] = _gelu_tanh(y).astype(BF16)

    for r0 in range(0, rows, RELAYOUT_ROWS):
        for h in range(2):
            ys = [z_ref[g, r0:r0 + RELAYOUT_ROWS, h * LANES:(h + 1) * LANES] for g in range(groups)]
            for t8, acc in enumerate(skew(ys)):
                if t8:
                    acc = lane_roll(acc, LANES - t8 * S5_C)
                o_ref[8 * h + t8, r0:r0 + RELAYOUT_ROWS] = acc


def _s5_scan(u_t, uc_t, bsz, m1, m2, m3f, m3b, lam):
    T, rows, width = u_t.shape
    ctx_rows = uc_t.shape[1]
    gpb = LANES // S5_C
    kern = functools.partial(_s5_kernel, nbatch=bsz, nchunk=rows // bsz, nctx=ctx_rows // bsz)
    one = pl.Buffered(1)
    mat = pl.BlockSpec((gpb, 2 * LANES, 2 * LANES), lambda j: (j, 0, 0), pipeline_mode=one)
    return pl.pallas_call(
        kern,
        grid=(width // LANES,),
        in_specs=[pl.BlockSpec((T, rows, LANES), lambda j: (0, 0, j)),
                  pl.BlockSpec((T, ctx_rows, LANES), lambda j: (0, 0, j)),
                  mat, mat, mat, mat,
                  pl.BlockSpec((gpb, 2, LANES), lambda j: (j, 0, 0))],
        out_specs=pl.BlockSpec((T, rows, LANES), lambda j: (0, 0, j), pipeline_mode=one),
        out_shape=jax.ShapeDtypeStruct(u_t.shape, BF16),
        scratch_shapes=[pltpu.VMEM((gpb, rows + ctx_rows, 2 * LANES), BF16),
                        pltpu.VMEM((SCAN_GROUPS, rows + ctx_rows, 2 * LANES), F32),
                        pltpu.VMEM((SCAN_GROUPS, rows, 2 * LANES), F32),
                        pltpu.VMEM((SCAN_GROUPS, rows, 2 * LANES), F32)],
        compiler_params=_cparams("arbitrary"),
        name="s5_scan",
    )(u_t, uc_t, m1, m2, m3f, m3b, lam)


def _dft_tables(n_seq):
    na_sig = n_seq // FFT_NB
    ka = np.arange(FFT_NA)[:, None]
    nb = np.arange(FFT_NB)[:, None, None]

    def stage1(n_in):
        na = np.arange(n_in)[None, :]
        ang = -2.0 * np.pi * (na * ka / FFT_NA)[None] - 2.0 * np.pi * (nb * ka[None]) / FFT_N
        return np.cos(ang), np.sin(ang)

    c, s = stage1(na_sig)
    f1 = np.concatenate([np.concatenate([c, -s], axis=2), np.concatenate([s, c], axis=2)], axis=1)
    c, s = stage1(FFT_NA)
    f1_real = np.concatenate([c, s], axis=1)
    kb = np.arange(FFT_NB)[:, None]
    nbv = np.arange(FFT_NB)[None, :]
    ang = -2.0 * np.pi * kb * nbv / FFT_NB
    c, s = np.cos(ang), np.sin(ang)
    f2 = np.block([[c, -s], [s, c]])
    f2_inv = np.block([[c, s], [-s, c]]) / FFT_NB
    nap = np.arange(na_sig)[:, None]
    kav = np.arange(FFT_NA)[None, :]
    ang = 2.0 * np.pi * (nap * kav / FFT_NA)[None] + 2.0 * np.pi * (nb * kav[None]) / FFT_N
    c, s = np.cos(ang) / FFT_NA, np.sin(ang) / FFT_NA
    f3 = np.concatenate([np.concatenate([c, -s], axis=2), np.concatenate([s, c], axis=2)], axis=1)
    as_bf16 = lambda a: jnp.asarray(a.astype(np.float32)).astype(BF16)
    return as_bf16(f1), as_bf16(f1_real), as_bf16(f2), as_bf16(f2_inv), as_bf16(f3)


def _conv3_slab(z_ref, plane, nb0, w, b, nbk):
    def rows(i):
        return z_ref[plane, 0, i].astype(F32)

    mid = [rows(nb0 + i) for i in range(SLAB)]
    na = mid[0].shape[0]
    ridx = lax.broadcasted_iota(jnp.int32, mid[0].shape, 0)
    wrapped = rows((nb0 + nbk - 1) % nbk)
    before = jnp.where(nb0 == 0, jnp.where(ridx == 0, 0.0, pltpu.roll(wrapped, 1, axis=0)), wrapped)
    wrapped = rows((nb0 + SLAB) % nbk)
    after = jnp.where(nb0 + SLAB == nbk, jnp.where(ridx == na - 1, 0.0, pltpu.roll(wrapped, na - 1, axis=0)),
                      wrapped)
    ext = [before] + mid + [after]
    return [ext[i] * w[0:1] + ext[i + 1] * w[1:2] + ext[i + 2] * w[2:3] + b for i in range(SLAB)]


def _for_row_slabs(nbk, fn):
    lax.fori_loop(0, nbk // SLAB, lambda j, c: (fn(pl.multiple_of(j * SLAB, SLAB)), c)[1], 0, unroll=2)


def _regroup(x):
    return jnp.swapaxes(x, 0, 1)


def _hyena_kernel(zv_ref, zg1_ref, zg2_ref, wv_ref, bv_ref, wg1_ref, bg1_ref, wg2_ref, bg2_ref,
                  kf_ref, f1_ref, f2_ref, f2i_ref, f3_ref, o_ref, s_ref, z1_ref,
                  *, nbk, na, ngrp, kag):
    step = pl.program_id(2)
    hb = FFT_NB

    def stage1(src_ref, conv):
        def slab(nb0):
            if conv:
                planes = [[v.astype(BF16) for v in _conv3_slab(src_ref, pln, nb0, wv_ref[...], bv_ref[...], nbk)]
                          for pln in range(2)]
            tiles = []
            for i in range(SLAB):
                nb = nb0 + i
                parts = [planes[0][i], planes[1][i]] if conv else [src_ref[0, 0, nb], src_ref[1, 0, nb]]
                rhs = jnp.concatenate(parts, axis=0)
                tiles.append(jnp.dot(f1_ref[nb], rhs, preferred_element_type=F32).astype(BF16))
            s_ref[:, pl.ds(nb0, SLAB), :] = _regroup(jnp.stack(tiles, axis=0))
        _for_row_slabs(nbk, slab)

    def stage3(gate_ref, wg, bg, dst_ref):
        def slab(nb0):
            rows = _regroup(s_ref[:, pl.ds(nb0, SLAB), :])
            gates = [_conv3_slab(gate_ref, pln, nb0, wg, bg, nbk) for pln in range(2)]
            for i in range(SLAB):
                nb = nb0 + i
                y = jnp.dot(f3_ref[nb], rows[i], preferred_element_type=F32)
                for pln in range(2):
                    dst_ref[pln, 0, nb] = (gates[pln][i] * y[pln * na:(pln + 1) * na]).astype(dst_ref.dtype)
        _for_row_slabs(nbk, slab)

    @pl.when(step == 0)
    def _():
        stage1(zv_ref, True)

    @pl.when(step == ngrp)
    def _():
        stage1(z1_ref, False)

    base = (step % ngrp) * kag
    loaded = [jnp.concatenate([s_ref[base + k], s_ref[FFT_NA + base + k]], axis=0) for k in range(kag)]
    results = []
    for k, rhs in enumerate(loaded):
        spec = jnp.dot(f2_ref[...], rhs, preferred_element_type=F32).astype(BF16)
        kf = kf_ref[0, k]
        sr, si = spec[:hb], spec[hb:]
        kr, ki = kf[:hb], kf[hb:]
        prod = jnp.concatenate([sr * kr - si * ki, sr * ki + si * kr], axis=0)
        results.append(jnp.dot(f2i_ref[...], prod, preferred_element_type=F32).astype(BF16))
    for k, back in enumerate(results):
        s_ref[base + k] = back[:hb]
        s_ref[FFT_NA + base + k] = back[hb:]

    @pl.when(step == ngrp - 1)
    def _():
        stage3(zg1_ref, wg1_ref[...], bg1_ref[...], z1_ref)

    @pl.when(step == 2 * ngrp - 1)
    def _():
        stage3(zg2_ref, wg2_ref[...], bg2_ref[...], o_ref)


def _hyena(z5, conv_w, conv_b, kf, tables, cb=MXU_N, kag=2 * SLAB):
    f1, _, f2, f2i, f3 = tables
    _, npair, nbk, na, _ = z5.shape
    c = kf.shape[-1]
    ncb = c // cb
    ngrp = FFT_NA // kag
    one = pl.Buffered(1)
    zspec = lambda grp: pl.BlockSpec((2, 1, nbk, na, cb), lambda p, j, s: (0, p, 0, 0, grp * ncb + j))
    wspec = lambda grp: pl.BlockSpec((3, cb), lambda p, j, s: (0, grp * ncb + j))
    bspec = lambda grp: pl.BlockSpec((1, cb), lambda p, j, s: (0, grp * ncb + j))
    const = lambda a: pl.BlockSpec(a.shape, lambda p, j, s: tuple(0 for _ in a.shape), pipeline_mode=one)
    cw = conv_w.astype(F32)
    cbias = conv_b.astype(F32).reshape(1, -1)
    return pl.pallas_call(
        functools.partial(_hyena_kernel, nbk=nbk, na=na, ngrp=ngrp, kag=kag),
        grid=(npair, ncb, 2 * ngrp),
        in_specs=[zspec(0), zspec(1), zspec(2), wspec(0), bspec(0), wspec(1), bspec(1), wspec(2), bspec(2),
                  pl.BlockSpec((1, kag, 2 * FFT_NB, cb), lambda p, j, s: (s // ngrp, s % ngrp, 0, j)),
                  const(f1), const(f2), const(f2i), const(f3)],
        out_specs=pl.BlockSpec((2, 1, nbk, na, cb), lambda p, j, s: (0, p, 0, 0, j), pipeline_mode=one),
        out_shape=jax.ShapeDtypeStruct((2, npair, nbk, na, c), BF16),
        scratch_shapes=[pltpu.VMEM((2 * FFT_NA, nbk, cb), BF16),
                        pltpu.VMEM((2, 1, nbk, na, cb), BF16)],
        compiler_params=_cparams("arbitrary", "arbitrary", "arbitrary"),
        name="hyena_conv",
    )(z5, z5, z5, cw, cbias, cw, cbias, cw, cbias, kf, f1, f2, f2i, f3)


def _filter_kernel(embt_ref, tv_ref, w1t_ref, b1_ref, w2t_ref, b2_ref, fr_ref, w3_ref, dec_ref, bias_ref,
                   f1_ref, f2_ref, k_ref, s_ref, h_ref, *, nbk, kag):
    first = (pl.program_id(0) == 0) & (pl.program_id(1) == 0)
    step = pl.program_id(2)
    half = FFT_NA // 2
    lanes_per_pass = 8 * FFT_NA

    @pl.when(first & (step == 0))
    def _():
        fr = fr_ref[...]
        for i in range(embt_ref.shape[1] // lanes_per_pass):
            cols = slice(i * lanes_per_pass, (i + 1) * lanes_per_pass)
            h = jnp.sin(fr * (jnp.dot(w1t_ref[...], embt_ref[:, cols], precision=HI,
                                      preferred_element_type=F32) + b1_ref[...]))
            h = jnp.sin(fr * (jnp.dot(w2t_ref[...], h, precision=HI, preferred_element_type=F32) + b2_ref[...]))
            h_ref[i * lanes_per_pass:(i + 1) * lanes_per_pass, :] = h.T.astype(BF16)

    @pl.when(step == 0)
    def _():
        dec = dec_ref[0]

        def slab(nb0):
            rows = h_ref[pl.ds(pl.multiple_of(nb0 * FFT_NA, SLAB * FFT_NA), SLAB * FFT_NA), :]
            rows = rows.reshape(SLAB, FFT_NA, rows.shape[-1])
            fwd = jnp.dot(rows[:, :half].reshape(SLAB * half, -1), w3_ref[0, 0], preferred_element_type=F32)
            bwd = jnp.dot(rows[:, half:].reshape(SLAB * half, -1), w3_ref[0, 1], preferred_element_type=F32)
            tiles = []
            for i in range(SLAB):
                nb = nb0 + i
                tv = tv_ref[nb]
                window = jnp.exp(-tv[:, 0:1] * dec) * tv[:, 1:2]
                taps = slice(i * half, (i + 1) * half)
                filt = jnp.concatenate([fwd[taps], bwd[taps]], axis=0) * window
                tiles.append(jnp.dot(f1_ref[nb], filt.astype(BF16), preferred_element_type=F32).astype(BF16))
            s_ref[:, pl.ds(nb0, SLAB), :] = _regroup(jnp.stack(tiles, axis=0))
        _for_row_slabs(nbk, slab)

    base = step * kag
    bias = bias_ref[0]
    for k in range(kag):
        rhs = jnp.concatenate([s_ref[base + k], s_ref[FFT_NA + base + k]], axis=0)
        spec = jnp.dot(f2_ref[...], rhs, preferred_element_type=F32)
        k_ref[0, k] = jnp.concatenate([spec[:FFT_NB] + bias, spec[FFT_NB:]], axis=0).astype(BF16)


def _filter_spectra(n, w1, b1, w2, b2, freq, w3, decay, bias, tables, cb=MXU_N, kag=4 * SLAB):
    _, f1_real, f2, _, _ = tables
    hid = w2.shape[0]
    c = decay.shape[-1]
    t = np.linspace(0.0, 1.0, n, dtype=np.float32)[:, None]
    w = (2.0 * np.pi * np.arange(n, dtype=np.float32) / n).astype(np.float32)
    bands = np.linspace(1e-4, HY_BANDS - 1, HY_BANDS, dtype=np.float32)
    ang = w[:, None] * bands[None, :]
    emb = np.concatenate([t, np.cos(ang), -np.sin(ang)], axis=-1).astype(np.float32)
    kpad = 64
    idx = FFT_NB * np.arange(FFT_NA)[None, :] + np.arange(FFT_NB)[:, None]
    lagi = np.minimum(np.where(idx < n, idx, FFT_N - idx), n - 1)
    embt = np.zeros((kpad, FFT_N), np.float32)
    embt[:emb.shape[1]] = emb[lagi.reshape(-1)].T
    tv = np.stack([t[lagi, 0], (idx != n).astype(np.float32)], axis=-1)
    col = lambda a: a.reshape(hid, 1).astype(F32)
    w1t = jnp.pad(w1.astype(F32), ((0, kpad - w1.shape[0]), (0, 0))).T
    w3r = w3.astype(BF16).reshape(hid, HY_ORDER, 2, c).transpose(1, 2, 0, 3)
    dec = jnp.abs(decay.astype(F32)).reshape(HY_ORDER, 1, c)
    one = pl.Buffered(1)
    const = lambda shape: pl.BlockSpec(shape, lambda o, j, s: tuple(0 for _ in shape), pipeline_mode=one)
    per_channel = pl.BlockSpec((1, 1, cb), lambda o, j, s: (o, 0, j))
    return pl.pallas_call(
        functools.partial(_filter_kernel, nbk=FFT_NB, kag=kag),
        grid=(HY_ORDER, c // cb, FFT_NA // kag),
        in_specs=[const(embt.shape), const(tv.shape), const((hid, kpad)), const((hid, 1)),
                  const((hid, hid)), const((hid, 1)), const((hid, 1)),
                  pl.BlockSpec((1, 2, hid, cb), lambda o, j, s: (o, 0, 0, j)),
                  per_channel, per_channel,
                  const(f1_real.shape), const(f2.shape)],
        out_specs=pl.BlockSpec((1, kag, 2 * FFT_NB, cb), lambda o, j, s: (o, s, 0, j)),
        out_shape=jax.ShapeDtypeStruct((HY_ORDER, FFT_NA, 2 * FFT_NB, c), BF16),
        scratch_shapes=[pltpu.VMEM((2 * FFT_NA, FFT_NB, cb), BF16),
                        pltpu.VMEM((FFT_N, hid), BF16)],
        compiler_params=_cparams("arbitrary", "arbitrary", "arbitrary"),
        name="hyena_filter",
    )(jnp.asarray(embt), jnp.asarray(tv), w1t, col(b1), w2.astype(F32).T, col(b2), col(freq), w3r, dec,
      bias.astype(F32).reshape(HY_ORDER, 1, c), f1_real, f2)


def _out_kernel(x_ref, pos_ref, mod_ref, ys_ref, yh_ref, gw_ref, gb_ref, g5_ref, gh_ref, wo_ref,
                g2_ref, w1_ref, w2_ref, gf_ref, o_ref, *, d):
    nbatch, tokens, _ = x_ref.shape
    rows = nbatch * tokens
    m = mod_ref[...]
    gate1, shift2 = m[:, :, 2 * d:3 * d], m[:, :, 3 * d:4 * d]
    scale2, gate2 = m[:, :, 4 * d:5 * d], m[:, :, 5 * d:6 * d]
    per_batch = lambda v: v.reshape(nbatch, tokens, d)
    h = x_ref[...] + pos_ref[...][None]
    ys = _from_slot_major(ys_ref[...], nbatch)
    ab = jnp.dot(ys, gw_ref[...], preferred_element_type=F32) + gb_ref[...]
    half = ab.shape[-1] // 2
    y5 = ab[:, :half] * jax.nn.sigmoid(ab[:, half:])
    yh = yh_ref[...].reshape(rows, -1).astype(F32)
    mix = jnp.concatenate([_rms(y5, g5_ref[...]), _rms(yh, gh_ref[...])], axis=-1)
    h = h + gate1 * per_batch(jnp.dot(mix.astype(BF16), wo_ref[...], preferred_element_type=F32))
    hn = _rms(h, g2_ref[...][None]) * (1.0 + scale2) + shift2
    hid = jnp.dot(hn.reshape(rows, d).astype(BF16), w1_ref[...], preferred_element_type=F32)
    hid = jnp.square(jnp.maximum(hid, 0.0))
    h = h + gate2 * per_batch(jnp.dot(hid.astype(BF16), w2_ref[...], preferred_element_type=F32))
    o_ref[...] = _rms(h, gf_ref[...][None])


def _output_stage(x, pos, mod3, ys_t, yh, glu_w, glu_b, g5, gh, w_out, g2, w1, w2, gf, tt=64):
    bsz, n, d = x.shape
    dh = ys_t.shape[-1]
    one = pl.Buffered(1)
    const = lambda a: pl.BlockSpec(a.shape, lambda i: tuple(0 for _ in a.shape), pipeline_mode=one)
    row = lambda a: a.reshape(1, -1).astype(F32)
    glu_b, g5, gh, g2, gf = row(glu_b), row(g5), row(gh), row(g2), row(gf)
    return pl.pallas_call(
        functools.partial(_out_kernel, d=d),
        grid=(n // tt,),
        in_specs=[pl.BlockSpec((bsz, tt, d), lambda i: (0, i, 0)),
                  pl.BlockSpec((tt, d), lambda i: (i, 0)),
                  pl.BlockSpec((bsz, 1, mod3.shape[-1]), lambda i: (0, 0, 0)),
                  pl.BlockSpec((S5_T, tt // S5_T * bsz, dh), lambda i: (0, i, 0)),
                  pl.BlockSpec((bsz, tt, yh.shape[-1]), lambda i: (0, i, 0)),
                  const(glu_w), const(glu_b), const(g5), const(gh), const(w_out), const(g2),
                  const(w1), const(w2), const(gf)],
        out_specs=pl.BlockSpec((bsz, tt, d), lambda i: (0, i, 0)),
        out_shape=jax.ShapeDtypeStruct((bsz, n, d), F32),
        compiler_params=_cparams("arbitrary"),
        name="mix_mlp_out",
    )(x, pos, mod3, ys_t, yh, glu_w, glu_b, g5, gh, w_out, g2, w1, w2, gf)


def _pos_table(n, d):
    rows = n // GRID_W
    row = np.repeat(np.arange(rows, dtype=np.float32), GRID_W)
    col = np.tile(np.arange(GRID_W, dtype=np.float32), rows)
    quarter = d // 4
    omega = (1.0 / (POS_BASE ** (np.arange(quarter, dtype=np.float32) / quarter))).astype(np.float32)

    def enc(p):
        ang = p[:, None] * omega[None, :]
        return np.concatenate([np.sin(ang), np.cos(ang)], axis=-1)

    return jnp.asarray(np.concatenate([enc(row), enc(col)], axis=-1).astype(np.float32))


def kernel(x, c, ctx, c_ctx, ada_w, ada_b, norm1_g, w_in, s5_a_re, s5_a_im, s5_log_step, s5_b_re,
           s5_b_im, s5_c_re, s5_c_im, s5_d, s5_glu_w, s5_glu_b, hy_conv_w, hy_conv_b, hy_f_w1,
           hy_f_b1, hy_f_w2, hy_f_b2, hy_f_freq, hy_f_w3, hy_decay, hy_bias, mix_g_s5, mix_g_hy,
           w_out, norm2_g, mlp_w1, mlp_w2, final_g):
    bsz, n, d = x.shape
    depth = ada_w.shape[0]
    d_s5 = s5_d.shape[-1]
    d_hy = hy_bias.shape[-1]
    nctx = ctx.shape[1]
    npair = bsz // 2
    nbk, na = FFT_NB, n // FFT_NB
    pos = _pos_table(n, d)
    tables = _dft_tables(n)

    assert depth == 1 and bsz % 2 == 0 and n % (FFT_NB * 8) == 0 and 2 * n == FFT_N
    mod_rows = 16
    c_rows = jnp.concatenate([c, c_ctx[None], jnp.zeros((mod_rows - bsz - 1, d), c.dtype)], axis=0)
    mod3 = _modulation(c_rows, ada_w[0], ada_b[0]).reshape(mod_rows, 1, N_MOD * d)

    w_in_b = w_in[0].astype(BF16)
    u_t, z = _project(x, pos, mod3, None, norm1_g[0], w_in_b, d_s5, 3 * d_hy, tt=128)
    (uc_t,) = _project(ctx, None, mod3, bsz, norm1_g[0], w_in_b, d_s5, 0, tt=128)

    s5_mats = _s5_tables(s5_a_re[0], s5_a_im[0], s5_log_step[0], s5_b_re[0], s5_b_im[0],
                         s5_c_re[0], s5_c_im[0], s5_d[0])
    ys_t = _s5_scan(u_t, uc_t, bsz, *s5_mats)

    kf = _filter_spectra(n, hy_f_w1[0], hy_f_b1[0], hy_f_w2[0], hy_f_b2[0], hy_f_freq[0],
                         hy_f_w3[0], hy_decay[0], hy_bias[0], tables)
    z5 = z.reshape(bsz, na, nbk, 3 * d_hy).transpose(0, 2, 1, 3).reshape(2, npair, nbk, na, 3 * d_hy)
    yh = _hyena(z5, hy_conv_w[0], hy_conv_b[0], kf, tables)
    yh = yh.reshape(bsz, nbk, na, d_hy).transpose(0, 2, 1, 3).reshape(bsz, n, d_hy)

    return _output_stage(x, pos, mod3, ys_t, yh, s5_glu_w[0].astype(BF16), s5_glu_b[0], mix_g_s5[0],
                         mix_g_hy[0], w_out[0].astype(BF16), norm2_g[0], mlp_w1[0].astype(BF16),
                         mlp_w2[0].astype(BF16), final_g)
```

```python
import functools
import math

import numpy as np
import jax
import jax.numpy as jnp
from jax import lax
from jax.experimental import pallas as pl
from jax.experimental.pallas import tpu as pltpu

F32 = jnp.float32
BF16 = jnp.bfloat16
HI = lax.Precision.HIGHEST

EPS = 1e-6
GRID_W = 64
POS_BASE = 10000.0
N_MOD = 6
S5_C = 16
S5_P = 64
S5_T = 16
HY_BANDS = 16
HY_ORDER = 2

FFT_N = 8192
FFT_NA = 128
FFT_NB = 64

LANES = 128
MXU_N = 256
SLAB = 16
RELAYOUT_ROWS = 256
SCAN_GROUPS = 2
FF_CHUNKS = 2
VMEM_LIMIT = 56 * 1024 * 1024


def _cparams(*sem):
    return pltpu.CompilerParams(dimension_semantics=sem, vmem_limit_bytes=VMEM_LIMIT)


def _rms(x, g):
    return x * lax.rsqrt(jnp.mean(x * x, axis=-1, keepdims=True) + EPS) * g


def _mod_kernel(c_ref, w_ref, b_ref, o_ref):
    c = c_ref[...]
    a = c * jax.nn.sigmoid(c)
    o_ref[...] = jnp.dot(a.astype(BF16), w_ref[...].astype(BF16),
                         preferred_element_type=F32) + b_ref[...]


def _modulation(c_rows, ada_w, ada_b):
    rows, d = c_rows.shape
    n = ada_w.shape[1]
    bn = 1024
    return pl.pallas_call(
        _mod_kernel,
        grid=(n // bn,),
        in_specs=[pl.BlockSpec((rows, d), lambda j: (0, 0)),
                  pl.BlockSpec((d, bn), lambda j: (0, j)),
                  pl.BlockSpec((1, bn), lambda j: (0, j))],
        out_specs=pl.BlockSpec((rows, bn), lambda j: (0, j)),
        out_shape=jax.ShapeDtypeStruct((rows, n), F32),
        compiler_params=_cparams("arbitrary"),
        name="ada_mod",
    )(c_rows, ada_w, ada_b.reshape(1, n))


def _to_slot_major(v, nbatch):
    nchunk = v.shape[0] // (nbatch * S5_T)
    pieces = v.reshape(nbatch * nchunk, S5_T, v.shape[1])
    by_chunk = jnp.stack([pieces[b * nchunk + k] for k in range(nchunk) for b in range(nbatch)], axis=0)
    return _regroup(by_chunk)


def _from_slot_major(slots, nbatch):
    nchunk = slots.shape[1] // nbatch
    by_chunk = _regroup(slots)
    pieces = jnp.stack([by_chunk[k * nbatch + b] for b in range(nbatch) for k in range(nchunk)], axis=0)
    return pieces.reshape(nbatch * nchunk * S5_T, slots.shape[2])


def _proj_kernel(*refs, d, use_pos, d_s5):
    if use_pos:
        x_ref, pos_ref, mod_ref, g_ref, w_ref = refs[:5]
        outs = refs[5:]
        h = x_ref[...] + pos_ref[...][None]
    else:
        x_ref, mod_ref, g_ref, w_ref = refs[:4]
        outs = refs[4:]
        h = x_ref[...]
    nbatch, tokens, _ = h.shape
    m = mod_ref[...]
    hn = _rms(h, g_ref[...][None]) * (1.0 + m[:, :, d:2 * d]) + m[:, :, 0:d]
    proj = jnp.dot(hn.reshape(nbatch * tokens, d).astype(BF16), w_ref[...], preferred_element_type=F32)
    outs[0][...] = _to_slot_major(proj[:, :d_s5].astype(BF16), nbatch)
    if len(outs) > 1:
        outs[1][...] = proj[:, d_s5:].reshape(nbatch, tokens, -1).astype(BF16)


def _project(x, pos, mod3, mod_row, g, w_bf16, d_s5, d_rest, tt):
    bsz, n, d = x.shape
    use_pos = pos is not None
    in_specs = [pl.BlockSpec((bsz, tt, d), lambda i: (0, i, 0))]
    args = [x]
    if use_pos:
        in_specs.append(pl.BlockSpec((tt, d), lambda i: (i, 0)))
        args.append(pos)
    mod_rows = bsz if mod_row is None else 1
    in_specs += [pl.BlockSpec((mod_rows, 1, mod3.shape[-1]), lambda i: (0 if mod_row is None else mod_row, 0, 0)),
                 pl.BlockSpec((1, d), lambda i: (0, 0)),
                 pl.BlockSpec((d, d_s5 + d_rest), lambda i: (0, 0))]
    args += [mod3, g.reshape(1, d), w_bf16]
    rows = tt // S5_T * bsz
    out_specs = [pl.BlockSpec((S5_T, rows, d_s5), lambda i: (0, i, 0))]
    out_shape = [jax.ShapeDtypeStruct((S5_T, n // S5_T * bsz, d_s5), BF16)]
    if d_rest:
        out_specs.append(pl.BlockSpec((bsz, tt, d_rest), lambda i: (0, i, 0)))
        out_shape.append(jax.ShapeDtypeStruct((bsz, n, d_rest), BF16))
    return pl.pallas_call(
        functools.partial(_proj_kernel, d=d, use_pos=use_pos, d_s5=d_s5),
        grid=(n // tt,),
        in_specs=in_specs,
        out_specs=out_specs,
        out_shape=out_shape,
        compiler_params=_cparams("arbitrary"),
        name="norm_proj",
    )(*args)


def _s5_tables(a_re, a_im, log_step, b_re, b_im, c_re, c_im, d_skip):
    T, C, P = S5_T, S5_C, S5_P
    G = a_re.shape[1]
    gpb = LANES // C
    slots = np.arange(T)
    tok = np.stack([8 * (slots // 8) + (slots % 8 - o) % 8 for o in range(gpb)])
    tok_g = tok[np.arange(G) % gpb]

    step = jnp.exp(log_step.astype(F32))[..., None]
    ar = a_re.astype(F32) * step
    ai = a_im.astype(F32) * step

    def powers(d, expo):
        e = jnp.asarray(np.broadcast_to(expo, (G, expo.shape[-1])).astype(np.float32))[:, :, None]
        mag = jnp.exp(ar[d][:, None, :] * e)
        return mag * jnp.cos(ai[d][:, None, :] * e), mag * jnp.sin(ai[d][:, None, :] * e)

    lr, li = a_re.astype(F32), a_im.astype(F32)
    first = [powers(d, np.ones(1)) for d in range(2)]
    nr = jnp.stack([first[d][0][:, 0] for d in range(2)]) - 1.0
    ni = jnp.stack([first[d][1][:, 0] for d in range(2)])
    den = lr * lr + li * li
    qr = (nr * lr + ni * li) / den
    qi = (ni * lr - nr * li) / den
    bbr = qr[..., None] * b_re - qi[..., None] * b_im
    bbi = qr[..., None] * b_im + qi[..., None] * b_re
    cr, ci = c_re.astype(F32), c_im.astype(F32)

    kerns = []
    for d in range(2):
        pr, pi = powers(d, np.arange(T))
        wr = cr[d][:, None] * pr[:, :, None, :] - ci[d][:, None] * pi[:, :, None, :]
        wi = cr[d][:, None] * pi[:, :, None, :] + ci[d][:, None] * pr[:, :, None, :]
        kerns.append(jnp.einsum('gtcp,gpe->gtce', wr, bbr[d], precision=HI)
                     - jnp.einsum('gtcp,gpe->gtce', wi, bbi[d], precision=HI))
    kcat = jnp.concatenate(kerns, axis=1)
    kpad = jnp.pad(kcat.reshape(G, 2 * T * C, C), ((0, 0), (0, 0), (0, LANES - C)))
    d_lanes = jnp.tile(d_skip.astype(F32).reshape(G, 1, C), (1, 1, 2 * T))
    m1 = pl.pallas_call(
        _s5_intra_kernel,
        grid=(G // gpb,),
        in_specs=[pl.BlockSpec((gpb, 2 * T * C, LANES), lambda s: (s, 0, 0)),
                  pl.BlockSpec((gpb, 1, 2 * T * C), lambda s: (s, 0, 0))],
        out_specs=pl.BlockSpec((gpb, T * C, T * C), lambda s: (s, 0, 0)),
        out_shape=jax.ShapeDtypeStruct((G, T * C, T * C), BF16),
        compiler_params=_cparams("arbitrary"),
        name="s5_intra_table",
    )(kpad, d_lanes)

    def in_mat(d, expo):
        p_r, p_i = powers(d, expo)
        br, bi = bbr[d].transpose(0, 2, 1)[:, None], bbi[d].transpose(0, 2, 1)[:, None]
        er = p_r[:, :, None, :] * br - p_i[:, :, None, :] * bi
        ei = p_r[:, :, None, :] * bi + p_i[:, :, None, :] * br
        return er.reshape(G, T * C, P), ei.reshape(G, T * C, P)

    efr, efi = in_mat(0, T - 1 - tok_g)
    ebr, ebi = in_mat(1, tok_g)
    m2 = jnp.concatenate([efr, ebr, efi, ebi], axis=-1)

    def out_mat(d, expo):
        p_r, p_i = powers(d, expo)
        crt = cr[d].transpose(0, 2, 1)[:, :, None, :]
        cit = ci[d].transpose(0, 2, 1)[:, :, None, :]
        prt = p_r.transpose(0, 2, 1)[:, :, :, None]
        pit = p_i.transpose(0, 2, 1)[:, :, :, None]
        vr = crt * prt - cit * pit
        vi = crt * pit + cit * prt
        return vr.reshape(G, P, T * C), -vi.reshape(G, P, T * C)

    vfr, vfi = out_mat(0, tok_g + 1)
    vbr, vbi = out_mat(1, T - tok_g)
    m3 = jnp.concatenate([vfr, vbr, vfi, vbi], axis=1)

    last = [powers(d, np.full(1, T)) for d in range(2)]
    lam = jnp.stack([jnp.concatenate([last[0][c][:, 0], last[1][c][:, 0]], axis=-1)
                     for c in range(2)], axis=1)
    is_fwd = jnp.asarray((np.arange(4 * P) // P) % 2 == 0)[None, :, None]
    m3f = jnp.where(is_fwd, m3, 0.0).astype(BF16)
    m3b = jnp.where(is_fwd, 0.0, m3).astype(BF16)
    return m1, m2.astype(BF16), m3f, m3b, lam


def _s5_intra_kernel(k_ref, d_ref, o_ref):
    T, C = S5_T, S5_C
    wide = 2 * T * C
    row = lax.broadcasted_iota(jnp.int32, (C, wide), 0)
    lane = lax.broadcasted_iota(jnp.int32, (C, wide), 1)
    for g in range(k_ref.shape[0]):
        k = k_ref[g]
        shift = C
        while shift < LANES:
            k = k + pltpu.roll(k, shift, axis=1)
            shift *= 2
        k = jnp.concatenate([k] * (wide // LANES), axis=1)
        block = lambda x, k=k: k[x * C:(x + 1) * C]
        strip = block(0) + block(T) + jnp.where(row == lane % C, d_ref[g], 0.0)
        for j in range(2 * T - 1):
            if j != T - 1:
                x = T - 1 - j if j < T - 1 else T + (j - (T - 1))
                strip = jnp.where(lane // C == j, block(x), strip)
        block_rows = []
        for l in range(T):
            t = 8 * (l // 8) + (l % 8 - g) % 8
            start = (T - 1 - t) * C
            window = pltpu.roll(strip, (wide - start) % wide, axis=1) if start else strip
            block_rows.append(window[:, :T * C])
        m = jnp.concatenate(block_rows, axis=0)
        if g:
            m = jnp.concatenate([pltpu.roll(m[:, h * LANES:(h + 1) * LANES], g * C, axis=1) for h in range(2)],
                                axis=1)
        o_ref[g] = m.T.astype(BF16)


def _gelu_tanh(x):
    return 0.5 * x * (1.0 + jnp.tanh(math.sqrt(2.0 / math.pi) * (x + 0.044715 * (x * x * x))))


def _s5_kernel(u_ref, uc_ref, m1_ref, m2_ref, m3f_ref, m3b_ref, lam_ref, o_ref,
               z_ref, e_ref, sf_ref, sb_ref, *, nbatch, nchunk, nctx):
    groups = LANES // S5_C
    rows = nbatch * nchunk
    lane = lax.broadcasted_iota(jnp.int32, (1, LANES), 1)
    slot_bits = [(((lane // S5_C) >> j) & 1) == 1 for j in range(3)]

    def skew(xs):
        cur = [xs[(-k) % groups] for k in range(groups)]
        for j, bit in enumerate(slot_bits):
            cur = [jnp.where(bit, cur[(i - (1 << j)) % groups], cur[i]) for i in range(groups)]
        return cur

    def lane_roll(v, shift):
        if v.dtype != BF16:
            return pltpu.roll(v, shift, axis=1)
        return pltpu.bitcast(pltpu.roll(pltpu.bitcast(v, jnp.uint32), shift, axis=1), BF16)

    def to_chunk_rows(read_t, nrows, row0):
        for h in range(2):
            rolled = []
            for t8 in range(groups):
                v = read_t(8 * h + t8)
                rolled.append(lane_roll(v, t8 * S5_C) if t8 else v)
            for g, zg in enumerate(skew(rolled)):
                z_ref[g, row0:row0 + nrows, h * LANES:(h + 1) * LANES] = zg

    for r0 in range(0, rows, RELAYOUT_ROWS):
        to_chunk_rows(lambda t, r0=r0: u_ref[t, r0:r0 + RELAYOUT_ROWS], RELAYOUT_ROWS, r0)
    to_chunk_rows(lambda t: uc_ref[t], nbatch * nctx, rows)

    wide = 2 * LANES
    fwd_lane = (lax.broadcasted_iota(jnp.int32, (1, wide), 1) % LANES) < (LANES // 2)
    tile = lambda k: pl.ds(pl.multiple_of(k * nbatch, nbatch), nbatch)

    npar = e_ref.shape[0]
    for g0 in range(0, groups, npar):
        gs = range(g0, g0 + npar)
        for j, g in enumerate(gs):
            e_ref[j] = jnp.dot(z_ref[g], m2_ref[g], preferred_element_type=F32)
        lam_r = [lam_ref[g, 0:1, :] for g in gs]
        lam_i = [lam_ref[g, 1:2, :] for g in gs]

        def advance(state, kf, kb):
            out = []
            for j, (sr, si) in enumerate(state):
                e2 = jnp.where(fwd_lane, e_ref[j, tile(kf), :], e_ref[j, tile(kb), :])
                out.append((lam_r[j] * sr - lam_i[j] * si + e2[:, :LANES],
                            lam_r[j] * si + lam_i[j] * sr + e2[:, LANES:]))
            return tuple(out)

        zero = jnp.zeros((nbatch, LANES), F32)
        state = tuple((zero, zero) for _ in gs)
        for i in range(nctx):
            state = advance(state, nchunk + i, nchunk + nctx - 1 - i)

        def body(i, state):
            kf, kb = i, nchunk - 1 - i
            for j, (sr, si) in enumerate(state):
                s = jnp.concatenate([sr, si], axis=-1)
                sf_ref[j, tile(kf), :] = s
                sb_ref[j, tile(kb), :] = s
            return advance(state, kf, kb)

        lax.fori_loop(0, nchunk, body, state, unroll=4)

        for j, g in enumerate(gs):
            y = (jnp.dot(z_ref[g, 0:rows, :], m1_ref[g], preferred_element_type=F32)
                 + jnp.dot(sf_ref[j].astype(BF16), m3f_ref[g], preferred_element_type=F32)
                 + jnp.dot(sb_ref[j].astype(BF16), m3b_ref[g], preferred_element_type=F32))
            z_ref[g, 0:rows, :] = _gelu_tanh(y).astype(BF16)

    for r0 in range(0, rows, RELAYOUT_ROWS):
        for h in range(2):
            ys = [z_ref[g, r0:r0 + RELAYOUT_ROWS, h * LANES:(h + 1) * LANES] for g in range(groups)]
            for t8, acc in enumerate(skew(ys)):
                if t8:
                    acc = lane_roll(acc, LANES - t8 * S5_C)
                o_ref[8 * h + t8, r0:r0 + RELAYOUT_ROWS] = acc


def _s5_scan(u_t, uc_t, bsz, m1, m2, m3f, m3b, lam):
    T, rows, width = u_t.shape
    ctx_rows = uc_t.shape[1]
    gpb = LANES // S5_C
    kern = functools.partial(_s5_kernel, nbatch=bsz, nchunk=rows // bsz, nctx=ctx_rows // bsz)
    one = pl.Buffered(1)
    mat = pl.BlockSpec((gpb, 2 * LANES, 2 * LANES), lambda j: (j, 0, 0), pipeline_mode=one)
    return pl.pallas_call(
        kern,
        grid=(width // LANES,),
        in_specs=[pl.BlockSpec((T, rows, LANES), lambda j: (0, 0, j)),
                  pl.BlockSpec((T, ctx_rows, LANES), lambda j: (0, 0, j)),
                  mat, mat, mat, mat,
                  pl.BlockSpec((gpb, 2, LANES), lambda j: (j, 0, 0))],
        out_specs=pl.BlockSpec((T, rows, LANES), lambda j: (0, 0, j), pipeline_mode=one),
        out_shape=jax.ShapeDtypeStruct(u_t.shape, BF16),
        scratch_shapes=[pltpu.VMEM((gpb, rows + ctx_rows, 2 * LANES), BF16),
                        pltpu.VMEM((SCAN_GROUPS, rows + ctx_rows, 2 * LANES), F32),
                        pltpu.VMEM((SCAN_GROUPS, rows, 2 * LANES), F32),
                        pltpu.VMEM((SCAN_GROUPS, rows, 2 * LANES), F32)],
        compiler_params=_cparams("arbitrary"),
        name="s5_scan",
    )(u_t, uc_t, m1, m2, m3f, m3b, lam)


def _dft_tables(n_seq):
    na_sig = n_seq // FFT_NB
    ka = np.arange(FFT_NA)[:, None]
    nb = np.arange(FFT_NB)[:, None, None]

    def stage1(n_in):
        na = np.arange(n_in)[None, :]
        ang = -2.0 * np.pi * (na * ka / FFT_NA)[None] - 2.0 * np.pi * (nb * ka[None]) / FFT_N
        return np.cos(ang), np.sin(ang)

    c, s = stage1(na_sig)
    f1 = np.concatenate([np.concatenate([c, -s], axis=2), np.concatenate([s, c], axis=2)], axis=1)
    c, s = stage1(FFT_NA)
    f1_real = np.concatenate([c, s], axis=1)
    kb = np.arange(FFT_NB)[:, None]
    nbv = np.arange(FFT_NB)[None, :]
    ang = -2.0 * np.pi * kb * nbv / FFT_NB
    c, s = np.cos(ang), np.sin(ang)
    f2 = np.block([[c, -s], [s, c]])
    f2_inv = np.block([[c, s], [-s, c]]) / FFT_NB
    nap = np.arange(na_sig)[:, None]
    kav = np.arange(FFT_NA)[None, :]
    ang = 2.0 * np.pi * (nap * kav / FFT_NA)[None] + 2.0 * np.pi * (nb * kav[None]) / FFT_N
    c, s = np.cos(ang) / FFT_NA, np.sin(ang) / FFT_NA
    f3 = np.concatenate([np.concatenate([c, -s], axis=2), np.concatenate([s, c], axis=2)], axis=1)
    as_bf16 = lambda a: jnp.asarray(a.astype(np.float32)).astype(BF16)
    return as_bf16(f1), as_bf16(f1_real), as_bf16(f2), as_bf16(f2_inv), as_bf16(f3)


def _conv3_slab(z_ref, plane, nb0, w, b, nbk):
    def rows(i):
        return z_ref[plane, 0, i].astype(F32)

    mid = [rows(nb0 + i) for i in range(SLAB)]
    na = mid[0].shape[0]
    ridx = lax.broadcasted_iota(jnp.int32, mid[0].shape, 0)
    wrapped = rows((nb0 + nbk - 1) % nbk)
    before = jnp.where(nb0 == 0, jnp.where(ridx == 0, 0.0, pltpu.roll(wrapped, 1, axis=0)), wrapped)
    wrapped = rows((nb0 + SLAB) % nbk)
    after = jnp.where(nb0 + SLAB == nbk, jnp.where(ridx == na - 1, 0.0, pltpu.roll(wrapped, na - 1, axis=0)),
                      wrapped)
    ext = [before] + mid + [after]
    return [ext[i] * w[0:1] + ext[i + 1] * w[1:2] + ext[i + 2] * w[2:3] + b for i in range(SLAB)]


def _for_row_slabs(nbk, fn):
    lax.fori_loop(0, nbk // SLAB, lambda j, c: (fn(pl.multiple_of(j * SLAB, SLAB)), c)[1], 0, unroll=2)


def _regroup(x):
    return jnp.swapaxes(x, 0, 1)


def _hyena_kernel(zv_ref, zg1_ref, zg2_ref, wv_ref, bv_ref, wg1_ref, bg1_ref, wg2_ref, bg2_ref,
                  kf_ref, f1_ref, f2_ref, f2i_ref, f3_ref, o_ref, s_ref, z1_ref,
                  *, nbk, na, ngrp, kag):
    step = pl.program_id(2)
    hb = FFT_NB

    def stage1(src_ref, conv):
        def slab(nb0):
            if conv:
                planes = [[v.astype(BF16) for v in _conv3_slab(src_ref, pln, nb0, wv_ref[...], bv_ref[...], nbk)]
                          for pln in range(2)]
            tiles = []
            for i in range(SLAB):
                nb = nb0 + i
                parts = [planes[0][i], planes[1][i]] if conv else [src_ref[0, 0, nb], src_ref[1, 0, nb]]
                rhs = jnp.concatenate(parts, axis=0)
                tiles.append(jnp.dot(f1_ref[nb], rhs, preferred_element_type=F32).astype(BF16))
            s_ref[:, pl.ds(nb0, SLAB), :] = _regroup(jnp.stack(tiles, axis=0))
        _for_row_slabs(nbk, slab)

    def stage3(gate_ref, wg, bg, dst_ref):
        def slab(nb0):
            rows = _regroup(s_ref[:, pl.ds(nb0, SLAB), :])
            gates = [_conv3_slab(gate_ref, pln, nb0, wg, bg, nbk) for pln in range(2)]
            for i in range(SLAB):
                nb = nb0 + i
                y = jnp.dot(f3_ref[nb], rows[i], preferred_element_type=F32)
                for pln in range(2):
                    dst_ref[pln, 0, nb] = (gates[pln][i] * y[pln * na:(pln + 1) * na]).astype(dst_ref.dtype)
        _for_row_slabs(nbk, slab)

    @pl.when(step == 0)
    def _():
        stage1(zv_ref, True)

    @pl.when(step == ngrp)
    def _():
        stage1(z1_ref, False)

    base = (step % ngrp) * kag
    loaded = [jnp.concatenate([s_ref[base + k], s_ref[FFT_NA + base + k]], axis=0) for k in range(kag)]
    results = []
    for k, rhs in enumerate(loaded):
        spec = jnp.dot(f2_ref[...], rhs, preferred_element_type=F32).astype(BF16)
        kf = kf_ref[0, k]
        sr, si = spec[:hb], spec[hb:]
        kr, ki = kf[:hb], kf[hb:]
        prod = jnp.concatenate([sr * kr - si * ki, sr * ki + si * kr], axis=0)
        results.append(jnp.dot(f2i_ref[...], prod, preferred_element_type=F32).astype(BF16))
    for k, back in enumerate(results):
        s_ref[base + k] = back[:hb]
        s_ref[FFT_NA + base + k] = back[hb:]

    @pl.when(step == ngrp - 1)
    def _():
        stage3(zg1_ref, wg1_ref[...], bg1_ref[...], z1_ref)

    @pl.when(step == 2 * ngrp - 1)
    def _():
        stage3(zg2_ref, wg2_ref[...], bg2_ref[...], o_ref)


def _hyena(z5, conv_w, conv_b, kf, tables, cb=MXU_N, kag=2 * SLAB):
    f1, _, f2, f2i, f3 = tables
    _, npair, nbk, na, _ = z5.shape
    c = kf.shape[-1]
    ncb = c // cb
    ngrp = FFT_NA // kag
    one = pl.Buffered(1)
    zspec = lambda grp: pl.BlockSpec((2, 1, nbk, na, cb), lambda p, j, s: (0, p, 0, 0, grp * ncb + j))
    wspec = lambda grp: pl.BlockSpec((3, cb), lambda p, j, s: (0, grp * ncb + j))
    bspec = lambda grp: pl.BlockSpec((1, cb), lambda p, j, s: (0, grp * ncb + j))
    const = lambda a: pl.BlockSpec(a.shape, lambda p, j, s: tuple(0 for _ in a.shape), pipeline_mode=one)
    cw = conv_w.astype(F32)
    cbias = conv_b.astype(F32).reshape(1, -1)
    return pl.pallas_call(
        functools.partial(_hyena_kernel, nbk=nbk, na=na, ngrp=ngrp, kag=kag),
        grid=(npair, ncb, 2 * ngrp),
        in_specs=[zspec(0), zspec(1), zspec(2), wspec(0), bspec(0), wspec(1), bspec(1), wspec(2), bspec(2),
                  pl.BlockSpec((1, kag, 2 * FFT_NB, cb), lambda p, j, s: (s // ngrp, s % ngrp, 0, j)),
                  const(f1), const(f2), const(f2i), const(f3)],
        out_specs=pl.BlockSpec((2, 1, nbk, na, cb), lambda p, j, s: (0, p, 0, 0, j), pipeline_mode=one),
        out_shape=jax.ShapeDtypeStruct((2, npair, nbk, na, c), BF16),
        scratch_shapes=[pltpu.VMEM((2 * FFT_NA, nbk, cb), BF16),
                        pltpu.VMEM((2, 1, nbk, na, cb), BF16)],
        compiler_params=_cparams("arbitrary", "arbitrary", "arbitrary"),
        name="hyena_conv",
    )(z5, z5, z5, cw, cbias, cw, cbias, cw, cbias, kf, f1, f2, f2i, f3)


def _filter_kernel(embt_ref, tv_ref, w1t_ref, b1_ref, w2t_ref, b2_ref, fr_ref, w3_ref, dec_ref, bias_ref,
                   f1_ref, f2_ref, k_ref, s_ref, h_ref, *, nbk, kag):
    first = (pl.program_id(0) == 0) & (pl.program_id(1) == 0)
    step = pl.program_id(2)
    half = FFT_NA // 2
    lanes_per_pass = 8 * FFT_NA

    @pl.when(first & (step == 0))
    def _():
        fr = fr_ref[...]
        for i in range(embt_ref.shape[1] // lanes_per_pass):
            cols = slice(i * lanes_per_pass, (i + 1) * lanes_per_pass)
            h = jnp.sin(fr * (jnp.dot(w1t_ref[...], embt_ref[:, cols], precision=HI,
                                      preferred_element_type=F32) + b1_ref[...]))
            h = jnp.sin(fr * (jnp.dot(w2t_ref[...], h, precision=HI, preferred_element_type=F32) + b2_ref[...]))
            h_ref[i * lanes_per_pass:(i + 1) * lanes_per_pass, :] = h.T.astype(BF16)

    @pl.when(step == 0)
    def _():
        dec = dec_ref[0]

        def slab(nb0):
            rows = h_ref[pl.ds(pl.multiple_of(nb0 * FFT_NA, SLAB * FFT_NA), SLAB * FFT_NA), :]
            rows = rows.reshape(SLAB, FFT_NA, rows.shape[-1])
            fwd = jnp.dot(rows[:, :half].reshape(SLAB * half, -1), w3_ref[0, 0], preferred_element_type=F32)
            bwd = jnp.dot(rows[:, half:].reshape(SLAB * half, -1), w3_ref[0, 1], preferred_element_type=F32)
            tiles = []
            for i in range(SLAB):
                nb = nb0 + i
                tv = tv_ref[nb]
                window = jnp.exp(-tv[:, 0:1] * dec) * tv[:, 1:2]
                taps = slice(i * half, (i + 1) * half)
                filt = jnp.concatenate([fwd[taps], bwd[taps]], axis=0) * window
                tiles.append(jnp.dot(f1_ref[nb], filt.astype(BF16), preferred_element_type=F32).astype(BF16))
            s_ref[:, pl.ds(nb0, SLAB), :] = _regroup(jnp.stack(tiles, axis=0))
        _for_row_slabs(nbk, slab)

    base = step * kag
    bias = bias_ref[0]
    for k in range(kag):
        rhs = jnp.concatenate([s_ref[base + k], s_ref[FFT_NA + base + k]], axis=0)
        spec = jnp.dot(f2_ref[...], rhs, preferred_element_type=F32)
        k_ref[0, k] = jnp.concatenate([spec[:FFT_NB] + bias, spec[FFT_NB:]], axis=0).astype(BF16)


def _filter_spectra(n, w1, b1, w2, b2, freq, w3, decay, bias, tables, cb=MXU_N, kag=4 * SLAB):
    _, f1_real, f2, _, _ = tables
    hid = w2.shape[0]
    c = decay.shape[-1]
    t = np.linspace(0.0, 1.0, n, dtype=np.float32)[:, None]
    w = (2.0 * np.pi * np.arange(n, dtype=np.float32) / n).astype(np.float32)
    bands = np.linspace(1e-4, HY_BANDS - 1, HY_BANDS, dtype=np.float32)
    ang = w[:, None] * bands[None, :]
    emb = np.concatenate([t, np.cos(ang), -np.sin(ang)], axis=-1).astype(np.float32)
    kpad = 64
    idx = FFT_NB * np.arange(FFT_NA)[None, :] + np.arange(FFT_NB)[:, None]
    lagi = np.minimum(np.where(idx < n, idx, FFT_N - idx), n - 1)
    embt = np.zeros((kpad, FFT_N), np.float32)
    embt[:emb.shape[1]] = emb[lagi.reshape(-1)].T
    tv = np.stack([t[lagi, 0], (idx != n).astype(np.float32)], axis=-1)
    col = lambda a: a.reshape(hid, 1).astype(F32)
    w1t = jnp.pad(w1.astype(F32), ((0, kpad - w1.shape[0]), (0, 0))).T
    w3r = w3.astype(BF16).reshape(hid, HY_ORDER, 2, c).transpose(1, 2, 0, 3)
    dec = jnp.abs(decay.astype(F32)).reshape(HY_ORDER, 1, c)
    one = pl.Buffered(1)
    const = lambda shape: pl.BlockSpec(shape, lambda o, j, s: tuple(0 for _ in shape), pipeline_mode=one)
    per_channel = pl.BlockSpec((1, 1, cb), lambda o, j, s: (o, 0, j))
    return pl.pallas_call(
        functools.partial(_filter_kernel, nbk=FFT_NB, kag=kag),
        grid=(HY_ORDER, c // cb, FFT_NA // kag),
        in_specs=[const(embt.shape), const(tv.shape), const((hid, kpad)), const((hid, 1)),
                  const((hid, hid)), const((hid, 1)), const((hid, 1)),
                  pl.BlockSpec((1, 2, hid, cb), lambda o, j, s: (o, 0, 0, j)),
                  per_channel, per_channel,
                  const(f1_real.shape), const(f2.shape)],
        out_specs=pl.BlockSpec((1, kag, 2 * FFT_NB, cb), lambda o, j, s: (o, s, 0, j)),
        out_shape=jax.ShapeDtypeStruct((HY_ORDER, FFT_NA, 2 * FFT_NB, c), BF16),
        scratch_shapes=[pltpu.VMEM((2 * FFT_NA, FFT_NB, cb), BF16),
                        pltpu.VMEM((FFT_N, hid), BF16)],
        compiler_params=_cparams("arbitrary", "arbitrary", "arbitrary"),
        name="hyena_filter",
    )(jnp.asarray(embt), jnp.asarray(tv), w1t, col(b1), w2.astype(F32).T, col(b2), col(freq), w3r, dec,
      bias.astype(F32).reshape(HY_ORDER, 1, c), f1_real, f2)


def _out_kernel(x_ref, pos_ref, mod_ref, ys_ref, yh_ref, gw_ref, gb_ref, g5_ref, gh_ref, wo_ref,
                g2_ref, w1_ref, w2_ref, gf_ref, o_ref, *, d):
    nbatch, tokens, _ = x_ref.shape
    rows = nbatch * tokens
    m = mod_ref[...]
    gate1, shift2 = m[:, :, 2 * d:3 * d], m[:, :, 3 * d:4 * d]
    scale2, gate2 = m[:, :, 4 * d:5 * d], m[:, :, 5 * d:6 * d]
    per_batch = lambda v: v.reshape(nbatch, tokens, d)
    h = x_ref[...] + pos_ref[...][None]
    ys = _from_slot_major(ys_ref[...], nbatch)
    ab = jnp.dot(ys, gw_ref[...], preferred_element_type=F32) + gb_ref[...]
    half = ab.shape[-1] // 2
    y5 = ab[:, :half] * jax.nn.sigmoid(ab[:, half:])
    yh = yh_ref[...].reshape(rows, -1).astype(F32)
    mix = jnp.concatenate([_rms(y5, g5_ref[...]), _rms(yh, gh_ref[...])], axis=-1)
    h = h + gate1 * per_batch(jnp.dot(mix.astype(BF16), wo_ref[...], preferred_element_type=F32))
    hn = _rms(h, g2_ref[...][None]) * (1.0 + scale2) + shift2
    hn = hn.reshape(rows, d).astype(BF16)
    width = w1_ref.shape[1] // FF_CHUNKS
    mlp = jnp.zeros((rows, d), F32)
    for j in range(FF_CHUNKS):
        cols = slice(j * width, (j + 1) * width)
        hid = jnp.dot(hn, w1_ref[:, cols], preferred_element_type=F32)
        hid = jnp.square(jnp.maximum(hid, 0.0))
        mlp = mlp + jnp.dot(hid.astype(BF16), w2_ref[cols, :], preferred_element_type=F32)
    h = h + gate2 * per_batch(mlp)
    o_ref[...] = _rms(h, gf_ref[...][None])


def _output_stage(x, pos, mod3, ys_t, yh, glu_w, glu_b, g5, gh, w_out, g2, w1, w2, gf, tt=64):
    bsz, n, d = x.shape
    dh = ys_t.shape[-1]
    one = pl.Buffered(1)
    const = lambda a: pl.BlockSpec(a.shape, lambda i: tuple(0 for _ in a.shape), pipeline_mode=one)
    row = lambda a: a.reshape(1, -1).astype(F32)
    glu_b, g5, gh, g2, gf = row(glu_b), row(g5), row(gh), row(g2), row(gf)
    return pl.pallas_call(
        functools.partial(_out_kernel, d=d),
        grid=(n // tt,),
        in_specs=[pl.BlockSpec((bsz, tt, d), lambda i: (0, i, 0)),
                  pl.BlockSpec((tt, d), lambda i: (i, 0)),
                  pl.BlockSpec((bsz, 1, mod3.shape[-1]), lambda i: (0, 0, 0)),
                  pl.BlockSpec((S5_T, tt // S5_T * bsz, dh), lambda i: (0, i, 0)),
                  pl.BlockSpec((bsz, tt, yh.shape[-1]), lambda i: (0, i, 0)),
                  const(glu_w), const(glu_b), const(g5), const(gh), const(w_out), const(g2),
                  const(w1), const(w2), const(gf)],
        out_specs=pl.BlockSpec((bsz, tt, d), lambda i: (0, i, 0)),
        out_shape=jax.ShapeDtypeStruct((bsz, n, d), F32),
        compiler_params=_cparams("arbitrary"),
        name="mix_mlp_out",
    )(x, pos, mod3, ys_t, yh, glu_w, glu_b, g5, gh, w_out, g2, w1, w2, gf)


def _pos_table(n, d):
    rows = n // GRID_W
    row = np.repeat(np.arange(rows, dtype=np.float32), GRID_W)
    col = np.tile(np.arange(GRID_W, dtype=np.float32), rows)
    quarter = d // 4
    omega = (1.0 / (POS_BASE ** (np.arange(quarter, dtype=np.float32) / quarter))).astype(np.float32)

    def enc(p):
        ang = p[:, None] * omega[None, :]
        return np.concatenate([np.sin(ang), np.cos(ang)], axis=-1)

    return jnp.asarray(np.concatenate([enc(row), enc(col)], axis=-1).astype(np.float32))


def kernel(x, c, ctx, c_ctx, ada_w, ada_b, norm1_g, w_in, s5_a_re, s5_a_im, s5_log_step, s5_b_re,
           s5_b_im, s5_c_re, s5_c_im, s5_d, s5_glu_w, s5_glu_b, hy_conv_w, hy_conv_b, hy_f_w1,
           hy_f_b1, hy_f_w2, hy_f_b2, hy_f_freq, hy_f_w3, hy_decay, hy_bias, mix_g_s5, mix_g_hy,
           w_out, norm2_g, mlp_w1, mlp_w2, final_g):
    bsz, n, d = x.shape
    depth = ada_w.shape[0]
    d_s5 = s5_d.shape[-1]
    d_hy = hy_bias.shape[-1]
    nctx = ctx.shape[1]
    npair = bsz // 2
    nbk, na = FFT_NB, n // FFT_NB
    pos = _pos_table(n, d)
    tables = _dft_tables(n)

    assert depth == 1 and bsz % 2 == 0 and n % (FFT_NB * 8) == 0 and 2 * n == FFT_N
    mod_rows = 16
    c_rows = jnp.concatenate([c, c_ctx[None], jnp.zeros((mod_rows - bsz - 1, d), c.dtype)], axis=0)
    mod3 = _modulation(c_rows, ada_w[0], ada_b[0]).reshape(mod_rows, 1, N_MOD * d)

    w_in_b = w_in[0].astype(BF16)
    u_t, z = _project(x, pos, mod3, None, norm1_g[0], w_in_b, d_s5, 3 * d_hy, tt=128)
    (uc_t,) = _project(ctx, None, mod3, bsz, norm1_g[0], w_in_b, d_s5, 0, tt=128)

    s5_mats = _s5_tables(s5_a_re[0], s5_a_im[0], s5_log_step[0], s5_b_re[0], s5_b_im[0],
                         s5_c_re[0], s5_c_im[0], s5_d[0])
    ys_t = _s5_scan(u_t, uc_t, bsz, *s5_mats)

    kf = _filter_spectra(n, hy_f_w1[0], hy_f_b1[0], hy_f_w2[0], hy_f_b2[0], hy_f_freq[0],
                         hy_f_w3[0], hy_decay[0], hy_bias[0], tables)
    z5 = z.reshape(bsz, na, nbk, 3 * d_hy).transpose(0, 2, 1, 3).reshape(2, npair, nbk, na, 3 * d_hy)
    yh = _hyena(z5, hy_conv_w[0], hy_conv_b[0], kf, tables)
    yh = yh.reshape(bsz, nbk, na, d_hy).transpose(0, 2, 1, 3).reshape(bsz, n, d_hy)

    return _output_stage(x, pos, mod3, ys_t, yh, s5_glu_w[0].astype(BF16), s5_glu_b[0], mix_g_s5[0],
                         mix_g_hy[0], w_out[0].astype(BF16), norm2_g[0], mlp_w1[0].astype(BF16),
                         mlp_w2[0].astype(BF16), final_g)
```
